```python
import jax, jax.numpy as jnp
from jax import lax
import numpy as np

D_MODEL = 2048
BATCH = 4
SEQ = 2048
DEPTH = 1

GRID_W = 64
CTX_LEN = 256
EPS = 1e-6
N_MOD = 6

MLSTM_WIDTH = D_MODEL // 2
MLSTM_HEADS = 8
MLSTM_HEAD_DIM = MLSTM_WIDTH // MLSTM_HEADS
MLSTM_CHUNK = 64

LRU_WIDTH = D_MODEL // 2
LRU_BLOCKS = 8
LRU_BLOCK_DIM = LRU_WIDTH // LRU_BLOCKS
LRU_C = 8.0
CONV_WIDTH = 4
CONV_PAD = (2, 1)

N_GROUPS = 4
EXPERTS_PER_GROUP = 8
N_EXPERTS = N_GROUPS * EXPERTS_PER_GROUP
TOP_K = 2
D_EXPERT = D_MODEL // 2
MOE_BLOCK = 128

IN_SIZES = (MLSTM_WIDTH,) * 5 + (4 * MLSTM_HEADS, LRU_WIDTH, LRU_WIDTH, D_MODEL, D_MODEL)
IN_DIM = sum(IN_SIZES)
IN_SPLITS = tuple(int(s) for s in np.cumsum(IN_SIZES)[:-1])

kernel_name = 'hybrid_mlstm_rglru_hmoe_dit'


def rmsnorm(x, g):
    xf = x.astype(jnp.float32)
    y = xf * lax.rsqrt(jnp.mean(xf * xf, axis=-1, keepdims=True) + EPS)
    return (y * g).astype(x.dtype)


def prefix_flip(z, n_ctx):
    return jnp.concatenate([jnp.flip(z[:, :n_ctx], 1), jnp.flip(z[:, n_ctx:], 1)], axis=1)


def raster_to_col(z, rows):
    b, s, ch = z.shape
    return z.reshape(b, rows, GRID_W, ch).transpose(0, 2, 1, 3).reshape(b, s, ch)


def col_to_raster(z, rows):
    b, s, ch = z.shape
    return z.reshape(b, GRID_W, rows, ch).transpose(0, 2, 1, 3).reshape(b, s, ch)


def mlstm_scan(q, k, v, i_pre, logf):
    b_, n_, h_, dh = q.shape
    L = MLSTM_CHUNK
    nc = n_ // L

    def chunks(z):
        z = z.reshape((b_, nc, L, h_) + z.shape[3:])
        return jnp.moveaxis(jnp.moveaxis(z, 1, 0), 3, 2)

    causal = jnp.tril(jnp.ones((L, L), dtype=bool))

    def step(carry, inp):
        C, n, m = carry
        qc, kc, vc, ic, fc = inp
        b = jnp.cumsum(fc, axis=-1)
        dmat = jnp.where(causal, b[..., :, None] - b[..., None, :] + ic[..., None, :], -jnp.inf)
        g = b + m[..., None]
        m_t = jnp.maximum(g, dmat.max(-1))
        w_intra = jnp.exp(dmat - m_t[..., None])
        w_inter = jnp.exp(g - m_t)
        a = jnp.einsum('bhtd,bhsd->bhts', qc, kc) * w_intra
        num = jnp.einsum('bhts,bhsd->bhtd', a, vc) + w_inter[..., None] * jnp.einsum('bhtk,bhvk->bhtv', qc, C)
        den = a.sum(-1) + w_inter * jnp.einsum('bhtk,bhk->bht', qc, n)
        h = num / jnp.maximum(jnp.abs(den), jnp.exp(-m_t))[..., None]
        b_last = b[..., -1]
        wlog = b_last[..., None] - b + ic
        m_new = jnp.maximum(b_last + m, wlog.max(-1))
        ws = jnp.exp(wlog - m_new[..., None])
        decay = jnp.exp(b_last + m - m_new)
        C = decay[..., None, None] * C + jnp.einsum('bhs,bhsv,bhsk->bhvk', ws, vc, kc)
        n = decay[..., None] * n + jnp.einsum('bhs,bhsk->bhk', ws, kc)
        return (C, n, m_new), h

    init = (jnp.zeros((b_, h_, dh, dh), jnp.float32), jnp.zeros((b_, h_, dh), jnp.float32),
            jnp.full((b_, h_), -1e30, jnp.float32))
    _, hs = lax.scan(step, init, (chunks(q), chunks(k), chunks(v), chunks(i_pre), chunks(logf)))
    return jnp.swapaxes(jnp.moveaxis(hs, 0, 1), 2, 3).reshape(b_, n_, h_, dh)


def head_rmsnorm(h, g):
    return h * lax.rsqrt(jnp.mean(h * h, axis=-1, keepdims=True) + EPS) * g


def mlstm_branch(q, k, v, o_f, o_b, gates, head_g, n_ctx):
    b_, n_, _ = q.shape
    f32 = jnp.float32
    heads = lambda z: z.astype(f32).reshape(b_, n_, MLSTM_HEADS, MLSTM_HEAD_DIM)
    qh, kh, vh = heads(q), heads(k) * MLSTM_HEAD_DIM ** -0.5, heads(v)
    gt = gates.astype(f32).reshape(b_, n_, 4, MLSTM_HEADS)
    fl = lambda z: prefix_flip(z, n_ctx)
    h_f = head_rmsnorm(mlstm_scan(qh, kh, vh, gt[:, :, 0], jax.nn.log_sigmoid(gt[:, :, 1])), head_g)
    h_b = fl(head_rmsnorm(mlstm_scan(fl(qh), fl(kh), fl(vh), fl(gt[:, :, 2]),
                                     jax.nn.log_sigmoid(fl(gt[:, :, 3]))), head_g))
    y = (jax.nn.sigmoid(o_f.astype(f32)) * h_f.reshape(b_, n_, MLSTM_WIDTH)
         + jax.nn.sigmoid(o_b.astype(f32)) * h_b.reshape(b_, n_, MLSTM_WIDTH))
    return y.astype(q.dtype)


def depthwise_conv(z, w, b):
    out = lax.conv_general_dilated(z, w[:, None, :], window_strides=(1,), padding=[CONV_PAD],
                                   dimension_numbers=('NWC', 'WIO', 'NWC'), feature_group_count=z.shape[-1])
    return out + b


def linear_scan(a, b):
    def combine(l, r):
        return (l[0] * r[0], r[0] * l[1] + r[1])
    _, h = lax.associative_scan(combine, (a, b), axis=1)
    return h


def rglru_direction(z, w_a, b_a, w_x, b_x, lam):
    b_, n_, _ = z.shape
    zb = z.reshape(b_, n_, LRU_BLOCKS, LRU_BLOCK_DIM)
    r = jax.nn.sigmoid(jnp.einsum('bnkc,kcd->bnkd', zb, w_a.astype(jnp.float32)).reshape(b_, n_, LRU_WIDTH) + b_a)
    i = jax.nn.sigmoid(jnp.einsum('bnkc,kcd->bnkd', zb, w_x.astype(jnp.float32)).reshape(b_, n_, LRU_WIDTH) + b_x)
    log_a = -LRU_C * r * jax.nn.softplus(-lam.astype(jnp.float32))
    a = jnp.exp(log_a)
    mult = jnp.sqrt(-jnp.expm1(2.0 * log_a))
    return linear_scan(a, mult * i * z)


def rglru_branch(u_x, u_g, conv_w, conv_b, w_a, b_a, w_x, b_x, lam, n_ctx, rows):
    x_ctx = depthwise_conv(u_x[:, :n_ctx], conv_w, conv_b)
    x_lat = depthwise_conv(raster_to_col(u_x[:, n_ctx:], rows), conv_w, conv_b)
    z = jnp.concatenate([x_ctx, x_lat], axis=1).astype(jnp.float32)
    h_f = rglru_direction(z, w_a[0], b_a[0], w_x[0], b_x[0], lam[0])
    h_b = prefix_flip(rglru_direction(prefix_flip(z, n_ctx), w_a[1], b_a[1], w_x[1], b_x[1], lam[1]), n_ctx)
    h = h_f + h_b
    h = jnp.concatenate([h[:, :n_ctx], col_to_raster(h[:, n_ctx:], rows)], axis=1)
    return (h * jax.nn.gelu(u_g.astype(jnp.float32))).astype(u_x.dtype)


def grouped_experts(h, expert, weight, w1, w3, w2):
    t_, d_ = h.shape
    n_assign = expert.shape[0]
    tok = jnp.arange(n_assign, dtype=jnp.int32) // TOP_K
    order = jnp.argsort(expert)
    e_sorted, tok_sorted, w_sorted = expert[order], tok[order], weight[order]
    counts = jnp.bincount(expert, length=N_EXPERTS)
    padded = (counts + MOE_BLOCK - 1) // MOE_BLOCK * MOE_BLOCK
    start = jnp.cumsum(counts) - counts
    pend = jnp.cumsum(padded)
    pstart = pend - padded
    dest = pstart[e_sorted] + jnp.arange(n_assign, dtype=jnp.int32) - start[e_sorted]
    n_blocks = -(-n_assign // MOE_BLOCK) + N_EXPERTS
    n_slots = n_blocks * MOE_BLOCK
    slot_tok = jnp.full((n_slots,), t_, jnp.int32).at[dest].set(tok_sorted)
    slot_w = jnp.zeros((n_slots,), h.dtype).at[dest].set(w_sorted)
    block_expert = jnp.minimum(jnp.searchsorted(pend, jnp.arange(n_blocks) * MOE_BLOCK, side='right'),
                               N_EXPERTS - 1)
    h_pad = jnp.concatenate([h, jnp.zeros((1, d_), h.dtype)], axis=0)
    xs = h_pad[slot_tok].reshape(n_blocks, MOE_BLOCK, d_)

    def run_block(args):
        xb, e = args
        return (jax.nn.silu(xb @ w1[e]) * (xb @ w3[e])) @ w2[e]

    ys = lax.map(run_block, (xs, block_expert)).reshape(n_slots, d_)
    out = jnp.zeros((t_ + 1, d_), ys.dtype).at[slot_tok].add(ys * slot_w[:, None])
    return out[:t_]


def hier_moe(h, w_rg, w_re, w1, w3, w2):
    t_ = h.shape[0]
    hf = h.astype(jnp.float32)
    g_logits = hf @ w_rg.astype(jnp.float32)
    g_prob = jax.nn.softmax(g_logits, axis=-1)
    p_g, g_sel = lax.top_k(g_prob, 1)
    e_logits = (hf @ w_re.astype(jnp.float32)).reshape(t_, N_GROUPS, EXPERTS_PER_GROUP)
    e_logits = jnp.take_along_axis(e_logits, g_sel[:, :, None], axis=1)[:, 0]
    top_val, top_idx = lax.top_k(e_logits, TOP_K)
    weights = (p_g * jax.nn.softmax(top_val, axis=-1)).astype(h.dtype)
    expert = g_sel * EXPERTS_PER_GROUP + top_idx
    return grouped_experts(h, expert.reshape(-1).astype(jnp.int32), weights.reshape(-1), w1, w3, w2)


def setup_inputs(seed: int = 0) -> dict:
    key = jax.random.key(seed)
    ks = jax.random.split(key, 32)
    L, D, H = DEPTH, D_MODEL, MLSTM_HEADS
    f32 = jnp.float32
    nrm = lambda k, shape: jax.random.normal(k, shape, f32)
    lin = lambda k, shape, fan_in: nrm(k, shape) * fan_in ** -0.5
    u = jax.random.uniform(ks[17], (L, 2, LRU_WIDTH), f32, minval=0.9, maxval=0.999)
    p = u ** (1.0 / LRU_C)
    i_bias = 0.1 * nrm(ks[8], (L, 2, H))
    f_bias = jnp.linspace(3.0, 6.0, H, dtype=f32) + 0.1 * nrm(ks[9], (L, 2, H))
    b_gates = jnp.stack([i_bias[:, 0], f_bias[:, 0], i_bias[:, 1], f_bias[:, 1]], axis=1).reshape(L, 4 * H)
    return {
        'x': nrm(ks[0], (BATCH, SEQ, D)),
        'c': nrm(ks[1], (BATCH, D)),
        'ctx': nrm(ks[2], (BATCH, CTX_LEN, D)),
        'c_ctx': nrm(ks[3], (D,)),
        'w_mod': 0.5 * lin(ks[4], (L, D, N_MOD * D), D),
        'b_mod': 0.01 * nrm(ks[5], (L, N_MOD * D)),
        'norm1_g': 1.0 + 0.05 * nrm(ks[6], (L, D)),
        'w_in': lin(ks[7], (L, D, IN_DIM), D),
        'b_gates': b_gates,
        'mlstm_head_g': 1.0 + 0.05 * nrm(ks[10], (L, H, MLSTM_HEAD_DIM)),
        'conv_w': lin(ks[11], (L, CONV_WIDTH, LRU_WIDTH), CONV_WIDTH),
        'conv_b': 0.01 * nrm(ks[12], (L, LRU_WIDTH)),
        'lru_wa': lin(ks[13], (L, 2, LRU_BLOCKS, LRU_BLOCK_DIM, LRU_BLOCK_DIM), LRU_BLOCK_DIM),
        'lru_ba': 0.01 * nrm(ks[14], (L, 2, LRU_WIDTH)),
        'lru_wx': lin(ks[15], (L, 2, LRU_BLOCKS, LRU_BLOCK_DIM, LRU_BLOCK_DIM), LRU_BLOCK_DIM),
        'lru_bx': 0.01 * nrm(ks[16], (L, 2, LRU_WIDTH)),
        'lru_lam': jnp.log(p) - jnp.log1p(-p),
        'w_pa': lin(ks[18], (L, MLSTM_WIDTH, D), MLSTM_WIDTH),
        'w_pb': lin(ks[19], (L, LRU_WIDTH, D), LRU_WIDTH),
        'w_out': lin(ks[20], (L, D, D), D),
        'norm2_g': 1.0 + 0.05 * nrm(ks[21], (L, D)),
        'w_rg': lin(ks[22], (L, D, N_GROUPS), D),
        'w_re': lin(ks[23], (L, D, N_EXPERTS), D),
        'w1': lin(ks[24], (L, N_EXPERTS, D, D_EXPERT), D),
        'w3': lin(ks[25], (L, N_EXPERTS, D, D_EXPERT), D),
        'w2': lin(ks[26], (L, N_EXPERTS, D_EXPERT, D), D_EXPERT),
        'final_g': 1.0 + 0.05 * nrm(ks[27], (D,)),
    }


def reference(x, c, ctx, c_ctx, w_mod, b_mod, norm1_g, w_in, b_gates, mlstm_head_g, conv_w, conv_b,
              lru_wa, lru_ba, lru_wx, lru_bx, lru_lam, w_pa, w_pb, w_out, norm2_g, w_rg, w_re,
              w1, w3, w2, final_g):
    b_, s_, d_ = x.shape
    n_ctx = ctx.shape[1]
    rows = s_ // GRID_W
    for l in range(DEPTH):
        update_ctx = l < DEPTH - 1
        mod = jax.nn.silu(c) @ w_mod[l] + b_mod[l]
        mod_c = jax.nn.silu(c_ctx) @ w_mod[l] + b_mod[l]
        sh1, sc1, g1, sh2, sc2, g2 = jnp.split(mod[:, None, :], N_MOD, axis=-1)
        csh1, csc1, cg1, csh2, csc2, cg2 = jnp.split(mod_c, N_MOD, axis=-1)

        h = jnp.concatenate([rmsnorm(ctx, norm1_g[l]) * (1 + csc1) + csh1,
                             rmsnorm(x, norm1_g[l]) * (1 + sc1) + sh1], axis=1)
        u = h @ w_in[l]
        q, k, v, o_f, o_b, gates, u_x, u_g, gate_a, gate_b = jnp.split(u, IN_SPLITS, axis=-1)
        y_a = mlstm_branch(q, k, v, o_f, o_b, gates + b_gates[l], mlstm_head_g[l], n_ctx)
        y_b = rglru_branch(u_x, u_g, conv_w[l], conv_b[l], lru_wa[l], lru_ba[l], lru_wx[l], lru_bx[l],
                           lru_lam[l], n_ctx, rows)
        keep = 0 if update_ctx else n_ctx
        mixed = (jax.nn.sigmoid(gate_a[:, keep:]) * (y_a[:, keep:] @ w_pa[l])
                 + jax.nn.sigmoid(gate_b[:, keep:]) * (y_b[:, keep:] @ w_pb[l])) @ w_out[l]
        x = x + g1 * mixed[:, -s_:]
        if update_ctx:
            ctx = ctx + cg1 * mixed[:, :n_ctx]

        h2 = (rmsnorm(x, norm2_g[l]) * (1 + sc2) + sh2).reshape(b_ * s_, d_)
        if update_ctx:
            h2c = (rmsnorm(ctx, norm2_g[l]) * (1 + csc2) + csh2).reshape(b_ * n_ctx, d_)
            y = hier_moe(jnp.concatenate([h2c, h2], axis=0), w_rg[l], w_re[l], w1[l], w3[l], w2[l])
            ctx = ctx + cg2 * y[:b_ * n_ctx].reshape(b_, n_ctx, d_)
            y = y[b_ * n_ctx:]
        else:
            y = hier_moe(h2, w_rg[l], w_re[l], w1[l], w3[l], w2[l])
        x = x + g2 * y.reshape(b_, s_, d_)
    return rmsnorm(x, final_g)
```

```python
import functools

import jax
import jax.numpy as jnp
from jax import lax
from jax.experimental import pallas as pl
from jax.experimental.pallas import tpu as pltpu

F32 = jnp.float32
BF16 = jnp.bfloat16

EPS = 1e-6
GRID_W = 64
MLSTM_HEADS = 8
LRU_BLOCKS = 8
LRU_C = 8.0
N_GROUPS = 4
EXPERTS_PER_GROUP = 8
N_EXPERTS = N_GROUPS * EXPERTS_PER_GROUP
TOP_K = 2

VMEM_LIMIT_BYTES = 56 * 1024 * 1024
LANES = 128

MLSTM_CHUNK = 128
MOE_ITEM_ROWS = 1024
MOE_SUB_ROWS = 256
MOE_F_CHUNK = 256


def _cparams(sem):
    return pltpu.CompilerParams(dimension_semantics=sem, vmem_limit_bytes=VMEM_LIMIT_BYTES)


def _sigmoid(x):
    return 1.0 / (1.0 + jnp.exp(-x))


def _dot(a, b):
    return jnp.dot(a, b, preferred_element_type=F32)


def _split3(x):
    x1 = x.astype(BF16)
    r1 = x - x1.astype(F32)
    x2 = r1.astype(BF16)
    x3 = (r1 - x2.astype(F32)).astype(BF16)
    return x1, x2, x3


def _mod_kernel(c_ref, w_ref, b_ref, o_ref):
    c = c_ref[...]
    s = (c * _sigmoid(c)).astype(BF16)
    o_ref[...] = _dot(s, w_ref[...].astype(BF16)) + b_ref[...]


def _modulation(cc, w_mod, b_mod):
    m, d = cc.shape
    n = w_mod.shape[1]
    tn = 1024
    return pl.pallas_call(
        _mod_kernel,
        out_shape=jax.ShapeDtypeStruct((m, n), F32),
        grid=(n // tn,),
        in_specs=[
            pl.BlockSpec((m, d), lambda j: (0, 0)),
            pl.BlockSpec((d, tn), lambda j: (0, j)),
            pl.BlockSpec((1, tn), lambda j: (0, j)),
        ],
        out_specs=pl.BlockSpec((m, tn), lambda j: (0, j)),
        compiler_params=_cparams(("parallel",)),
        name="modulation",
    )(cc, w_mod, b_mod)


def _inproj_kernel(x_ref, gain_ref, shift_ref, w_ref, wg_ref, bg_ref, u_ref, g_ref, h_scr):
    @pl.when(pl.program_id(2) == 0)
    def _():
        x = x_ref[0]
        ms = jnp.mean(x * x, axis=-1, keepdims=True)
        h = x * lax.rsqrt(ms + EPS) * gain_ref[0] + shift_ref[0]
        hb = h.astype(BF16)
        h_scr[...] = hb
        g_ref[0] = _dot(hb, wg_ref[...]) + bg_ref[...]

    u_ref[0] = _dot(h_scr[...], w_ref[...]).astype(BF16)


def _inproj(x, gain, shift, w_main, w_gates, b_gates, nj, tm):
    b_, s_, d = x.shape
    tn = 1024
    per_batch_mod = gain.shape[0] > 1
    wmap = lambda b, i, j: (0, j)
    mmap = (lambda b, i, j: (b, 0, 0)) if per_batch_mod else (lambda b, i, j: (0, 0, 0))
    return pl.pallas_call(
        _inproj_kernel,
        out_shape=(jax.ShapeDtypeStruct((b_, s_, nj * tn), BF16),
                   jax.ShapeDtypeStruct((b_, s_, LANES), F32)),
        grid=(b_, s_ // tm, nj),
        in_specs=[
            pl.BlockSpec((1, tm, d), lambda b, i, j: (b, i, 0)),
            pl.BlockSpec((1, 1, d), mmap),
            pl.BlockSpec((1, 1, d), mmap),
            pl.BlockSpec((d, tn), wmap),
            pl.BlockSpec((d, LANES), lambda b, i, j: (0, 0)),
            pl.BlockSpec((1, LANES), lambda b, i, j: (0, 0)),
        ],
        out_specs=(pl.BlockSpec((1, tm, tn), lambda b, i, j: (b, i, j)),
                   pl.BlockSpec((1, tm, LANES), lambda b, i, j: (b, i, 0))),
        scratch_shapes=[pltpu.VMEM((tm, d), BF16)],
        compiler_params=_cparams(("parallel", "parallel", "arbitrary")),
        name="inproj",
    )(x, gain, shift, w_main, w_gates, b_gates)


COL_Q, COL_K, COL_V, COL_UX, COL_OF, COL_OB, COL_UG, COL_GA, COL_GB = 0, 1, 2, 3, 4, 5, 6, 7, 9
N_COL_TILES = 11
N_CTX_TILES = 4


def _mlstm_kernel(nc, nl, L, dh,
                  qc_f, kc_f, vc_f, ql_f, kl_f, vl_f, gc_f, gl_f, o_f,
                  qc_b, kc_b, vc_b, ql_b, kl_b, vl_b, gc_b, gl_b, o_b,
                  hg_ref, tri_ref,
                  out_f, out_b, c_scr, m_scr):
    s = pl.program_id(1)
    nh = MLSTM_HEADS
    scale = dh ** -0.5

    @pl.when(s == 0)
    def _():
        c_scr[...] = jnp.zeros_like(c_scr)
        m_scr[...] = jnp.full_like(m_scr, -1e30)

    is_ctx = s < nc
    ones_col = (lax.broadcasted_iota(jnp.int32, (L, dh), 1) == 0).astype(BF16)
    row = lax.broadcasted_iota(jnp.int32, (L, L), 0)
    col = lax.broadcasted_iota(jnp.int32, (L, L), 1)

    for d, (qc, kc, vc, ql, kl, vl, gc, gl, o_ref, out_ref) in enumerate((
            (qc_f, kc_f, vc_f, ql_f, kl_f, vl_f, gc_f, gl_f, o_f, out_f),
            (qc_b, kc_b, vc_b, ql_b, kl_b, vl_b, gc_b, gl_b, o_b, out_b))):
        q = jnp.where(is_ctx, qc[0], ql[0])
        k = jnp.where(is_ctx, kc[0], kl[0])
        v = jnp.where(is_ctx, vc[0], vl[0])
        g = jnp.where(is_ctx, gc[0], gl[0])
        gt = g.T
        ls = jnp.minimum(g, 0.0) - jnp.log(1.0 + jnp.exp(-jnp.abs(g)))
        lst = jnp.minimum(gt, 0.0) - jnp.log(1.0 + jnp.exp(-jnp.abs(gt)))
        tri = tri_ref[d]
        mask = (col <= row) if d == 0 else (col >= row)
        l1, l2, l3 = _split3(ls)
        bcol_all = _dot(tri, l1) + _dot(tri, l2) + _dot(tri, l3)
        t1, t2, t3 = _split3(lst)
        trit = tri_ref[1 - d]
        brow_all = _dot(t1, trit) + _dot(t2, trit) + _dot(t3, trit)
        last = L - 1 if d == 0 else 0
        for h in range(nh):
            u = d * nh + h
            ci = 16 * d + h
            cf = 16 * d + 8 + h
            hs = slice(h * dh, (h + 1) * dh)
            qh, kh, vh = q[:, hs], k[:, hs], v[:, hs]
            vext = jnp.concatenate([vh, ones_col], axis=1)
            bc = bcol_all[:, cf:cf + 1]
            ic = g[:, ci:ci + 1]
            cr = gt[ci:ci + 1, :] - brow_all[cf:cf + 1, :]
            tot = bcol_all[last:last + 1, cf:cf + 1]
            m_prev = m_scr[u][0:1, 0:1]
            c_prev = c_scr[u]

            sqk = lax.dot_general(qh, kh, (((1,), (1,)), ((), ())), preferred_element_type=F32) * scale
            dm = jnp.where(mask, bc + cr, -jnp.inf)
            rowmax = jnp.max(dm, axis=-1, keepdims=True)
            gg = bc + m_prev
            mt = jnp.maximum(gg, rowmax)
            w_intra = jnp.exp(dm - mt)
            w_inter = jnp.exp(gg - mt)
            a = (sqk * w_intra).astype(BF16)
            r = _dot(a, vext) + w_inter * _dot(qh, c_prev.astype(BF16))

            @pl.when(jnp.logical_not(is_ctx))
            def _(r=r, mt=mt, h=h, hs=hs, o_ref=o_ref, out_ref=out_ref):
                num = r[:, :dh]
                den = r[:, dh:dh + 1]
                hh = num / jnp.maximum(jnp.abs(den), jnp.exp(-mt))
                hn = hh * lax.rsqrt(jnp.mean(hh * hh, axis=-1, keepdims=True) + EPS) * hg_ref[h:h + 1, :]
                gate = _sigmoid(o_ref[0, :, hs].astype(F32))
                out_ref[0, :, hs] = (gate * hn).astype(BF16)

            wlog = tot - bc + ic
            m_new = jnp.maximum(tot + m_prev, jnp.max(wlog, axis=0, keepdims=True))
            ws = jnp.exp(wlog - m_new) * scale
            decay = jnp.exp(tot + m_prev - m_new)
            kw = (kh.astype(F32) * ws).astype(BF16)
            upd = lax.dot_general(kw, vext, (((0,), (0,)), ((), ())), preferred_element_type=F32)
            c_scr[u] = decay * c_prev + upd
            m_scr[u] = jnp.broadcast_to(m_new, m_scr.shape[1:])


def _mlstm(u_lat, u_ctx, g_lat, g_ctx, head_g, L):
    b_, s_, _ = u_lat.shape
    n_ctx = u_ctx.shape[1]
    nh = MLSTM_HEADS
    w = 1024
    dh = w // nh
    nc, nl = n_ctx // L, s_ // L
    steps = nc + nl
    t0 = jnp.tril(jnp.ones((L, L), F32))
    tri = jnp.stack([t0, t0.T]).astype(BF16)

    def cidx(d):
        if d == 0:
            return lambda st: jnp.minimum(st, nc - 1), lambda st: jnp.maximum(st - nc, 0)
        return lambda st: jnp.maximum(nc - 1 - st, 0), lambda st: jnp.minimum(nl - 1 + nc - st, nl - 1)

    in_specs, args = [], []
    for d in range(2):
        fc, fl = cidx(d)
        for t in range(3):
            in_specs.append(pl.BlockSpec((1, L, w), lambda b, st, fc=fc, t=t: (b, fc(st), t)))
            args.append(u_ctx)
        for t in (COL_Q, COL_K, COL_V):
            in_specs.append(pl.BlockSpec((1, L, w), lambda b, st, fl=fl, t=t: (b, fl(st), t)))
            args.append(u_lat)
        in_specs.append(pl.BlockSpec((1, L, LANES), lambda b, st, fc=fc: (b, fc(st), 0)))
        args.append(g_ctx)
        in_specs.append(pl.BlockSpec((1, L, LANES), lambda b, st, fl=fl: (b, fl(st), 0)))
        args.append(g_lat)
        in_specs.append(pl.BlockSpec((1, L, w), lambda b, st, fl=fl, t=COL_OF + d: (b, fl(st), t)))
        args.append(u_lat)
    in_specs += [pl.BlockSpec((nh, dh), lambda b, st: (0, 0)),
                 pl.BlockSpec((2, L, L), lambda b, st: (0, 0, 0))]
    args += [head_g, tri]
    out_specs = tuple(pl.BlockSpec((1, L, w), lambda b, st, fl=cidx(d)[1]: (b, fl(st), 0)) for d in range(2))
    return pl.pallas_call(
        functools.partial(_mlstm_kernel, nc, nl, L, dh),
        out_shape=(jax.ShapeDtypeStruct((b_, s_, w), BF16),) * 2,
        grid=(b_, steps),
        in_specs=in_specs,
        out_specs=out_specs,
        scratch_shapes=[pltpu.VMEM((2 * nh, dh, 2 * dh), F32), pltpu.VMEM((2 * nh, 8, LANES), F32)],
        compiler_params=_cparams(("parallel", "arbitrary")),
        name="mlstm",
    )(*args)


def _rglru_kernel(n_ctx, n_tot, ux_ref, cw_ref, cb_ref, w_ref, b_ref, lam_ref, out_ref,
                  a_f, b_f, a_b, b_b, h_f, h_b):
    u = ux_ref[0].astype(F32)
    t_idx = lax.broadcasted_iota(jnp.int32, (n_tot, 1), 0)
    seg_lo = jnp.where(t_idx < n_ctx, 0, n_ctx)
    seg_hi = jnp.where(t_idx < n_ctx, n_ctx, n_tot)
    z = jnp.zeros_like(u) + cb_ref[...]
    for j, off in enumerate((-2, -1, 0, 1)):
        if off == 0:
            src = u
        else:
            src = pltpu.roll(u, (-off) % n_tot, 0)
            ok = (t_idx + off >= seg_lo) & (t_idx + off < seg_hi)
            src = jnp.where(ok, src, 0.0)
        z = z + cw_ref[j:j + 1, :] * src
    p = _dot(z.astype(BF16), w_ref[0]) + b_ref[...]
    c = LANES
    for d, (a_scr, b_scr) in enumerate(((a_f, b_f), (a_b, b_b))):
        r = _sigmoid(p[:, (2 * d) * c:(2 * d + 1) * c])
        i = _sigmoid(p[:, (2 * d + 1) * c:(2 * d + 2) * c])
        lam = lam_ref[d:d + 1, :]
        softplus = jnp.maximum(-lam, 0.0) + jnp.log(1.0 + jnp.exp(-jnp.abs(lam)))
        log_a = -LRU_C * r * softplus
        a = jnp.exp(log_a)
        a_scr[...] = a
        b_scr[...] = jnp.sqrt(1.0 - a * a) * i * z

    sub = lax.broadcasted_iota(jnp.int32, (8, c), 0)
    ng = n_tot // 8
    ngc = n_ctx // 8

    def group_scan(a, b, carry, reverse):
        for sft in (1, 2, 4):
            if reverse:
                a_s = pltpu.roll(a, 8 - sft, 0)
                b_s = pltpu.roll(b, 8 - sft, 0)
                ok = sub < 8 - sft
            else:
                a_s = pltpu.roll(a, sft, 0)
                b_s = pltpu.roll(b, sft, 0)
                ok = sub >= sft
            b = jnp.where(ok, a * b_s + b, b)
            a = jnp.where(ok, a * a_s, a)
        h = b + a * carry
        return h, (h[0:1, :] if reverse else h[7:8, :])

    def body(it, carry):
        cf, cb = carry
        rf = pl.multiple_of(it * 8, 8)
        hf, cf = group_scan(a_f[pl.ds(rf, 8), :], b_f[pl.ds(rf, 8), :], cf, False)
        h_f[pl.ds(rf, 8), :] = hf
        jb = jnp.where(it < ngc, ngc - 1 - it, ng - 1 + ngc - it)
        rb = pl.multiple_of(jb * 8, 8)
        hb, cb = group_scan(a_b[pl.ds(rb, 8), :], b_b[pl.ds(rb, 8), :], cb, True)
        h_b[pl.ds(rb, 8), :] = hb
        return cf, cb

    zero = jnp.zeros((1, c), F32)
    lax.fori_loop(0, ng, body, (zero, zero))
    out_ref[0] = (h_f[n_ctx:, :] + h_b[n_ctx:, :]).astype(BF16)


def _rglru(ux_seq, conv_w, conv_b, w_cat, b_cat, lam, n_ctx):
    b_, n_tot, w = ux_seq.shape
    nb = LRU_BLOCKS
    c = w // nb
    return pl.pallas_call(
        functools.partial(_rglru_kernel, n_ctx, n_tot),
        out_shape=jax.ShapeDtypeStruct((b_, n_tot - n_ctx, w), BF16),
        grid=(b_, nb),
        in_specs=[
            pl.BlockSpec((1, n_tot, c), lambda b, k: (b, 0, k)),
            pl.BlockSpec((4, c), lambda b, k: (0, k)),
            pl.BlockSpec((1, c), lambda b, k: (0, k)),
            pl.BlockSpec((1, c, 4 * c), lambda b, k: (k, 0, 0)),
            pl.BlockSpec((1, 4 * c), lambda b, k: (0, k)),
            pl.BlockSpec((2, c), lambda b, k: (0, k)),
        ],
        out_specs=pl.BlockSpec((1, n_tot - n_ctx, c), lambda b, k: (b, 0, k)),
        scratch_shapes=[pltpu.VMEM((n_tot, c), F32)] * 6,
        compiler_params=_cparams(("parallel", "parallel")),
        name="rglru",
    )(ux_seq, conv_w, conv_b, w_cat, b_cat, lam)


def _outproj_kernel(yaf_ref, yab_ref, hr_ref, ug_ref, ga0_ref, ga1_ref, gb0_ref, gb1_ref, x_ref,
                    g1_ref, gain_ref, shift_ref, wpa_ref, wpb_ref, wout_ref, wr_ref,
                    x1_ref, h2_ref, lg_ref):
    ya = (yaf_ref[0].astype(F32) + yab_ref[0].astype(F32)).astype(BF16)
    ug = ug_ref[0].astype(F32)
    gelu = 0.5 * ug * (1.0 + jnp.tanh(0.7978845608028654 * (ug + 0.044715 * ug * ug * ug)))
    yb = (hr_ref[0].astype(F32) * gelu).astype(BF16)
    pa = _dot(ya, wpa_ref[...])
    pb = _dot(yb, wpb_ref[...])
    ga = jnp.concatenate([ga0_ref[0], ga1_ref[0]], axis=1).astype(F32)
    gb = jnp.concatenate([gb0_ref[0], gb1_ref[0]], axis=1).astype(F32)
    mix = (_sigmoid(ga) * pa + _sigmoid(gb) * pb).astype(BF16)
    x1 = x_ref[0] + g1_ref[0] * _dot(mix, wout_ref[...])
    x1_ref[0] = x1
    ms = jnp.mean(x1 * x1, axis=-1, keepdims=True)
    h2 = x1 * lax.rsqrt(ms + EPS) * gain_ref[0] + shift_ref[0]
    h2_ref[0] = h2
    wr = wr_ref[...]
    h1, h2b, _ = _split3(h2)
    w1, w2b, _ = _split3(wr)
    lg_ref[0] = _dot(h1, w1) + (_dot(h1, w2b) + _dot(h2b, w1))


def _outproj(ya_f, ya_b, hr, u_lat, x, g1, gain2, shift2, w_pa, w_pb, w_out, w_r, tm):
    b_, s_, d = x.shape
    w = ya_f.shape[2]
    row = lambda b, i: (b, i, 0)
    const2 = lambda b, i: (0, 0)
    bmap = lambda b, i: (b, 0, 0)
    ucol = lambda t: (lambda b, i: (b, i, t))
    single = pl.Buffered(1)
    return pl.pallas_call(
        _outproj_kernel,
        out_shape=(jax.ShapeDtypeStruct((b_, s_, d), F32),
                   jax.ShapeDtypeStruct((b_, s_, d), F32),
                   jax.ShapeDtypeStruct((b_, s_, LANES), F32)),
        grid=(b_, s_ // tm),
        in_specs=[
            pl.BlockSpec((1, tm, w), row), pl.BlockSpec((1, tm, w), row), pl.BlockSpec((1, tm, w), row),
            pl.BlockSpec((1, tm, w), ucol(COL_UG)),
            pl.BlockSpec((1, tm, w), ucol(COL_GA)), pl.BlockSpec((1, tm, w), ucol(COL_GA + 1)),
            pl.BlockSpec((1, tm, w), ucol(COL_GB)), pl.BlockSpec((1, tm, w), ucol(COL_GB + 1)),
            pl.BlockSpec((1, tm, d), row),
            pl.BlockSpec((1, 1, d), bmap), pl.BlockSpec((1, 1, d), bmap), pl.BlockSpec((1, 1, d), bmap),
            pl.BlockSpec((w, d), const2, pipeline_mode=single),
            pl.BlockSpec((w, d), const2, pipeline_mode=single),
            pl.BlockSpec((d, d), const2, pipeline_mode=single),
            pl.BlockSpec((d, LANES), const2, pipeline_mode=single),
        ],
        out_specs=(pl.BlockSpec((1, tm, d), row), pl.BlockSpec((1, tm, d), row),
                   pl.BlockSpec((1, tm, LANES), row)),
        compiler_params=_cparams(("parallel", "parallel")),
        name="outproj",
    )(ya_f, ya_b, hr, u_lat, u_lat, u_lat, u_lat, u_lat, x, g1, gain2, shift2, w_pa, w_pb, w_out, w_r)


def _route_kernel(lg_ref, o_ref):
    lg = lg_ref[...]
    lane = lax.broadcasted_iota(jnp.int32, lg.shape, 1)
    neg = -jnp.inf
    big = jnp.int32(1 << 20)
    g_l = jnp.where(lane < N_GROUPS, lg, neg)
    g_max = jnp.max(g_l, axis=-1, keepdims=True)
    g_sel = jnp.min(jnp.where(g_l == g_max, lane, big), axis=-1, keepdims=True)
    p_g = 1.0 / jnp.sum(jnp.exp(g_l - g_max), axis=-1, keepdims=True)
    lo = N_GROUPS + g_sel * EXPERTS_PER_GROUP
    e_l = jnp.where((lane >= lo) & (lane < lo + EXPERTS_PER_GROUP), lg, neg)
    v1 = jnp.max(e_l, axis=-1, keepdims=True)
    i1 = jnp.min(jnp.where(e_l == v1, lane, big), axis=-1, keepdims=True)
    e_l2 = jnp.where(lane == i1, neg, e_l)
    v2 = jnp.max(e_l2, axis=-1, keepdims=True)
    i2 = jnp.min(jnp.where(e_l2 == v2, lane, big), axis=-1, keepdims=True)
    e2 = jnp.exp(v2 - v1)
    w1 = p_g / (1.0 + e2)
    w2 = p_g * e2 / (1.0 + e2)
    out = jnp.where(lane == 0, (i1 - N_GROUPS).astype(F32),
          jnp.where(lane == 1, (i2 - N_GROUPS).astype(F32),
          jnp.where(lane == 2, w1, jnp.where(lane == 3, w2, 0.0))))
    o_ref[...] = out


def _route(logits, tm):
    t_, _ = logits.shape
    return pl.pallas_call(
        _route_kernel,
        out_shape=jax.ShapeDtypeStruct((t_, LANES), F32),
        grid=(t_ // tm,),
        in_specs=[pl.BlockSpec((tm, LANES), lambda i: (i, 0))],
        out_specs=pl.BlockSpec((tm, LANES), lambda i: (i, 0)),
        compiler_params=_cparams(("parallel",)),
        name="route",
    )(logits)


def _moe_kernel(R, SUB, nf, item_e, item_rows, tok_ref, dst_ref, w_ref, h2_hbm, w1_ref, w3_ref, w2_ref,
                y_hbm, x_scr, acc_scr, w1b, w3b, w2b, gsem, ssem):
    i = pl.program_id(0)
    j = pl.program_id(1)
    rows = item_rows[i]
    nsub = (rows + SUB - 1) // SUB
    nrow_pad = nsub * SUB

    def gather_copy(r):
        tok = tok_ref[0, 0, r]
        return pltpu.make_async_copy(h2_hbm.at[pl.ds(tok, 1), :], x_scr.at[pl.ds(r, 1), :], gsem)

    def scatter_copy(r):
        dst = dst_ref[0, 0, r]
        return pltpu.make_async_copy(acc_scr.at[pl.ds(r, 1), :], y_hbm.at[pl.ds(dst, 1), :], ssem)

    @pl.when((j == 0) & (rows > 0))
    def _():
        def start(r, c):
            gather_copy(r).start()
            return c
        lax.fori_loop(0, nrow_pad, start, 0)

        def wait(r, c):
            gather_copy(r).wait()
            return c
        lax.fori_loop(0, nrow_pad, wait, 0)

    @pl.when(rows > 0)
    def _():
        w1b[...] = w1_ref[0].astype(BF16)
        w3b[...] = w3_ref[0].astype(BF16)
        w2b[...] = w2_ref[0].astype(BF16)

        def sub(sb, c):
            r0 = pl.multiple_of(sb * SUB, SUB)
            x = x_scr[pl.ds(r0, SUB), :].astype(BF16)
            h1 = _dot(x, w1b[...])
            h3 = _dot(x, w3b[...])
            hh = (h1 * _sigmoid(h1) * h3).astype(BF16)
            contrib = _dot(hh, w2b[...])

            @pl.when(j == 0)
            def _():
                acc_scr[pl.ds(r0, SUB), :] = contrib

            @pl.when(j > 0)
            def _():
                acc_scr[pl.ds(r0, SUB), :] += contrib
            return c
        lax.fori_loop(0, nsub, sub, 0)

    @pl.when((j == nf - 1) & (rows > 0))
    def _():
        def scale(sb, c):
            r0 = pl.multiple_of(sb * SUB, SUB)
            acc_scr[pl.ds(r0, SUB), :] = acc_scr[pl.ds(r0, SUB), :] * w_ref[0, pl.ds(r0, SUB), :]
            return c
        lax.fori_loop(0, nsub, scale, 0)

        def start(r, c):
            scatter_copy(r).start()
            return c
        lax.fori_loop(0, rows, start, 0)

        def wait(r, c):
            scatter_copy(r).wait()
            return c
        lax.fori_loop(0, rows, wait, 0)


def _moe(h2, item_e, item_rows, slot_tok, slot_dst, slot_w, w1, w3, w2, n_assign):
    t_, d = h2.shape
    ne, _, f = w1.shape
    R, SUB, tf = MOE_ITEM_ROWS, MOE_SUB_ROWS, MOE_F_CHUNK
    ni = item_e.shape[0]
    nf = f // tf
    grid_spec = pltpu.PrefetchScalarGridSpec(
        num_scalar_prefetch=2,
        grid=(ni, nf),
        in_specs=[
            pl.BlockSpec((1, 1, R), lambda i, j, ie, ir: (i, 0, 0), memory_space=pltpu.SMEM),
            pl.BlockSpec((1, 1, R), lambda i, j, ie, ir: (i, 0, 0), memory_space=pltpu.SMEM),
            pl.BlockSpec((1, R, 1), lambda i, j, ie, ir: (i, 0, 0)),
            pl.BlockSpec(memory_space=pl.ANY),
            pl.BlockSpec((1, d, tf), lambda i, j, ie, ir: (ie[i], 0, j)),
            pl.BlockSpec((1, d, tf), lambda i, j, ie, ir: (ie[i], 0, j)),
            pl.BlockSpec((1, tf, d), lambda i, j, ie, ir: (ie[i], j, 0)),
        ],
        out_specs=pl.BlockSpec(memory_space=pl.ANY),
        scratch_shapes=[
            pltpu.VMEM((R, d), F32), pltpu.VMEM((R, d), F32),
            pltpu.VMEM((d, tf), BF16), pltpu.VMEM((d, tf), BF16), pltpu.VMEM((tf, d), BF16),
            pltpu.SemaphoreType.DMA(()), pltpu.SemaphoreType.DMA(()),
        ],
    )
    return pl.pallas_call(
        functools.partial(_moe_kernel, R, SUB, nf),
        out_shape=jax.ShapeDtypeStruct((n_assign, d), F32),
        grid_spec=grid_spec,
        compiler_params=_cparams(("arbitrary", "arbitrary")),
        name="moe_experts",
    )(item_e, item_rows, slot_tok, slot_dst, slot_w, h2, w1, w3, w2)


def _moe_plan(expert, weight, t_):
    R = MOE_ITEM_ROWS
    na = expert.shape[0]
    ni = N_EXPERTS + na // R
    tok = jnp.arange(na, dtype=jnp.int32) // TOP_K
    kk = jnp.arange(na, dtype=jnp.int32) % TOP_K
    onehot = (expert[:, None] == jnp.arange(N_EXPERTS, dtype=jnp.int32)[None, :]).astype(jnp.int32)
    csum = jnp.cumsum(onehot, axis=0)
    rank = jnp.take_along_axis(csum, expert[:, None], axis=1)[:, 0] - 1
    counts = csum[-1]
    n_items = (counts + R - 1) // R
    item_end = jnp.cumsum(n_items)
    item_start = item_end - n_items
    slot = item_start[expert] * R + rank
    slot_tok = jnp.zeros((ni * R,), jnp.int32).at[slot].set(tok)
    slot_dst = jnp.zeros((ni * R,), jnp.int32).at[slot].set(kk * t_ + tok)
    slot_w = jnp.zeros((ni * R,), F32).at[slot].set(weight)
    ii = jnp.arange(ni, dtype=jnp.int32)
    e_of = jnp.minimum(jnp.searchsorted(item_end, ii, side="right"), N_EXPERTS - 1).astype(jnp.int32)
    rows = jnp.clip(counts[e_of] - (ii - item_start[e_of]) * R, 0, R).astype(jnp.int32)
    rows = jnp.where(ii < item_end[-1], rows, 0)
    last_e = e_of[jnp.maximum(item_end[-1] - 1, 0)]
    e_of = jnp.where(ii < item_end[-1], e_of, last_e)
    return (e_of, rows, slot_tok.reshape(ni, 1, R), slot_dst.reshape(ni, 1, R), slot_w.reshape(ni, R, 1))


def _final_kernel(x1_ref, y0_ref, y1_ref, g2_ref, fg_ref, o_ref):
    x = x1_ref[0] + g2_ref[0] * (y0_ref[0, 0] + y1_ref[0, 0])
    ms = jnp.mean(x * x, axis=-1, keepdims=True)
    o_ref[0] = x * lax.rsqrt(ms + EPS) * fg_ref[...]


def _final(x1, ybuf, g2, final_g, tm):
    b_, s_, d = x1.shape
    y4 = ybuf.reshape(TOP_K, b_, s_, d)
    return pl.pallas_call(
        _final_kernel,
        out_shape=jax.ShapeDtypeStruct((b_, s_, d), F32),
        grid=(b_, s_ // tm),
        in_specs=[
            pl.BlockSpec((1, tm, d), lambda b, i: (b, i, 0)),
            pl.BlockSpec((1, 1, tm, d), lambda b, i: (0, b, i, 0)),
            pl.BlockSpec((1, 1, tm, d), lambda b, i: (1, b, i, 0)),
            pl.BlockSpec((1, 1, d), lambda b, i: (b, 0, 0)),
            pl.BlockSpec((1, d), lambda b, i: (0, 0)),
        ],
        out_specs=pl.BlockSpec((1, tm, d), lambda b, i: (b, i, 0)),
        compiler_params=_cparams(("parallel", "parallel")),
        name="final_norm",
    )(x1, y4, y4, g2, final_g)


def kernel(x, c, ctx, c_ctx, w_mod, b_mod, norm1_g, w_in, b_gates, mlstm_head_g, conv_w, conv_b, lru_wa, lru_ba,
           lru_wx, lru_bx, lru_lam, w_pa, w_pb, w_out, norm2_g, w_rg, w_re, w1, w3, w2, final_g):
    b_, s_, d = x.shape
    n_ctx = ctx.shape[1]
    rows = s_ // GRID_W
    assert w_mod.shape[0] == 1, "single layer"
    wm = d // 2
    nh = MLSTM_HEADS

    cc = jnp.zeros((8, d), F32).at[:b_].set(c).at[b_].set(c_ctx)
    mod = _modulation(cc, w_mod[0], b_mod[0][None, :])
    sh1, sc1, g1, sh2, sc2, g2 = [mod[:b_, i * d:(i + 1) * d][:, None, :] for i in range(6)]
    csh1, csc1 = mod[b_:b_ + 1, 0:d][:, None, :], mod[b_:b_ + 1, d:2 * d][:, None, :]

    w = w_in[0]
    o_q, o_k, o_v, o_of, o_ob, o_g = 0, wm, 2 * wm, 3 * wm, 4 * wm, 5 * wm
    o_ux = o_g + 4 * nh
    o_ug, o_ga, o_gb = o_ux + wm, o_ux + 2 * wm, o_ux + 2 * wm + d
    seg = lambda a, n: w[:, a:a + n]
    w_main = jnp.concatenate([seg(o_q, wm), seg(o_k, wm), seg(o_v, wm), seg(o_ux, wm), seg(o_of, wm), seg(o_ob, wm),
                              seg(o_ug, wm), seg(o_ga, d), seg(o_gb, d)], axis=1).astype(BF16)
    w_gates = jnp.zeros((d, LANES), F32).at[:, :4 * nh].set(seg(o_g, 4 * nh)).astype(BF16)
    bg_pad = jnp.zeros((1, LANES), F32).at[0, :4 * nh].set(b_gates[0])

    ng = norm1_g[0][None, None, :]
    u_lat, g_lat = _inproj(x, ng * (1.0 + sc1), sh1, w_main, w_gates, bg_pad, N_COL_TILES, min(1024, s_))
    u_ctx, g_ctx = _inproj(ctx, ng * (1.0 + csc1), csh1, w_main, w_gates, bg_pad, N_CTX_TILES, n_ctx)

    ya_f, ya_b = _mlstm(u_lat, u_ctx, g_lat, g_ctx, mlstm_head_g[0], MLSTM_CHUNK)

    ux_lat = u_lat[:, :, COL_UX * wm:(COL_UX + 1) * wm]
    ux_col = ux_lat.reshape(b_, rows, GRID_W, wm).transpose(0, 2, 1, 3).reshape(b_, s_, wm)
    ux_seq = jnp.concatenate([u_ctx[:, :, COL_UX * wm:(COL_UX + 1) * wm], ux_col], axis=1)
    wa, wx = lru_wa[0], lru_wx[0]
    w_cat = jnp.concatenate([wa[0], wx[0], wa[1], wx[1]], axis=-1).astype(BF16)
    cblk = wm // LRU_BLOCKS
    blk = lambda v: v.reshape(LRU_BLOCKS, cblk)
    b_cat = jnp.concatenate([blk(lru_ba[0, 0]), blk(lru_bx[0, 0]), blk(lru_ba[0, 1]), blk(lru_bx[0, 1])],
                            axis=-1).reshape(1, 4 * wm)
    h_col = _rglru(ux_seq, conv_w[0], conv_b[0][None, :], w_cat, b_cat, lru_lam[0], n_ctx)
    hr = h_col.reshape(b_, GRID_W, rows, wm).transpose(0, 2, 1, 3).reshape(b_, s_, wm)

    w_r = jnp.zeros((d, LANES), F32).at[:, :N_GROUPS].set(w_rg[0]).at[:, N_GROUPS:N_GROUPS + N_EXPERTS].set(w_re[0])
    n2 = norm2_g[0][None, None, :]
    x1, h2, logits = _outproj(ya_f, ya_b, hr, u_lat, x, g1, n2 * (1.0 + sc2), sh2,
                              w_pa[0].astype(BF16), w_pb[0].astype(BF16), w_out[0].astype(BF16), w_r, 256)

    t_ = b_ * s_
    routed = _route(logits.reshape(t_, LANES), 512)
    expert = routed[:, :TOP_K].astype(jnp.int32).reshape(-1)
    weight = routed[:, TOP_K:2 * TOP_K].reshape(-1)
    item_e, item_rows, slot_tok, slot_dst, slot_w = _moe_plan(expert, weight, t_)
    ybuf = _moe(h2.reshape(t_, d), item_e, item_rows, slot_tok, slot_dst, slot_w, w1[0], w3[0], w2[0],
                TOP_K * t_)

    return _final(x1, ybuf, g2, final_g[None, :], 256)
```

```python
import functools

import jax
import jax.numpy as jnp
from jax import lax
from jax.experimental import pallas as pl
from jax.experimental.pallas import tpu as pltpu

F32 = jnp.float32
BF16 = jnp.bfloat16

EPS = 1e-6
GRID_W = 64
MLSTM_HEADS = 8
LRU_BLOCKS = 8
LRU_C = 8.0
N_GROUPS = 4
EXPERTS_PER_GROUP = 8
N_EXPERTS = N_GROUPS * EXPERTS_PER_GROUP
TOP_K = 2

VMEM_LIMIT_BYTES = 56 * 1024 * 1024
LANES = 128

MLSTM_CHUNK = 128
MOE_ITEM_ROWS = 1024
MOE_SUB_ROWS = 256
MOE_F_CHUNK = 256


def _cparams(sem):
    return pltpu.CompilerParams(dimension_semantics=sem, vmem_limit_bytes=VMEM_LIMIT_BYTES)


def _sigmoid(x):
    return 1.0 / (1.0 + jnp.exp(-x))


def _dot(a, b):
    return jnp.dot(a, b, preferred_element_type=F32)


def _split3(x):
    x1 = x.astype(BF16)
    r1 = x - x1.astype(F32)
    x2 = r1.astype(BF16)
    x3 = (r1 - x2.astype(F32)).astype(BF16)
    return x1, x2, x3


def _mod_kernel(c_ref, w_ref, b_ref, o_ref):
    c = c_ref[...]
    s = (c * _sigmoid(c)).astype(BF16)
    o_ref[...] = _dot(s, w_ref[...].astype(BF16)) + b_ref[...]


def _modulation(cc, w_mod, b_mod):
    m, d = cc.shape
    n = w_mod.shape[1]
    tn = 1024
    return pl.pallas_call(
        _mod_kernel,
        out_shape=jax.ShapeDtypeStruct((m, n), F32),
        grid=(n // tn,),
        in_specs=[
            pl.BlockSpec((m, d), lambda j: (0, 0)),
            pl.BlockSpec((d, tn), lambda j: (0, j)),
            pl.BlockSpec((1, tn), lambda j: (0, j)),
        ],
        out_specs=pl.BlockSpec((m, tn), lambda j: (0, j)),
        compiler_params=_cparams(("parallel",)),
        name="modulation",
    )(cc, w_mod, b_mod)


def _inproj_kernel(x_ref, gain_ref, shift_ref, w_ref, wg_ref, bg_ref, u_ref, g_ref, h_scr):
    @pl.when(pl.program_id(2) == 0)
    def _():
        x = x_ref[0]
        ms = jnp.mean(x * x, axis=-1, keepdims=True)
        h = x * lax.rsqrt(ms + EPS) * gain_ref[0] + shift_ref[0]
        hb = h.astype(BF16)
        h_scr[...] = hb
        g_ref[0] = _dot(hb, wg_ref[...]) + bg_ref[...]

    u_ref[0] = _dot(h_scr[...], w_ref[...]).astype(BF16)


def _inproj(x, gain, shift, w_main, w_gates, b_gates, nj, tm):
    b_, s_, d = x.shape
    tn = 1024
    per_batch_mod = gain.shape[0] > 1
    wmap = lambda b, i, j: (0, j)
    mmap = (lambda b, i, j: (b, 0, 0)) if per_batch_mod else (lambda b, i, j: (0, 0, 0))
    return pl.pallas_call(
        _inproj_kernel,
        out_shape=(jax.ShapeDtypeStruct((b_, s_, nj * tn), BF16),
                   jax.ShapeDtypeStruct((b_, s_, LANES), F32)),
        grid=(b_, s_ // tm, nj),
        in_specs=[
            pl.BlockSpec((1, tm, d), lambda b, i, j: (b, i, 0)),
            pl.BlockSpec((1, 1, d), mmap),
            pl.BlockSpec((1, 1, d), mmap),
            pl.BlockSpec((d, tn), wmap),
            pl.BlockSpec((d, LANES), lambda b, i, j: (0, 0)),
            pl.BlockSpec((1, LANES), lambda b, i, j: (0, 0)),
        ],
        out_specs=(pl.BlockSpec((1, tm, tn), lambda b, i, j: (b, i, j)),
                   pl.BlockSpec((1, tm, LANES), lambda b, i, j: (b, i, 0))),
        scratch_shapes=[pltpu.VMEM((tm, d), BF16)],
        compiler_params=_cparams(("parallel", "parallel", "arbitrary")),
        name="inproj",
    )(x, gain, shift, w_main, w_gates, b_gates)


COL_Q, COL_K, COL_V, COL_UX, COL_OF, COL_OB, COL_UG, COL_GA, COL_GB = 0, 1, 2, 3, 4, 5, 6, 7, 9
N_COL_TILES = 11
N_CTX_TILES = 4


def _mlstm_kernel(nc, nl, L, dh,
                  qc_f, kc_f, vc_f, ql_f, kl_f, vl_f, gc_f, gl_f, o_f,
                  qc_b, kc_b, vc_b, ql_b, kl_b, vl_b, gc_b, gl_b, o_b,
                  hg_ref, tri_ref,
                  out_f, out_b, c_scr, m_scr):
    s = pl.program_id(1)
    nh = MLSTM_HEADS
    scale = dh ** -0.5

    @pl.when(s == 0)
    def _():
        c_scr[...] = jnp.zeros_like(c_scr)
        m_scr[...] = jnp.full_like(m_scr, -1e30)

    is_ctx = s < nc
    ones_col = (lax.broadcasted_iota(jnp.int32, (L, dh), 1) == 0).astype(BF16)
    row = lax.broadcasted_iota(jnp.int32, (L, L), 0)
    col = lax.broadcasted_iota(jnp.int32, (L, L), 1)

    for d, (qc, kc, vc, ql, kl, vl, gc, gl, o_ref, out_ref) in enumerate((
            (qc_f, kc_f, vc_f, ql_f, kl_f, vl_f, gc_f, gl_f, o_f, out_f),
            (qc_b, kc_b, vc_b, ql_b, kl_b, vl_b, gc_b, gl_b, o_b, out_b))):
        q = jnp.where(is_ctx, qc[0], ql[0])
        k = jnp.where(is_ctx, kc[0], kl[0])
        v = jnp.where(is_ctx, vc[0], vl[0])
        g = jnp.where(is_ctx, gc[0], gl[0])
        gt = g.T
        ls = jnp.minimum(g, 0.0) - jnp.log(1.0 + jnp.exp(-jnp.abs(g)))
        lst = jnp.minimum(gt, 0.0) - jnp.log(1.0 + jnp.exp(-jnp.abs(gt)))
        tri = tri_ref[d]
        mask = (col <= row) if d == 0 else (col >= row)
        l1, l2, l3 = _split3(ls)
        bcol_all = _dot(tri, l1) + _dot(tri, l2) + _dot(tri, l3)
        t1, t2, t3 = _split3(lst)
        trit = tri_ref[1 - d]
        brow_all = _dot(t1, trit) + _dot(t2, trit) + _dot(t3, trit)
        last = L - 1 if d == 0 else 0
        for h in range(nh):
            u = d * nh + h
            ci = 16 * d + h
            cf = 16 * d + 8 + h
            hs = slice(h * dh, (h + 1) * dh)
            qh, kh, vh = q[:, hs], k[:, hs], v[:, hs]
            vext = jnp.concatenate([vh, ones_col], axis=1)
            bc = bcol_all[:, cf:cf + 1]
            ic = g[:, ci:ci + 1]
            cr = gt[ci:ci + 1, :] - brow_all[cf:cf + 1, :]
            tot = bcol_all[last:last + 1, cf:cf + 1]
            m_prev = m_scr[u][0:1, 0:1]
            c_prev = c_scr[u]

            sqk = lax.dot_general(qh, kh, (((1,), (1,)), ((), ())), preferred_element_type=F32) * scale
            dm = jnp.where(mask, bc + cr, -jnp.inf)
            rowmax = jnp.max(dm, axis=-1, keepdims=True)
            gg = bc + m_prev
            mt = jnp.maximum(gg, rowmax)
            w_intra = jnp.exp(dm - mt)
            w_inter = jnp.exp(gg - mt)
            a = (sqk * w_intra).astype(BF16)
            r = _dot(a, vext) + w_inter * _dot(qh, c_prev.astype(BF16))

            num = r[:, :dh]
            den = r[:, dh:dh + 1]
            hh = num / jnp.maximum(jnp.abs(den), jnp.exp(-mt))
            hn = hh * lax.rsqrt(jnp.mean(hh * hh, axis=-1, keepdims=True) + EPS) * hg_ref[h:h + 1, :]
            gate = _sigmoid(o_ref[0, :, hs].astype(F32))
            out_ref[0, :, hs] = (gate * hn).astype(BF16)

            wlog = tot - bc + ic
            m_new = jnp.maximum(tot + m_prev, jnp.max(wlog, axis=0, keepdims=True))
            ws = jnp.exp(wlog - m_new) * scale
            decay = jnp.exp(tot + m_prev - m_new)
            kw = (kh.astype(F32) * ws).astype(BF16)
            upd = lax.dot_general(kw, vext, (((0,), (0,)), ((), ())), preferred_element_type=F32)
            c_scr[u] = decay * c_prev + upd
            m_scr[u] = jnp.broadcast_to(m_new, m_scr.shape[1:])


def _mlstm(u_lat, u_ctx, g_lat, g_ctx, head_g, L):
    b_, s_, _ = u_lat.shape
    n_ctx = u_ctx.shape[1]
    nh = MLSTM_HEADS
    w = 1024
    dh = w // nh
    nc, nl = n_ctx // L, s_ // L
    steps = nc + nl
    t0 = jnp.tril(jnp.ones((L, L), F32))
    tri = jnp.stack([t0, t0.T]).astype(BF16)

    def cidx(d):
        if d == 0:
            return lambda st: jnp.minimum(st, nc - 1), lambda st: jnp.maximum(st - nc, 0)
        return lambda st: jnp.maximum(nc - 1 - st, 0), lambda st: jnp.minimum(nl - 1 + nc - st, nl - 1)

    in_specs, args = [], []
    for d in range(2):
        fc, fl = cidx(d)
        for t in range(3):
            in_specs.append(pl.BlockSpec((1, L, w), lambda b, st, fc=fc, t=t: (b, fc(st), t)))
            args.append(u_ctx)
        for t in (COL_Q, COL_K, COL_V):
            in_specs.append(pl.BlockSpec((1, L, w), lambda b, st, fl=fl, t=t: (b, fl(st), t)))
            args.append(u_lat)
        in_specs.append(pl.BlockSpec((1, L, LANES), lambda b, st, fc=fc: (b, fc(st), 0)))
        args.append(g_ctx)
        in_specs.append(pl.BlockSpec((1, L, LANES), lambda b, st, fl=fl: (b, fl(st), 0)))
        args.append(g_lat)
        in_specs.append(pl.BlockSpec((1, L, w), lambda b, st, fl=fl, t=COL_OF + d: (b, fl(st), t)))
        args.append(u_lat)
    in_specs += [pl.BlockSpec((nh, dh), lambda b, st: (0, 0)),
                 pl.BlockSpec((2, L, L), lambda b, st: (0, 0, 0))]
    args += [head_g, tri]
    out_specs = tuple(pl.BlockSpec((1, L, w), lambda b, st, fl=cidx(d)[1]: (b, fl(st), 0)) for d in range(2))
    return pl.pallas_call(
        functools.partial(_mlstm_kernel, nc, nl, L, dh),
        out_shape=(jax.ShapeDtypeStruct((b_, s_, w), BF16),) * 2,
        grid=(b_, steps),
        in_specs=in_specs,
        out_specs=out_specs,
        scratch_shapes=[pltpu.VMEM((2 * nh, dh, 2 * dh), F32), pltpu.VMEM((2 * nh, 8, LANES), F32)],
        compiler_params=_cparams(("parallel", "arbitrary")),
        name="mlstm",
    )(*args)


def _rglru_kernel(n_ctx, n_tot, ux_ref, cw_ref, cb_ref, w_ref, b_ref, lam_ref, out_ref,
                  a_f, b_f, a_b, b_b, h_f, h_b):
    u = ux_ref[0].astype(F32)
    t_idx = lax.broadcasted_iota(jnp.int32, (n_tot, 1), 0)
    seg_lo = jnp.where(t_idx < n_ctx, 0, n_ctx)
    seg_hi = jnp.where(t_idx < n_ctx, n_ctx, n_tot)
    z = jnp.zeros_like(u) + cb_ref[...]
    for j, off in enumerate((-2, -1, 0, 1)):
        if off == 0:
            src = u
        else:
            src = pltpu.roll(u, (-off) % n_tot, 0)
            ok = (t_idx + off >= seg_lo) & (t_idx + off < seg_hi)
            src = jnp.where(ok, src, 0.0)
        z = z + cw_ref[j:j + 1, :] * src
    p = _dot(z.astype(BF16), w_ref[0]) + b_ref[...]
    c = LANES
    for d, (a_scr, b_scr) in enumerate(((a_f, b_f), (a_b, b_b))):
        r = _sigmoid(p[:, (2 * d) * c:(2 * d + 1) * c])
        i = _sigmoid(p[:, (2 * d + 1) * c:(2 * d + 2) * c])
        lam = lam_ref[d:d + 1, :]
        softplus = jnp.maximum(-lam, 0.0) + jnp.log(1.0 + jnp.exp(-jnp.abs(lam)))
        log_a = -LRU_C * r * softplus
        a = jnp.exp(log_a)
        a_scr[...] = a
        b_scr[...] = jnp.sqrt(1.0 - a * a) * i * z

    sub = lax.broadcasted_iota(jnp.int32, (8, c), 0)
    ng = n_tot // 8
    ngc = n_ctx // 8

    def group_scan(a, b, carry, reverse):
        for sft in (1, 2, 4):
            if reverse:
                a_s = pltpu.roll(a, 8 - sft, 0)
                b_s = pltpu.roll(b, 8 - sft, 0)
                ok = sub < 8 - sft
            else:
                a_s = pltpu.roll(a, sft, 0)
                b_s = pltpu.roll(b, sft, 0)
                ok = sub >= sft
            b = jnp.where(ok, a * b_s + b, b)
            a = jnp.where(ok, a * a_s, a)
        h = b + a * carry
        return h, (h[0:1, :] if reverse else h[7:8, :])

    def body(it, carry):
        cf, cb = carry
        rf = pl.multiple_of(it * 8, 8)
        hf, cf = group_scan(a_f[pl.ds(rf, 8), :], b_f[pl.ds(rf, 8), :], cf, False)
        h_f[pl.ds(rf, 8), :] = hf
        jb = jnp.where(it < ngc, ngc - 1 - it, ng - 1 + ngc - it)
        rb = pl.multiple_of(jb * 8, 8)
        hb, cb = group_scan(a_b[pl.ds(rb, 8), :], b_b[pl.ds(rb, 8), :], cb, True)
        h_b[pl.ds(rb, 8), :] = hb
        return cf, cb

    zero = jnp.zeros((1, c), F32)
    lax.fori_loop(0, ng, body, (zero, zero), unroll=4)
    out_ref[0] = (h_f[n_ctx:, :] + h_b[n_ctx:, :]).astype(BF16)


def _rglru(ux_seq, conv_w, conv_b, w_cat, b_cat, lam, n_ctx):
    b_, n_tot, w = ux_seq.shape
    nb = LRU_BLOCKS
    c = w // nb
    return pl.pallas_call(
        functools.partial(_rglru_kernel, n_ctx, n_tot),
        out_shape=jax.ShapeDtypeStruct((b_, n_tot - n_ctx, w), BF16),
        grid=(b_, nb),
        in_specs=[
            pl.BlockSpec((1, n_tot, c), lambda b, k: (b, 0, k)),
            pl.BlockSpec((4, c), lambda b, k: (0, k)),
            pl.BlockSpec((1, c), lambda b, k: (0, k)),
            pl.BlockSpec((1, c, 4 * c), lambda b, k: (k, 0, 0)),
            pl.BlockSpec((1, 4 * c), lambda b, k: (0, k)),
            pl.BlockSpec((2, c), lambda b, k: (0, k)),
        ],
        out_specs=pl.BlockSpec((1, n_tot - n_ctx, c), lambda b, k: (b, 0, k)),
        scratch_shapes=[pltpu.VMEM((n_tot, c), F32)] * 6,
        compiler_params=_cparams(("parallel", "parallel")),
        name="rglru",
    )(ux_seq, conv_w, conv_b, w_cat, b_cat, lam)


def _outproj_kernel(yaf_ref, yab_ref, hr_ref, ug_ref, ga0_ref, ga1_ref, gb0_ref, gb1_ref, x_ref,
                    g1_ref, gain_ref, shift_ref, wpa_ref, wpb_ref, wout_ref, wr_ref,
                    x1_ref, h2_ref, lg_ref):
    ya = (yaf_ref[0].astype(F32) + yab_ref[0].astype(F32)).astype(BF16)
    ug = ug_ref[0].astype(F32)
    gelu = 0.5 * ug * (1.0 + jnp.tanh(0.7978845608028654 * (ug + 0.044715 * ug * ug * ug)))
    yb = (hr_ref[0].astype(F32) * gelu).astype(BF16)
    pa = _dot(ya, wpa_ref[...])
    pb = _dot(yb, wpb_ref[...])
    ga = jnp.concatenate([ga0_ref[0], ga1_ref[0]], axis=1).astype(F32)
    gb = jnp.concatenate([gb0_ref[0], gb1_ref[0]], axis=1).astype(F32)
    mix = (_sigmoid(ga) * pa + _sigmoid(gb) * pb).astype(BF16)
    x1 = x_ref[0] + g1_ref[0] * _dot(mix, wout_ref[...])
    x1_ref[0] = x1
    ms = jnp.mean(x1 * x1, axis=-1, keepdims=True)
    h2 = x1 * lax.rsqrt(ms + EPS) * gain_ref[0] + shift_ref[0]
    h2_ref[0] = h2
    wr = wr_ref[...]
    h1, h2b, _ = _split3(h2)
    w1, w2b, _ = _split3(wr)
    lg_ref[0] = _dot(h1, w1) + (_dot(h1, w2b) + _dot(h2b, w1))


def _outproj(ya_f, ya_b, hr, u_lat, x, g1, gain2, shift2, w_pa, w_pb, w_out, w_r, tm):
    b_, s_, d = x.shape
    w = ya_f.shape[2]
    row = lambda b, i: (b, i, 0)
    const2 = lambda b, i: (0, 0)
    bmap = lambda b, i: (b, 0, 0)
    ucol = lambda t: (lambda b, i: (b, i, t))
    single = pl.Buffered(1)
    return pl.pallas_call(
        _outproj_kernel,
        out_shape=(jax.ShapeDtypeStruct((b_, s_, d), F32),
                   jax.ShapeDtypeStruct((b_, s_, d), F32),
                   jax.ShapeDtypeStruct((b_, s_, LANES), F32)),
        grid=(b_, s_ // tm),
        in_specs=[
            pl.BlockSpec((1, tm, w), row), pl.BlockSpec((1, tm, w), row), pl.BlockSpec((1, tm, w), row),
            pl.BlockSpec((1, tm, w), ucol(COL_UG)),
            pl.BlockSpec((1, tm, w), ucol(COL_GA)), pl.BlockSpec((1, tm, w), ucol(COL_GA + 1)),
            pl.BlockSpec((1, tm, w), ucol(COL_GB)), pl.BlockSpec((1, tm, w), ucol(COL_GB + 1)),
            pl.BlockSpec((1, tm, d), row),
            pl.BlockSpec((1, 1, d), bmap), pl.BlockSpec((1, 1, d), bmap), pl.BlockSpec((1, 1, d), bmap),
            pl.BlockSpec((w, d), const2, pipeline_mode=single),
            pl.BlockSpec((w, d), const2, pipeline_mode=single),
            pl.BlockSpec((d, d), const2, pipeline_mode=single),
            pl.BlockSpec((d, LANES), const2, pipeline_mode=single),
        ],
        out_specs=(pl.BlockSpec((1, tm, d), row), pl.BlockSpec((1, tm, d), row),
                   pl.BlockSpec((1, tm, LANES), row)),
        compiler_params=_cparams(("parallel", "parallel")),
        name="outproj",
    )(ya_f, ya_b, hr, u_lat, u_lat, u_lat, u_lat, u_lat, x, g1, gain2, shift2, w_pa, w_pb, w_out, w_r)


def _route_kernel(lg_ref, o_ref):
    lg = lg_ref[...]
    lane = lax.broadcasted_iota(jnp.int32, lg.shape, 1)
    neg = -jnp.inf
    big = jnp.int32(1 << 20)
    g_l = jnp.where(lane < N_GROUPS, lg, neg)
    g_max = jnp.max(g_l, axis=-1, keepdims=True)
    g_sel = jnp.min(jnp.where(g_l == g_max, lane, big), axis=-1, keepdims=True)
    p_g = 1.0 / jnp.sum(jnp.exp(g_l - g_max), axis=-1, keepdims=True)
    lo = N_GROUPS + g_sel * EXPERTS_PER_GROUP
    e_l = jnp.where((lane >= lo) & (lane < lo + EXPERTS_PER_GROUP), lg, neg)
    v1 = jnp.max(e_l, axis=-1, keepdims=True)
    i1 = jnp.min(jnp.where(e_l == v1, lane, big), axis=-1, keepdims=True)
    e_l2 = jnp.where(lane == i1, neg, e_l)
    v2 = jnp.max(e_l2, axis=-1, keepdims=True)
    i2 = jnp.min(jnp.where(e_l2 == v2, lane, big), axis=-1, keepdims=True)
    e2 = jnp.exp(v2 - v1)
    w1 = p_g / (1.0 + e2)
    w2 = p_g * e2 / (1.0 + e2)
    out = jnp.where(lane == 0, (i1 - N_GROUPS).astype(F32),
          jnp.where(lane == 1, (i2 - N_GROUPS).astype(F32),
          jnp.where(lane == 2, w1, jnp.where(lane == 3, w2, 0.0))))
    o_ref[...] = out


def _route(logits, tm):
    t_, _ = logits.shape
    return pl.pallas_call(
        _route_kernel,
        out_shape=jax.ShapeDtypeStruct((t_, LANES), F32),
        grid=(t_ // tm,),
        in_specs=[pl.BlockSpec((tm, LANES), lambda i: (i, 0))],
        out_specs=pl.BlockSpec((tm, LANES), lambda i: (i, 0)),
        compiler_params=_cparams(("parallel",)),
        name="route",
    )(logits)


MOE_DMA_UNROLL = 32


def _moe_kernel(R, SUB, nf, ni, item_e, item_rows, tok_ref, tokn_ref, dst_ref, h2_hbm, w1_ref, w3_ref, w2_ref,
                y_hbm, x_scr, acc_scr, w1b, w3b, w2b, gsem, ssem):
    i = pl.program_id(0)
    j = pl.program_id(1)
    nsub_of = lambda r: (r + SUB - 1) // SUB
    rows = item_rows[i]
    nsub = nsub_of(rows)
    nsub_next = jnp.where(i + 1 < ni, nsub_of(item_rows[jnp.minimum(i + 1, ni - 1)]), 0)
    rows_prev = jnp.where(i > 0, item_rows[jnp.maximum(i - 1, 0)], 0)
    buf = i % 2
    groups = SUB // MOE_DMA_UNROLL

    def gather_block(tref, b, q):
        def grp(gi, c):
            base = q * SUB + gi * MOE_DMA_UNROLL
            for k in range(MOE_DMA_UNROLL):
                tok = tref[0, 0, base + k]
                pltpu.make_async_copy(h2_hbm.at[pl.ds(tok, 1), :], x_scr.at[b, pl.ds(base + k, 1), :],
                                      gsem.at[b]).start()
            return c
        lax.fori_loop(0, groups, grp, 0)

    def gather_wait_block(b):
        pltpu.make_async_copy(h2_hbm.at[pl.ds(0, SUB), :], x_scr.at[b, pl.ds(0, SUB), :], gsem.at[b]).wait()

    def scatter_copy(r):
        dst = dst_ref[0, 0, r]
        return pltpu.make_async_copy(acc_scr.at[pl.ds(r, 1), :], y_hbm.at[pl.ds(dst, 1), :], ssem)

    def repeat(n, fn, lo=0):
        def body(q, c):
            fn(q)
            return c
        lax.fori_loop(lo, n, body, 0)

    def scatter_rows(n):
        def grp(gi):
            for k in range(MOE_DMA_UNROLL):
                scatter_copy(gi * MOE_DMA_UNROLL + k).start()
        full = n // MOE_DMA_UNROLL
        repeat(full, grp)
        repeat(n, lambda r: scatter_copy(r).start(), lo=full * MOE_DMA_UNROLL)

    def scatter_wait_rows(n):
        u = MOE_DMA_UNROLL
        full = n // u
        repeat(full, lambda q: pltpu.make_async_copy(acc_scr.at[pl.ds(0, u), :], y_hbm.at[pl.ds(0, u), :],
                                                      ssem).wait())
        repeat(n, lambda r: scatter_copy(0).wait(), lo=full * u)

    @pl.when((i == 0) & (j == 0))
    def _():
        repeat(nsub, lambda q: gather_block(tok_ref, 0, q))

    @pl.when(j < nsub_next)
    def _():
        gather_block(tokn_ref, 1 - buf, j)

    @pl.when(j == 0)
    def _():
        repeat(nsub, lambda q: gather_wait_block(buf))
        scatter_wait_rows(rows_prev)

    @pl.when(rows > 0)
    def _():
        w1b[...] = w1_ref[0].astype(BF16)
        w3b[...] = w3_ref[0].astype(BF16)
        w2b[...] = w2_ref[0].astype(BF16)

        def sub(sb, c):
            r0 = pl.multiple_of(sb * SUB, SUB)
            x = x_scr[buf, pl.ds(r0, SUB), :].astype(BF16)
            h1 = _dot(x, w1b[...])
            h3 = _dot(x, w3b[...])
            hh = (h1 * _sigmoid(h1) * h3).astype(BF16)
            contrib = _dot(hh, w2b[...])

            @pl.when(j == 0)
            def _():
                acc_scr[pl.ds(r0, SUB), :] = contrib

            @pl.when(j > 0)
            def _():
                acc_scr[pl.ds(r0, SUB), :] += contrib
            return c
        lax.fori_loop(0, nsub, sub, 0)

    @pl.when(j == nf - 1)
    def _():
        scatter_rows(rows)

        @pl.when(i == ni - 1)
        def _():
            scatter_wait_rows(rows)


def _moe(h2, item_e, item_rows, slot_tok, slot_dst, w1, w3, w2, n_out_rows):
    t_, d = h2.shape
    ne, _, f = w1.shape
    R, SUB, tf = MOE_ITEM_ROWS, MOE_SUB_ROWS, MOE_F_CHUNK
    ni = item_e.shape[0]
    nf = f // tf
    assert R == nf * SUB
    grid_spec = pltpu.PrefetchScalarGridSpec(
        num_scalar_prefetch=2,
        grid=(ni, nf),
        in_specs=[
            pl.BlockSpec((1, 1, R), lambda i, j, ie, ir: (i, 0, 0), memory_space=pltpu.SMEM),
            pl.BlockSpec((1, 1, R), lambda i, j, ie, ir: (jnp.minimum(i + 1, ni - 1), 0, 0),
                         memory_space=pltpu.SMEM),
            pl.BlockSpec((1, 1, R), lambda i, j, ie, ir: (i, 0, 0), memory_space=pltpu.SMEM),
            pl.BlockSpec(memory_space=pl.ANY),
            pl.BlockSpec((1, d, tf), lambda i, j, ie, ir: (ie[i], 0, j)),
            pl.BlockSpec((1, d, tf), lambda i, j, ie, ir: (ie[i], 0, j)),
            pl.BlockSpec((1, tf, d), lambda i, j, ie, ir: (ie[i], j, 0)),
        ],
        out_specs=pl.BlockSpec(memory_space=pl.ANY),
        scratch_shapes=[
            pltpu.VMEM((2, R, d), F32), pltpu.VMEM((R, d), F32),
            pltpu.VMEM((d, tf), BF16), pltpu.VMEM((d, tf), BF16), pltpu.VMEM((tf, d), BF16),
            pltpu.SemaphoreType.DMA((2,)), pltpu.SemaphoreType.DMA(()),
        ],
    )
    return pl.pallas_call(
        functools.partial(_moe_kernel, R, SUB, nf, ni),
        out_shape=jax.ShapeDtypeStruct((n_out_rows, d), F32),
        grid_spec=grid_spec,
        compiler_params=_cparams(("arbitrary", "arbitrary")),
        name="moe_experts",
    )(item_e, item_rows, slot_tok, slot_tok, slot_dst, h2, w1, w3, w2)


def _moe_plan(expert, t_):
    R = MOE_ITEM_ROWS
    na = expert.shape[0]
    ni = N_EXPERTS + na // R
    onehot = (expert[:, None] == jnp.arange(N_EXPERTS, dtype=jnp.int32)[None, :]).astype(jnp.int32)
    csum = jnp.cumsum(onehot, axis=0)
    rank = jnp.sum(csum * onehot, axis=1) - 1
    counts = csum[-1]
    n_items = (counts + R - 1) // R
    item_end = jnp.cumsum(n_items)
    item_start = item_end - n_items
    slot = jnp.sum(item_start[None, :] * onehot, axis=1) * R + rank
    slot_a = jnp.full((ni * R,), -1, jnp.int32).at[slot].set(jnp.arange(na, dtype=jnp.int32))
    valid = slot_a >= 0
    slot_tok = jnp.where(valid, slot_a // TOP_K, 0)
    slot_dst = jnp.where(valid, (slot_a % TOP_K) * t_ + slot_a // TOP_K, 0)
    ii = jnp.arange(ni, dtype=jnp.int32)
    e_of = jnp.minimum(jnp.sum((item_end[None, :] <= ii[:, None]).astype(jnp.int32), axis=1), N_EXPERTS - 1)
    oh_e = (e_of[:, None] == jnp.arange(N_EXPERTS, dtype=jnp.int32)[None, :]).astype(jnp.int32)
    cnt_e = jnp.sum(oh_e * counts[None, :], axis=1)
    start_e = jnp.sum(oh_e * item_start[None, :], axis=1)
    rows = jnp.clip(cnt_e - (ii - start_e) * R, 0, R).astype(jnp.int32)
    live = ii < item_end[-1]
    rows = jnp.where(live, rows, 0)
    last_e = jnp.max(jnp.where(live, e_of, 0))
    e_of = jnp.where(live, e_of, last_e).astype(jnp.int32)
    return e_of, rows, slot_tok.reshape(ni, 1, R), slot_dst.reshape(ni, 1, R)


def _final_kernel(x1_ref, y0_ref, y1_ref, rt_ref, g2_ref, fg_ref, o_ref):
    rt = rt_ref[0]
    y = rt[:, TOP_K:TOP_K + 1] * y0_ref[...] + rt[:, TOP_K + 1:TOP_K + 2] * y1_ref[...]
    x = x1_ref[0] + g2_ref[0] * y
    ms = jnp.mean(x * x, axis=-1, keepdims=True)
    o_ref[0] = x * lax.rsqrt(ms + EPS) * fg_ref[...]


def _final(x1, ybuf, routed, g2, final_g, tm):
    b_, s_, d = x1.shape
    nt = s_ // tm
    return pl.pallas_call(
        _final_kernel,
        out_shape=jax.ShapeDtypeStruct((b_, s_, d), F32),
        grid=(b_, nt),
        in_specs=[
            pl.BlockSpec((1, tm, d), lambda b, i: (b, i, 0)),
            pl.BlockSpec((tm, d), lambda b, i: (b * nt + i, 0)),
            pl.BlockSpec((tm, d), lambda b, i: (b_ * nt + b * nt + i, 0)),
            pl.BlockSpec((1, tm, LANES), lambda b, i: (b, i, 0)),
            pl.BlockSpec((1, 1, d), lambda b, i: (b, 0, 0)),
            pl.BlockSpec((1, d), lambda b, i: (0, 0)),
        ],
        out_specs=pl.BlockSpec((1, tm, d), lambda b, i: (b, i, 0)),
        compiler_params=_cparams(("parallel", "parallel")),
        name="final_norm",
    )(x1, ybuf, ybuf, routed, g2, final_g)


def kernel(x, c, ctx, c_ctx, w_mod, b_mod, norm1_g, w_in, b_gates, mlstm_head_g, conv_w, conv_b, lru_wa, lru_ba,
           lru_wx, lru_bx, lru_lam, w_pa, w_pb, w_out, norm2_g, w_rg, w_re, w1, w3, w2, final_g):
    b_, s_, d = x.shape
    n_ctx = ctx.shape[1]
    rows = s_ // GRID_W
    assert w_mod.shape[0] == 1, "single layer"
    wm = d // 2
    nh = MLSTM_HEADS

    cc = jnp.zeros((8, d), F32).at[:b_].set(c).at[b_].set(c_ctx)
    mod = _modulation(cc, w_mod[0], b_mod[0][None, :])
    sh1, sc1, g1, sh2, sc2, g2 = [mod[:b_, i * d:(i + 1) * d][:, None, :] for i in range(6)]
    csh1, csc1 = mod[b_:b_ + 1, 0:d][:, None, :], mod[b_:b_ + 1, d:2 * d][:, None, :]

    w = w_in[0]
    o_q, o_k, o_v, o_of, o_ob, o_g = 0, wm, 2 * wm, 3 * wm, 4 * wm, 5 * wm
    o_ux = o_g + 4 * nh
    o_ug, o_ga, o_gb = o_ux + wm, o_ux + 2 * wm, o_ux + 2 * wm + d
    seg = lambda a, n: w[:, a:a + n]
    w_main = jnp.concatenate([seg(o_q, wm), seg(o_k, wm), seg(o_v, wm), seg(o_ux, wm), seg(o_of, wm), seg(o_ob, wm),
                              seg(o_ug, wm), seg(o_ga, d), seg(o_gb, d)], axis=1).astype(BF16)
    w_gates = jnp.zeros((d, LANES), F32).at[:, :4 * nh].set(seg(o_g, 4 * nh)).astype(BF16)
    bg_pad = jnp.zeros((1, LANES), F32).at[0, :4 * nh].set(b_gates[0])

    ng = norm1_g[0][None, None, :]
    u_lat, g_lat = _inproj(x, ng * (1.0 + sc1), sh1, w_main, w_gates, bg_pad, N_COL_TILES, min(1024, s_))
    u_ctx, g_ctx = _inproj(ctx, ng * (1.0 + csc1), csh1, w_main, w_gates, bg_pad, N_CTX_TILES, n_ctx)

    ya_f, ya_b = _mlstm(u_lat, u_ctx, g_lat, g_ctx, mlstm_head_g[0], MLSTM_CHUNK)

    ux_lat = u_lat[:, :, COL_UX * wm:(COL_UX + 1) * wm]
    ux_col = ux_lat.reshape(b_, rows, GRID_W, wm).transpose(0, 2, 1, 3).reshape(b_, s_, wm)
    ux_seq = jnp.concatenate([u_ctx[:, :, COL_UX * wm:(COL_UX + 1) * wm], ux_col], axis=1)
    wa, wx = lru_wa[0], lru_wx[0]
    w_cat = jnp.concatenate([wa[0], wx[0], wa[1], wx[1]], axis=-1).astype(BF16)
    cblk = wm // LRU_BLOCKS
    blk = lambda v: v.reshape(LRU_BLOCKS, cblk)
    b_cat = jnp.concatenate([blk(lru_ba[0, 0]), blk(lru_bx[0, 0]), blk(lru_ba[0, 1]), blk(lru_bx[0, 1])],
                            axis=-1).reshape(1, 4 * wm)
    h_col = _rglru(ux_seq, conv_w[0], conv_b[0][None, :], w_cat, b_cat, lru_lam[0], n_ctx)
    hr = h_col.reshape(b_, GRID_W, rows, wm).transpose(0, 2, 1, 3).reshape(b_, s_, wm)

    w_r = jnp.zeros((d, LANES), F32).at[:, :N_GROUPS].set(w_rg[0]).at[:, N_GROUPS:N_GROUPS + N_EXPERTS].set(w_re[0])
    n2 = norm2_g[0][None, None, :]
    x1, h2, logits = _outproj(ya_f, ya_b, hr, u_lat, x, g1, n2 * (1.0 + sc2), sh2,
                              w_pa[0].astype(BF16), w_pb[0].astype(BF16), w_out[0].astype(BF16), w_r, 256)

    t_ = b_ * s_
    routed = _route(logits.reshape(t_, LANES), 512)
    expert = routed[:, :TOP_K].astype(jnp.int32).reshape(-1)
    item_e, item_rows, slot_tok, slot_dst = _moe_plan(expert, t_)
    ybuf = _moe(h2.reshape(t_, d), item_e, item_rows, slot_tok, slot_dst, w1[0], w3[0], w2[0],
                TOP_K * t_)

    return _final(x1, ybuf, routed.reshape(b_, s_, LANES), g2, final_g[None, :], 256)
```

```python
import functools

import jax
import jax.numpy as jnp
from jax import lax
from jax.experimental import pallas as pl
from jax.experimental.pallas import tpu as pltpu

F32 = jnp.float32
BF16 = jnp.bfloat16

EPS = 1e-6
GRID_W = 64
MLSTM_HEADS = 8
LRU_BLOCKS = 8
LRU_C = 8.0
N_GROUPS = 4
EXPERTS_PER_GROUP = 8
N_EXPERTS = N_GROUPS * EXPERTS_PER_GROUP
TOP_K = 2

VMEM_LIMIT_BYTES = 56 * 1024 * 1024
LANES = 128

MLSTM_CHUNK = 128
MOE_ITEM_ROWS = 1024
MOE_SUB_ROWS = 256
MOE_F_CHUNK = 256


def _cparams(sem):
    return pltpu.CompilerParams(dimension_semantics=sem, vmem_limit_bytes=VMEM_LIMIT_BYTES)


def _sigmoid(x):
    return 1.0 / (1.0 + jnp.exp(-x))


def _dot(a, b):
    return jnp.dot(a, b, preferred_element_type=F32)


def _split3(x):
    x1 = x.astype(BF16)
    r1 = x - x1.astype(F32)
    x2 = r1.astype(BF16)
    x3 = (r1 - x2.astype(F32)).astype(BF16)
    return x1, x2, x3


def _mod_kernel(c_ref, w_ref, b_ref, o_ref):
    c = c_ref[...]
    s = (c * _sigmoid(c)).astype(BF16)
    o_ref[...] = _dot(s, w_ref[...].astype(BF16)) + b_ref[...]


def _modulation(cc, w_mod, b_mod):
    m, d = cc.shape
    n = w_mod.shape[1]
    tn = 1024
    return pl.pallas_call(
        _mod_kernel,
        out_shape=jax.ShapeDtypeStruct((m, n), F32),
        grid=(n // tn,),
        in_specs=[
            pl.BlockSpec((m, d), lambda j: (0, 0)),
            pl.BlockSpec((d, tn), lambda j: (0, j)),
            pl.BlockSpec((1, tn), lambda j: (0, j)),
        ],
        out_specs=pl.BlockSpec((m, tn), lambda j: (0, j)),
        compiler_params=_cparams(("parallel",)),
        name="modulation",
    )(cc, w_mod, b_mod)


N_PRE_TILES = 5


def _inproj_kernel(tile_of, x_ref, gain_ref, shift_ref, wa_ref, wb_ref, wg_ref, bg_ref, u_ref, g_ref, h_scr):
    j = pl.program_id(2)

    @pl.when(j == 0)
    def _():
        x = x_ref[0]
        ms = jnp.mean(x * x, axis=-1, keepdims=True)
        h = x * lax.rsqrt(ms + EPS) * gain_ref[0] + shift_ref[0]
        hb = h.astype(BF16)
        h_scr[...] = hb
        g_ref[0] = _dot(hb, wg_ref[...]) + bg_ref[...]

    @pl.when(tile_of(j) < N_PRE_TILES)
    def _():
        u_ref[0] = _dot(h_scr[...], wa_ref[0].astype(BF16)).astype(BF16)

    @pl.when(tile_of(j) >= N_PRE_TILES)
    def _():
        u_ref[0] = _dot(h_scr[...], wb_ref[...]).astype(BF16)


def _inproj(x, gain, shift, w_in, w_post, w_gates, b_gates, tiles, tm):
    b_, s_, d = x.shape
    tn = 1024
    nj = len(tiles)
    if tiles == tuple(range(nj)):
        tile_of = lambda j: j
    else:
        assert tiles == (COL_Q, COL_K, COL_V, COL_UX), tiles
        tile_of = lambda j: jnp.where(j == 3, COL_UX, j)
    per_batch_mod = gain.shape[0] > 1
    mmap = (lambda b, i, j: (b, 0, 0)) if per_batch_mod else (lambda b, i, j: (0, 0, 0))
    return pl.pallas_call(
        functools.partial(_inproj_kernel, tile_of),
        out_shape=(jax.ShapeDtypeStruct((b_, s_, nj * tn), BF16),
                   jax.ShapeDtypeStruct((b_, s_, LANES), F32)),
        grid=(b_, s_ // tm, nj),
        in_specs=[
            pl.BlockSpec((1, tm, d), lambda b, i, j: (b, i, 0), pipeline_mode=pl.Buffered(1)),
            pl.BlockSpec((1, 1, d), mmap),
            pl.BlockSpec((1, 1, d), mmap),
            pl.BlockSpec((1, d, tn), lambda b, i, j: (0, 0, jnp.minimum(tile_of(j), N_PRE_TILES - 1))),
            pl.BlockSpec((d, tn), lambda b, i, j: (0, jnp.maximum(tile_of(j) - N_PRE_TILES, 0))),
            pl.BlockSpec((d, LANES), lambda b, i, j: (0, 0)),
            pl.BlockSpec((1, LANES), lambda b, i, j: (0, 0)),
        ],
        out_specs=(pl.BlockSpec((1, tm, tn), lambda b, i, j: (b, i, j)),
                   pl.BlockSpec((1, tm, LANES), lambda b, i, j: (b, i, 0))),
        scratch_shapes=[pltpu.VMEM((tm, d), BF16)],
        compiler_params=_cparams(("parallel", "parallel", "arbitrary")),
        name="inproj",
    )(x, gain, shift, w_in, w_post, w_gates, b_gates)


COL_Q, COL_K, COL_V, COL_OF, COL_OB, COL_UX, COL_UG, COL_GA, COL_GB = 0, 1, 2, 3, 4, 5, 6, 7, 9
CTX_UX = 3
N_COL_TILES = 11


def _mlstm_kernel(nc, nl, L, dh,
                  qc_f, kc_f, vc_f, ql_f, kl_f, vl_f, gc_f, gl_f, o_f,
                  qc_b, kc_b, vc_b, ql_b, kl_b, vl_b, gc_b, gl_b, o_b,
                  hg_ref, tri_ref,
                  out_f, out_b, c_scr, m_scr):
    s = pl.program_id(1)
    nh = MLSTM_HEADS
    scale = dh ** -0.5

    @pl.when(s == 0)
    def _():
        c_scr[...] = jnp.zeros_like(c_scr)
        m_scr[...] = jnp.full_like(m_scr, -1e30)

    is_ctx = s < nc
    ones_col = (lax.broadcasted_iota(jnp.int32, (L, dh), 1) == 0).astype(BF16)
    row = lax.broadcasted_iota(jnp.int32, (L, L), 0)
    col = lax.broadcasted_iota(jnp.int32, (L, L), 1)

    for d, (qc, kc, vc, ql, kl, vl, gc, gl, o_ref, out_ref) in enumerate((
            (qc_f, kc_f, vc_f, ql_f, kl_f, vl_f, gc_f, gl_f, o_f, out_f),
            (qc_b, kc_b, vc_b, ql_b, kl_b, vl_b, gc_b, gl_b, o_b, out_b))):
        q = jnp.where(is_ctx, qc[0], ql[0])
        k = jnp.where(is_ctx, kc[0], kl[0])
        v = jnp.where(is_ctx, vc[0], vl[0])
        g = jnp.where(is_ctx, gc[0], gl[0])
        gt = g.T
        ls = jnp.minimum(g, 0.0) - jnp.log(1.0 + jnp.exp(-jnp.abs(g)))
        lst = jnp.minimum(gt, 0.0) - jnp.log(1.0 + jnp.exp(-jnp.abs(gt)))
        tri = tri_ref[d]
        mask = (col <= row) if d == 0 else (col >= row)
        l1, l2, l3 = _split3(ls)
        bcol_all = _dot(tri, l1) + _dot(tri, l2) + _dot(tri, l3)
        t1, t2, t3 = _split3(lst)
        trit = tri_ref[1 - d]
        brow_all = _dot(t1, trit) + _dot(t2, trit) + _dot(t3, trit)
        last = L - 1 if d == 0 else 0
        for h in range(nh):
            u = d * nh + h
            ci = 16 * d + h
            cf = 16 * d + 8 + h
            hs = slice(h * dh, (h + 1) * dh)
            qh, kh, vh = q[:, hs], k[:, hs], v[:, hs]
            vext = jnp.concatenate([vh, ones_col], axis=1)
            bc = bcol_all[:, cf:cf + 1]
            ic = g[:, ci:ci + 1]
            cr = gt[ci:ci + 1, :] - brow_all[cf:cf + 1, :]
            tot = bcol_all[last:last + 1, cf:cf + 1]
            m_prev = m_scr[u][0:1, 0:1]
            c_prev = c_scr[u]

            sqk = lax.dot_general(qh, kh, (((1,), (1,)), ((), ())), preferred_element_type=F32) * scale
            dm = jnp.where(mask, bc + cr, -jnp.inf)
            rowmax = jnp.max(dm, axis=-1, keepdims=True)
            gg = bc + m_prev
            mt = jnp.maximum(gg, rowmax)
            w_intra = jnp.exp(dm - mt)
            w_inter = jnp.exp(gg - mt)
            a = (sqk * w_intra).astype(BF16)
            r = _dot(a, vext) + w_inter * _dot(qh, c_prev.astype(BF16))

            num = r[:, :dh]
            den = r[:, dh:dh + 1]
            hh = num / jnp.maximum(jnp.abs(den), jnp.exp(-mt))
            hn = hh * lax.rsqrt(jnp.mean(hh * hh, axis=-1, keepdims=True) + EPS) * hg_ref[h:h + 1, :]
            gate = _sigmoid(o_ref[0, :, hs].astype(F32))
            out_ref[0, :, hs] = (gate * hn).astype(BF16)

            wlog = tot - bc + ic
            m_new = jnp.maximum(tot + m_prev, jnp.max(wlog, axis=0, keepdims=True))
            ws = jnp.exp(wlog - m_new) * scale
            decay = jnp.exp(tot + m_prev - m_new)
            kw = (kh.astype(F32) * ws).astype(BF16)
            upd = lax.dot_general(kw, vext, (((0,), (0,)), ((), ())), preferred_element_type=F32)
            c_scr[u] = decay * c_prev + upd
            m_scr[u] = jnp.broadcast_to(m_new, m_scr.shape[1:])


def _mlstm(u_lat, u_ctx, g_lat, g_ctx, head_g, L):
    b_, s_, _ = u_lat.shape
    n_ctx = u_ctx.shape[1]
    nh = MLSTM_HEADS
    w = 1024
    dh = w // nh
    nc, nl = n_ctx // L, s_ // L
    steps = nc + nl
    t0 = jnp.tril(jnp.ones((L, L), F32))
    tri = jnp.stack([t0, t0.T]).astype(BF16)

    def cidx(d):
        if d == 0:
            return lambda st: jnp.minimum(st, nc - 1), lambda st: jnp.maximum(st - nc, 0)
        return lambda st: jnp.maximum(nc - 1 - st, 0), lambda st: jnp.minimum(nl - 1 + nc - st, nl - 1)

    in_specs, args = [], []
    for d in range(2):
        fc, fl = cidx(d)
        for t in range(3):
            in_specs.append(pl.BlockSpec((1, L, w), lambda b, st, fc=fc, t=t: (b, fc(st), t)))
            args.append(u_ctx)
        for t in (COL_Q, COL_K, COL_V):
            in_specs.append(pl.BlockSpec((1, L, w), lambda b, st, fl=fl, t=t: (b, fl(st), t)))
            args.append(u_lat)
        in_specs.append(pl.BlockSpec((1, L, LANES), lambda b, st, fc=fc: (b, fc(st), 0)))
        args.append(g_ctx)
        in_specs.append(pl.BlockSpec((1, L, LANES), lambda b, st, fl=fl: (b, fl(st), 0)))
        args.append(g_lat)
        in_specs.append(pl.BlockSpec((1, L, w), lambda b, st, fl=fl, t=COL_OF + d: (b, fl(st), t)))
        args.append(u_lat)
    in_specs += [pl.BlockSpec((nh, dh), lambda b, st: (0, 0)),
                 pl.BlockSpec((2, L, L), lambda b, st: (0, 0, 0))]
    args += [head_g, tri]
    out_specs = tuple(pl.BlockSpec((1, L, w), lambda b, st, fl=cidx(d)[1]: (b, fl(st), 0)) for d in range(2))
    return pl.pallas_call(
        functools.partial(_mlstm_kernel, nc, nl, L, dh),
        out_shape=(jax.ShapeDtypeStruct((b_, s_, w), BF16),) * 2,
        grid=(b_, steps),
        in_specs=in_specs,
        out_specs=out_specs,
        scratch_shapes=[pltpu.VMEM((2 * nh, dh, 2 * dh), F32), pltpu.VMEM((2 * nh, 8, LANES), F32)],
        compiler_params=_cparams(("parallel", "arbitrary")),
        name="mlstm",
    )(*args)


def _rglru_kernel(n_ctx, n_tot, ux_ref, cw_ref, cb_ref, w_ref, b_ref, lam_ref, out_ref,
                  a_f, b_f, a_b, b_b, h_f, h_b):
    u = ux_ref[0].astype(F32)
    t_idx = lax.broadcasted_iota(jnp.int32, (n_tot, 1), 0)
    seg_lo = jnp.where(t_idx < n_ctx, 0, n_ctx)
    seg_hi = jnp.where(t_idx < n_ctx, n_ctx, n_tot)
    z = jnp.zeros_like(u) + cb_ref[...]
    for j, off in enumerate((-2, -1, 0, 1)):
        if off == 0:
            src = u
        else:
            src = pltpu.roll(u, (-off) % n_tot, 0)
            ok = (t_idx + off >= seg_lo) & (t_idx + off < seg_hi)
            src = jnp.where(ok, src, 0.0)
        z = z + cw_ref[j:j + 1, :] * src
    p = _dot(z.astype(BF16), w_ref[0]) + b_ref[...]
    c = LANES
    for d, (a_scr, b_scr) in enumerate(((a_f, b_f), (a_b, b_b))):
        r = _sigmoid(p[:, (2 * d) * c:(2 * d + 1) * c])
        i = _sigmoid(p[:, (2 * d + 1) * c:(2 * d + 2) * c])
        lam = lam_ref[d:d + 1, :]
        softplus = jnp.maximum(-lam, 0.0) + jnp.log(1.0 + jnp.exp(-jnp.abs(lam)))
        log_a = -LRU_C * r * softplus
        a = jnp.exp(log_a)
        a_scr[...] = a
        b_scr[...] = jnp.sqrt(1.0 - a * a) * i * z

    sub = lax.broadcasted_iota(jnp.int32, (8, c), 0)
    ng = n_tot // 8
    ngc = n_ctx // 8

    def group_scan(a, b, carry, reverse):
        for sft in (1, 2, 4):
            if reverse:
                a_s = pltpu.roll(a, 8 - sft, 0)
                b_s = pltpu.roll(b, 8 - sft, 0)
                ok = sub < 8 - sft
            else:
                a_s = pltpu.roll(a, sft, 0)
                b_s = pltpu.roll(b, sft, 0)
                ok = sub >= sft
            b = jnp.where(ok, a * b_s + b, b)
            a = jnp.where(ok, a * a_s, a)
        h = b + a * carry
        return h, (h[0:1, :] if reverse else h[7:8, :])

    def body(it, carry):
        cf, cb = carry
        rf = pl.multiple_of(it * 8, 8)
        hf, cf = group_scan(a_f[pl.ds(rf, 8), :], b_f[pl.ds(rf, 8), :], cf, False)
        h_f[pl.ds(rf, 8), :] = hf
        jb = jnp.where(it < ngc, ngc - 1 - it, ng - 1 + ngc - it)
        rb = pl.multiple_of(jb * 8, 8)
        hb, cb = group_scan(a_b[pl.ds(rb, 8), :], b_b[pl.ds(rb, 8), :], cb, True)
        h_b[pl.ds(rb, 8), :] = hb
        return cf, cb

    zero = jnp.zeros((1, c), F32)
    lax.fori_loop(0, ng, body, (zero, zero), unroll=4)
    out_ref[0] = (h_f[n_ctx:, :] + h_b[n_ctx:, :]).astype(BF16)


def _rglru(ux_seq, conv_w, conv_b, w_cat, b_cat, lam, n_ctx):
    b_, n_tot, w = ux_seq.shape
    nb = LRU_BLOCKS
    c = w // nb
    return pl.pallas_call(
        functools.partial(_rglru_kernel, n_ctx, n_tot),
        out_shape=jax.ShapeDtypeStruct((b_, n_tot - n_ctx, w), BF16),
        grid=(b_, nb),
        in_specs=[
            pl.BlockSpec((1, n_tot, c), lambda b, k: (b, 0, k)),
            pl.BlockSpec((4, c), lambda b, k: (0, k)),
            pl.BlockSpec((1, c), lambda b, k: (0, k)),
            pl.BlockSpec((1, c, 4 * c), lambda b, k: (k, 0, 0)),
            pl.BlockSpec((1, 4 * c), lambda b, k: (0, k)),
            pl.BlockSpec((2, c), lambda b, k: (0, k)),
        ],
        out_specs=pl.BlockSpec((1, n_tot - n_ctx, c), lambda b, k: (b, 0, k)),
        scratch_shapes=[pltpu.VMEM((n_tot, c), F32)] * 6,
        compiler_params=_cparams(("parallel", "parallel")),
        name="rglru",
    )(ux_seq, conv_w, conv_b, w_cat, b_cat, lam)


def _outproj_kernel(yaf_ref, yab_ref, hr_ref, ug_ref, ga0_ref, ga1_ref, gb0_ref, gb1_ref, x_ref,
                    g1_ref, gain_ref, shift_ref, wpa_ref, wpb_ref, wout_ref, wr_ref,
                    x1_ref, h2_ref, lg_ref):
    ya = (yaf_ref[0].astype(F32) + yab_ref[0].astype(F32)).astype(BF16)
    ug = ug_ref[0].astype(F32)
    gelu = 0.5 * ug * (1.0 + jnp.tanh(0.7978845608028654 * (ug + 0.044715 * ug * ug * ug)))
    yb = (hr_ref[0].astype(F32) * gelu).astype(BF16)
    pa = _dot(ya, wpa_ref[...])
    pb = _dot(yb, wpb_ref[...])
    ga = jnp.concatenate([ga0_ref[0], ga1_ref[0]], axis=1).astype(F32)
    gb = jnp.concatenate([gb0_ref[0], gb1_ref[0]], axis=1).astype(F32)
    mix = (_sigmoid(ga) * pa + _sigmoid(gb) * pb).astype(BF16)
    x1 = x_ref[0] + g1_ref[0] * _dot(mix, wout_ref[...])
    x1_ref[0] = x1
    ms = jnp.mean(x1 * x1, axis=-1, keepdims=True)
    h2 = x1 * lax.rsqrt(ms + EPS) * gain_ref[0] + shift_ref[0]
    h2_ref[0] = h2
    wr = wr_ref[...]
    h1, h2b, _ = _split3(h2)
    w1, w2b, _ = _split3(wr)
    lg_ref[0] = _dot(h1, w1) + (_dot(h1, w2b) + _dot(h2b, w1))


def _outproj(ya_f, ya_b, hr, u_lat, x, g1, gain2, shift2, w_pa, w_pb, w_out, w_r, tm):
    b_, s_, d = x.shape
    w = ya_f.shape[2]
    row = lambda b, i: (b, i, 0)
    const2 = lambda b, i: (0, 0)
    bmap = lambda b, i: (b, 0, 0)
    ucol = lambda t: (lambda b, i: (b, i, t))
    single = pl.Buffered(1)
    return pl.pallas_call(
        _outproj_kernel,
        out_shape=(jax.ShapeDtypeStruct((b_, s_, d), F32),
                   jax.ShapeDtypeStruct((b_, s_, d), F32),
                   jax.ShapeDtypeStruct((b_, s_, LANES), F32)),
        grid=(b_, s_ // tm),
        in_specs=[
            pl.BlockSpec((1, tm, w), row), pl.BlockSpec((1, tm, w), row), pl.BlockSpec((1, tm, w), row),
            pl.BlockSpec((1, tm, w), ucol(COL_UG)),
            pl.BlockSpec((1, tm, w), ucol(COL_GA)), pl.BlockSpec((1, tm, w), ucol(COL_GA + 1)),
            pl.BlockSpec((1, tm, w), ucol(COL_GB)), pl.BlockSpec((1, tm, w), ucol(COL_GB + 1)),
            pl.BlockSpec((1, tm, d), row),
            pl.BlockSpec((1, 1, d), bmap), pl.BlockSpec((1, 1, d), bmap), pl.BlockSpec((1, 1, d), bmap),
            pl.BlockSpec((w, d), const2, pipeline_mode=single),
            pl.BlockSpec((w, d), const2, pipeline_mode=single),
            pl.BlockSpec((d, d), const2, pipeline_mode=single),
            pl.BlockSpec((d, LANES), const2, pipeline_mode=single),
        ],
        out_specs=(pl.BlockSpec((1, tm, d), row), pl.BlockSpec((1, tm, d), row),
                   pl.BlockSpec((1, tm, LANES), row)),
        compiler_params=_cparams(("parallel", "parallel")),
        name="outproj",
    )(ya_f, ya_b, hr, u_lat, u_lat, u_lat, u_lat, u_lat, x, g1, gain2, shift2, w_pa, w_pb, w_out, w_r)


def _route_kernel(lg_ref, o_ref):
    lg = lg_ref[...]
    lane = lax.broadcasted_iota(jnp.int32, lg.shape, 1)
    neg = -jnp.inf
    big = jnp.int32(1 << 20)
    g_l = jnp.where(lane < N_GROUPS, lg, neg)
    g_max = jnp.max(g_l, axis=-1, keepdims=True)
    g_sel = jnp.min(jnp.where(g_l == g_max, lane, big), axis=-1, keepdims=True)
    p_g = 1.0 / jnp.sum(jnp.exp(g_l - g_max), axis=-1, keepdims=True)
    lo = N_GROUPS + g_sel * EXPERTS_PER_GROUP
    e_l = jnp.where((lane >= lo) & (lane < lo + EXPERTS_PER_GROUP), lg, neg)
    v1 = jnp.max(e_l, axis=-1, keepdims=True)
    i1 = jnp.min(jnp.where(e_l == v1, lane, big), axis=-1, keepdims=True)
    e_l2 = jnp.where(lane == i1, neg, e_l)
    v2 = jnp.max(e_l2, axis=-1, keepdims=True)
    i2 = jnp.min(jnp.where(e_l2 == v2, lane, big), axis=-1, keepdims=True)
    e2 = jnp.exp(v2 - v1)
    w1 = p_g / (1.0 + e2)
    w2 = p_g * e2 / (1.0 + e2)
    out = jnp.where(lane == 0, (i1 - N_GROUPS).astype(F32),
          jnp.where(lane == 1, (i2 - N_GROUPS).astype(F32),
          jnp.where(lane == 2, w1, jnp.where(lane == 3, w2, 0.0))))
    o_ref[...] = out


def _route(logits, tm):
    t_, _ = logits.shape
    return pl.pallas_call(
        _route_kernel,
        out_shape=jax.ShapeDtypeStruct((t_, LANES), F32),
        grid=(t_ // tm,),
        in_specs=[pl.BlockSpec((tm, LANES), lambda i: (i, 0))],
        out_specs=pl.BlockSpec((tm, LANES), lambda i: (i, 0)),
        compiler_params=_cparams(("parallel",)),
        name="route",
    )(logits)


MOE_DMA_UNROLL = 32


def _moe_kernel(R, SUB, nf, ni, item_e, item_rows, tok_ref, tokn_ref, dst_ref, h2_hbm, w1_ref, w3_ref, w2_ref,
                y_hbm, x_scr, acc_scr, w1b, w3b, w2b, gsem, ssem):
    i = pl.program_id(0)
    j = pl.program_id(1)
    nsub_of = lambda r: (r + SUB - 1) // SUB
    rows = item_rows[i]
    nsub = nsub_of(rows)
    nsub_next = jnp.where(i + 1 < ni, nsub_of(item_rows[jnp.minimum(i + 1, ni - 1)]), 0)
    rows_prev = jnp.where(i > 0, item_rows[jnp.maximum(i - 1, 0)], 0)
    buf = i % 2
    groups = SUB // MOE_DMA_UNROLL

    def gather_block(tref, b, q):
        def grp(gi, c):
            base = pl.multiple_of(q * SUB + gi * MOE_DMA_UNROLL, MOE_DMA_UNROLL)
            for k in range(MOE_DMA_UNROLL):
                tok = tref[0, 0, base + k]
                pltpu.make_async_copy(h2_hbm.at[pl.ds(tok, 1), :], x_scr.at[b, pl.ds(base + k, 1), :],
                                      gsem.at[b]).start()
            return c
        lax.fori_loop(0, groups, grp, 0)

    def gather_wait_block(b):
        pltpu.make_async_copy(h2_hbm.at[pl.ds(0, SUB), :], x_scr.at[b, pl.ds(0, SUB), :], gsem.at[b]).wait()

    def scatter_copy(b, r):
        dst = dst_ref[0, 0, r]
        return pltpu.make_async_copy(acc_scr.at[b, pl.ds(r, 1), :], y_hbm.at[pl.ds(dst, 1), :], ssem)

    def repeat(n, fn, lo=0):
        def body(q, c):
            fn(q)
            return c
        lax.fori_loop(lo, n, body, 0)

    def scatter_rows(b, n):
        def grp(gi):
            base = pl.multiple_of(gi * MOE_DMA_UNROLL, MOE_DMA_UNROLL)
            for k in range(MOE_DMA_UNROLL):
                scatter_copy(b, base + k).start()
        full = n // MOE_DMA_UNROLL
        repeat(full, grp)
        repeat(n, lambda r: scatter_copy(b, r).start(), lo=full * MOE_DMA_UNROLL)

    def scatter_wait_rows(n):
        u = MOE_DMA_UNROLL
        full = n // u
        repeat(full, lambda q: pltpu.make_async_copy(acc_scr.at[0, pl.ds(0, u), :], y_hbm.at[pl.ds(0, u), :],
                                                      ssem).wait())
        repeat(n, lambda r: scatter_copy(0, 0).wait(), lo=full * u)

    @pl.when((i == 0) & (j == 0))
    def _():
        repeat(nsub, lambda q: gather_block(tok_ref, 0, q))

    @pl.when(j < nsub_next)
    def _():
        gather_block(tokn_ref, 1 - buf, j)

    @pl.when(j == 0)
    def _():
        repeat(nsub, lambda q: gather_wait_block(buf))

    @pl.when(rows > 0)
    def _():
        w1b[...] = w1_ref[0].astype(BF16)
        w3b[...] = w3_ref[0].astype(BF16)
        w2b[...] = w2_ref[0].astype(BF16)

        def sub(sb, c):
            r0 = pl.multiple_of(sb * SUB, SUB)
            x = x_scr[buf, pl.ds(r0, SUB), :].astype(BF16)
            h1 = _dot(x, w1b[...])
            h3 = _dot(x, w3b[...])
            hh = (h1 * _sigmoid(h1) * h3).astype(BF16)
            contrib = _dot(hh, w2b[...])

            @pl.when(j == 0)
            def _():
                acc_scr[buf, pl.ds(r0, SUB), :] = contrib

            @pl.when(j > 0)
            def _():
                acc_scr[buf, pl.ds(r0, SUB), :] += contrib
            return c
        lax.fori_loop(0, nsub, sub, 0)

    @pl.when(j == nf - 1)
    def _():
        scatter_wait_rows(rows_prev)
        scatter_rows(buf, rows)

        @pl.when(i == ni - 1)
        def _():
            scatter_wait_rows(rows)


def _moe(h2, item_e, item_rows, slot_tok, slot_dst, w1, w3, w2, n_out_rows):
    t_, d = h2.shape
    ne, _, f = w1.shape
    R, SUB, tf = MOE_ITEM_ROWS, MOE_SUB_ROWS, MOE_F_CHUNK
    ni = item_e.shape[0]
    nf = f // tf
    assert R == nf * SUB
    grid_spec = pltpu.PrefetchScalarGridSpec(
        num_scalar_prefetch=2,
        grid=(ni, nf),
        in_specs=[
            pl.BlockSpec((1, 1, R), lambda i, j, ie, ir: (i, 0, 0), memory_space=pltpu.SMEM),
            pl.BlockSpec((1, 1, R), lambda i, j, ie, ir: (jnp.minimum(i + 1, ni - 1), 0, 0),
                         memory_space=pltpu.SMEM),
            pl.BlockSpec((1, 1, R), lambda i, j, ie, ir: (i, 0, 0), memory_space=pltpu.SMEM),
            pl.BlockSpec(memory_space=pl.ANY),
            pl.BlockSpec((1, d, tf), lambda i, j, ie, ir: (ie[i], 0, j)),
            pl.BlockSpec((1, d, tf), lambda i, j, ie, ir: (ie[i], 0, j)),
            pl.BlockSpec((1, tf, d), lambda i, j, ie, ir: (ie[i], j, 0)),
        ],
        out_specs=pl.BlockSpec(memory_space=pl.ANY),
        scratch_shapes=[
            pltpu.VMEM((2, R, d), F32), pltpu.VMEM((2, R, d), F32),
            pltpu.VMEM((d, tf), BF16), pltpu.VMEM((d, tf), BF16), pltpu.VMEM((tf, d), BF16),
            pltpu.SemaphoreType.DMA((2,)), pltpu.SemaphoreType.DMA(()),
        ],
    )
    return pl.pallas_call(
        functools.partial(_moe_kernel, R, SUB, nf, ni),
        out_shape=jax.ShapeDtypeStruct((n_out_rows, d), F32),
        grid_spec=grid_spec,
        compiler_params=_cparams(("arbitrary", "arbitrary")),
        name="moe_experts",
    )(item_e, item_rows, slot_tok, slot_tok, slot_dst, h2, w1, w3, w2)


def _moe_plan(expert, t_):
    R = MOE_ITEM_ROWS
    na = expert.shape[0]
    ni = N_EXPERTS + na // R
    onehot = (expert[:, None] == jnp.arange(N_EXPERTS, dtype=jnp.int32)[None, :]).astype(jnp.int32)
    csum = jnp.cumsum(onehot, axis=0)
    rank = jnp.sum(csum * onehot, axis=1) - 1
    counts = csum[-1]
    n_items = (counts + R - 1) // R
    item_end = jnp.cumsum(n_items)
    item_start = item_end - n_items
    slot = jnp.sum(item_start[None, :] * onehot, axis=1) * R + rank
    slot_a = jnp.full((ni * R,), -1, jnp.int32).at[slot].set(jnp.arange(na, dtype=jnp.int32))
    valid = slot_a >= 0
    slot_tok = jnp.where(valid, slot_a // TOP_K, 0)
    slot_dst = jnp.where(valid, (slot_a % TOP_K) * t_ + slot_a // TOP_K, 0)
    ii = jnp.arange(ni, dtype=jnp.int32)
    e_of = jnp.minimum(jnp.sum((item_end[None, :] <= ii[:, None]).astype(jnp.int32), axis=1), N_EXPERTS - 1)
    oh_e = (e_of[:, None] == jnp.arange(N_EXPERTS, dtype=jnp.int32)[None, :]).astype(jnp.int32)
    cnt_e = jnp.sum(oh_e * counts[None, :], axis=1)
    start_e = jnp.sum(oh_e * item_start[None, :], axis=1)
    rows = jnp.clip(cnt_e - (ii - start_e) * R, 0, R).astype(jnp.int32)
    live = ii < item_end[-1]
    rows = jnp.where(live, rows, 0)
    last_e = jnp.max(jnp.where(live, e_of, 0))
    e_of = jnp.where(live, e_of, last_e).astype(jnp.int32)
    return e_of, rows, slot_tok.reshape(ni, 1, R), slot_dst.reshape(ni, 1, R)


def _final_kernel(x1_ref, y0_ref, y1_ref, rt_ref, g2_ref, fg_ref, o_ref):
    rt = rt_ref[0]
    y = rt[:, TOP_K:TOP_K + 1] * y0_ref[...] + rt[:, TOP_K + 1:TOP_K + 2] * y1_ref[...]
    x = x1_ref[0] + g2_ref[0] * y
    ms = jnp.mean(x * x, axis=-1, keepdims=True)
    o_ref[0] = x * lax.rsqrt(ms + EPS) * fg_ref[...]


def _final(x1, ybuf, routed, g2, final_g, tm):
    b_, s_, d = x1.shape
    nt = s_ // tm
    return pl.pallas_call(
        _final_kernel,
        out_shape=jax.ShapeDtypeStruct((b_, s_, d), F32),
        grid=(b_, nt),
        in_specs=[
            pl.BlockSpec((1, tm, d), lambda b, i: (b, i, 0)),
            pl.BlockSpec((tm, d), lambda b, i: (b * nt + i, 0)),
            pl.BlockSpec((tm, d), lambda b, i: (b_ * nt + b * nt + i, 0)),
            pl.BlockSpec((1, tm, LANES), lambda b, i: (b, i, 0)),
            pl.BlockSpec((1, 1, d), lambda b, i: (b, 0, 0)),
            pl.BlockSpec((1, d), lambda b, i: (0, 0)),
        ],
        out_specs=pl.BlockSpec((1, tm, d), lambda b, i: (b, i, 0)),
        compiler_params=_cparams(("parallel", "parallel")),
        name="final_norm",
    )(x1, ybuf, ybuf, routed, g2, final_g)


def kernel(x, c, ctx, c_ctx, w_mod, b_mod, norm1_g, w_in, b_gates, mlstm_head_g, conv_w, conv_b, lru_wa, lru_ba,
           lru_wx, lru_bx, lru_lam, w_pa, w_pb, w_out, norm2_g, w_rg, w_re, w1, w3, w2, final_g):
    b_, s_, d = x.shape
    n_ctx = ctx.shape[1]
    rows = s_ // GRID_W
    assert w_mod.shape[0] == 1, "single layer"
    wm = d // 2
    nh = MLSTM_HEADS

    cc = jnp.zeros((8, d), F32).at[:b_].set(c).at[b_].set(c_ctx)
    mod = _modulation(cc, w_mod[0], b_mod[0][None, :])
    sh1, sc1, g1, sh2, sc2, g2 = [mod[:b_, i * d:(i + 1) * d][:, None, :] for i in range(6)]
    csh1, csc1 = mod[b_:b_ + 1, 0:d][:, None, :], mod[b_:b_ + 1, d:2 * d][:, None, :]

    o_g = N_PRE_TILES * wm
    o_ux = o_g + 4 * nh
    w_post = w_in[0, :, o_ux:].astype(BF16)
    w_gates = jnp.zeros((d, LANES), F32).at[:, :4 * nh].set(w_in[0, :, o_g:o_ux]).astype(BF16)
    bg_pad = jnp.zeros((1, LANES), F32).at[0, :4 * nh].set(b_gates[0])

    ng = norm1_g[0][None, None, :]
    u_lat, g_lat = _inproj(x, ng * (1.0 + sc1), sh1, w_in, w_post, w_gates, bg_pad,
                           tuple(range(N_COL_TILES)), min(1024, s_))
    u_ctx, g_ctx = _inproj(ctx, ng * (1.0 + csc1), csh1, w_in, w_post, w_gates, bg_pad,
                           (COL_Q, COL_K, COL_V, COL_UX), n_ctx)

    ya_f, ya_b = _mlstm(u_lat, u_ctx, g_lat, g_ctx, mlstm_head_g[0], MLSTM_CHUNK)

    ux_lat = u_lat[:, :, COL_UX * wm:(COL_UX + 1) * wm]
    ux_col = ux_lat.reshape(b_, rows, GRID_W, wm).transpose(0, 2, 1, 3).reshape(b_, s_, wm)
    ux_seq = jnp.concatenate([u_ctx[:, :, CTX_UX * wm:(CTX_UX + 1) * wm], ux_col], axis=1)
    wa, wx = lru_wa[0], lru_wx[0]
    w_cat = jnp.concatenate([wa[0], wx[0], wa[1], wx[1]], axis=-1).astype(BF16)
    cblk = wm // LRU_BLOCKS
    blk = lambda v: v.reshape(LRU_BLOCKS, cblk)
    b_cat = jnp.concatenate([blk(lru_ba[0, 0]), blk(lru_bx[0, 0]), blk(lru_ba[0, 1]), blk(lru_bx[0, 1])],
                            axis=-1).reshape(1, 4 * wm)
    h_col = _rglru(ux_seq, conv_w[0], conv_b[0][None, :], w_cat, b_cat, lru_lam[0], n_ctx)
    hr = h_col.reshape(b_, GRID_W, rows, wm).transpose(0, 2, 1, 3).reshape(b_, s_, wm)

    w_r = jnp.zeros((d, LANES), F32).at[:, :N_GROUPS].set(w_rg[0]).at[:, N_GROUPS:N_GROUPS + N_EXPERTS].set(w_re[0])
    n2 = norm2_g[0][None, None, :]
    x1, h2, logits = _outproj(ya_f, ya_b, hr, u_lat, x, g1, n2 * (1.0 + sc2), sh2,
                              w_pa[0].astype(BF16), w_pb[0].astype(BF16), w_out[0].astype(BF16), w_r, 256)

    t_ = b_ * s_
    routed = _route(logits.reshape(t_, LANES), 512)
    expert = routed[:, :TOP_K].astype(jnp.int32).reshape(-1)
    item_e, item_rows, slot_tok, slot_dst = _moe_plan(expert, t_)
    ybuf = _moe(h2.reshape(t_, d), item_e, item_rows, slot_tok, slot_dst, w1[0], w3[0], w2[0],
                TOP_K * t_)

    return _final(x1, ybuf, routed.reshape(b_, s_, LANES), g2, final_g[None, :], 256)
```

```python
import functools

import jax
import jax.numpy as jnp
from jax import lax
from jax.experimental import pallas as pl
from jax.experimental.pallas import tpu as pltpu

F32 = jnp.float32
BF16 = jnp.bfloat16

EPS = 1e-6
GRID_W = 64
MLSTM_HEADS = 8
LRU_BLOCKS = 8
LRU_C = 8.0
N_GROUPS = 4
EXPERTS_PER_GROUP = 8
N_EXPERTS = N_GROUPS * EXPERTS_PER_GROUP
TOP_K = 2

VMEM_LIMIT_BYTES = 56 * 1024 * 1024
LANES = 128

MLSTM_CHUNK = 128
MOE_ITEM_ROWS = 768
MOE_SUB_ROWS = 256
MOE_F_CHUNK = 256


def _cparams(sem):
    return pltpu.CompilerParams(dimension_semantics=sem, vmem_limit_bytes=VMEM_LIMIT_BYTES)


def _sigmoid(x):
    return 1.0 / (1.0 + jnp.exp(-x))


def _dot(a, b):
    return jnp.dot(a, b, preferred_element_type=F32)


def _split3(x):
    x1 = x.astype(BF16)
    r1 = x - x1.astype(F32)
    x2 = r1.astype(BF16)
    x3 = (r1 - x2.astype(F32)).astype(BF16)
    return x1, x2, x3


def _mod_kernel(c_ref, w_ref, b_ref, o_ref):
    c = c_ref[...]
    s = (c * _sigmoid(c)).astype(BF16)
    o_ref[...] = _dot(s, w_ref[...].astype(BF16)) + b_ref[...]


def _modulation(cc, w_mod, b_mod):
    m, d = cc.shape
    n = w_mod.shape[1]
    tn = 1024
    return pl.pallas_call(
        _mod_kernel,
        out_shape=jax.ShapeDtypeStruct((m, n), F32),
        grid=(n // tn,),
        in_specs=[
            pl.BlockSpec((m, d), lambda j: (0, 0)),
            pl.BlockSpec((d, tn), lambda j: (0, j)),
            pl.BlockSpec((1, tn), lambda j: (0, j)),
        ],
        out_specs=pl.BlockSpec((m, tn), lambda j: (0, j)),
        compiler_params=_cparams(("parallel",)),
        name="modulation",
    )(cc, w_mod, b_mod)


N_PRE_TILES = 5
N_GATE_COLS = 32


def _dot_nt(a, b):
    return lax.dot_general(a, b, (((1,), (1,)), ((), ())), preferred_element_type=F32)


def _inproj_kernel(x_ref, gain_ref, shift_ref, wt_ref, wg_ref, bg_ref, u_ref, g_ref, h_scr):
    @pl.when(pl.program_id(2) == 0)
    def _():
        x = x_ref[0]
        ms = jnp.mean(x * x, axis=-1, keepdims=True)
        h = x * lax.rsqrt(ms + EPS) * gain_ref[0] + shift_ref[0]
        hb = h.astype(BF16)
        h_scr[...] = hb
        gates = _dot_nt(hb, wg_ref[...].astype(BF16))
        pad = jnp.zeros((gates.shape[0], LANES - N_GATE_COLS), F32)
        g_ref[0] = jnp.concatenate([gates, pad], axis=1) + bg_ref[...]

    u_ref[0] = _dot_nt(h_scr[...], wt_ref[...].astype(BF16)).astype(BF16)


def _inproj(x, gain, shift, w_t, b_gates, tiles, tm):
    b_, s_, d = x.shape
    tn = 1024
    nj = len(tiles)
    if tiles == tuple(range(nj)):
        tile_of = lambda j: j
    else:
        assert tiles == (COL_Q, COL_K, COL_V, COL_UX), tiles
        tile_of = lambda j: jnp.where(j == 3, COL_UX, j)
    row_of = lambda j: tile_of(j) * tn + jnp.where(tile_of(j) >= N_PRE_TILES, N_GATE_COLS, 0)
    per_batch_mod = gain.shape[0] > 1
    mmap = (lambda b, i, j: (b, 0, 0)) if per_batch_mod else (lambda b, i, j: (0, 0, 0))
    gate_blk = N_PRE_TILES * tn // N_GATE_COLS
    return pl.pallas_call(
        _inproj_kernel,
        out_shape=(jax.ShapeDtypeStruct((b_, s_, nj * tn), BF16),
                   jax.ShapeDtypeStruct((b_, s_, LANES), F32)),
        grid=(b_, s_ // tm, nj),
        in_specs=[
            pl.BlockSpec((1, tm, d), lambda b, i, j: (b, i, 0)),
            pl.BlockSpec((1, 1, d), mmap),
            pl.BlockSpec((1, 1, d), mmap),
            pl.BlockSpec((pl.Element(tn), pl.Element(d)), lambda b, i, j: (pl.multiple_of(row_of(j), 8), 0)),
            pl.BlockSpec((N_GATE_COLS, d), lambda b, i, j: (gate_blk, 0)),
            pl.BlockSpec((1, LANES), lambda b, i, j: (0, 0)),
        ],
        out_specs=(pl.BlockSpec((1, tm, tn), lambda b, i, j: (b, i, j)),
                   pl.BlockSpec((1, tm, LANES), lambda b, i, j: (b, i, 0))),
        scratch_shapes=[pltpu.VMEM((tm, d), BF16)],
        compiler_params=_cparams(("parallel", "parallel", "arbitrary")),
        name="inproj",
    )(x, gain, shift, w_t, w_t, b_gates)


COL_Q, COL_K, COL_V, COL_OF, COL_OB, COL_UX, COL_UG, COL_GA, COL_GB = 0, 1, 2, 3, 4, 5, 6, 7, 9
CTX_UX = 3
N_COL_TILES = 11


def _mlstm_kernel(nc, nl, L, dh,
                  qc_f, kc_f, vc_f, ql_f, kl_f, vl_f, gc_f, gl_f, o_f,
                  qc_b, kc_b, vc_b, ql_b, kl_b, vl_b, gc_b, gl_b, o_b,
                  hg_ref, tri_ref,
                  out_f, out_b, c_scr, m_scr):
    s = pl.program_id(1)
    nh = MLSTM_HEADS
    scale = dh ** -0.5

    @pl.when(s == 0)
    def _():
        c_scr[...] = jnp.zeros_like(c_scr)
        m_scr[...] = jnp.full_like(m_scr, -1e30)

    is_ctx = s < nc
    ones_col = (lax.broadcasted_iota(jnp.int32, (L, dh), 1) == 0).astype(BF16)
    row = lax.broadcasted_iota(jnp.int32, (L, L), 0)
    col = lax.broadcasted_iota(jnp.int32, (L, L), 1)

    for d, (qc, kc, vc, ql, kl, vl, gc, gl, o_ref, out_ref) in enumerate((
            (qc_f, kc_f, vc_f, ql_f, kl_f, vl_f, gc_f, gl_f, o_f, out_f),
            (qc_b, kc_b, vc_b, ql_b, kl_b, vl_b, gc_b, gl_b, o_b, out_b))):
        q = jnp.where(is_ctx, qc[0], ql[0])
        k = jnp.where(is_ctx, kc[0], kl[0])
        v = jnp.where(is_ctx, vc[0], vl[0])
        g = jnp.where(is_ctx, gc[0], gl[0])
        gt = g.T
        ls = jnp.minimum(g, 0.0) - jnp.log(1.0 + jnp.exp(-jnp.abs(g)))
        lst = jnp.minimum(gt, 0.0) - jnp.log(1.0 + jnp.exp(-jnp.abs(gt)))
        tri = tri_ref[d]
        mask = (col <= row) if d == 0 else (col >= row)
        l1, l2, l3 = _split3(ls)
        bcol_all = _dot(tri, l1) + _dot(tri, l2) + _dot(tri, l3)
        t1, t2, t3 = _split3(lst)
        trit = tri_ref[1 - d]
        brow_all = _dot(t1, trit) + _dot(t2, trit) + _dot(t3, trit)
        last = L - 1 if d == 0 else 0
        for h in range(nh):
            u = d * nh + h
            ci = 16 * d + h
            cf = 16 * d + 8 + h
            hs = slice(h * dh, (h + 1) * dh)
            qh, kh, vh = q[:, hs], k[:, hs], v[:, hs]
            vext = jnp.concatenate([vh, ones_col], axis=1)
            bc = bcol_all[:, cf:cf + 1]
            ic = g[:, ci:ci + 1]
            cr = gt[ci:ci + 1, :] - brow_all[cf:cf + 1, :]
            tot = bcol_all[last:last + 1, cf:cf + 1]
            m_prev = m_scr[u][0:1, 0:1]
            c_prev = c_scr[u]

            sqk = lax.dot_general(qh, kh, (((1,), (1,)), ((), ())), preferred_element_type=F32) * scale
            dm = jnp.where(mask, bc + cr, -jnp.inf)
            rowmax = jnp.max(dm, axis=-1, keepdims=True)
            gg = bc + m_prev
            mt = jnp.maximum(gg, rowmax)
            w_intra = jnp.exp(dm - mt)
            w_inter = jnp.exp(gg - mt)
            a = (sqk * w_intra).astype(BF16)
            r = _dot(a, vext) + w_inter * _dot(qh, c_prev.astype(BF16))

            num = r[:, :dh]
            den = r[:, dh:dh + 1]
            hh = num / jnp.maximum(jnp.abs(den), jnp.exp(-mt))
            hn = hh * lax.rsqrt(jnp.mean(hh * hh, axis=-1, keepdims=True) + EPS) * hg_ref[h:h + 1, :]
            gate = _sigmoid(o_ref[0, :, hs].astype(F32))
            out_ref[0, :, hs] = (gate * hn).astype(BF16)

            wlog = tot - bc + ic
            m_new = jnp.maximum(tot + m_prev, jnp.max(wlog, axis=0, keepdims=True))
            ws = jnp.exp(wlog - m_new) * scale
            decay = jnp.exp(tot + m_prev - m_new)
            kw = (kh.astype(F32) * ws).astype(BF16)
            upd = lax.dot_general(kw, vext, (((0,), (0,)), ((), ())), preferred_element_type=F32)
            c_scr[u] = decay * c_prev + upd
            m_scr[u] = jnp.broadcast_to(m_new, m_scr.shape[1:])


def _mlstm(u_lat, u_ctx, g_lat, g_ctx, head_g, L):
    b_, s_, _ = u_lat.shape
    n_ctx = u_ctx.shape[1]
    nh = MLSTM_HEADS
    w = 1024
    dh = w // nh
    nc, nl = n_ctx // L, s_ // L
    steps = nc + nl
    t0 = jnp.tril(jnp.ones((L, L), F32))
    tri = jnp.stack([t0, t0.T]).astype(BF16)

    def cidx(d):
        if d == 0:
            return lambda st: jnp.minimum(st, nc - 1), lambda st: jnp.maximum(st - nc, 0)
        return lambda st: jnp.maximum(nc - 1 - st, 0), lambda st: jnp.minimum(nl - 1 + nc - st, nl - 1)

    in_specs, args = [], []
    for d in range(2):
        fc, fl = cidx(d)
        for t in range(3):
            in_specs.append(pl.BlockSpec((1, L, w), lambda b, st, fc=fc, t=t: (b, fc(st), t)))
            args.append(u_ctx)
        for t in (COL_Q, COL_K, COL_V):
            in_specs.append(pl.BlockSpec((1, L, w), lambda b, st, fl=fl, t=t: (b, fl(st), t)))
            args.append(u_lat)
        in_specs.append(pl.BlockSpec((1, L, LANES), lambda b, st, fc=fc: (b, fc(st), 0)))
        args.append(g_ctx)
        in_specs.append(pl.BlockSpec((1, L, LANES), lambda b, st, fl=fl: (b, fl(st), 0)))
        args.append(g_lat)
        in_specs.append(pl.BlockSpec((1, L, w), lambda b, st, fl=fl, t=COL_OF + d: (b, fl(st), t)))
        args.append(u_lat)
    in_specs += [pl.BlockSpec((nh, dh), lambda b, st: (0, 0)),
                 pl.BlockSpec((2, L, L), lambda b, st: (0, 0, 0))]
    args += [head_g, tri]
    out_specs = tuple(pl.BlockSpec((1, L, w), lambda b, st, fl=cidx(d)[1]: (b, fl(st), 0)) for d in range(2))
    return pl.pallas_call(
        functools.partial(_mlstm_kernel, nc, nl, L, dh),
        out_shape=(jax.ShapeDtypeStruct((b_, s_, w), BF16),) * 2,
        grid=(b_, steps),
        in_specs=in_specs,
        out_specs=out_specs,
        scratch_shapes=[pltpu.VMEM((2 * nh, dh, 2 * dh), F32), pltpu.VMEM((2 * nh, 8, LANES), F32)],
        compiler_params=_cparams(("parallel", "arbitrary")),
        name="mlstm",
    )(*args)


def _rglru_kernel(n_ctx, n_tot, ux_ref, cw_ref, cb_ref, w_ref, b_ref, lam_ref, out_ref,
                  a_f, b_f, a_b, b_b, h_f, h_b):
    u = ux_ref[0].astype(F32)
    t_idx = lax.broadcasted_iota(jnp.int32, (n_tot, 1), 0)
    seg_lo = jnp.where(t_idx < n_ctx, 0, n_ctx)
    seg_hi = jnp.where(t_idx < n_ctx, n_ctx, n_tot)
    z = jnp.zeros_like(u) + cb_ref[...]
    for j, off in enumerate((-2, -1, 0, 1)):
        if off == 0:
            src = u
        else:
            src = pltpu.roll(u, (-off) % n_tot, 0)
            ok = (t_idx + off >= seg_lo) & (t_idx + off < seg_hi)
            src = jnp.where(ok, src, 0.0)
        z = z + cw_ref[j:j + 1, :] * src
    p = _dot(z.astype(BF16), w_ref[0]) + b_ref[...]
    c = LANES
    for d, (a_scr, b_scr) in enumerate(((a_f, b_f), (a_b, b_b))):
        r = _sigmoid(p[:, (2 * d) * c:(2 * d + 1) * c])
        i = _sigmoid(p[:, (2 * d + 1) * c:(2 * d + 2) * c])
        lam = lam_ref[d:d + 1, :]
        softplus = jnp.maximum(-lam, 0.0) + jnp.log(1.0 + jnp.exp(-jnp.abs(lam)))
        log_a = -LRU_C * r * softplus
        a = jnp.exp(log_a)
        a_scr[...] = a
        b_scr[...] = jnp.sqrt(1.0 - a * a) * i * z

    sub = lax.broadcasted_iota(jnp.int32, (8, c), 0)
    ng = n_tot // 8
    ngc = n_ctx // 8

    def group_scan(a, b, carry, reverse):
        for sft in (1, 2, 4):
            if reverse:
                a_s = pltpu.roll(a, 8 - sft, 0)
                b_s = pltpu.roll(b, 8 - sft, 0)
                ok = sub < 8 - sft
            else:
                a_s = pltpu.roll(a, sft, 0)
                b_s = pltpu.roll(b, sft, 0)
                ok = sub >= sft
            b = jnp.where(ok, a * b_s + b, b)
            a = jnp.where(ok, a * a_s, a)
        h = b + a * carry
        return h, (h[0:1, :] if reverse else h[7:8, :])

    def body(it, carry):
        cf, cb = carry
        rf = pl.multiple_of(it * 8, 8)
        hf, cf = group_scan(a_f[pl.ds(rf, 8), :], b_f[pl.ds(rf, 8), :], cf, False)
        h_f[pl.ds(rf, 8), :] = hf
        jb = jnp.where(it < ngc, ngc - 1 - it, ng - 1 + ngc - it)
        rb = pl.multiple_of(jb * 8, 8)
        hb, cb = group_scan(a_b[pl.ds(rb, 8), :], b_b[pl.ds(rb, 8), :], cb, True)
        h_b[pl.ds(rb, 8), :] = hb
        return cf, cb

    zero = jnp.zeros((1, c), F32)
    lax.fori_loop(0, ng, body, (zero, zero), unroll=4)
    out_ref[0] = (h_f[n_ctx:, :] + h_b[n_ctx:, :]).astype(BF16)


def _rglru(ux_seq, conv_w, conv_b, w_cat, b_cat, lam, n_ctx):
    b_, n_tot, w = ux_seq.shape
    nb = LRU_BLOCKS
    c = w // nb
    return pl.pallas_call(
        functools.partial(_rglru_kernel, n_ctx, n_tot),
        out_shape=jax.ShapeDtypeStruct((b_, n_tot - n_ctx, w), BF16),
        grid=(b_, nb),
        in_specs=[
            pl.BlockSpec((1, n_tot, c), lambda b, k: (b, 0, k)),
            pl.BlockSpec((4, c), lambda b, k: (0, k)),
            pl.BlockSpec((1, c), lambda b, k: (0, k)),
            pl.BlockSpec((1, c, 4 * c), lambda b, k: (k, 0, 0)),
            pl.BlockSpec((1, 4 * c), lambda b, k: (0, k)),
            pl.BlockSpec((2, c), lambda b, k: (0, k)),
        ],
        out_specs=pl.BlockSpec((1, n_tot - n_ctx, c), lambda b, k: (b, 0, k)),
        scratch_shapes=[pltpu.VMEM((n_tot, c), F32)] * 6,
        compiler_params=_cparams(("parallel", "parallel")),
        name="rglru",
    )(ux_seq, conv_w, conv_b, w_cat, b_cat, lam)


def _outproj_kernel(yaf_ref, yab_ref, hr_ref, ug_ref, ga0_ref, ga1_ref, gb0_ref, gb1_ref, x_ref,
                    g1_ref, gain_ref, shift_ref, wpa_ref, wpb_ref, wout_ref, wr_ref,
                    x1_ref, h2_ref, lg_ref):
    ya = (yaf_ref[0].astype(F32) + yab_ref[0].astype(F32)).astype(BF16)
    ug = ug_ref[0].astype(F32)
    gelu = 0.5 * ug * (1.0 + jnp.tanh(0.7978845608028654 * (ug + 0.044715 * ug * ug * ug)))
    yb = (hr_ref[0].astype(F32) * gelu).astype(BF16)
    pa = _dot(ya, wpa_ref[...])
    pb = _dot(yb, wpb_ref[...])
    ga = jnp.concatenate([ga0_ref[0], ga1_ref[0]], axis=1).astype(F32)
    gb = jnp.concatenate([gb0_ref[0], gb1_ref[0]], axis=1).astype(F32)
    mix = (_sigmoid(ga) * pa + _sigmoid(gb) * pb).astype(BF16)
    x1 = x_ref[0] + g1_ref[0] * _dot(mix, wout_ref[...])
    x1_ref[0] = x1
    ms = jnp.mean(x1 * x1, axis=-1, keepdims=True)
    h2 = x1 * lax.rsqrt(ms + EPS) * gain_ref[0] + shift_ref[0]
    ns = h2.shape[1] // LANES
    for s in range(ns):
        h2_ref[pl.ds(s, h2.shape[0], stride=ns), :] = h2[:, s * LANES:(s + 1) * LANES]
    wr = wr_ref[...]
    h1, h2b, _ = _split3(h2)
    w1, w2b, _ = _split3(wr)
    lg_ref[0] = _dot(h1, w1) + (_dot(h1, w2b) + _dot(h2b, w1))


def _outproj(ya_f, ya_b, hr, u_lat, x, g1, gain2, shift2, w_pa, w_pb, w_out, w_r, tm):
    b_, s_, d = x.shape
    w = ya_f.shape[2]
    row = lambda b, i: (b, i, 0)
    const2 = lambda b, i: (0, 0)
    bmap = lambda b, i: (b, 0, 0)
    ucol = lambda t: (lambda b, i: (b, i, t))
    single = pl.Buffered(1)
    return pl.pallas_call(
        _outproj_kernel,
        out_shape=(jax.ShapeDtypeStruct((b_, s_, d), F32),
                   jax.ShapeDtypeStruct((b_ * s_ * (d // LANES), LANES), F32),
                   jax.ShapeDtypeStruct((b_, s_, LANES), F32)),
        grid=(b_, s_ // tm),
        in_specs=[
            pl.BlockSpec((1, tm, w), row), pl.BlockSpec((1, tm, w), row), pl.BlockSpec((1, tm, w), row),
            pl.BlockSpec((1, tm, w), ucol(COL_UG)),
            pl.BlockSpec((1, tm, w), ucol(COL_GA)), pl.BlockSpec((1, tm, w), ucol(COL_GA + 1)),
            pl.BlockSpec((1, tm, w), ucol(COL_GB)), pl.BlockSpec((1, tm, w), ucol(COL_GB + 1)),
            pl.BlockSpec((1, tm, d), row),
            pl.BlockSpec((1, 1, d), bmap), pl.BlockSpec((1, 1, d), bmap), pl.BlockSpec((1, 1, d), bmap),
            pl.BlockSpec((w, d), const2, pipeline_mode=single),
            pl.BlockSpec((w, d), const2, pipeline_mode=single),
            pl.BlockSpec((d, d), const2, pipeline_mode=single),
            pl.BlockSpec((d, LANES), const2, pipeline_mode=single),
        ],
        out_specs=(pl.BlockSpec((1, tm, d), row),
                   pl.BlockSpec((tm * (d // LANES), LANES), lambda b, i: (b * (s_ // tm) + i, 0)),
                   pl.BlockSpec((1, tm, LANES), row)),
        compiler_params=_cparams(("parallel", "parallel")),
        name="outproj",
    )(ya_f, ya_b, hr, u_lat, u_lat, u_lat, u_lat, u_lat, x, g1, gain2, shift2, w_pa, w_pb, w_out, w_r)


def _route_kernel(lg_ref, o_ref):
    lg = lg_ref[...]
    lane = lax.broadcasted_iota(jnp.int32, lg.shape, 1)
    neg = -jnp.inf
    big = jnp.int32(1 << 20)
    g_l = jnp.where(lane < N_GROUPS, lg, neg)
    g_max = jnp.max(g_l, axis=-1, keepdims=True)
    g_sel = jnp.min(jnp.where(g_l == g_max, lane, big), axis=-1, keepdims=True)
    p_g = 1.0 / jnp.sum(jnp.exp(g_l - g_max), axis=-1, keepdims=True)
    lo = N_GROUPS + g_sel * EXPERTS_PER_GROUP
    e_l = jnp.where((lane >= lo) & (lane < lo + EXPERTS_PER_GROUP), lg, neg)
    v1 = jnp.max(e_l, axis=-1, keepdims=True)
    i1 = jnp.min(jnp.where(e_l == v1, lane, big), axis=-1, keepdims=True)
    e_l2 = jnp.where(lane == i1, neg, e_l)
    v2 = jnp.max(e_l2, axis=-1, keepdims=True)
    i2 = jnp.min(jnp.where(e_l2 == v2, lane, big), axis=-1, keepdims=True)
    e2 = jnp.exp(v2 - v1)
    w1 = p_g / (1.0 + e2)
    w2 = p_g * e2 / (1.0 + e2)
    out = jnp.where(lane == 0, (i1 - N_GROUPS).astype(F32),
          jnp.where(lane == 1, (i2 - N_GROUPS).astype(F32),
          jnp.where(lane == 2, w1, jnp.where(lane == 3, w2, 0.0))))
    o_ref[...] = out


def _route(logits, tm):
    t_, _ = logits.shape
    return pl.pallas_call(
        _route_kernel,
        out_shape=jax.ShapeDtypeStruct((t_, LANES), F32),
        grid=(t_ // tm,),
        in_specs=[pl.BlockSpec((tm, LANES), lambda i: (i, 0))],
        out_specs=pl.BlockSpec((tm, LANES), lambda i: (i, 0)),
        compiler_params=_cparams(("parallel",)),
        name="route",
    )(logits)


MOE_DMA_UNROLL = 32


def _moe_kernel(R, SUB, nf, ni, item_e, item_rows, tok_ref, tokn_ref, dst_ref, h2_hbm, w1_ref, w3_ref, w2_ref,
                y_hbm, x_scr, xb_scr, acc_scr, y_scr, w1b, w3b, w2b, gsem, ssem):
    i = pl.program_id(0)
    j = pl.program_id(1)
    nsub_of = lambda r: (r + SUB - 1) // SUB
    rows = item_rows[i]
    nsub = nsub_of(rows)
    nsub_next = jnp.where(i + 1 < ni, nsub_of(item_rows[jnp.minimum(i + 1, ni - 1)]), 0)
    rows_prev = jnp.where(i > 0, item_rows[jnp.maximum(i - 1, 0)], 0)
    buf = i % 2
    groups = SUB // MOE_DMA_UNROLL

    ns = acc_scr.shape[1] // LANES

    def tok_rows(ref, r):
        return ref.at[pl.ds(pl.multiple_of(r * ns, ns), ns), :]

    def gather_block(tref, b, q):
        def grp(gi, c):
            base = pl.multiple_of(q * SUB + gi * MOE_DMA_UNROLL, MOE_DMA_UNROLL)
            for k in range(MOE_DMA_UNROLL):
                tok = tref[0, 0, base + k]
                pltpu.make_async_copy(tok_rows(h2_hbm, tok), tok_rows(x_scr.at[b], base + k),
                                      gsem.at[b]).start(priority=1)
            return c
        lax.fori_loop(0, groups, grp, 0)

    def gather_wait_block(b):
        n = SUB * ns
        pltpu.make_async_copy(h2_hbm.at[pl.ds(0, n), :], x_scr.at[b, pl.ds(0, n), :], gsem.at[b]).wait()

    def scatter_copy(r):
        dst = dst_ref[0, 0, r]
        return pltpu.make_async_copy(tok_rows(y_scr, r), tok_rows(y_hbm, dst), ssem)

    def repeat(n, fn, lo=0):
        def body(q, c):
            fn(q)
            return c
        lax.fori_loop(lo, n, body, 0)

    def scatter_rows(n):
        def grp(gi):
            base = pl.multiple_of(gi * MOE_DMA_UNROLL, MOE_DMA_UNROLL)
            for k in range(MOE_DMA_UNROLL):
                scatter_copy(base + k).start(priority=k % 2)
        full = n // MOE_DMA_UNROLL
        repeat(full, grp)
        repeat(n, lambda r: scatter_copy(r).start(), lo=full * MOE_DMA_UNROLL)

    def scatter_wait_rows(n):
        u = MOE_DMA_UNROLL * ns
        full = n // MOE_DMA_UNROLL
        repeat(full, lambda q: pltpu.make_async_copy(y_scr.at[pl.ds(0, u), :], y_hbm.at[pl.ds(0, u), :],
                                                      ssem).wait())
        repeat(n, lambda r: scatter_copy(0).wait(), lo=full * MOE_DMA_UNROLL)

    @pl.when((i == 0) & (j == 0))
    def _():
        repeat(nsub, lambda q: gather_block(tok_ref, 0, q))

    @pl.when(j < nsub_next)
    def _():
        gather_block(tokn_ref, 1 - buf, j)

    @pl.when(j == 0)
    def _():
        repeat(nsub, lambda q: gather_wait_block(buf))

    @pl.when(j == nf - 1)
    def _():
        scatter_wait_rows(rows_prev)

    @pl.when(rows > 0)
    def _():
        w1b[...] = w1_ref[0].astype(BF16)
        w3b[...] = w3_ref[0].astype(BF16)
        w2b[...] = w2_ref[0].astype(BF16)

        @pl.when(j == 0)
        def _():
            def unsplit(sb, c):
                r0 = pl.multiple_of(sb * SUB, SUB)
                s0 = pl.multiple_of(sb * (SUB * ns), SUB * ns)
                xb_scr[pl.ds(r0, SUB), :] = jnp.concatenate(
                    [x_scr[buf, pl.ds(s0 + s, SUB, stride=ns), :] for s in range(ns)], axis=1).astype(BF16)
                return c
            lax.fori_loop(0, nsub, unsplit, 0)

        def sub(sb, c):
            r0 = pl.multiple_of(sb * SUB, SUB)
            s0 = pl.multiple_of(sb * (SUB * ns), SUB * ns)
            x = xb_scr[pl.ds(r0, SUB), :]
            h1 = _dot(x, w1b[...])
            h3 = _dot(x, w3b[...])
            hh = (h1 * _sigmoid(h1) * h3).astype(BF16)
            contrib = _dot(hh, w2b[...])

            @pl.when(j == 0)
            def _():
                acc_scr[pl.ds(r0, SUB), :] = contrib

            @pl.when((j > 0) & (j < nf - 1))
            def _():
                acc_scr[pl.ds(r0, SUB), :] += contrib

            @pl.when(j == nf - 1)
            def _():
                tot = acc_scr[pl.ds(r0, SUB), :] + contrib
                for s in range(ns):
                    y_scr[pl.ds(s0 + s, SUB, stride=ns), :] = tot[:, s * LANES:(s + 1) * LANES]
            return c
        lax.fori_loop(0, nsub, sub, 0)

    @pl.when(j == nf - 1)
    def _():
        scatter_rows(rows)

        @pl.when(i == ni - 1)
        def _():
            scatter_wait_rows(rows)


def _moe(h2s, item_e, item_rows, slot_tok, slot_dst, w1, w3, w2, n_out_rows):
    ne, d, f = w1.shape
    ns = d // LANES
    R, SUB, tf = MOE_ITEM_ROWS, MOE_SUB_ROWS, MOE_F_CHUNK
    ni = item_e.shape[0]
    nf = f // tf
    assert R % SUB == 0 and R <= nf * SUB and nf >= 2
    grid_spec = pltpu.PrefetchScalarGridSpec(
        num_scalar_prefetch=2,
        grid=(ni, nf),
        in_specs=[
            pl.BlockSpec((1, 1, R), lambda i, j, ie, ir: (i, 0, 0), memory_space=pltpu.SMEM),
            pl.BlockSpec((1, 1, R), lambda i, j, ie, ir: (jnp.minimum(i + 1, ni - 1), 0, 0),
                         memory_space=pltpu.SMEM),
            pl.BlockSpec((1, 1, R), lambda i, j, ie, ir: (i, 0, 0), memory_space=pltpu.SMEM),
            pl.BlockSpec(memory_space=pl.ANY),
            pl.BlockSpec((1, d, tf), lambda i, j, ie, ir: (ie[i], 0, j)),
            pl.BlockSpec((1, d, tf), lambda i, j, ie, ir: (ie[i], 0, j)),
            pl.BlockSpec((1, tf, d), lambda i, j, ie, ir: (ie[i], j, 0)),
        ],
        out_specs=pl.BlockSpec(memory_space=pl.ANY),
        scratch_shapes=[
            pltpu.VMEM((2, R * ns, LANES), F32), pltpu.VMEM((R, d), BF16),
            pltpu.VMEM((R, d), F32), pltpu.VMEM((R * ns, LANES), F32),
            pltpu.VMEM((d, tf), BF16), pltpu.VMEM((d, tf), BF16), pltpu.VMEM((tf, d), BF16),
            pltpu.SemaphoreType.DMA((2,)), pltpu.SemaphoreType.DMA(()),
        ],
    )
    return pl.pallas_call(
        functools.partial(_moe_kernel, R, SUB, nf, ni),
        out_shape=jax.ShapeDtypeStruct((n_out_rows * ns, LANES), F32),
        grid_spec=grid_spec,
        compiler_params=_cparams(("arbitrary", "arbitrary")),
        name="moe_experts",
    )(item_e, item_rows, slot_tok, slot_tok, slot_dst, h2s, w1, w3, w2)


def _moe_plan(expert, t_):
    R = MOE_ITEM_ROWS
    na = expert.shape[0]
    ni = N_EXPERTS + na // R
    onehot = (expert[:, None] == jnp.arange(N_EXPERTS, dtype=jnp.int32)[None, :]).astype(jnp.int32)
    csum = jnp.cumsum(onehot, axis=0)
    rank = jnp.sum(csum * onehot, axis=1) - 1
    counts = csum[-1]
    n_items = (counts + R - 1) // R
    item_end = jnp.cumsum(n_items)
    item_start = item_end - n_items
    slot = jnp.sum(item_start[None, :] * onehot, axis=1) * R + rank
    slot_a = jnp.full((ni * R,), -1, jnp.int32).at[slot].set(jnp.arange(na, dtype=jnp.int32))
    valid = slot_a >= 0
    slot_tok = jnp.where(valid, slot_a // TOP_K, 0)
    slot_dst = jnp.where(valid, (slot_a % TOP_K) * t_ + slot_a // TOP_K, 0)
    ii = jnp.arange(ni, dtype=jnp.int32)
    e_of = jnp.minimum(jnp.sum((item_end[None, :] <= ii[:, None]).astype(jnp.int32), axis=1), N_EXPERTS - 1)
    oh_e = (e_of[:, None] == jnp.arange(N_EXPERTS, dtype=jnp.int32)[None, :]).astype(jnp.int32)
    cnt_e = jnp.sum(oh_e * counts[None, :], axis=1)
    start_e = jnp.sum(oh_e * item_start[None, :], axis=1)
    rows = jnp.clip(cnt_e - (ii - start_e) * R, 0, R).astype(jnp.int32)
    live = ii < item_end[-1]
    rows = jnp.where(live, rows, 0)
    last_e = jnp.max(jnp.where(live, e_of, 0))
    e_of = jnp.where(live, e_of, last_e).astype(jnp.int32)
    return e_of, rows, slot_tok.reshape(ni, 1, R), slot_dst.reshape(ni, 1, R)


def _final_kernel(x1_ref, y0_ref, y1_ref, rt_ref, g2_ref, fg_ref, o_ref):
    rt = rt_ref[0]
    tm = rt.shape[0]
    ns = y0_ref.shape[0] // tm
    unsplit = lambda ref: jnp.concatenate([ref[pl.ds(s, tm, stride=ns), :] for s in range(ns)], axis=1)
    y = rt[:, TOP_K:TOP_K + 1] * unsplit(y0_ref) + rt[:, TOP_K + 1:TOP_K + 2] * unsplit(y1_ref)
    x = x1_ref[0] + g2_ref[0] * y
    ms = jnp.mean(x * x, axis=-1, keepdims=True)
    o_ref[0] = x * lax.rsqrt(ms + EPS) * fg_ref[...]


def _final(x1, ybuf, routed, g2, final_g, tm):
    b_, s_, d = x1.shape
    nt = s_ // tm
    ns = d // LANES
    return pl.pallas_call(
        _final_kernel,
        out_shape=jax.ShapeDtypeStruct((b_, s_, d), F32),
        grid=(b_, nt),
        in_specs=[
            pl.BlockSpec((1, tm, d), lambda b, i: (b, i, 0)),
            pl.BlockSpec((tm * ns, LANES), lambda b, i: (b * nt + i, 0)),
            pl.BlockSpec((tm * ns, LANES), lambda b, i: (b_ * nt + b * nt + i, 0)),
            pl.BlockSpec((1, tm, LANES), lambda b, i: (b, i, 0)),
            pl.BlockSpec((1, 1, d), lambda b, i: (b, 0, 0)),
            pl.BlockSpec((1, d), lambda b, i: (0, 0)),
        ],
        out_specs=pl.BlockSpec((1, tm, d), lambda b, i: (b, i, 0)),
        compiler_params=_cparams(("parallel", "parallel")),
        name="final_norm",
    )(x1, ybuf, ybuf, routed, g2, final_g)


def kernel(x, c, ctx, c_ctx, w_mod, b_mod, norm1_g, w_in, b_gates, mlstm_head_g, conv_w, conv_b, lru_wa, lru_ba,
           lru_wx, lru_bx, lru_lam, w_pa, w_pb, w_out, norm2_g, w_rg, w_re, w1, w3, w2, final_g):
    b_, s_, d = x.shape
    n_ctx = ctx.shape[1]
    rows = s_ // GRID_W
    assert w_mod.shape[0] == 1, "single layer"
    wm = d // 2
    nh = MLSTM_HEADS

    cc = jnp.zeros((8, d), F32).at[:b_].set(c).at[b_].set(c_ctx)
    mod = _modulation(cc, w_mod[0], b_mod[0][None, :])
    sh1, sc1, g1, sh2, sc2, g2 = [mod[:b_, i * d:(i + 1) * d][:, None, :] for i in range(6)]
    csh1, csc1 = mod[b_:b_ + 1, 0:d][:, None, :], mod[b_:b_ + 1, d:2 * d][:, None, :]

    assert wm == 1024 and 4 * nh == N_GATE_COLS
    w_t = jnp.transpose(w_in[0])
    bg_pad = jnp.zeros((1, LANES), F32).at[0, :4 * nh].set(b_gates[0])

    ng = norm1_g[0][None, None, :]
    u_lat, g_lat = _inproj(x, ng * (1.0 + sc1), sh1, w_t, bg_pad, tuple(range(N_COL_TILES)), min(1024, s_))
    u_ctx, g_ctx = _inproj(ctx, ng * (1.0 + csc1), csh1, w_t, bg_pad, (COL_Q, COL_K, COL_V, COL_UX), n_ctx)

    ya_f, ya_b = _mlstm(u_lat, u_ctx, g_lat, g_ctx, mlstm_head_g[0], MLSTM_CHUNK)

    ux_lat = u_lat[:, :, COL_UX * wm:(COL_UX + 1) * wm]
    ux_col = ux_lat.reshape(b_, rows, GRID_W, wm).transpose(0, 2, 1, 3).reshape(b_, s_, wm)
    ux_seq = jnp.concatenate([u_ctx[:, :, CTX_UX * wm:(CTX_UX + 1) * wm], ux_col], axis=1)
    wa, wx = lru_wa[0], lru_wx[0]
    w_cat = jnp.concatenate([wa[0], wx[0], wa[1], wx[1]], axis=-1).astype(BF16)
    cblk = wm // LRU_BLOCKS
    blk = lambda v: v.reshape(LRU_BLOCKS, cblk)
    b_cat = jnp.concatenate([blk(lru_ba[0, 0]), blk(lru_bx[0, 0]), blk(lru_ba[0, 1]), blk(lru_bx[0, 1])],
                            axis=-1).reshape(1, 4 * wm)
    h_col = _rglru(ux_seq, conv_w[0], conv_b[0][None, :], w_cat, b_cat, lru_lam[0], n_ctx)
    hr = h_col.reshape(b_, GRID_W, rows, wm).transpose(0, 2, 1, 3).reshape(b_, s_, wm)

    w_r = jnp.zeros((d, LANES), F32).at[:, :N_GROUPS].set(w_rg[0]).at[:, N_GROUPS:N_GROUPS + N_EXPERTS].set(w_re[0])
    n2 = norm2_g[0][None, None, :]
    x1, h2, logits = _outproj(ya_f, ya_b, hr, u_lat, x, g1, n2 * (1.0 + sc2), sh2,
                              w_pa[0].astype(BF16), w_pb[0].astype(BF16), w_out[0].astype(BF16), w_r, 256)

    t_ = b_ * s_
    routed = _route(logits.reshape(t_, LANES), 512)
    expert = routed[:, :TOP_K].astype(jnp.int32).reshape(-1)
    item_e, item_rows, slot_tok, slot_dst = _moe_plan(expert, t_)
    ybuf = _moe(h2, item_e, item_rows, slot_tok, slot_dst, w1[0], w3[0], w2[0], TOP_K * t_)

    return _final(x1, ybuf, routed.reshape(b_, s_, LANES), g2, final_g[None, :], 256)
```

```python
import functools

import jax
import jax.numpy as jnp
from jax import lax
from jax.experimental import pallas as pl
from jax.experimental.pallas import tpu as pltpu

F32 = jnp.float32
BF16 = jnp.bfloat16

EPS = 1e-6
GRID_W = 64
MLSTM_HEADS = 8
LRU_BLOCKS = 8
LRU_C = 8.0
N_GROUPS = 4
EXPERTS_PER_GROUP = 8
N_EXPERTS = N_GROUPS * EXPERTS_PER_GROUP
TOP_K = 2

VMEM_LIMIT_BYTES = 56 * 1024 * 1024
LANES = 128

MLSTM_CHUNK = 128
MOE_ITEM_ROWS = 768
MOE_SUB_ROWS = 256
MOE_F_CHUNK = 256


def _cparams(sem):
    return pltpu.CompilerParams(dimension_semantics=sem, vmem_limit_bytes=VMEM_LIMIT_BYTES)


def _sigmoid(x):
    return 1.0 / (1.0 + jnp.exp(-x))


def _dot(a, b):
    return jnp.dot(a, b, preferred_element_type=F32)


def _split3(x):
    x1 = x.astype(BF16)
    r1 = x - x1.astype(F32)
    x2 = r1.astype(BF16)
    x3 = (r1 - x2.astype(F32)).astype(BF16)
    return x1, x2, x3


def _mod_kernel(c_ref, w_ref, b_ref, o_ref):
    c = c_ref[...]
    s = (c * _sigmoid(c)).astype(BF16)
    o_ref[...] = _dot(s, w_ref[...].astype(BF16)) + b_ref[...]


def _modulation(cc, w_mod, b_mod):
    m, d = cc.shape
    n = w_mod.shape[1]
    tn = 1024
    return pl.pallas_call(
        _mod_kernel,
        out_shape=jax.ShapeDtypeStruct((m, n), F32),
        grid=(n // tn,),
        in_specs=[
            pl.BlockSpec((m, d), lambda j: (0, 0)),
            pl.BlockSpec((d, tn), lambda j: (0, j)),
            pl.BlockSpec((1, tn), lambda j: (0, j)),
        ],
        out_specs=pl.BlockSpec((m, tn), lambda j: (0, j)),
        compiler_params=_cparams(("parallel",)),
        name="modulation",
    )(cc, w_mod, b_mod)


N_PRE_TILES = 5
N_GATE_COLS = 32


def _dot_nt(a, b):
    return lax.dot_general(a, b, (((1,), (1,)), ((), ())), preferred_element_type=F32)


def _inproj_kernel(x_ref, gain_ref, shift_ref, wt_ref, wg_ref, bg_ref, u_ref, g_ref, h_scr):
    @pl.when(pl.program_id(2) == 0)
    def _():
        x = x_ref[0]
        ms = jnp.mean(x * x, axis=-1, keepdims=True)
        h = x * lax.rsqrt(ms + EPS) * gain_ref[0] + shift_ref[0]
        hb = h.astype(BF16)
        h_scr[...] = hb
        gates = _dot_nt(hb, wg_ref[...].astype(BF16))
        pad = jnp.zeros((gates.shape[0], LANES - N_GATE_COLS), F32)
        g_ref[0] = jnp.concatenate([gates, pad], axis=1) + bg_ref[...]

    u_ref[0] = _dot_nt(h_scr[...], wt_ref[...].astype(BF16)).astype(BF16)


def _inproj(x, gain, shift, w_t, b_gates, tiles, tm):
    b_, s_, d = x.shape
    tn = 1024
    nj = len(tiles)
    if tiles == tuple(range(nj)):
        tile_of = lambda j: j
    else:
        assert tiles == (COL_Q, COL_K, COL_V, COL_UX), tiles
        tile_of = lambda j: jnp.where(j == 3, COL_UX, j)
    row_of = lambda j: tile_of(j) * tn + jnp.where(tile_of(j) >= N_PRE_TILES, N_GATE_COLS, 0)
    per_batch_mod = gain.shape[0] > 1
    mmap = (lambda b, i, j: (b, 0, 0)) if per_batch_mod else (lambda b, i, j: (0, 0, 0))
    gate_blk = N_PRE_TILES * tn // N_GATE_COLS
    return pl.pallas_call(
        _inproj_kernel,
        out_shape=(jax.ShapeDtypeStruct((b_, s_, nj * tn), BF16),
                   jax.ShapeDtypeStruct((b_, s_, LANES), F32)),
        grid=(b_, s_ // tm, nj),
        in_specs=[
            pl.BlockSpec((1, tm, d), lambda b, i, j: (b, i, 0)),
            pl.BlockSpec((1, 1, d), mmap),
            pl.BlockSpec((1, 1, d), mmap),
            pl.BlockSpec((pl.Element(tn), pl.Element(d)), lambda b, i, j: (pl.multiple_of(row_of(j), 8), 0)),
            pl.BlockSpec((N_GATE_COLS, d), lambda b, i, j: (gate_blk, 0)),
            pl.BlockSpec((1, LANES), lambda b, i, j: (0, 0)),
        ],
        out_specs=(pl.BlockSpec((1, tm, tn), lambda b, i, j: (b, i, j)),
                   pl.BlockSpec((1, tm, LANES), lambda b, i, j: (b, i, 0))),
        scratch_shapes=[pltpu.VMEM((tm, d), BF16)],
        compiler_params=_cparams(("parallel", "parallel", "arbitrary")),
        name="inproj",
    )(x, gain, shift, w_t, w_t, b_gates)


COL_Q, COL_K, COL_V, COL_OF, COL_OB, COL_UX, COL_UG, COL_GA, COL_GB = 0, 1, 2, 3, 4, 5, 6, 7, 9
CTX_UX = 3
N_COL_TILES = 11


def _mlstm_kernel(nc, nl, L, dh,
                  qc_f, kc_f, vc_f, ql_f, kl_f, vl_f, gc_f, gl_f, o_f,
                  qc_b, kc_b, vc_b, ql_b, kl_b, vl_b, gc_b, gl_b, o_b,
                  hg_ref, tri_ref,
                  out_f, out_b, c_scr, m_scr):
    s = pl.program_id(1)
    nh = MLSTM_HEADS
    scale = dh ** -0.5

    @pl.when(s == 0)
    def _():
        c_scr[...] = jnp.zeros_like(c_scr)
        m_scr[...] = jnp.full_like(m_scr, -1e30)

    is_ctx = s < nc
    ones_col = (lax.broadcasted_iota(jnp.int32, (L, dh), 1) == 0).astype(BF16)
    row = lax.broadcasted_iota(jnp.int32, (L, L), 0)
    col = lax.broadcasted_iota(jnp.int32, (L, L), 1)

    for d, (qc, kc, vc, ql, kl, vl, gc, gl, o_ref, out_ref) in enumerate((
            (qc_f, kc_f, vc_f, ql_f, kl_f, vl_f, gc_f, gl_f, o_f, out_f),
            (qc_b, kc_b, vc_b, ql_b, kl_b, vl_b, gc_b, gl_b, o_b, out_b))):
        q = jnp.where(is_ctx, qc[0], ql[0])
        k = jnp.where(is_ctx, kc[0], kl[0])
        v = jnp.where(is_ctx, vc[0], vl[0])
        g = jnp.where(is_ctx, gc[0], gl[0])
        gt = g.T
        ls = jnp.minimum(g, 0.0) - jnp.log(1.0 + jnp.exp(-jnp.abs(g)))
        lst = jnp.minimum(gt, 0.0) - jnp.log(1.0 + jnp.exp(-jnp.abs(gt)))
        tri = tri_ref[d]
        mask = (col <= row) if d == 0 else (col >= row)
        l1, l2, l3 = _split3(ls)
        bcol_all = _dot(tri, l1) + _dot(tri, l2) + _dot(tri, l3)
        t1, t2, t3 = _split3(lst)
        trit = tri_ref[1 - d]
        brow_all = _dot(t1, trit) + _dot(t2, trit) + _dot(t3, trit)
        last = L - 1 if d == 0 else 0
        for h in range(nh):
            u = d * nh + h
            ci = 16 * d + h
            cf = 16 * d + 8 + h
            hs = slice(h * dh, (h + 1) * dh)
            qh, kh, vh = q[:, hs], k[:, hs], v[:, hs]
            vext = jnp.concatenate([vh, ones_col], axis=1)
            bc = bcol_all[:, cf:cf + 1]
            ic = g[:, ci:ci + 1]
            cr = gt[ci:ci + 1, :] - brow_all[cf:cf + 1, :]
            tot = bcol_all[last:last + 1, cf:cf + 1]
            m_prev = m_scr[u][0:1, 0:1]
            c_prev = c_scr[u]

            sqk = lax.dot_general(qh, kh, (((1,), (1,)), ((), ())), preferred_element_type=F32) * scale
            dm = jnp.where(mask, bc + cr, -jnp.inf)
            rowmax = jnp.max(dm, axis=-1, keepdims=True)
            gg = bc + m_prev
            mt = jnp.maximum(gg, rowmax)
            w_intra = jnp.exp(dm - mt)
            w_inter = jnp.exp(gg - mt)
            a = (sqk * w_intra).astype(BF16)
            r = _dot(a, vext) + w_inter * _dot(qh, c_prev.astype(BF16))

            num = r[:, :dh]
            den = r[:, dh:dh + 1]
            hh = num / jnp.maximum(jnp.abs(den), jnp.exp(-mt))
            hn = hh * lax.rsqrt(jnp.mean(hh * hh, axis=-1, keepdims=True) + EPS) * hg_ref[h:h + 1, :]
            gate = _sigmoid(o_ref[0, :, hs].astype(F32))
            out_ref[0, :, hs] = (gate * hn).astype(BF16)

            wlog = tot - bc + ic
            m_new = jnp.maximum(tot + m_prev, jnp.max(wlog, axis=0, keepdims=True))
            ws = jnp.exp(wlog - m_new) * scale
            decay = jnp.exp(tot + m_prev - m_new)
            kw = (kh.astype(F32) * ws).astype(BF16)
            upd = lax.dot_general(kw, vext, (((0,), (0,)), ((), ())), preferred_element_type=F32)
            c_scr[u] = decay * c_prev + upd
            m_scr[u] = jnp.broadcast_to(m_new, m_scr.shape[1:])


def _mlstm(u_lat, u_ctx, g_lat, g_ctx, head_g, L):
    b_, s_, _ = u_lat.shape
    n_ctx = u_ctx.shape[1]
    nh = MLSTM_HEADS
    w = 1024
    dh = w // nh
    nc, nl = n_ctx // L, s_ // L
    steps = nc + nl
    t0 = jnp.tril(jnp.ones((L, L), F32))
    tri = jnp.stack([t0, t0.T]).astype(BF16)

    def cidx(d):
        if d == 0:
            return lambda st: jnp.minimum(st, nc - 1), lambda st: jnp.maximum(st - nc, 0)
        return lambda st: jnp.maximum(nc - 1 - st, 0), lambda st: jnp.minimum(nl - 1 + nc - st, nl - 1)

    in_specs, args = [], []
    for d in range(2):
        fc, fl = cidx(d)
        for t in range(3):
            in_specs.append(pl.BlockSpec((1, L, w), lambda b, st, fc=fc, t=t: (b, fc(st), t)))
            args.append(u_ctx)
        for t in (COL_Q, COL_K, COL_V):
            in_specs.append(pl.BlockSpec((1, L, w), lambda b, st, fl=fl, t=t: (b, fl(st), t)))
            args.append(u_lat)
        in_specs.append(pl.BlockSpec((1, L, LANES), lambda b, st, fc=fc: (b, fc(st), 0)))
        args.append(g_ctx)
        in_specs.append(pl.BlockSpec((1, L, LANES), lambda b, st, fl=fl: (b, fl(st), 0)))
        args.append(g_lat)
        in_specs.append(pl.BlockSpec((1, L, w), lambda b, st, fl=fl, t=COL_OF + d: (b, fl(st), t)))
        args.append(u_lat)
    in_specs += [pl.BlockSpec((nh, dh), lambda b, st: (0, 0)),
                 pl.BlockSpec((2, L, L), lambda b, st: (0, 0, 0))]
    args += [head_g, tri]
    out_specs = tuple(pl.BlockSpec((1, L, w), lambda b, st, fl=cidx(d)[1]: (b, fl(st), 0)) for d in range(2))
    return pl.pallas_call(
        functools.partial(_mlstm_kernel, nc, nl, L, dh),
        out_shape=(jax.ShapeDtypeStruct((b_, s_, w), BF16),) * 2,
        grid=(b_, steps),
        in_specs=in_specs,
        out_specs=out_specs,
        scratch_shapes=[pltpu.VMEM((2 * nh, dh, 2 * dh), F32), pltpu.VMEM((2 * nh, 8, LANES), F32)],
        compiler_params=_cparams(("parallel", "arbitrary")),
        name="mlstm",
    )(*args)


def _rglru_kernel(n_ctx, n_tot, ux_ref, cw_ref, cb_ref, w_ref, b_ref, lam_ref, out_ref,
                  a_f, b_f, a_b, b_b, h_f, h_b):
    u = ux_ref[0].astype(F32)
    t_idx = lax.broadcasted_iota(jnp.int32, (n_tot, 1), 0)
    seg_lo = jnp.where(t_idx < n_ctx, 0, n_ctx)
    seg_hi = jnp.where(t_idx < n_ctx, n_ctx, n_tot)
    z = jnp.zeros_like(u) + cb_ref[...]
    for j, off in enumerate((-2, -1, 0, 1)):
        if off == 0:
            src = u
        else:
            src = pltpu.roll(u, (-off) % n_tot, 0)
            ok = (t_idx + off >= seg_lo) & (t_idx + off < seg_hi)
            src = jnp.where(ok, src, 0.0)
        z = z + cw_ref[j:j + 1, :] * src
    p = _dot(z.astype(BF16), w_ref[0]) + b_ref[...]
    c = LANES
    for d, (a_scr, b_scr) in enumerate(((a_f, b_f), (a_b, b_b))):
        r = _sigmoid(p[:, (2 * d) * c:(2 * d + 1) * c])
        i = _sigmoid(p[:, (2 * d + 1) * c:(2 * d + 2) * c])
        lam = lam_ref[d:d + 1, :]
        softplus = jnp.maximum(-lam, 0.0) + jnp.log(1.0 + jnp.exp(-jnp.abs(lam)))
        log_a = -LRU_C * r * softplus
        a = jnp.exp(log_a)
        a_scr[...] = a
        b_scr[...] = jnp.sqrt(1.0 - a * a) * i * z

    sub = lax.broadcasted_iota(jnp.int32, (8, c), 0)
    ng = n_tot // 8
    ngc = n_ctx // 8

    def group_scan(a, b, carry, reverse):
        for sft in (1, 2, 4):
            if reverse:
                a_s = pltpu.roll(a, 8 - sft, 0)
                b_s = pltpu.roll(b, 8 - sft, 0)
                ok = sub < 8 - sft
            else:
                a_s = pltpu.roll(a, sft, 0)
                b_s = pltpu.roll(b, sft, 0)
                ok = sub >= sft
            b = jnp.where(ok, a * b_s + b, b)
            a = jnp.where(ok, a * a_s, a)
        h = b + a * carry
        return h, (h[0:1, :] if reverse else h[7:8, :])

    def body(it, carry):
        cf, cb = carry
        rf = pl.multiple_of(it * 8, 8)
        hf, cf = group_scan(a_f[pl.ds(rf, 8), :], b_f[pl.ds(rf, 8), :], cf, False)
        h_f[pl.ds(rf, 8), :] = hf
        jb = jnp.where(it < ngc, ngc - 1 - it, ng - 1 + ngc - it)
        rb = pl.multiple_of(jb * 8, 8)
        hb, cb = group_scan(a_b[pl.ds(rb, 8), :], b_b[pl.ds(rb, 8), :], cb, True)
        h_b[pl.ds(rb, 8), :] = hb
        return cf, cb

    zero = jnp.zeros((1, c), F32)
    lax.fori_loop(0, ng, body, (zero, zero), unroll=4)
    out_ref[0] = (h_f[n_ctx:, :] + h_b[n_ctx:, :]).astype(BF16)


def _rglru(ux_seq, conv_w, conv_b, w_cat, b_cat, lam, n_ctx):
    b_, n_tot, w = ux_seq.shape
    nb = LRU_BLOCKS
    c = w // nb
    return pl.pallas_call(
        functools.partial(_rglru_kernel, n_ctx, n_tot),
        out_shape=jax.ShapeDtypeStruct((b_, n_tot - n_ctx, w), BF16),
        grid=(b_, nb),
        in_specs=[
            pl.BlockSpec((1, n_tot, c), lambda b, k: (b, 0, k)),
            pl.BlockSpec((4, c), lambda b, k: (0, k)),
            pl.BlockSpec((1, c), lambda b, k: (0, k)),
            pl.BlockSpec((1, c, 4 * c), lambda b, k: (k, 0, 0)),
            pl.BlockSpec((1, 4 * c), lambda b, k: (0, k)),
            pl.BlockSpec((2, c), lambda b, k: (0, k)),
        ],
        out_specs=pl.BlockSpec((1, n_tot - n_ctx, c), lambda b, k: (b, 0, k)),
        scratch_shapes=[pltpu.VMEM((n_tot, c), F32)] * 6,
        compiler_params=_cparams(("parallel", "parallel")),
        name="rglru",
    )(ux_seq, conv_w, conv_b, w_cat, b_cat, lam)


def _outproj_kernel(yaf_ref, yab_ref, hr_ref, ug_ref, ga0_ref, ga1_ref, gb0_ref, gb1_ref, x_ref,
                    g1_ref, gain_ref, shift_ref, wpa_ref, wpb_ref, wout_ref, wr_ref,
                    x1_ref, h2_ref, lg_ref):
    ya = (yaf_ref[0].astype(F32) + yab_ref[0].astype(F32)).astype(BF16)
    ug = ug_ref[0].astype(F32)
    gelu = 0.5 * ug * (1.0 + jnp.tanh(0.7978845608028654 * (ug + 0.044715 * ug * ug * ug)))
    yb = (hr_ref[0].astype(F32) * gelu).astype(BF16)
    pa = _dot(ya, wpa_ref[...])
    pb = _dot(yb, wpb_ref[...])
    ga = jnp.concatenate([ga0_ref[0], ga1_ref[0]], axis=1).astype(F32)
    gb = jnp.concatenate([gb0_ref[0], gb1_ref[0]], axis=1).astype(F32)
    mix = (_sigmoid(ga) * pa + _sigmoid(gb) * pb).astype(BF16)
    x1 = x_ref[0] + g1_ref[0] * _dot(mix, wout_ref[...])
    x1_ref[0] = x1
    ms = jnp.mean(x1 * x1, axis=-1, keepdims=True)
    h2 = x1 * lax.rsqrt(ms + EPS) * gain_ref[0] + shift_ref[0]
    ns = h2.shape[1] // LANES
    for s in range(ns):
        h2_ref[pl.ds(s, h2.shape[0], stride=ns), :] = h2[:, s * LANES:(s + 1) * LANES]
    wr = wr_ref[...]
    h1, h2b, _ = _split3(h2)
    w1, w2b, _ = _split3(wr)
    lg_ref[0] = _dot(h1, w1) + (_dot(h1, w2b) + _dot(h2b, w1))


def _outproj(ya_f, ya_b, hr, u_lat, x, g1, gain2, shift2, w_pa, w_pb, w_out, w_r, tm):
    b_, s_, d = x.shape
    w = ya_f.shape[2]
    row = lambda b, i: (b, i, 0)
    const2 = lambda b, i: (0, 0)
    bmap = lambda b, i: (b, 0, 0)
    ucol = lambda t: (lambda b, i: (b, i, t))
    single = pl.Buffered(1)
    return pl.pallas_call(
        _outproj_kernel,
        out_shape=(jax.ShapeDtypeStruct((b_, s_, d), F32),
                   jax.ShapeDtypeStruct((b_ * s_ * (d // LANES), LANES), F32),
                   jax.ShapeDtypeStruct((b_, s_, LANES), F32)),
        grid=(b_, s_ // tm),
        in_specs=[
            pl.BlockSpec((1, tm, w), row), pl.BlockSpec((1, tm, w), row), pl.BlockSpec((1, tm, w), row),
            pl.BlockSpec((1, tm, w), ucol(COL_UG)),
            pl.BlockSpec((1, tm, w), ucol(COL_GA)), pl.BlockSpec((1, tm, w), ucol(COL_GA + 1)),
            pl.BlockSpec((1, tm, w), ucol(COL_GB)), pl.BlockSpec((1, tm, w), ucol(COL_GB + 1)),
            pl.BlockSpec((1, tm, d), row),
            pl.BlockSpec((1, 1, d), bmap), pl.BlockSpec((1, 1, d), bmap), pl.BlockSpec((1, 1, d), bmap),
            pl.BlockSpec((w, d), const2, pipeline_mode=single),
            pl.BlockSpec((w, d), const2, pipeline_mode=single),
            pl.BlockSpec((d, d), const2, pipeline_mode=single),
            pl.BlockSpec((d, LANES), const2, pipeline_mode=single),
        ],
        out_specs=(pl.BlockSpec((1, tm, d), row),
                   pl.BlockSpec((tm * (d // LANES), LANES), lambda b, i: (b * (s_ // tm) + i, 0)),
                   pl.BlockSpec((1, tm, LANES), row)),
        compiler_params=_cparams(("parallel", "parallel")),
        name="outproj",
    )(ya_f, ya_b, hr, u_lat, u_lat, u_lat, u_lat, u_lat, x, g1, gain2, shift2, w_pa, w_pb, w_out, w_r)


def _route_kernel(lg_ref, o_ref):
    lg = lg_ref[...]
    lane = lax.broadcasted_iota(jnp.int32, lg.shape, 1)
    neg = -jnp.inf
    big = jnp.int32(1 << 20)
    g_l = jnp.where(lane < N_GROUPS, lg, neg)
    g_max = jnp.max(g_l, axis=-1, keepdims=True)
    g_sel = jnp.min(jnp.where(g_l == g_max, lane, big), axis=-1, keepdims=True)
    p_g = 1.0 / jnp.sum(jnp.exp(g_l - g_max), axis=-1, keepdims=True)
    lo = N_GROUPS + g_sel * EXPERTS_PER_GROUP
    e_l = jnp.where((lane >= lo) & (lane < lo + EXPERTS_PER_GROUP), lg, neg)
    v1 = jnp.max(e_l, axis=-1, keepdims=True)
    i1 = jnp.min(jnp.where(e_l == v1, lane, big), axis=-1, keepdims=True)
    e_l2 = jnp.where(lane == i1, neg, e_l)
    v2 = jnp.max(e_l2, axis=-1, keepdims=True)
    i2 = jnp.min(jnp.where(e_l2 == v2, lane, big), axis=-1, keepdims=True)
    e2 = jnp.exp(v2 - v1)
    w1 = p_g / (1.0 + e2)
    w2 = p_g * e2 / (1.0 + e2)
    out = jnp.where(lane == 0, (i1 - N_GROUPS).astype(F32),
          jnp.where(lane == 1, (i2 - N_GROUPS).astype(F32),
          jnp.where(lane == 2, w1, jnp.where(lane == 3, w2, 0.0))))
    o_ref[...] = out


def _route(logits, tm):
    t_, _ = logits.shape
    return pl.pallas_call(
        _route_kernel,
        out_shape=jax.ShapeDtypeStruct((t_, LANES), F32),
        grid=(t_ // tm,),
        in_specs=[pl.BlockSpec((tm, LANES), lambda i: (i, 0))],
        out_specs=pl.BlockSpec((tm, LANES), lambda i: (i, 0)),
        compiler_params=_cparams(("parallel",)),
        name="route",
    )(logits)


MOE_DMA_UNROLL = 32


MOE_W_DEPTH = 3
MOE_W_SPLIT = 2


def _moe_kernel(R, SUB, nf, ni, item_e, item_rows, tok_ref, tokn_ref, dst_ref, h2_hbm, w1_hbm, w3_hbm, w2_hbm,
                y_hbm, x_scr, xb_scr, acc_scr, y_scr, w1r, w3r, w2r, w1b, w3b, w2b, gsem, ssem, wsem):
    i = pl.program_id(0)
    j = pl.program_id(1)
    nsub_of = lambda r: (r + SUB - 1) // SUB
    rows = item_rows[i]
    nsub = nsub_of(rows)
    nsub_next = jnp.where(i + 1 < ni, nsub_of(item_rows[jnp.minimum(i + 1, ni - 1)]), 0)
    rows_prev = jnp.where(i > 0, item_rows[jnp.maximum(i - 1, 0)], 0)
    buf = i % 2
    groups = SUB // MOE_DMA_UNROLL

    d_model, tf = w1r.shape[1], w1r.shape[2]

    def weight_copies(c):
        it = c // nf
        jc = c % nf
        e = item_e[jnp.minimum(it, ni - 1)]
        slot = c % MOE_W_DEPTH
        f0 = pl.multiple_of(jc * tf, tf)
        out = []
        for h in range(MOE_W_SPLIT):
            ra, rb = d_model // MOE_W_SPLIT, tf // MOE_W_SPLIT
            out.append(pltpu.make_async_copy(w1_hbm.at[e, pl.ds(h * ra, ra), pl.ds(f0, tf)],
                                             w1r.at[slot, pl.ds(h * ra, ra), :], wsem.at[slot]))
            out.append(pltpu.make_async_copy(w3_hbm.at[e, pl.ds(h * ra, ra), pl.ds(f0, tf)],
                                             w3r.at[slot, pl.ds(h * ra, ra), :], wsem.at[slot]))
            out.append(pltpu.make_async_copy(w2_hbm.at[e, pl.ds(f0 + h * rb, rb), :],
                                             w2r.at[slot, pl.ds(h * rb, rb), :], wsem.at[slot]))
        return out

    def chunk_live(c):
        it = c // nf
        return (it < ni) & (item_rows[jnp.minimum(it, ni - 1)] > 0)

    cur = i * nf + j

    @pl.when((cur == 0) & chunk_live(0))
    def _():
        for cp in weight_copies(0):
            cp.start()

        @pl.when(chunk_live(1))
        def _():
            for cp in weight_copies(1):
                cp.start()

    @pl.when((rows > 0) & chunk_live(cur + 2))
    def _():
        for cp in weight_copies(cur + 2):
            cp.start()

    ns = acc_scr.shape[1] // LANES

    def tok_rows(ref, r):
        return ref.at[pl.ds(pl.multiple_of(r * ns, ns), ns), :]

    def gather_block(tref, b, q):
        def grp(gi, c):
            base = pl.multiple_of(q * SUB + gi * MOE_DMA_UNROLL, MOE_DMA_UNROLL)
            for k in range(MOE_DMA_UNROLL):
                tok = tref[0, 0, base + k]
                pltpu.make_async_copy(tok_rows(h2_hbm, tok), tok_rows(x_scr.at[b], base + k),
                                      gsem.at[b]).start(priority=1)
            return c
        lax.fori_loop(0, groups, grp, 0)

    def gather_wait_block(b):
        n = SUB * ns
        pltpu.make_async_copy(h2_hbm.at[pl.ds(0, n), :], x_scr.at[b, pl.ds(0, n), :], gsem.at[b]).wait()

    def scatter_copy(r):
        dst = dst_ref[0, 0, r]
        return pltpu.make_async_copy(tok_rows(y_scr, r), tok_rows(y_hbm, dst), ssem)

    def repeat(n, fn, lo=0):
        def body(q, c):
            fn(q)
            return c
        lax.fori_loop(lo, n, body, 0)

    def scatter_rows(n):
        def grp(gi):
            base = pl.multiple_of(gi * MOE_DMA_UNROLL, MOE_DMA_UNROLL)
            for k in range(MOE_DMA_UNROLL):
                scatter_copy(base + k).start(priority=k % 2)
        full = n // MOE_DMA_UNROLL
        repeat(full, grp)
        repeat(n, lambda r: scatter_copy(r).start(), lo=full * MOE_DMA_UNROLL)

    def scatter_wait_rows(n):
        u = MOE_DMA_UNROLL * ns
        full = n // MOE_DMA_UNROLL
        repeat(full, lambda q: pltpu.make_async_copy(y_scr.at[pl.ds(0, u), :], y_hbm.at[pl.ds(0, u), :],
                                                      ssem).wait())
        repeat(n, lambda r: scatter_copy(0).wait(), lo=full * MOE_DMA_UNROLL)

    @pl.when((i == 0) & (j == 0))
    def _():
        repeat(nsub, lambda q: gather_block(tok_ref, 0, q))

    @pl.when(j < nsub_next)
    def _():
        gather_block(tokn_ref, 1 - buf, j)

    @pl.when(j == 0)
    def _():
        repeat(nsub, lambda q: gather_wait_block(buf))

    @pl.when(j == nf - 1)
    def _():
        scatter_wait_rows(rows_prev)

    @pl.when(rows > 0)
    def _():
        for cp in weight_copies(cur):
            cp.wait()
        slot = cur % MOE_W_DEPTH
        w1b[...] = w1r[slot].astype(BF16)
        w3b[...] = w3r[slot].astype(BF16)
        w2b[...] = w2r[slot].astype(BF16)

        @pl.when(j == 0)
        def _():
            def unsplit(sb, c):
                r0 = pl.multiple_of(sb * SUB, SUB)
                s0 = pl.multiple_of(sb * (SUB * ns), SUB * ns)
                xb_scr[pl.ds(r0, SUB), :] = jnp.concatenate(
                    [x_scr[buf, pl.ds(s0 + s, SUB, stride=ns), :] for s in range(ns)], axis=1).astype(BF16)
                return c
            lax.fori_loop(0, nsub, unsplit, 0)

        def sub(sb, c):
            r0 = pl.multiple_of(sb * SUB, SUB)
            s0 = pl.multiple_of(sb * (SUB * ns), SUB * ns)
            x = xb_scr[pl.ds(r0, SUB), :]
            h1 = _dot(x, w1b[...])
            h3 = _dot(x, w3b[...])
            hh = (h1 * _sigmoid(h1) * h3).astype(BF16)
            contrib = _dot(hh, w2b[...])

            @pl.when(j == 0)
            def _():
                acc_scr[pl.ds(r0, SUB), :] = contrib

            @pl.when((j > 0) & (j < nf - 1))
            def _():
                acc_scr[pl.ds(r0, SUB), :] += contrib

            @pl.when(j == nf - 1)
            def _():
                tot = acc_scr[pl.ds(r0, SUB), :] + contrib
                for s in range(ns):
                    y_scr[pl.ds(s0 + s, SUB, stride=ns), :] = tot[:, s * LANES:(s + 1) * LANES]
            return c
        lax.fori_loop(0, nsub, sub, 0)

    @pl.when(j == nf - 1)
    def _():
        scatter_rows(rows)

        @pl.when(i == ni - 1)
        def _():
            scatter_wait_rows(rows)


def _moe(h2s, item_e, item_rows, slot_tok, slot_dst, w1, w3, w2, n_out_rows):
    ne, d, f = w1.shape
    ns = d // LANES
    R, SUB, tf = MOE_ITEM_ROWS, MOE_SUB_ROWS, MOE_F_CHUNK
    ni = item_e.shape[0]
    nf = f // tf
    assert R % SUB == 0 and R <= nf * SUB and nf >= 2
    grid_spec = pltpu.PrefetchScalarGridSpec(
        num_scalar_prefetch=2,
        grid=(ni, nf),
        in_specs=[
            pl.BlockSpec((1, 1, R), lambda i, j, ie, ir: (i, 0, 0), memory_space=pltpu.SMEM),
            pl.BlockSpec((1, 1, R), lambda i, j, ie, ir: (jnp.minimum(i + 1, ni - 1), 0, 0),
                         memory_space=pltpu.SMEM),
            pl.BlockSpec((1, 1, R), lambda i, j, ie, ir: (i, 0, 0), memory_space=pltpu.SMEM),
            pl.BlockSpec(memory_space=pl.ANY),
            pl.BlockSpec(memory_space=pl.ANY),
            pl.BlockSpec(memory_space=pl.ANY),
            pl.BlockSpec(memory_space=pl.ANY),
        ],
        out_specs=pl.BlockSpec(memory_space=pl.ANY),
        scratch_shapes=[
            pltpu.VMEM((2, R * ns, LANES), F32), pltpu.VMEM((R, d), BF16),
            pltpu.VMEM((R, d), F32), pltpu.VMEM((R * ns, LANES), F32),
            pltpu.VMEM((MOE_W_DEPTH, d, tf), F32), pltpu.VMEM((MOE_W_DEPTH, d, tf), F32),
            pltpu.VMEM((MOE_W_DEPTH, tf, d), F32),
            pltpu.VMEM((d, tf), BF16), pltpu.VMEM((d, tf), BF16), pltpu.VMEM((tf, d), BF16),
            pltpu.SemaphoreType.DMA((2,)), pltpu.SemaphoreType.DMA(()), pltpu.SemaphoreType.DMA((MOE_W_DEPTH,)),
        ],
    )
    return pl.pallas_call(
        functools.partial(_moe_kernel, R, SUB, nf, ni),
        out_shape=jax.ShapeDtypeStruct((n_out_rows * ns, LANES), F32),
        grid_spec=grid_spec,
        compiler_params=_cparams(("arbitrary", "arbitrary")),
        name="moe_experts",
    )(item_e, item_rows, slot_tok, slot_tok, slot_dst, h2s, w1, w3, w2)


def _moe_plan(expert, t_):
    R = MOE_ITEM_ROWS
    na = expert.shape[0]
    ni = N_EXPERTS + na // R
    onehot = (expert[:, None] == jnp.arange(N_EXPERTS, dtype=jnp.int32)[None, :]).astype(jnp.int32)
    csum = jnp.cumsum(onehot, axis=0)
    rank = jnp.sum(csum * onehot, axis=1) - 1
    counts = csum[-1]
    n_items = (counts + R - 1) // R
    item_end = jnp.cumsum(n_items)
    item_start = item_end - n_items
    slot = jnp.sum(item_start[None, :] * onehot, axis=1) * R + rank
    slot_a = jnp.full((ni * R,), -1, jnp.int32).at[slot].set(jnp.arange(na, dtype=jnp.int32))
    valid = slot_a >= 0
    slot_tok = jnp.where(valid, slot_a // TOP_K, 0)
    slot_dst = jnp.where(valid, (slot_a % TOP_K) * t_ + slot_a // TOP_K, 0)
    ii = jnp.arange(ni, dtype=jnp.int32)
    e_of = jnp.minimum(jnp.sum((item_end[None, :] <= ii[:, None]).astype(jnp.int32), axis=1), N_EXPERTS - 1)
    oh_e = (e_of[:, None] == jnp.arange(N_EXPERTS, dtype=jnp.int32)[None, :]).astype(jnp.int32)
    cnt_e = jnp.sum(oh_e * counts[None, :], axis=1)
    start_e = jnp.sum(oh_e * item_start[None, :], axis=1)
    rows = jnp.clip(cnt_e - (ii - start_e) * R, 0, R).astype(jnp.int32)
    live = ii < item_end[-1]
    rows = jnp.where(live, rows, 0)
    last_e = jnp.max(jnp.where(live, e_of, 0))
    e_of = jnp.where(live, e_of, last_e).astype(jnp.int32)
    return e_of, rows, slot_tok.reshape(ni, 1, R), slot_dst.reshape(ni, 1, R)


def _final_kernel(x1_ref, y0_ref, y1_ref, rt_ref, g2_ref, fg_ref, o_ref):
    rt = rt_ref[0]
    tm = rt.shape[0]
    ns = y0_ref.shape[0] // tm
    unsplit = lambda ref: jnp.concatenate([ref[pl.ds(s, tm, stride=ns), :] for s in range(ns)], axis=1)
    y = rt[:, TOP_K:TOP_K + 1] * unsplit(y0_ref) + rt[:, TOP_K + 1:TOP_K + 2] * unsplit(y1_ref)
    x = x1_ref[0] + g2_ref[0] * y
    ms = jnp.mean(x * x, axis=-1, keepdims=True)
    o_ref[0] = x * lax.rsqrt(ms + EPS) * fg_ref[...]


def _final(x1, ybuf, routed, g2, final_g, tm):
    b_, s_, d = x1.shape
    nt = s_ // tm
    ns = d // LANES
    return pl.pallas_call(
        _final_kernel,
        out_shape=jax.ShapeDtypeStruct((b_, s_, d), F32),
        grid=(b_, nt),
        in_specs=[
            pl.BlockSpec((1, tm, d), lambda b, i: (b, i, 0)),
            pl.BlockSpec((tm * ns, LANES), lambda b, i: (b * nt + i, 0)),
            pl.BlockSpec((tm * ns, LANES), lambda b, i: (b_ * nt + b * nt + i, 0)),
            pl.BlockSpec((1, tm, LANES), lambda b, i: (b, i, 0)),
            pl.BlockSpec((1, 1, d), lambda b, i: (b, 0, 0)),
            pl.BlockSpec((1, d), lambda b, i: (0, 0)),
        ],
        out_specs=pl.BlockSpec((1, tm, d), lambda b, i: (b, i, 0)),
        compiler_params=_cparams(("parallel", "parallel")),
        name="final_norm",
    )(x1, ybuf, ybuf, routed, g2, final_g)


def kernel(x, c, ctx, c_ctx, w_mod, b_mod, norm1_g, w_in, b_gates, mlstm_head_g, conv_w, conv_b, lru_wa, lru_ba,
           lru_wx, lru_bx, lru_lam, w_pa, w_pb, w_out, norm2_g, w_rg, w_re, w1, w3, w2, final_g):
    b_, s_, d = x.shape
    n_ctx = ctx.shape[1]
    rows = s_ // GRID_W
    assert w_mod.shape[0] == 1, "single layer"
    wm = d // 2
    nh = MLSTM_HEADS

    cc = jnp.zeros((8, d), F32).at[:b_].set(c).at[b_].set(c_ctx)
    mod = _modulation(cc, w_mod[0], b_mod[0][None, :])
    sh1, sc1, g1, sh2, sc2, g2 = [mod[:b_, i * d:(i + 1) * d][:, None, :] for i in range(6)]
    csh1, csc1 = mod[b_:b_ + 1, 0:d][:, None, :], mod[b_:b_ + 1, d:2 * d][:, None, :]

    assert wm == 1024 and 4 * nh == N_GATE_COLS
    w_t = jnp.transpose(w_in[0])
    bg_pad = jnp.zeros((1, LANES), F32).at[0, :4 * nh].set(b_gates[0])

    ng = norm1_g[0][None, None, :]
    u_lat, g_lat = _inproj(x, ng * (1.0 + sc1), sh1, w_t, bg_pad, tuple(range(N_COL_TILES)), min(1024, s_))
    u_ctx, g_ctx = _inproj(ctx, ng * (1.0 + csc1), csh1, w_t, bg_pad, (COL_Q, COL_K, COL_V, COL_UX), n_ctx)

    ya_f, ya_b = _mlstm(u_lat, u_ctx, g_lat, g_ctx, mlstm_head_g[0], MLSTM_CHUNK)

    ux_lat = u_lat[:, :, COL_UX * wm:(COL_UX + 1) * wm]
    ux_col = ux_lat.reshape(b_, rows, GRID_W, wm).transpose(0, 2, 1, 3).reshape(b_, s_, wm)
    ux_seq = jnp.concatenate([u_ctx[:, :, CTX_UX * wm:(CTX_UX + 1) * wm], ux_col], axis=1)
    wa, wx = lru_wa[0], lru_wx[0]
    w_cat = jnp.concatenate([wa[0], wx[0], wa[1], wx[1]], axis=-1).astype(BF16)
    cblk = wm // LRU_BLOCKS
    blk = lambda v: v.reshape(LRU_BLOCKS, cblk)
    b_cat = jnp.concatenate([blk(lru_ba[0, 0]), blk(lru_bx[0, 0]), blk(lru_ba[0, 1]), blk(lru_bx[0, 1])],
                            axis=-1).reshape(1, 4 * wm)
    h_col = _rglru(ux_seq, conv_w[0], conv_b[0][None, :], w_cat, b_cat, lru_lam[0], n_ctx)
    hr = h_col.reshape(b_, GRID_W, rows, wm).transpose(0, 2, 1, 3).reshape(b_, s_, wm)

    w_r = jnp.zeros((d, LANES), F32).at[:, :N_GROUPS].set(w_rg[0]).at[:, N_GROUPS:N_GROUPS + N_EXPERTS].set(w_re[0])
    n2 = norm2_g[0][None, None, :]
    x1, h2, logits = _outproj(ya_f, ya_b, hr, u_lat, x, g1, n2 * (1.0 + sc2), sh2,
                              w_pa[0].astype(BF16), w_pb[0].astype(BF16), w_out[0].astype(BF16), w_r, 256)

    t_ = b_ * s_
    routed = _route(logits.reshape(t_, LANES), 512)
    expert = routed[:, :TOP_K].astype(jnp.int32).reshape(-1)
    item_e, item_rows, slot_tok, slot_dst = _moe_plan(expert, t_)
    ybuf = _moe(h2, item_e, item_rows, slot_tok, slot_dst, w1[0], w3[0], w2[0], TOP_K * t_)

    return _final(x1, ybuf, routed.reshape(b_, s_, LANES), g2, final_g[None, :], 256)
```

```python
import functools
import math

import jax
import jax.numpy as jnp
from jax import lax
from jax.experimental import pallas as pl
from jax.experimental.pallas import tpu as pltpu

F32 = jnp.float32
BF16 = jnp.bfloat16

EPS = 1e-6
GRID_W = 64
MLSTM_HEADS = 8
LRU_BLOCKS = 8
LRU_C = 8.0
N_GROUPS = 4
EXPERTS_PER_GROUP = 8
N_EXPERTS = N_GROUPS * EXPERTS_PER_GROUP
TOP_K = 2

VMEM_LIMIT_BYTES = 56 * 1024 * 1024
LANES = 128

MLSTM_CHUNK = 128
MOE_ITEM_ROWS = 768
MOE_SUB_ROWS = 256
MOE_F_CHUNK = 256


def _cparams(sem):
    return pltpu.CompilerParams(dimension_semantics=sem, vmem_limit_bytes=VMEM_LIMIT_BYTES)


def _sigmoid(x):
    return 1.0 / (1.0 + jnp.exp(-x))


def _dot(a, b):
    return jnp.dot(a, b, preferred_element_type=F32)


def _split3(x):
    x1 = x.astype(BF16)
    r1 = x - x1.astype(F32)
    x2 = r1.astype(BF16)
    x3 = (r1 - x2.astype(F32)).astype(BF16)
    return x1, x2, x3


def _mod_kernel(c_ref, w_ref, b_ref, o_ref):
    c = c_ref[...]
    s = (c * _sigmoid(c)).astype(BF16)
    o_ref[...] = _dot(s, w_ref[...].astype(BF16)) + b_ref[...]


def _modulation(cc, w_mod, b_mod):
    m, d = cc.shape
    n = w_mod.shape[1]
    tn = 1024
    return pl.pallas_call(
        _mod_kernel,
        out_shape=jax.ShapeDtypeStruct((m, n), F32),
        grid=(n // tn,),
        in_specs=[
            pl.BlockSpec((m, d), lambda j: (0, 0)),
            pl.BlockSpec((d, tn), lambda j: (0, j)),
            pl.BlockSpec((1, tn), lambda j: (0, j)),
        ],
        out_specs=pl.BlockSpec((m, tn), lambda j: (0, j)),
        compiler_params=_cparams(("parallel",)),
        name="modulation",
    )(cc, w_mod, b_mod)


N_PRE_TILES = 5
N_GATE_COLS = 32


def _dot_nt(a, b):
    return lax.dot_general(a, b, (((1,), (1,)), ((), ())), preferred_element_type=F32)


def _inproj_kernel(x_ref, gain_ref, shift_ref, wt_ref, wg_ref, bg_ref, u_ref, g_ref, h_scr):
    @pl.when(pl.program_id(2) == 0)
    def _():
        x = x_ref[0]
        ms = jnp.mean(x * x, axis=-1, keepdims=True)
        h = x * lax.rsqrt(ms + EPS) * gain_ref[0] + shift_ref[0]
        hb = h.astype(BF16)
        h_scr[...] = hb
        gates = _dot_nt(hb, wg_ref[...].astype(BF16))
        pad = jnp.zeros((gates.shape[0], LANES - N_GATE_COLS), F32)
        g_ref[0] = jnp.concatenate([gates, pad], axis=1) + bg_ref[...]

    u_ref[0] = _dot_nt(h_scr[...], wt_ref[...].astype(BF16)).astype(BF16)


def _inproj(x, gain, shift, w_t, b_gates, tiles, tm):
    b_, s_, d = x.shape
    tn = 1024
    nj = len(tiles)
    if tiles == tuple(range(nj)):
        tile_of = lambda j: j
    else:
        assert tiles == (COL_Q, COL_K, COL_V, COL_UX), tiles
        tile_of = lambda j: jnp.where(j == 3, COL_UX, j)
    row_of = lambda j: tile_of(j) * tn + jnp.where(tile_of(j) >= N_PRE_TILES, N_GATE_COLS, 0)
    per_batch_mod = gain.shape[0] > 1
    mmap = (lambda b, i, j: (b, 0, 0)) if per_batch_mod else (lambda b, i, j: (0, 0, 0))
    gate_blk = N_PRE_TILES * tn // N_GATE_COLS
    return pl.pallas_call(
        _inproj_kernel,
        out_shape=(jax.ShapeDtypeStruct((b_, s_, nj * tn), BF16),
                   jax.ShapeDtypeStruct((b_, s_, LANES), F32)),
        grid=(b_, s_ // tm, nj),
        in_specs=[
            pl.BlockSpec((1, tm, d), lambda b, i, j: (b, i, 0)),
            pl.BlockSpec((1, 1, d), mmap),
            pl.BlockSpec((1, 1, d), mmap),
            pl.BlockSpec((pl.Element(tn), pl.Element(d)), lambda b, i, j: (pl.multiple_of(row_of(j), 8), 0)),
            pl.BlockSpec((N_GATE_COLS, d), lambda b, i, j: (gate_blk, 0)),
            pl.BlockSpec((1, LANES), lambda b, i, j: (0, 0)),
        ],
        out_specs=(pl.BlockSpec((1, tm, tn), lambda b, i, j: (b, i, j)),
                   pl.BlockSpec((1, tm, LANES), lambda b, i, j: (b, i, 0))),
        scratch_shapes=[pltpu.VMEM((tm, d), BF16)],
        compiler_params=_cparams(("parallel", "parallel", "arbitrary")),
        name="inproj",
    )(x, gain, shift, w_t, w_t, b_gates)


COL_Q, COL_K, COL_V, COL_OF, COL_OB, COL_UX, COL_UG, COL_GA, COL_GB = 0, 1, 2, 3, 4, 5, 6, 7, 9
CTX_UX = 3
N_COL_TILES = 11


def _mlstm_kernel(nc, nl, L, dh,
                  qc_f, kc_f, vc_f, ql_f, kl_f, vl_f, gc_f, gl_f, o_f,
                  qc_b, kc_b, vc_b, ql_b, kl_b, vl_b, gc_b, gl_b, o_b,
                  hg_ref, tri_ref,
                  out_f, out_b, c_scr, m_scr):
    s = pl.program_id(1)
    nh = MLSTM_HEADS
    scale = dh ** -0.5

    @pl.when(s == 0)
    def _():
        c_scr[...] = jnp.zeros_like(c_scr)
        m_scr[...] = jnp.full_like(m_scr, -1e30)

    is_ctx = s < nc
    ones_col = (lax.broadcasted_iota(jnp.int32, (L, dh), 1) == 0).astype(BF16)
    row = lax.broadcasted_iota(jnp.int32, (L, L), 0)
    col = lax.broadcasted_iota(jnp.int32, (L, L), 1)

    for d, (qc, kc, vc, ql, kl, vl, gc, gl, o_ref, out_ref) in enumerate((
            (qc_f, kc_f, vc_f, ql_f, kl_f, vl_f, gc_f, gl_f, o_f, out_f),
            (qc_b, kc_b, vc_b, ql_b, kl_b, vl_b, gc_b, gl_b, o_b, out_b))):
        q = jnp.where(is_ctx, qc[0], ql[0])
        k = jnp.where(is_ctx, kc[0], kl[0])
        v = jnp.where(is_ctx, vc[0], vl[0])
        g = jnp.where(is_ctx, gc[0], gl[0])
        gt = g.T
        ls = jnp.minimum(g, 0.0) - jnp.log(1.0 + jnp.exp(-jnp.abs(g)))
        lst = jnp.minimum(gt, 0.0) - jnp.log(1.0 + jnp.exp(-jnp.abs(gt)))
        tri = tri_ref[d]
        mask = (col <= row) if d == 0 else (col >= row)
        l1, l2, l3 = _split3(ls)
        bcol_all = _dot(tri, l1) + _dot(tri, l2) + _dot(tri, l3)
        t1, t2, t3 = _split3(lst)
        trit = tri_ref[1 - d]
        brow_all = _dot(t1, trit) + _dot(t2, trit) + _dot(t3, trit)
        last = L - 1 if d == 0 else 0

        bal = pltpu.roll(bcol_all, LANES - 8, axis=1)
        cmax = g - bal
        trow = lax.broadcasted_iota(jnp.int32, (L, LANES), 0)
        sft = 1
        while sft < L:
            if d == 0:
                shifted = jnp.where(trow >= sft, pltpu.roll(cmax, sft, axis=0), -jnp.inf)
            else:
                shifted = jnp.where(trow < L - sft, pltpu.roll(cmax, L - sft, axis=0), -jnp.inf)
            cmax = jnp.maximum(cmax, shifted)
            sft *= 2
        m_row = m_scr[d][0:1, :]
        gg = bal + m_row
        mt = jnp.maximum(gg, bal + cmax)
        e_col = bal - mt
        w_inter = jnp.exp(gg - mt)
        enm = jnp.exp(-mt)
        tot = bal[last:last + 1, :]
        wlog = tot - bal + g
        m_new = jnp.maximum(tot + m_row, jnp.max(wlog, axis=0, keepdims=True))
        ws = jnp.exp(wlog - m_new) * scale
        decay = jnp.exp(tot + m_row - m_new)
        m_scr[d] = jnp.broadcast_to(m_new, m_scr.shape[1:])

        for h in range(nh):
            u = d * nh + h
            ci = 16 * d + h
            cf = 16 * d + 8 + h
            hs = slice(h * dh, (h + 1) * dh)
            qh, kh, vh = q[:, hs], k[:, hs], v[:, hs]
            vext = jnp.concatenate([vh, ones_col], axis=1)
            cr = gt[ci:ci + 1, :] - brow_all[cf:cf + 1, :] + math.log(scale)
            c_prev = c_scr[u]

            sqk = lax.dot_general(qh, kh, (((1,), (1,)), ((), ())), preferred_element_type=F32)
            w_intra = jnp.exp(jnp.where(mask, e_col[:, ci:ci + 1] + cr, -jnp.inf))
            a = (sqk * w_intra).astype(BF16)
            qs = (qh.astype(F32) * w_inter[:, ci:ci + 1]).astype(BF16)
            r = _dot(jnp.concatenate([a, qs], axis=1),
                     jnp.concatenate([vext, c_prev.astype(BF16)], axis=0))

            num = r[:, :dh]
            den = r[:, dh:dh + 1]
            rden = 1.0 / jnp.maximum(jnp.abs(den), enm[:, ci:ci + 1])
            ssq = jnp.sum(num * num, axis=-1, keepdims=True)
            sc = rden * lax.rsqrt(rden * rden * ssq * (1.0 / dh) + EPS)
            gate = _sigmoid(o_ref[0, :, hs].astype(F32))
            out_ref[0, :, hs] = (num * sc * hg_ref[h:h + 1, :] * gate).astype(BF16)

            kw = (kh.astype(F32) * ws[:, ci:ci + 1]).astype(BF16)
            upd = lax.dot_general(kw, vext, (((0,), (0,)), ((), ())), preferred_element_type=F32)
            c_scr[u] = decay[:, ci:ci + 1] * c_prev + upd


def _mlstm(u_lat, u_ctx, g_lat, g_ctx, head_g, L):
    b_, s_, _ = u_lat.shape
    n_ctx = u_ctx.shape[1]
    nh = MLSTM_HEADS
    w = 1024
    dh = w // nh
    nc, nl = n_ctx // L, s_ // L
    steps = nc + nl
    t0 = jnp.tril(jnp.ones((L, L), F32))
    tri = jnp.stack([t0, t0.T]).astype(BF16)

    def cidx(d):
        if d == 0:
            return lambda st: jnp.minimum(st, nc - 1), lambda st: jnp.maximum(st - nc, 0)
        return lambda st: jnp.maximum(nc - 1 - st, 0), lambda st: jnp.minimum(nl - 1 + nc - st, nl - 1)

    in_specs, args = [], []
    for d in range(2):
        fc, fl = cidx(d)
        for t in range(3):
            in_specs.append(pl.BlockSpec((1, L, w), lambda b, st, fc=fc, t=t: (b, fc(st), t)))
            args.append(u_ctx)
        for t in (COL_Q, COL_K, COL_V):
            in_specs.append(pl.BlockSpec((1, L, w), lambda b, st, fl=fl, t=t: (b, fl(st), t)))
            args.append(u_lat)
        in_specs.append(pl.BlockSpec((1, L, LANES), lambda b, st, fc=fc: (b, fc(st), 0)))
        args.append(g_ctx)
        in_specs.append(pl.BlockSpec((1, L, LANES), lambda b, st, fl=fl: (b, fl(st), 0)))
        args.append(g_lat)
        in_specs.append(pl.BlockSpec((1, L, w), lambda b, st, fl=fl, t=COL_OF + d: (b, fl(st), t)))
        args.append(u_lat)
    in_specs += [pl.BlockSpec((nh, dh), lambda b, st: (0, 0)),
                 pl.BlockSpec((2, L, L), lambda b, st: (0, 0, 0))]
    args += [head_g, tri]
    out_specs = tuple(pl.BlockSpec((1, L, w), lambda b, st, fl=cidx(d)[1]: (b, fl(st), 0)) for d in range(2))
    return pl.pallas_call(
        functools.partial(_mlstm_kernel, nc, nl, L, dh),
        out_shape=(jax.ShapeDtypeStruct((b_, s_, w), BF16),) * 2,
        grid=(b_, steps),
        in_specs=in_specs,
        out_specs=out_specs,
        scratch_shapes=[pltpu.VMEM((2 * nh, dh, 2 * dh), F32), pltpu.VMEM((2, 8, LANES), F32)],
        compiler_params=_cparams(("parallel", "arbitrary")),
        name="mlstm",
    )(*args)


def _rglru_kernel(n_ctx, n_tot, ux_ref, cw_ref, cb_ref, w_ref, b_ref, lam_ref, out_ref,
                  a_f, b_f, a_b, b_b, h_f, h_b):
    u = ux_ref[0].astype(F32)
    t_idx = lax.broadcasted_iota(jnp.int32, (n_tot, 1), 0)
    seg_lo = jnp.where(t_idx < n_ctx, 0, n_ctx)
    seg_hi = jnp.where(t_idx < n_ctx, n_ctx, n_tot)
    z = jnp.zeros_like(u) + cb_ref[...]
    for j, off in enumerate((-2, -1, 0, 1)):
        if off == 0:
            src = u
        else:
            src = pltpu.roll(u, (-off) % n_tot, 0)
            ok = (t_idx + off >= seg_lo) & (t_idx + off < seg_hi)
            src = jnp.where(ok, src, 0.0)
        z = z + cw_ref[j:j + 1, :] * src
    p = _dot(z.astype(BF16), w_ref[0]) + b_ref[...]
    c = LANES
    for d, (a_scr, b_scr) in enumerate(((a_f, b_f), (a_b, b_b))):
        r = _sigmoid(p[:, (2 * d) * c:(2 * d + 1) * c])
        i = _sigmoid(p[:, (2 * d + 1) * c:(2 * d + 2) * c])
        lam = lam_ref[d:d + 1, :]
        softplus = jnp.maximum(-lam, 0.0) + jnp.log(1.0 + jnp.exp(-jnp.abs(lam)))
        log_a = -LRU_C * r * softplus
        a = jnp.exp(log_a)
        a_scr[...] = a
        b_scr[...] = jnp.sqrt(1.0 - a * a) * i * z

    sub = lax.broadcasted_iota(jnp.int32, (8, c), 0)
    ng = n_tot // 8
    ngc = n_ctx // 8

    def group_scan(a, b, carry, reverse):
        for sft in (1, 2, 4):
            if reverse:
                a_s = pltpu.roll(a, 8 - sft, 0)
                b_s = pltpu.roll(b, 8 - sft, 0)
                ok = sub < 8 - sft
            else:
                a_s = pltpu.roll(a, sft, 0)
                b_s = pltpu.roll(b, sft, 0)
                ok = sub >= sft
            b = jnp.where(ok, a * b_s + b, b)
            a = jnp.where(ok, a * a_s, a)
        h = b + a * carry
        return h, (h[0:1, :] if reverse else h[7:8, :])

    def body(it, carry):
        cf, cb = carry
        rf = pl.multiple_of(it * 8, 8)
        hf, cf = group_scan(a_f[pl.ds(rf, 8), :], b_f[pl.ds(rf, 8), :], cf, False)
        h_f[pl.ds(rf, 8), :] = hf
        jb = jnp.where(it < ngc, ngc - 1 - it, ng - 1 + ngc - it)
        rb = pl.multiple_of(jb * 8, 8)
        hb, cb = group_scan(a_b[pl.ds(rb, 8), :], b_b[pl.ds(rb, 8), :], cb, True)
        h_b[pl.ds(rb, 8), :] = hb
        return cf, cb

    zero = jnp.zeros((1, c), F32)
    lax.fori_loop(0, ng, body, (zero, zero), unroll=4)
    out_ref[0] = (h_f[n_ctx:, :] + h_b[n_ctx:, :]).astype(BF16)


def _rglru(ux_seq, conv_w, conv_b, w_cat, b_cat, lam, n_ctx):
    b_, n_tot, w = ux_seq.shape
    nb = LRU_BLOCKS
    c = w // nb
    return pl.pallas_call(
        functools.partial(_rglru_kernel, n_ctx, n_tot),
        out_shape=jax.ShapeDtypeStruct((b_, n_tot - n_ctx, w), BF16),
        grid=(b_, nb),
        in_specs=[
            pl.BlockSpec((1, n_tot, c), lambda b, k: (b, 0, k)),
            pl.BlockSpec((4, c), lambda b, k: (0, k)),
            pl.BlockSpec((1, c), lambda b, k: (0, k)),
            pl.BlockSpec((1, c, 4 * c), lambda b, k: (k, 0, 0)),
            pl.BlockSpec((1, 4 * c), lambda b, k: (0, k)),
            pl.BlockSpec((2, c), lambda b, k: (0, k)),
        ],
        out_specs=pl.BlockSpec((1, n_tot - n_ctx, c), lambda b, k: (b, 0, k)),
        scratch_shapes=[pltpu.VMEM((n_tot, c), F32)] * 6,
        compiler_params=_cparams(("parallel", "parallel")),
        name="rglru",
    )(ux_seq, conv_w, conv_b, w_cat, b_cat, lam)


def _outproj_kernel(yaf_ref, yab_ref, hr_ref, ug_ref, ga0_ref, ga1_ref, gb0_ref, gb1_ref, x_ref,
                    g1_ref, gain_ref, shift_ref, wpa_ref, wpb_ref, wout_ref, wr_ref,
                    x1_ref, h2_ref, lg_ref):
    ya = (yaf_ref[0].astype(F32) + yab_ref[0].astype(F32)).astype(BF16)
    ug = ug_ref[0].astype(F32)
    gelu = 0.5 * ug * (1.0 + jnp.tanh(0.7978845608028654 * (ug + 0.044715 * ug * ug * ug)))
    yb = (hr_ref[0].astype(F32) * gelu).astype(BF16)
    pa = _dot(ya, wpa_ref[...])
    pb = _dot(yb, wpb_ref[...])
    ga = jnp.concatenate([ga0_ref[0], ga1_ref[0]], axis=1).astype(F32)
    gb = jnp.concatenate([gb0_ref[0], gb1_ref[0]], axis=1).astype(F32)
    mix = (_sigmoid(ga) * pa + _sigmoid(gb) * pb).astype(BF16)
    x1 = x_ref[0] + g1_ref[0] * _dot(mix, wout_ref[...])
    x1_ref[0] = x1
    ms = jnp.mean(x1 * x1, axis=-1, keepdims=True)
    h2 = x1 * lax.rsqrt(ms + EPS) * gain_ref[0] + shift_ref[0]
    ns = h2.shape[1] // LANES
    for s in range(ns):
        h2_ref[pl.ds(s, h2.shape[0], stride=ns), :] = h2[:, s * LANES:(s + 1) * LANES]
    wr = wr_ref[...]
    h1, h2b, _ = _split3(h2)
    w1, w2b, _ = _split3(wr)
    lg_ref[0] = _dot(h1, w1) + (_dot(h1, w2b) + _dot(h2b, w1))


def _outproj(ya_f, ya_b, hr, u_lat, x, g1, gain2, shift2, w_pa, w_pb, w_out, w_r, tm):
    b_, s_, d = x.shape
    w = ya_f.shape[2]
    row = lambda b, i: (b, i, 0)
    const2 = lambda b, i: (0, 0)
    bmap = lambda b, i: (b, 0, 0)
    ucol = lambda t: (lambda b, i: (b, i, t))
    single = pl.Buffered(1)
    return pl.pallas_call(
        _outproj_kernel,
        out_shape=(jax.ShapeDtypeStruct((b_, s_, d), F32),
                   jax.ShapeDtypeStruct((b_ * s_ * (d // LANES), LANES), F32),
                   jax.ShapeDtypeStruct((b_, s_, LANES), F32)),
        grid=(b_, s_ // tm),
        in_specs=[
            pl.BlockSpec((1, tm, w), row), pl.BlockSpec((1, tm, w), row), pl.BlockSpec((1, tm, w), row),
            pl.BlockSpec((1, tm, w), ucol(COL_UG)),
            pl.BlockSpec((1, tm, w), ucol(COL_GA)), pl.BlockSpec((1, tm, w), ucol(COL_GA + 1)),
            pl.BlockSpec((1, tm, w), ucol(COL_GB)), pl.BlockSpec((1, tm, w), ucol(COL_GB + 1)),
            pl.BlockSpec((1, tm, d), row),
            pl.BlockSpec((1, 1, d), bmap), pl.BlockSpec((1, 1, d), bmap), pl.BlockSpec((1, 1, d), bmap),
            pl.BlockSpec((w, d), const2, pipeline_mode=single),
            pl.BlockSpec((w, d), const2, pipeline_mode=single),
            pl.BlockSpec((d, d), const2, pipeline_mode=single),
            pl.BlockSpec((d, LANES), const2, pipeline_mode=single),
        ],
        out_specs=(pl.BlockSpec((1, tm, d), row),
                   pl.BlockSpec((tm * (d // LANES), LANES), lambda b, i: (b * (s_ // tm) + i, 0)),
                   pl.BlockSpec((1, tm, LANES), row)),
        compiler_params=_cparams(("parallel", "parallel")),
        name="outproj",
    )(ya_f, ya_b, hr, u_lat, u_lat, u_lat, u_lat, u_lat, x, g1, gain2, shift2, w_pa, w_pb, w_out, w_r)


def _route_kernel(lg_ref, o_ref):
    lg = lg_ref[...]
    lane = lax.broadcasted_iota(jnp.int32, lg.shape, 1)
    neg = -jnp.inf
    big = jnp.int32(1 << 20)
    g_l = jnp.where(lane < N_GROUPS, lg, neg)
    g_max = jnp.max(g_l, axis=-1, keepdims=True)
    g_sel = jnp.min(jnp.where(g_l == g_max, lane, big), axis=-1, keepdims=True)
    p_g = 1.0 / jnp.sum(jnp.exp(g_l - g_max), axis=-1, keepdims=True)
    lo = N_GROUPS + g_sel * EXPERTS_PER_GROUP
    e_l = jnp.where((lane >= lo) & (lane < lo + EXPERTS_PER_GROUP), lg, neg)
    v1 = jnp.max(e_l, axis=-1, keepdims=True)
    i1 = jnp.min(jnp.where(e_l == v1, lane, big), axis=-1, keepdims=True)
    e_l2 = jnp.where(lane == i1, neg, e_l)
    v2 = jnp.max(e_l2, axis=-1, keepdims=True)
    i2 = jnp.min(jnp.where(e_l2 == v2, lane, big), axis=-1, keepdims=True)
    e2 = jnp.exp(v2 - v1)
    w1 = p_g / (1.0 + e2)
    w2 = p_g * e2 / (1.0 + e2)
    out = jnp.where(lane == 0, (i1 - N_GROUPS).astype(F32),
          jnp.where(lane == 1, (i2 - N_GROUPS).astype(F32),
          jnp.where(lane == 2, w1, jnp.where(lane == 3, w2, 0.0))))
    o_ref[...] = out


def _route(logits, tm):
    t_, _ = logits.shape
    return pl.pallas_call(
        _route_kernel,
        out_shape=jax.ShapeDtypeStruct((t_, LANES), F32),
        grid=(t_ // tm,),
        in_specs=[pl.BlockSpec((tm, LANES), lambda i: (i, 0))],
        out_specs=pl.BlockSpec((tm, LANES), lambda i: (i, 0)),
        compiler_params=_cparams(("parallel",)),
        name="route",
    )(logits)


MOE_DMA_UNROLL = 32


MOE_W_DEPTH = 3
MOE_W_SPLIT = 2


def _moe_kernel(R, SUB, nf, ni, item_e, item_rows, tok_ref, tokn_ref, dst_ref, h2_hbm, w1_hbm, w3_hbm, w2_hbm,
                y_hbm, x_scr, xb_scr, acc_scr, y_scr, w1r, w3r, w2r, w1b, w3b, w2b, gsem, ssem, wsem):
    i = pl.program_id(0)
    j = pl.program_id(1)
    nsub_of = lambda r: (r + SUB - 1) // SUB
    rows = item_rows[i]
    nsub = nsub_of(rows)
    nsub_next = jnp.where(i + 1 < ni, nsub_of(item_rows[jnp.minimum(i + 1, ni - 1)]), 0)
    rows_prev = jnp.where(i > 0, item_rows[jnp.maximum(i - 1, 0)], 0)
    buf = i % 2
    groups = SUB // MOE_DMA_UNROLL

    d_model, tf = w1r.shape[1], w1r.shape[2]

    def weight_copies(c):
        it = c // nf
        jc = c % nf
        e = item_e[jnp.minimum(it, ni - 1)]
        slot = c % MOE_W_DEPTH
        f0 = pl.multiple_of(jc * tf, tf)
        out = []
        for h in range(MOE_W_SPLIT):
            ra, rb = d_model // MOE_W_SPLIT, tf // MOE_W_SPLIT
            out.append(pltpu.make_async_copy(w1_hbm.at[e, pl.ds(h * ra, ra), pl.ds(f0, tf)],
                                             w1r.at[slot, pl.ds(h * ra, ra), :], wsem.at[slot]))
            out.append(pltpu.make_async_copy(w3_hbm.at[e, pl.ds(h * ra, ra), pl.ds(f0, tf)],
                                             w3r.at[slot, pl.ds(h * ra, ra), :], wsem.at[slot]))
            out.append(pltpu.make_async_copy(w2_hbm.at[e, pl.ds(f0 + h * rb, rb), :],
                                             w2r.at[slot, pl.ds(h * rb, rb), :], wsem.at[slot]))
        return out

    def chunk_live(c):
        it = c // nf
        return (it < ni) & (item_rows[jnp.minimum(it, ni - 1)] > 0)

    cur = i * nf + j

    @pl.when((cur == 0) & chunk_live(0))
    def _():
        for cp in weight_copies(0):
            cp.start()

        @pl.when(chunk_live(1))
        def _():
            for cp in weight_copies(1):
                cp.start()

    @pl.when((rows > 0) & chunk_live(cur + 2))
    def _():
        for cp in weight_copies(cur + 2):
            cp.start()

    ns = acc_scr.shape[1] // LANES

    def tok_rows(ref, r):
        return ref.at[pl.ds(pl.multiple_of(r * ns, ns), ns), :]

    def gather_block(tref, b, q):
        def grp(gi, c):
            base = pl.multiple_of(q * SUB + gi * MOE_DMA_UNROLL, MOE_DMA_UNROLL)
            for k in range(MOE_DMA_UNROLL):
                tok = tref[0, 0, base + k]
                pltpu.make_async_copy(tok_rows(h2_hbm, tok), tok_rows(x_scr.at[b], base + k),
                                      gsem.at[b]).start(priority=1)
            return c
        lax.fori_loop(0, groups, grp, 0)

    def gather_wait_block(b):
        n = SUB * ns
        pltpu.make_async_copy(h2_hbm.at[pl.ds(0, n), :], x_scr.at[b, pl.ds(0, n), :], gsem.at[b]).wait()

    def scatter_copy(r):
        dst = dst_ref[0, 0, r]
        return pltpu.make_async_copy(tok_rows(y_scr, r), tok_rows(y_hbm, dst), ssem)

    def repeat(n, fn, lo=0):
        def body(q, c):
            fn(q)
            return c
        lax.fori_loop(lo, n, body, 0)

    def scatter_rows(n):
        def grp(gi):
            base = pl.multiple_of(gi * MOE_DMA_UNROLL, MOE_DMA_UNROLL)
            for k in range(MOE_DMA_UNROLL):
                scatter_copy(base + k).start(priority=k % 2)
        full = n // MOE_DMA_UNROLL
        repeat(full, grp)
        repeat(n, lambda r: scatter_copy(r).start(), lo=full * MOE_DMA_UNROLL)

    def scatter_wait_rows(n):
        u = MOE_DMA_UNROLL * ns
        full = n // MOE_DMA_UNROLL
        repeat(full, lambda q: pltpu.make_async_copy(y_scr.at[pl.ds(0, u), :], y_hbm.at[pl.ds(0, u), :],
                                                      ssem).wait())
        repeat(n, lambda r: scatter_copy(0).wait(), lo=full * MOE_DMA_UNROLL)

    @pl.when((i == 0) & (j == 0))
    def _():
        repeat(nsub, lambda q: gather_block(tok_ref, 0, q))

    @pl.when(j < nsub_next)
    def _():
        gather_block(tokn_ref, 1 - buf, j)

    @pl.when(j == 0)
    def _():
        repeat(nsub, lambda q: gather_wait_block(buf))

    @pl.when(j == nf - 1)
    def _():
        scatter_wait_rows(rows_prev)

    @pl.when(rows > 0)
    def _():
        for cp in weight_copies(cur):
            cp.wait()
        slot = cur % MOE_W_DEPTH
        w1b[...] = w1r[slot].astype(BF16)
        w3b[...] = w3r[slot].astype(BF16)
        w2b[...] = w2r[slot].astype(BF16)

        @pl.when(j == 0)
        def _():
            def unsplit(sb, c):
                r0 = pl.multiple_of(sb * SUB, SUB)
                s0 = pl.multiple_of(sb * (SUB * ns), SUB * ns)
                xb_scr[pl.ds(r0, SUB), :] = jnp.concatenate(
                    [x_scr[buf, pl.ds(s0 + s, SUB, stride=ns), :] for s in range(ns)], axis=1).astype(BF16)
                return c
            lax.fori_loop(0, nsub, unsplit, 0)

        def sub(sb, c):
            r0 = pl.multiple_of(sb * SUB, SUB)
            s0 = pl.multiple_of(sb * (SUB * ns), SUB * ns)
            x = xb_scr[pl.ds(r0, SUB), :]
            h1 = _dot(x, w1b[...])
            h3 = _dot(x, w3b[...])
            hh = (h1 * _sigmoid(h1) * h3).astype(BF16)
            contrib = _dot(hh, w2b[...])

            @pl.when(j == 0)
            def _():
                acc_scr[pl.ds(r0, SUB), :] = contrib

            @pl.when((j > 0) & (j < nf - 1))
            def _():
                acc_scr[pl.ds(r0, SUB), :] += contrib

            @pl.when(j == nf - 1)
            def _():
                tot = acc_scr[pl.ds(r0, SUB), :] + contrib
                for s in range(ns):
                    y_scr[pl.ds(s0 + s, SUB, stride=ns), :] = tot[:, s * LANES:(s + 1) * LANES]
            return c
        lax.fori_loop(0, nsub, sub, 0)

    @pl.when(j == nf - 1)
    def _():
        scatter_rows(rows)

        @pl.when(i == ni - 1)
        def _():
            scatter_wait_rows(rows)


def _moe(h2s, item_e, item_rows, slot_tok, slot_dst, w1, w3, w2, n_out_rows):
    ne, d, f = w1.shape
    ns = d // LANES
    R, SUB, tf = MOE_ITEM_ROWS, MOE_SUB_ROWS, MOE_F_CHUNK
    ni = item_e.shape[0]
    nf = f // tf
    assert R % SUB == 0 and R <= nf * SUB and nf >= 2
    grid_spec = pltpu.PrefetchScalarGridSpec(
        num_scalar_prefetch=2,
        grid=(ni, nf),
        in_specs=[
            pl.BlockSpec((1, 1, R), lambda i, j, ie, ir: (i, 0, 0), memory_space=pltpu.SMEM),
            pl.BlockSpec((1, 1, R), lambda i, j, ie, ir: (jnp.minimum(i + 1, ni - 1), 0, 0),
                         memory_space=pltpu.SMEM),
            pl.BlockSpec((1, 1, R), lambda i, j, ie, ir: (i, 0, 0), memory_space=pltpu.SMEM),
            pl.BlockSpec(memory_space=pl.ANY),
            pl.BlockSpec(memory_space=pl.ANY),
            pl.BlockSpec(memory_space=pl.ANY),
            pl.BlockSpec(memory_space=pl.ANY),
        ],
        out_specs=pl.BlockSpec(memory_space=pl.ANY),
        scratch_shapes=[
            pltpu.VMEM((2, R * ns, LANES), F32), pltpu.VMEM((R, d), BF16),
            pltpu.VMEM((R, d), F32), pltpu.VMEM((R * ns, LANES), F32),
            pltpu.VMEM((MOE_W_DEPTH, d, tf), F32), pltpu.VMEM((MOE_W_DEPTH, d, tf), F32),
            pltpu.VMEM((MOE_W_DEPTH, tf, d), F32),
            pltpu.VMEM((d, tf), BF16), pltpu.VMEM((d, tf), BF16), pltpu.VMEM((tf, d), BF16),
            pltpu.SemaphoreType.DMA((2,)), pltpu.SemaphoreType.DMA(()), pltpu.SemaphoreType.DMA((MOE_W_DEPTH,)),
        ],
    )
    return pl.pallas_call(
        functools.partial(_moe_kernel, R, SUB, nf, ni),
        out_shape=jax.ShapeDtypeStruct((n_out_rows * ns, LANES), F32),
        grid_spec=grid_spec,
        compiler_params=_cparams(("arbitrary", "arbitrary")),
        name="moe_experts",
    )(item_e, item_rows, slot_tok, slot_tok, slot_dst, h2s, w1, w3, w2)


def _moe_plan(expert, t_):
    R = MOE_ITEM_ROWS
    na = expert.shape[0]
    ni = N_EXPERTS + na // R
    onehot = (expert[:, None] == jnp.arange(N_EXPERTS, dtype=jnp.int32)[None, :]).astype(jnp.int32)
    csum = jnp.cumsum(onehot, axis=0)
    rank = jnp.sum(csum * onehot, axis=1) - 1
    counts = csum[-1]
    n_items = (counts + R - 1) // R
    item_end = jnp.cumsum(n_items)
    item_start = item_end - n_items
    slot = jnp.sum(item_start[None, :] * onehot, axis=1) * R + rank
    slot_a = jnp.full((ni * R,), -1, jnp.int32).at[slot].set(jnp.arange(na, dtype=jnp.int32))
    valid = slot_a >= 0
    slot_tok = jnp.where(valid, slot_a // TOP_K, 0)
    slot_dst = jnp.where(valid, (slot_a % TOP_K) * t_ + slot_a // TOP_K, 0)
    ii = jnp.arange(ni, dtype=jnp.int32)
    e_of = jnp.minimum(jnp.sum((item_end[None, :] <= ii[:, None]).astype(jnp.int32), axis=1), N_EXPERTS - 1)
    oh_e = (e_of[:, None] == jnp.arange(N_EXPERTS, dtype=jnp.int32)[None, :]).astype(jnp.int32)
    cnt_e = jnp.sum(oh_e * counts[None, :], axis=1)
    start_e = jnp.sum(oh_e * item_start[None, :], axis=1)
    rows = jnp.clip(cnt_e - (ii - start_e) * R, 0, R).astype(jnp.int32)
    live = ii < item_end[-1]
    rows = jnp.where(live, rows, 0)
    last_e = jnp.max(jnp.where(live, e_of, 0))
    e_of = jnp.where(live, e_of, last_e).astype(jnp.int32)
    return e_of, rows, slot_tok.reshape(ni, 1, R), slot_dst.reshape(ni, 1, R)


def _final_kernel(x1_ref, y0_ref, y1_ref, rt_ref, g2_ref, fg_ref, o_ref):
    rt = rt_ref[0]
    tm = rt.shape[0]
    ns = y0_ref.shape[0] // tm
    unsplit = lambda ref: jnp.concatenate([ref[pl.ds(s, tm, stride=ns), :] for s in range(ns)], axis=1)
    y = rt[:, TOP_K:TOP_K + 1] * unsplit(y0_ref) + rt[:, TOP_K + 1:TOP_K + 2] * unsplit(y1_ref)
    x = x1_ref[0] + g2_ref[0] * y
    ms = jnp.mean(x * x, axis=-1, keepdims=True)
    o_ref[0] = x * lax.rsqrt(ms + EPS) * fg_ref[...]


def _final(x1, ybuf, routed, g2, final_g, tm):
    b_, s_, d = x1.shape
    nt = s_ // tm
    ns = d // LANES
    return pl.pallas_call(
        _final_kernel,
        out_shape=jax.ShapeDtypeStruct((b_, s_, d), F32),
        grid=(b_, nt),
        in_specs=[
            pl.BlockSpec((1, tm, d), lambda b, i: (b, i, 0)),
            pl.BlockSpec((tm * ns, LANES), lambda b, i: (b * nt + i, 0)),
            pl.BlockSpec((tm * ns, LANES), lambda b, i: (b_ * nt + b * nt + i, 0)),
            pl.BlockSpec((1, tm, LANES), lambda b, i: (b, i, 0)),
            pl.BlockSpec((1, 1, d), lambda b, i: (b, 0, 0)),
            pl.BlockSpec((1, d), lambda b, i: (0, 0)),
        ],
        out_specs=pl.BlockSpec((1, tm, d), lambda b, i: (b, i, 0)),
        compiler_params=_cparams(("parallel", "parallel")),
        name="final_norm",
    )(x1, ybuf, ybuf, routed, g2, final_g)


def kernel(x, c, ctx, c_ctx, w_mod, b_mod, norm1_g, w_in, b_gates, mlstm_head_g, conv_w, conv_b, lru_wa, lru_ba,
           lru_wx, lru_bx, lru_lam, w_pa, w_pb, w_out, norm2_g, w_rg, w_re, w1, w3, w2, final_g):
    b_, s_, d = x.shape
    n_ctx = ctx.shape[1]
    rows = s_ // GRID_W
    assert w_mod.shape[0] == 1, "single layer"
    wm = d // 2
    nh = MLSTM_HEADS

    cc = jnp.zeros((8, d), F32).at[:b_].set(c).at[b_].set(c_ctx)
    mod = _modulation(cc, w_mod[0], b_mod[0][None, :])
    sh1, sc1, g1, sh2, sc2, g2 = [mod[:b_, i * d:(i + 1) * d][:, None, :] for i in range(6)]
    csh1, csc1 = mod[b_:b_ + 1, 0:d][:, None, :], mod[b_:b_ + 1, d:2 * d][:, None, :]

    assert wm == 1024 and 4 * nh == N_GATE_COLS
    w_t = jnp.transpose(w_in[0])
    bg_pad = jnp.zeros((1, LANES), F32).at[0, :4 * nh].set(b_gates[0])

    ng = norm1_g[0][None, None, :]
    u_lat, g_lat = _inproj(x, ng * (1.0 + sc1), sh1, w_t, bg_pad, tuple(range(N_COL_TILES)), min(1024, s_))
    u_ctx, g_ctx = _inproj(ctx.reshape(1, b_ * n_ctx, d), ng * (1.0 + csc1), csh1, w_t, bg_pad,
                           (COL_Q, COL_K, COL_V, COL_UX), b_ * n_ctx)
    u_ctx = u_ctx.reshape(b_, n_ctx, -1)
    g_ctx = g_ctx.reshape(b_, n_ctx, LANES)

    ya_f, ya_b = _mlstm(u_lat, u_ctx, g_lat, g_ctx, mlstm_head_g[0], MLSTM_CHUNK)

    ux_lat = u_lat[:, :, COL_UX * wm:(COL_UX + 1) * wm]
    ux_col = ux_lat.reshape(b_, rows, GRID_W, wm).transpose(0, 2, 1, 3).reshape(b_, s_, wm)
    ux_seq = jnp.concatenate([u_ctx[:, :, CTX_UX * wm:(CTX_UX + 1) * wm], ux_col], axis=1)
    wa, wx = lru_wa[0], lru_wx[0]
    w_cat = jnp.concatenate([wa[0], wx[0], wa[1], wx[1]], axis=-1).astype(BF16)
    cblk = wm // LRU_BLOCKS
    blk = lambda v: v.reshape(LRU_BLOCKS, cblk)
    b_cat = jnp.concatenate([blk(lru_ba[0, 0]), blk(lru_bx[0, 0]), blk(lru_ba[0, 1]), blk(lru_bx[0, 1])],
                            axis=-1).reshape(1, 4 * wm)
    h_col = _rglru(ux_seq, conv_w[0], conv_b[0][None, :], w_cat, b_cat, lru_lam[0], n_ctx)
    hr = h_col.reshape(b_, GRID_W, rows, wm).transpose(0, 2, 1, 3).reshape(b_, s_, wm)

    w_r = jnp.zeros((d, LANES), F32).at[:, :N_GROUPS].set(w_rg[0]).at[:, N_GROUPS:N_GROUPS + N_EXPERTS].set(w_re[0])
    n2 = norm2_g[0][None, None, :]
    x1, h2, logits = _outproj(ya_f, ya_b, hr, u_lat, x, g1, n2 * (1.0 + sc2), sh2,
                              w_pa[0].astype(BF16), w_pb[0].astype(BF16), w_out[0].astype(BF16), w_r, 256)

    t_ = b_ * s_
    routed = _route(logits.reshape(t_, LANES), 512)
    expert = routed[:, :TOP_K].astype(jnp.int32).reshape(-1)
    item_e, item_rows, slot_tok, slot_dst = _moe_plan(expert, t_)
    ybuf = _moe(h2, item_e, item_rows, slot_tok, slot_dst, w1[0], w3[0], w2[0], TOP_K * t_)

    return _final(x1, ybuf, routed.reshape(b_, s_, LANES), g2, final_g[None, :], 256)
```

```python
import functools
import math

import jax
import jax.numpy as jnp
from jax import lax
from jax.experimental import pallas as pl
from jax.experimental.pallas import tpu as pltpu

F32 = jnp.float32
BF16 = jnp.bfloat16

EPS = 1e-6
GRID_W = 64
MLSTM_HEADS = 8
LRU_BLOCKS = 8
LRU_C = 8.0
N_GROUPS = 4
EXPERTS_PER_GROUP = 8
N_EXPERTS = N_GROUPS * EXPERTS_PER_GROUP
TOP_K = 2

VMEM_LIMIT_BYTES = 56 * 1024 * 1024
LANES = 128

MLSTM_CHUNK = 128
MOE_ITEM_ROWS = 768
MOE_SUB_ROWS = 256
MOE_F_CHUNK = 256


def _cparams(sem):
    return pltpu.CompilerParams(dimension_semantics=sem, vmem_limit_bytes=VMEM_LIMIT_BYTES)


def _sigmoid(x):
    return 0.5 * jnp.tanh(0.5 * x) + 0.5


def _dot(a, b):
    return jnp.dot(a, b, preferred_element_type=F32)


def _split3(x):
    x1 = x.astype(BF16)
    r1 = x - x1.astype(F32)
    x2 = r1.astype(BF16)
    x3 = (r1 - x2.astype(F32)).astype(BF16)
    return x1, x2, x3


def _mod_kernel(c_ref, w_ref, b_ref, o_ref):
    c = c_ref[...]
    s = (c * _sigmoid(c)).astype(BF16)
    o_ref[...] = _dot(s, w_ref[...].astype(BF16)) + b_ref[...]


def _modulation(cc, w_mod, b_mod):
    m, d = cc.shape
    n = w_mod.shape[1]
    tn = 1024
    return pl.pallas_call(
        _mod_kernel,
        out_shape=jax.ShapeDtypeStruct((m, n), F32),
        grid=(n // tn,),
        in_specs=[
            pl.BlockSpec((m, d), lambda j: (0, 0)),
            pl.BlockSpec((d, tn), lambda j: (0, j)),
            pl.BlockSpec((1, tn), lambda j: (0, j)),
        ],
        out_specs=pl.BlockSpec((m, tn), lambda j: (0, j)),
        compiler_params=_cparams(("parallel",)),
        name="modulation",
    )(cc, w_mod, b_mod)


N_PRE_TILES = 5
N_GATE_COLS = 32


def _dot_nt(a, b):
    return lax.dot_general(a, b, (((1,), (1,)), ((), ())), preferred_element_type=F32)


def _inproj_kernel(x_ref, gain_ref, shift_ref, wt_ref, wg_ref, bg_ref, u_ref, g_ref, h_scr):
    @pl.when(pl.program_id(2) == 0)
    def _():
        x = x_ref[0]
        ms = jnp.mean(x * x, axis=-1, keepdims=True)
        h = x * lax.rsqrt(ms + EPS) * gain_ref[0] + shift_ref[0]
        hb = h.astype(BF16)
        h_scr[...] = hb
        gates = _dot_nt(hb, wg_ref[...].astype(BF16))
        pad = jnp.zeros((gates.shape[0], LANES - N_GATE_COLS), F32)
        g_ref[0] = jnp.concatenate([gates, pad], axis=1) + bg_ref[...]

    u_ref[0] = _dot_nt(h_scr[...], wt_ref[...].astype(BF16)).astype(BF16)


def _inproj(x, gain, shift, w_t, b_gates, tiles, tm):
    b_, s_, d = x.shape
    tn = 1024
    nj = len(tiles)
    if tiles == tuple(range(nj)):
        tile_of = lambda j: j
    else:
        assert tiles == (COL_Q, COL_K, COL_V, COL_UX), tiles
        tile_of = lambda j: jnp.where(j == 3, COL_UX, j)
    row_of = lambda j: tile_of(j) * tn + jnp.where(tile_of(j) >= N_PRE_TILES, N_GATE_COLS, 0)
    per_batch_mod = gain.shape[0] > 1
    mmap = (lambda b, i, j: (b, 0, 0)) if per_batch_mod else (lambda b, i, j: (0, 0, 0))
    gate_blk = N_PRE_TILES * tn // N_GATE_COLS
    return pl.pallas_call(
        _inproj_kernel,
        out_shape=(jax.ShapeDtypeStruct((b_, s_, nj * tn), BF16),
                   jax.ShapeDtypeStruct((b_, s_, LANES), F32)),
        grid=(b_, s_ // tm, nj),
        in_specs=[
            pl.BlockSpec((1, tm, d), lambda b, i, j: (b, i, 0)),
            pl.BlockSpec((1, 1, d), mmap),
            pl.BlockSpec((1, 1, d), mmap),
            pl.BlockSpec((pl.Element(tn), pl.Element(d)), lambda b, i, j: (pl.multiple_of(row_of(j), 8), 0)),
            pl.BlockSpec((N_GATE_COLS, d), lambda b, i, j: (gate_blk, 0)),
            pl.BlockSpec((1, LANES), lambda b, i, j: (0, 0)),
        ],
        out_specs=(pl.BlockSpec((1, tm, tn), lambda b, i, j: (b, i, j)),
                   pl.BlockSpec((1, tm, LANES), lambda b, i, j: (b, i, 0))),
        scratch_shapes=[pltpu.VMEM((tm, d), BF16)],
        compiler_params=_cparams(("parallel", "parallel", "arbitrary")),
        name="inproj",
    )(x, gain, shift, w_t, w_t, b_gates)


COL_Q, COL_K, COL_V, COL_OF, COL_OB, COL_UX, COL_UG, COL_GA, COL_GB = 0, 1, 2, 3, 4, 5, 6, 7, 9
CTX_UX = 3
N_COL_TILES = 11


def _mlstm_kernel(nc, nl, L, dh,
                  qc_f, kc_f, vc_f, ql_f, kl_f, vl_f, gc_f, gl_f, o_f,
                  qc_b, kc_b, vc_b, ql_b, kl_b, vl_b, gc_b, gl_b, o_b,
                  hg_ref, tri_ref,
                  out_f, out_b, c_scr, m_scr):
    s = pl.program_id(1)
    nh = MLSTM_HEADS
    scale = dh ** -0.5

    @pl.when(s == 0)
    def _():
        c_scr[...] = jnp.zeros_like(c_scr)
        m_scr[...] = jnp.full_like(m_scr, -1e30)

    is_ctx = s < nc
    ones_col = (lax.broadcasted_iota(jnp.int32, (L, dh), 1) == 0).astype(BF16)
    row = lax.broadcasted_iota(jnp.int32, (L, L), 0)
    col = lax.broadcasted_iota(jnp.int32, (L, L), 1)

    for d, (qc, kc, vc, ql, kl, vl, gc, gl, o_ref, out_ref) in enumerate((
            (qc_f, kc_f, vc_f, ql_f, kl_f, vl_f, gc_f, gl_f, o_f, out_f),
            (qc_b, kc_b, vc_b, ql_b, kl_b, vl_b, gc_b, gl_b, o_b, out_b))):
        q = jnp.where(is_ctx, qc[0], ql[0])
        k = jnp.where(is_ctx, kc[0], kl[0])
        v = jnp.where(is_ctx, vc[0], vl[0])
        g = jnp.where(is_ctx, gc[0], gl[0])
        gt = g.T
        ls = jnp.minimum(g, 0.0) - jnp.log(1.0 + jnp.exp(-jnp.abs(g)))
        lst = jnp.minimum(gt, 0.0) - jnp.log(1.0 + jnp.exp(-jnp.abs(gt)))
        tri = tri_ref[d]
        mask = (col <= row) if d == 0 else (col >= row)
        l1, l2, l3 = _split3(ls)
        bcol_all = _dot(tri, l1) + _dot(tri, l2) + _dot(tri, l3)
        t1, t2, t3 = _split3(lst)
        trit = tri_ref[1 - d]
        brow_all = _dot(t1, trit) + _dot(t2, trit) + _dot(t3, trit)
        last = L - 1 if d == 0 else 0

        bal = pltpu.roll(bcol_all, LANES - 8, axis=1)
        cmax = g - bal
        trow = lax.broadcasted_iota(jnp.int32, (L, LANES), 0)
        sft = 1
        while sft < L:
            if d == 0:
                shifted = jnp.where(trow >= sft, pltpu.roll(cmax, sft, axis=0), -jnp.inf)
            else:
                shifted = jnp.where(trow < L - sft, pltpu.roll(cmax, L - sft, axis=0), -jnp.inf)
            cmax = jnp.maximum(cmax, shifted)
            sft *= 2
        m_row = m_scr[d][0:1, :]
        gg = bal + m_row
        mt = jnp.maximum(gg, bal + cmax)
        e_col = bal - mt
        w_inter = jnp.exp(gg - mt)
        enm = jnp.exp(-mt)
        tot = bal[last:last + 1, :]
        wlog = tot - bal + g
        m_new = jnp.maximum(tot + m_row, jnp.max(wlog, axis=0, keepdims=True))
        ws = jnp.exp(wlog - m_new) * scale
        decay = jnp.exp(tot + m_row - m_new)
        m_scr[d] = jnp.broadcast_to(m_new, m_scr.shape[1:])

        for h in range(nh):
            u = d * nh + h
            ci = 16 * d + h
            cf = 16 * d + 8 + h
            hs = slice(h * dh, (h + 1) * dh)
            qh, kh, vh = q[:, hs], k[:, hs], v[:, hs]
            vext = jnp.concatenate([vh, ones_col], axis=1)
            cr = gt[ci:ci + 1, :] - brow_all[cf:cf + 1, :] + math.log(scale)
            c_prev = c_scr[u]

            sqk = lax.dot_general(qh, kh, (((1,), (1,)), ((), ())), preferred_element_type=F32)
            w_intra = jnp.exp(jnp.where(mask, e_col[:, ci:ci + 1] + cr, -jnp.inf))
            a = (sqk * w_intra).astype(BF16)
            qs = (qh.astype(F32) * w_inter[:, ci:ci + 1]).astype(BF16)
            r = _dot(jnp.concatenate([a, qs], axis=1),
                     jnp.concatenate([vext, c_prev.astype(BF16)], axis=0))

            num = r[:, :dh]
            den = r[:, dh:dh + 1]
            rden = 1.0 / jnp.maximum(jnp.abs(den), enm[:, ci:ci + 1])
            ssq = jnp.sum(num * num, axis=-1, keepdims=True)
            sc = rden * lax.rsqrt(rden * rden * ssq * (1.0 / dh) + EPS)
            gate = _sigmoid(o_ref[0, :, hs].astype(F32))
            out_ref[0, :, hs] = (num * sc * hg_ref[h:h + 1, :] * gate).astype(BF16)

            kw = (kh.astype(F32) * ws[:, ci:ci + 1]).astype(BF16)
            upd = lax.dot_general(kw, vext, (((0,), (0,)), ((), ())), preferred_element_type=F32)
            c_scr[u] = decay[:, ci:ci + 1] * c_prev + upd


def _mlstm(u_lat, u_ctx, g_lat, g_ctx, head_g, L):
    b_, s_, _ = u_lat.shape
    n_ctx = u_ctx.shape[1]
    nh = MLSTM_HEADS
    w = 1024
    dh = w // nh
    nc, nl = n_ctx // L, s_ // L
    steps = nc + nl
    t0 = jnp.tril(jnp.ones((L, L), F32))
    tri = jnp.stack([t0, t0.T]).astype(BF16)

    def cidx(d):
        if d == 0:
            return lambda st: jnp.minimum(st, nc - 1), lambda st: jnp.maximum(st - nc, 0)
        return lambda st: jnp.maximum(nc - 1 - st, 0), lambda st: jnp.minimum(nl - 1 + nc - st, nl - 1)

    in_specs, args = [], []
    for d in range(2):
        fc, fl = cidx(d)
        for t in range(3):
            in_specs.append(pl.BlockSpec((1, L, w), lambda b, st, fc=fc, t=t: (b, fc(st), t)))
            args.append(u_ctx)
        for t in (COL_Q, COL_K, COL_V):
            in_specs.append(pl.BlockSpec((1, L, w), lambda b, st, fl=fl, t=t: (b, fl(st), t)))
            args.append(u_lat)
        in_specs.append(pl.BlockSpec((1, L, LANES), lambda b, st, fc=fc: (b, fc(st), 0)))
        args.append(g_ctx)
        in_specs.append(pl.BlockSpec((1, L, LANES), lambda b, st, fl=fl: (b, fl(st), 0)))
        args.append(g_lat)
        in_specs.append(pl.BlockSpec((1, L, w), lambda b, st, fl=fl, t=COL_OF + d: (b, fl(st), t)))
        args.append(u_lat)
    in_specs += [pl.BlockSpec((nh, dh), lambda b, st: (0, 0)),
                 pl.BlockSpec((2, L, L), lambda b, st: (0, 0, 0))]
    args += [head_g, tri]
    out_specs = tuple(pl.BlockSpec((1, L, w), lambda b, st, fl=cidx(d)[1]: (b, fl(st), 0)) for d in range(2))
    return pl.pallas_call(
        functools.partial(_mlstm_kernel, nc, nl, L, dh),
        out_shape=(jax.ShapeDtypeStruct((b_, s_, w), BF16),) * 2,
        grid=(b_, steps),
        in_specs=in_specs,
        out_specs=out_specs,
        scratch_shapes=[pltpu.VMEM((2 * nh, dh, 2 * dh), F32), pltpu.VMEM((2, 8, LANES), F32)],
        compiler_params=_cparams(("parallel", "arbitrary")),
        name="mlstm",
    )(*args)


SUBLANES = 8
LRU_BLK = SUBLANES * SUBLANES


def _rglru_kernel(n_ctx, n_tot, ux_ref, cw_ref, cb_ref, w_ref, b_ref, lam_ref, out_ref,
                  us, z_nat, a_f, b_f, a_b, b_b, h_f, h_b, o_nat):
    c = LANES
    g = SUBLANES
    n_lat = n_tot - n_ctx
    nq = n_tot // g
    nblk = n_tot // LRU_BLK
    nblk_ctx = n_ctx // LRU_BLK

    zeros_g = jnp.zeros((g, c), F32)
    us[0:g, :] = zeros_g
    us[g + n_ctx:2 * g + n_ctx, :] = zeros_g
    us[2 * g + n_tot:3 * g + n_tot, :] = zeros_g
    us[g:g + n_ctx, :] = ux_ref[0, 0:n_ctx, :].astype(F32)
    us[2 * g + n_ctx:2 * g + n_tot, :] = ux_ref[0, n_ctx:n_tot, :].astype(F32)

    def conv(base, n):
        acc = cb_ref[...] + cw_ref[2:3, :] * us[base:base + n, :]
        for j, off in ((0, -2), (1, -1), (3, 1)):
            acc = acc + cw_ref[j:j + 1, :] * us[base + off:base + off + n, :]
        return acc

    z_nat[0:n_ctx, :] = conv(g, n_ctx)
    z_nat[n_ctx:n_tot, :] = conv(2 * g + n_ctx, n_lat)

    z = jnp.concatenate([z_nat[pl.ds(k, nq, stride=g), :] for k in range(g)], axis=0)
    p = _dot(z.astype(BF16), w_ref[0]) + b_ref[...]
    for d, (a_scr, b_scr) in enumerate(((a_f, b_f), (a_b, b_b))):
        r = _sigmoid(p[:, (2 * d) * c:(2 * d + 1) * c])
        i = _sigmoid(p[:, (2 * d + 1) * c:(2 * d + 2) * c])
        lam = lam_ref[d:d + 1, :]
        softplus = jnp.maximum(-lam, 0.0) + jnp.log(1.0 + jnp.exp(-jnp.abs(lam)))
        a = jnp.exp2((-LRU_C * math.log2(math.e) * softplus) * r)
        a_scr[...] = a
        b_scr[...] = jnp.sqrt(1.0 - a * a) * i * z

    sub = lax.broadcasted_iota(jnp.int32, (g, c), 0)

    def block_scan(a_scr, b_scr, h_scr, blk, carry, reverse):
        r0 = pl.multiple_of(blk * g, g)
        order = range(g - 1, -1, -1) if reverse else range(g)
        hs, ps = {}, {}
        h = p_ = None
        for k in order:
            a = a_scr[pl.ds(k * nq + r0, g), :]
            b = b_scr[pl.ds(k * nq + r0, g), :]
            h = b if h is None else a * h + b
            p_ = a if p_ is None else a * p_
            hs[k], ps[k] = h, p_
        def shift(x, sft, fill):
            if reverse:
                return jnp.where(sub < g - sft, pltpu.roll(x, g - sft, 0), fill)
            return jnp.where(sub >= sft, pltpu.roll(x, sft, 0), fill)
        pe, he = shift(p_, 1, 1.0), shift(h, 1, 0.0)
        for sft in (1, 2, 4):
            he = pe * shift(he, sft, 0.0) + he
            pe = pe * shift(pe, sft, 1.0)
        cin = pe * carry + he
        for k in order:
            h_scr[pl.ds(k * nq + r0, g), :] = hs[k] + ps[k] * cin
        last = 0 if reverse else g - 1
        tot = p_ * cin + h
        return tot[last:last + 1, :]

    def body(it, carry):
        cf, cb = carry
        cf = block_scan(a_f, b_f, h_f, it, cf, False)
        jb = jnp.where(it < nblk_ctx, nblk_ctx - 1 - it, nblk - 1 + nblk_ctx - it)
        cb = block_scan(a_b, b_b, h_b, jb, cb, True)
        return cf, cb

    zero = jnp.zeros((1, c), F32)
    lax.fori_loop(0, nblk, body, (zero, zero), unroll=2)

    q_ctx = n_ctx // g
    for k in range(g):
        o_nat[pl.ds(k, n_lat // g, stride=g), :] = (h_f[k * nq + q_ctx:(k + 1) * nq, :]
                                                    + h_b[k * nq + q_ctx:(k + 1) * nq, :])
    out_ref[0] = o_nat[...].astype(BF16)


def _rglru(ux_seq, conv_w, conv_b, w_cat, b_cat, lam, n_ctx):
    b_, n_tot, w = ux_seq.shape
    nb = LRU_BLOCKS
    c = w // nb
    return pl.pallas_call(
        functools.partial(_rglru_kernel, n_ctx, n_tot),
        out_shape=jax.ShapeDtypeStruct((b_, n_tot - n_ctx, w), BF16),
        grid=(b_, nb),
        in_specs=[
            pl.BlockSpec((1, n_tot, c), lambda b, k: (b, 0, k)),
            pl.BlockSpec((4, c), lambda b, k: (0, k)),
            pl.BlockSpec((1, c), lambda b, k: (0, k)),
            pl.BlockSpec((1, c, 4 * c), lambda b, k: (k, 0, 0)),
            pl.BlockSpec((1, 4 * c), lambda b, k: (0, k)),
            pl.BlockSpec((2, c), lambda b, k: (0, k)),
        ],
        out_specs=pl.BlockSpec((1, n_tot - n_ctx, c), lambda b, k: (b, 0, k)),
        scratch_shapes=([pltpu.VMEM((n_tot + 3 * SUBLANES, c), F32)] + [pltpu.VMEM((n_tot, c), F32)] * 7
                        + [pltpu.VMEM((n_tot - n_ctx, c), F32)]),
        compiler_params=_cparams(("parallel", "parallel")),
        name="rglru",
    )(ux_seq, conv_w, conv_b, w_cat, b_cat, lam)


def _outproj_kernel(yaf_ref, yab_ref, hr_ref, ug_ref, ga0_ref, ga1_ref, gb0_ref, gb1_ref, x_ref,
                    g1_ref, gain_ref, shift_ref, wpa_ref, wpb_ref, wout_ref, wr_ref,
                    x1_ref, h2_ref, lg_ref):
    ya = (yaf_ref[0].astype(F32) + yab_ref[0].astype(F32)).astype(BF16)
    ug = ug_ref[0].astype(F32)
    gelu = 0.5 * ug * (1.0 + jnp.tanh(0.7978845608028654 * (ug + 0.044715 * ug * ug * ug)))
    yb = (hr_ref[0].astype(F32) * gelu).astype(BF16)
    pa = _dot(ya, wpa_ref[...])
    pb = _dot(yb, wpb_ref[...])
    ga = jnp.concatenate([ga0_ref[0], ga1_ref[0]], axis=1).astype(F32)
    gb = jnp.concatenate([gb0_ref[0], gb1_ref[0]], axis=1).astype(F32)
    mix = (_sigmoid(ga) * pa + _sigmoid(gb) * pb).astype(BF16)
    x1 = x_ref[0] + g1_ref[0] * _dot(mix, wout_ref[...])
    x1_ref[0] = x1
    ms = jnp.mean(x1 * x1, axis=-1, keepdims=True)
    h2 = x1 * lax.rsqrt(ms + EPS) * gain_ref[0] + shift_ref[0]
    ns = h2.shape[1] // LANES
    for s in range(ns):
        h2_ref[pl.ds(s, h2.shape[0], stride=ns), :] = h2[:, s * LANES:(s + 1) * LANES]
    wr = wr_ref[...]
    h1, h2b, _ = _split3(h2)
    w1, w2b, _ = _split3(wr)
    lg_ref[0] = _dot(h1, w1) + (_dot(h1, w2b) + _dot(h2b, w1))


def _outproj(ya_f, ya_b, hr, u_lat, x, g1, gain2, shift2, w_pa, w_pb, w_out, w_r, tm):
    b_, s_, d = x.shape
    w = ya_f.shape[2]
    row = lambda b, i: (b, i, 0)
    const2 = lambda b, i: (0, 0)
    bmap = lambda b, i: (b, 0, 0)
    ucol = lambda t: (lambda b, i: (b, i, t))
    single = pl.Buffered(1)
    return pl.pallas_call(
        _outproj_kernel,
        out_shape=(jax.ShapeDtypeStruct((b_, s_, d), F32),
                   jax.ShapeDtypeStruct((b_ * s_ * (d // LANES), LANES), F32),
                   jax.ShapeDtypeStruct((b_, s_, LANES), F32)),
        grid=(b_, s_ // tm),
        in_specs=[
            pl.BlockSpec((1, tm, w), row), pl.BlockSpec((1, tm, w), row), pl.BlockSpec((1, tm, w), row),
            pl.BlockSpec((1, tm, w), ucol(COL_UG)),
            pl.BlockSpec((1, tm, w), ucol(COL_GA)), pl.BlockSpec((1, tm, w), ucol(COL_GA + 1)),
            pl.BlockSpec((1, tm, w), ucol(COL_GB)), pl.BlockSpec((1, tm, w), ucol(COL_GB + 1)),
            pl.BlockSpec((1, tm, d), row),
            pl.BlockSpec((1, 1, d), bmap), pl.BlockSpec((1, 1, d), bmap), pl.BlockSpec((1, 1, d), bmap),
            pl.BlockSpec((w, d), const2, pipeline_mode=single),
            pl.BlockSpec((w, d), const2, pipeline_mode=single),
            pl.BlockSpec((d, d), const2, pipeline_mode=single),
            pl.BlockSpec((d, LANES), const2, pipeline_mode=single),
        ],
        out_specs=(pl.BlockSpec((1, tm, d), row),
                   pl.BlockSpec((tm * (d // LANES), LANES), lambda b, i: (b * (s_ // tm) + i, 0)),
                   pl.BlockSpec((1, tm, LANES), row)),
        compiler_params=_cparams(("parallel", "parallel")),
        name="outproj",
    )(ya_f, ya_b, hr, u_lat, u_lat, u_lat, u_lat, u_lat, x, g1, gain2, shift2, w_pa, w_pb, w_out, w_r)


def _route_kernel(lg_ref, o_ref):
    lg = lg_ref[...]
    lane = lax.broadcasted_iota(jnp.int32, lg.shape, 1)
    neg = -jnp.inf
    big = jnp.int32(1 << 20)
    g_l = jnp.where(lane < N_GROUPS, lg, neg)
    g_max = jnp.max(g_l, axis=-1, keepdims=True)
    g_sel = jnp.min(jnp.where(g_l == g_max, lane, big), axis=-1, keepdims=True)
    p_g = 1.0 / jnp.sum(jnp.exp(g_l - g_max), axis=-1, keepdims=True)
    lo = N_GROUPS + g_sel * EXPERTS_PER_GROUP
    e_l = jnp.where((lane >= lo) & (lane < lo + EXPERTS_PER_GROUP), lg, neg)
    v1 = jnp.max(e_l, axis=-1, keepdims=True)
    i1 = jnp.min(jnp.where(e_l == v1, lane, big), axis=-1, keepdims=True)
    e_l2 = jnp.where(lane == i1, neg, e_l)
    v2 = jnp.max(e_l2, axis=-1, keepdims=True)
    i2 = jnp.min(jnp.where(e_l2 == v2, lane, big), axis=-1, keepdims=True)
    e2 = jnp.exp(v2 - v1)
    w1 = p_g / (1.0 + e2)
    w2 = p_g * e2 / (1.0 + e2)
    out = jnp.where(lane == 0, (i1 - N_GROUPS).astype(F32),
          jnp.where(lane == 1, (i2 - N_GROUPS).astype(F32),
          jnp.where(lane == 2, w1, jnp.where(lane == 3, w2, 0.0))))
    o_ref[...] = out


def _route(logits, tm):
    t_, _ = logits.shape
    return pl.pallas_call(
        _route_kernel,
        out_shape=jax.ShapeDtypeStruct((t_, LANES), F32),
        grid=(t_ // tm,),
        in_specs=[pl.BlockSpec((tm, LANES), lambda i: (i, 0))],
        out_specs=pl.BlockSpec((tm, LANES), lambda i: (i, 0)),
        compiler_params=_cparams(("parallel",)),
        name="route",
    )(logits)


MOE_DMA_UNROLL = 32


MOE_W_DEPTH = 3
MOE_W_SPLIT = 2


def _moe_kernel(R, SUB, nf, ni, item_e, item_rows, tok_ref, tokn_ref, dst_ref, h2_hbm, w1_hbm, w3_hbm, w2_hbm,
                y_hbm, x_scr, xb_scr, acc_scr, y_scr, w1r, w3r, w2r, w1b, w3b, w2b, gsem, ssem, wsem):
    i = pl.program_id(0)
    j = pl.program_id(1)
    nsub_of = lambda r: (r + SUB - 1) // SUB
    rows = item_rows[i]
    nsub = nsub_of(rows)
    nsub_next = jnp.where(i + 1 < ni, nsub_of(item_rows[jnp.minimum(i + 1, ni - 1)]), 0)
    rows_prev = jnp.where(i > 0, item_rows[jnp.maximum(i - 1, 0)], 0)
    buf = i % 2
    groups = SUB // MOE_DMA_UNROLL

    d_model, tf = w1r.shape[1], w1r.shape[2]

    def weight_copies(c):
        it = c // nf
        jc = c % nf
        e = item_e[jnp.minimum(it, ni - 1)]
        slot = c % MOE_W_DEPTH
        f0 = pl.multiple_of(jc * tf, tf)
        out = []
        for h in range(MOE_W_SPLIT):
            ra, rb = d_model // MOE_W_SPLIT, tf // MOE_W_SPLIT
            out.append(pltpu.make_async_copy(w1_hbm.at[e, pl.ds(h * ra, ra), pl.ds(f0, tf)],
                                             w1r.at[slot, pl.ds(h * ra, ra), :], wsem.at[slot]))
            out.append(pltpu.make_async_copy(w3_hbm.at[e, pl.ds(h * ra, ra), pl.ds(f0, tf)],
                                             w3r.at[slot, pl.ds(h * ra, ra), :], wsem.at[slot]))
            out.append(pltpu.make_async_copy(w2_hbm.at[e, pl.ds(f0 + h * rb, rb), :],
                                             w2r.at[slot, pl.ds(h * rb, rb), :], wsem.at[slot]))
        return out

    def chunk_live(c):
        it = c // nf
        return (it < ni) & (item_rows[jnp.minimum(it, ni - 1)] > 0)

    cur = i * nf + j

    @pl.when((cur == 0) & chunk_live(0))
    def _():
        for cp in weight_copies(0):
            cp.start()

        @pl.when(chunk_live(1))
        def _():
            for cp in weight_copies(1):
                cp.start()

    @pl.when((rows > 0) & chunk_live(cur + 2))
    def _():
        for cp in weight_copies(cur + 2):
            cp.start()

    ns = acc_scr.shape[1] // LANES

    def tok_rows(ref, r):
        return ref.at[pl.ds(pl.multiple_of(r * ns, ns), ns), :]

    def gather_block(tref, b, q):
        def grp(gi, c):
            base = pl.multiple_of(q * SUB + gi * MOE_DMA_UNROLL, MOE_DMA_UNROLL)
            for k in range(MOE_DMA_UNROLL):
                tok = tref[0, 0, base + k]
                pltpu.make_async_copy(tok_rows(h2_hbm, tok), tok_rows(x_scr.at[b], base + k),
                                      gsem.at[b]).start(priority=1)
            return c
        lax.fori_loop(0, groups, grp, 0)

    def gather_wait_block(b):
        n = SUB * ns
        pltpu.make_async_copy(h2_hbm.at[pl.ds(0, n), :], x_scr.at[b, pl.ds(0, n), :], gsem.at[b]).wait()

    def scatter_copy(r):
        dst = dst_ref[0, 0, r]
        return pltpu.make_async_copy(tok_rows(y_scr, r), tok_rows(y_hbm, dst), ssem)

    def repeat(n, fn, lo=0):
        def body(q, c):
            fn(q)
            return c
        lax.fori_loop(lo, n, body, 0)

    def scatter_rows(n):
        def grp(gi):
            base = pl.multiple_of(gi * MOE_DMA_UNROLL, MOE_DMA_UNROLL)
            for k in range(MOE_DMA_UNROLL):
                scatter_copy(base + k).start(priority=k % 2)
        full = n // MOE_DMA_UNROLL
        repeat(full, grp)
        repeat(n, lambda r: scatter_copy(r).start(), lo=full * MOE_DMA_UNROLL)

    def scatter_wait_rows(n):
        u = MOE_DMA_UNROLL * ns
        full = n // MOE_DMA_UNROLL
        repeat(full, lambda q: pltpu.make_async_copy(y_scr.at[pl.ds(0, u), :], y_hbm.at[pl.ds(0, u), :],
                                                      ssem).wait())
        repeat(n, lambda r: scatter_copy(0).wait(), lo=full * MOE_DMA_UNROLL)

    @pl.when((i == 0) & (j == 0))
    def _():
        repeat(nsub, lambda q: gather_block(tok_ref, 0, q))

    @pl.when(j < nsub_next)
    def _():
        gather_block(tokn_ref, 1 - buf, j)

    @pl.when(j == 0)
    def _():
        repeat(nsub, lambda q: gather_wait_block(buf))

    @pl.when(j == nf - 1)
    def _():
        scatter_wait_rows(rows_prev)

    @pl.when(rows > 0)
    def _():
        for cp in weight_copies(cur):
            cp.wait()
        slot = cur % MOE_W_DEPTH
        w1b[...] = w1r[slot].astype(BF16)
        w3b[...] = w3r[slot].astype(BF16)
        w2b[...] = w2r[slot].astype(BF16)

        @pl.when(j == 0)
        def _():
            def unsplit(sb, c):
                r0 = pl.multiple_of(sb * SUB, SUB)
                s0 = pl.multiple_of(sb * (SUB * ns), SUB * ns)
                xb_scr[pl.ds(r0, SUB), :] = jnp.concatenate(
                    [x_scr[buf, pl.ds(s0 + s, SUB, stride=ns), :] for s in range(ns)], axis=1).astype(BF16)
                return c
            lax.fori_loop(0, nsub, unsplit, 0)

        def sub(sb, c):
            r0 = pl.multiple_of(sb * SUB, SUB)
            s0 = pl.multiple_of(sb * (SUB * ns), SUB * ns)
            x = xb_scr[pl.ds(r0, SUB), :]
            h1 = _dot(x, w1b[...])
            h3 = _dot(x, w3b[...])
            hh = (h1 * _sigmoid(h1) * h3).astype(BF16)
            contrib = _dot(hh, w2b[...])

            @pl.when(j == 0)
            def _():
                acc_scr[pl.ds(r0, SUB), :] = contrib

            @pl.when((j > 0) & (j < nf - 1))
            def _():
                acc_scr[pl.ds(r0, SUB), :] += contrib

            @pl.when(j == nf - 1)
            def _():
                tot = acc_scr[pl.ds(r0, SUB), :] + contrib
                for s in range(ns):
                    y_scr[pl.ds(s0 + s, SUB, stride=ns), :] = tot[:, s * LANES:(s + 1) * LANES]
            return c
        lax.fori_loop(0, nsub, sub, 0)

    @pl.when(j == nf - 1)
    def _():
        scatter_rows(rows)

        @pl.when(i == ni - 1)
        def _():
            scatter_wait_rows(rows)


def _moe(h2s, item_e, item_rows, slot_tok, slot_dst, w1, w3, w2, n_out_rows):
    ne, d, f = w1.shape
    ns = d // LANES
    R, SUB, tf = MOE_ITEM_ROWS, MOE_SUB_ROWS, MOE_F_CHUNK
    ni = item_e.shape[0]
    nf = f // tf
    assert R % SUB == 0 and R <= nf * SUB and nf >= 2
    grid_spec = pltpu.PrefetchScalarGridSpec(
        num_scalar_prefetch=2,
        grid=(ni, nf),
        in_specs=[
            pl.BlockSpec((1, 1, R), lambda i, j, ie, ir: (i, 0, 0), memory_space=pltpu.SMEM),
            pl.BlockSpec((1, 1, R), lambda i, j, ie, ir: (jnp.minimum(i + 1, ni - 1), 0, 0),
                         memory_space=pltpu.SMEM),
            pl.BlockSpec((1, 1, R), lambda i, j, ie, ir: (i, 0, 0), memory_space=pltpu.SMEM),
            pl.BlockSpec(memory_space=pl.ANY),
            pl.BlockSpec(memory_space=pl.ANY),
            pl.BlockSpec(memory_space=pl.ANY),
            pl.BlockSpec(memory_space=pl.ANY),
        ],
        out_specs=pl.BlockSpec(memory_space=pl.ANY),
        scratch_shapes=[
            pltpu.VMEM((2, R * ns, LANES), F32), pltpu.VMEM((R, d), BF16),
            pltpu.VMEM((R, d), F32), pltpu.VMEM((R * ns, LANES), F32),
            pltpu.VMEM((MOE_W_DEPTH, d, tf), F32), pltpu.VMEM((MOE_W_DEPTH, d, tf), F32),
            pltpu.VMEM((MOE_W_DEPTH, tf, d), F32),
            pltpu.VMEM((d, tf), BF16), pltpu.VMEM((d, tf), BF16), pltpu.VMEM((tf, d), BF16),
            pltpu.SemaphoreType.DMA((2,)), pltpu.SemaphoreType.DMA(()), pltpu.SemaphoreType.DMA((MOE_W_DEPTH,)),
        ],
    )
    return pl.pallas_call(
        functools.partial(_moe_kernel, R, SUB, nf, ni),
        out_shape=jax.ShapeDtypeStruct((n_out_rows * ns, LANES), F32),
        grid_spec=grid_spec,
        compiler_params=_cparams(("arbitrary", "arbitrary")),
        name="moe_experts",
    )(item_e, item_rows, slot_tok, slot_tok, slot_dst, h2s, w1, w3, w2)


def _moe_plan(expert, t_):
    R = MOE_ITEM_ROWS
    na = expert.shape[0]
    ni = N_EXPERTS + na // R
    onehot = (expert[:, None] == jnp.arange(N_EXPERTS, dtype=jnp.int32)[None, :]).astype(jnp.int32)
    csum = jnp.cumsum(onehot, axis=0)
    rank = jnp.sum(csum * onehot, axis=1) - 1
    counts = csum[-1]
    n_items = (counts + R - 1) // R
    item_end = jnp.cumsum(n_items)
    item_start = item_end - n_items
    slot = jnp.sum(item_start[None, :] * onehot, axis=1) * R + rank
    slot_a = jnp.full((ni * R,), -1, jnp.int32).at[slot].set(jnp.arange(na, dtype=jnp.int32))
    valid = slot_a >= 0
    slot_tok = jnp.where(valid, slot_a // TOP_K, 0)
    slot_dst = jnp.where(valid, (slot_a % TOP_K) * t_ + slot_a // TOP_K, 0)
    ii = jnp.arange(ni, dtype=jnp.int32)
    e_of = jnp.minimum(jnp.sum((item_end[None, :] <= ii[:, None]).astype(jnp.int32), axis=1), N_EXPERTS - 1)
    oh_e = (e_of[:, None] == jnp.arange(N_EXPERTS, dtype=jnp.int32)[None, :]).astype(jnp.int32)
    cnt_e = jnp.sum(oh_e * counts[None, :], axis=1)
    start_e = jnp.sum(oh_e * item_start[None, :], axis=1)
    rows = jnp.clip(cnt_e - (ii - start_e) * R, 0, R).astype(jnp.int32)
    live = ii < item_end[-1]
    rows = jnp.where(live, rows, 0)
    last_e = jnp.max(jnp.where(live, e_of, 0))
    e_of = jnp.where(live, e_of, last_e).astype(jnp.int32)
    return e_of, rows, slot_tok.reshape(ni, 1, R), slot_dst.reshape(ni, 1, R)


def _final_kernel(x1_ref, y0_ref, y1_ref, rt_ref, g2_ref, fg_ref, o_ref):
    rt = rt_ref[0]
    tm = rt.shape[0]
    ns = y0_ref.shape[0] // tm
    unsplit = lambda ref: jnp.concatenate([ref[pl.ds(s, tm, stride=ns), :] for s in range(ns)], axis=1)
    y = rt[:, TOP_K:TOP_K + 1] * unsplit(y0_ref) + rt[:, TOP_K + 1:TOP_K + 2] * unsplit(y1_ref)
    x = x1_ref[0] + g2_ref[0] * y
    ms = jnp.mean(x * x, axis=-1, keepdims=True)
    o_ref[0] = x * lax.rsqrt(ms + EPS) * fg_ref[...]


def _final(x1, ybuf, routed, g2, final_g, tm):
    b_, s_, d = x1.shape
    nt = s_ // tm
    ns = d // LANES
    return pl.pallas_call(
        _final_kernel,
        out_shape=jax.ShapeDtypeStruct((b_, s_, d), F32),
        grid=(b_, nt),
        in_specs=[
            pl.BlockSpec((1, tm, d), lambda b, i: (b, i, 0)),
            pl.BlockSpec((tm * ns, LANES), lambda b, i: (b * nt + i, 0)),
            pl.BlockSpec((tm * ns, LANES), lambda b, i: (b_ * nt + b * nt + i, 0)),
            pl.BlockSpec((1, tm, LANES), lambda b, i: (b, i, 0)),
            pl.BlockSpec((1, 1, d), lambda b, i: (b, 0, 0)),
            pl.BlockSpec((1, d), lambda b, i: (0, 0)),
        ],
        out_specs=pl.BlockSpec((1, tm, d), lambda b, i: (b, i, 0)),
        compiler_params=_cparams(("parallel", "parallel")),
        name="final_norm",
    )(x1, ybuf, ybuf, routed, g2, final_g)


def kernel(x, c, ctx, c_ctx, w_mod, b_mod, norm1_g, w_in, b_gates, mlstm_head_g, conv_w, conv_b, lru_wa, lru_ba,
           lru_wx, lru_bx, lru_lam, w_pa, w_pb, w_out, norm2_g, w_rg, w_re, w1, w3, w2, final_g):
    b_, s_, d = x.shape
    n_ctx = ctx.shape[1]
    rows = s_ // GRID_W
    assert w_mod.shape[0] == 1, "single layer"
    wm = d // 2
    nh = MLSTM_HEADS

    cc = jnp.zeros((8, d), F32).at[:b_].set(c).at[b_].set(c_ctx)
    mod = _modulation(cc, w_mod[0], b_mod[0][None, :])
    sh1, sc1, g1, sh2, sc2, g2 = [mod[:b_, i * d:(i + 1) * d][:, None, :] for i in range(6)]
    csh1, csc1 = mod[b_:b_ + 1, 0:d][:, None, :], mod[b_:b_ + 1, d:2 * d][:, None, :]

    assert wm == 1024 and 4 * nh == N_GATE_COLS
    w_t = jnp.transpose(w_in[0])
    bg_pad = jnp.zeros((1, LANES), F32).at[0, :4 * nh].set(b_gates[0])

    ng = norm1_g[0][None, None, :]
    u_lat, g_lat = _inproj(x, ng * (1.0 + sc1), sh1, w_t, bg_pad, tuple(range(N_COL_TILES)), min(1024, s_))
    u_ctx, g_ctx = _inproj(ctx.reshape(1, b_ * n_ctx, d), ng * (1.0 + csc1), csh1, w_t, bg_pad,
                           (COL_Q, COL_K, COL_V, COL_UX), b_ * n_ctx)
    u_ctx = u_ctx.reshape(b_, n_ctx, -1)
    g_ctx = g_ctx.reshape(b_, n_ctx, LANES)

    ya_f, ya_b = _mlstm(u_lat, u_ctx, g_lat, g_ctx, mlstm_head_g[0], MLSTM_CHUNK)

    ux_lat = u_lat[:, :, COL_UX * wm:(COL_UX + 1) * wm]
    ux_col = ux_lat.reshape(b_, rows, GRID_W, wm).transpose(0, 2, 1, 3).reshape(b_, s_, wm)
    ux_seq = jnp.concatenate([u_ctx[:, :, CTX_UX * wm:(CTX_UX + 1) * wm], ux_col], axis=1)
    wa, wx = lru_wa[0], lru_wx[0]
    w_cat = jnp.concatenate([wa[0], wx[0], wa[1], wx[1]], axis=-1).astype(BF16)
    cblk = wm // LRU_BLOCKS
    blk = lambda v: v.reshape(LRU_BLOCKS, cblk)
    b_cat = jnp.concatenate([blk(lru_ba[0, 0]), blk(lru_bx[0, 0]), blk(lru_ba[0, 1]), blk(lru_bx[0, 1])],
                            axis=-1).reshape(1, 4 * wm)
    h_col = _rglru(ux_seq, conv_w[0], conv_b[0][None, :], w_cat, b_cat, lru_lam[0], n_ctx)
    hr = h_col.reshape(b_, GRID_W, rows, wm).transpose(0, 2, 1, 3).reshape(b_, s_, wm)

    w_r = jnp.zeros((d, LANES), F32).at[:, :N_GROUPS].set(w_rg[0]).at[:, N_GROUPS:N_GROUPS + N_EXPERTS].set(w_re[0])
    n2 = norm2_g[0][None, None, :]
    x1, h2, logits = _outproj(ya_f, ya_b, hr, u_lat, x, g1, n2 * (1.0 + sc2), sh2,
                              w_pa[0].astype(BF16), w_pb[0].astype(BF16), w_out[0].astype(BF16), w_r, 256)

    t_ = b_ * s_
    routed = _route(logits.reshape(t_, LANES), 512)
    expert = routed[:, :TOP_K].astype(jnp.int32).reshape(-1)
    item_e, item_rows, slot_tok, slot_dst = _moe_plan(expert, t_)
    ybuf = _moe(h2, item_e, item_rows, slot_tok, slot_dst, w1[0], w3[0], w2[0], TOP_K * t_)

    return _final(x1, ybuf, routed.reshape(b_, s_, LANES), g2, final_g[None, :], 256)
```

```python
import functools
import math

import jax
import jax.numpy as jnp
from jax import lax
from jax.experimental import pallas as pl
from jax.experimental.pallas import tpu as pltpu

F32 = jnp.float32
BF16 = jnp.bfloat16

EPS = 1e-6
GRID_W = 64
MLSTM_HEADS = 8
LRU_BLOCKS = 8
LRU_C = 8.0
N_GROUPS = 4
EXPERTS_PER_GROUP = 8
N_EXPERTS = N_GROUPS * EXPERTS_PER_GROUP
TOP_K = 2

VMEM_LIMIT_BYTES = 56 * 1024 * 1024
LANES = 128

MLSTM_CHUNK = 128
MOE_ITEM_ROWS = 768
MOE_SUB_ROWS = 256
MOE_F_CHUNK = 256


def _cparams(sem):
    return pltpu.CompilerParams(dimension_semantics=sem, vmem_limit_bytes=VMEM_LIMIT_BYTES)


def _sigmoid(x):
    return 0.5 * jnp.tanh(0.5 * x) + 0.5


def _dot(a, b):
    return jnp.dot(a, b, preferred_element_type=F32)


def _split3(x):
    x1 = x.astype(BF16)
    r1 = x - x1.astype(F32)
    x2 = r1.astype(BF16)
    x3 = (r1 - x2.astype(F32)).astype(BF16)
    return x1, x2, x3


def _mod_kernel(c_ref, w_ref, b_ref, o_ref):
    c = c_ref[...]
    s = (c * _sigmoid(c)).astype(BF16)
    o_ref[...] = _dot(s, w_ref[...].astype(BF16)) + b_ref[...]


def _modulation(cc, w_mod, b_mod):
    m, d = cc.shape
    n = w_mod.shape[1]
    tn = 1024
    return pl.pallas_call(
        _mod_kernel,
        out_shape=jax.ShapeDtypeStruct((m, n), F32),
        grid=(n // tn,),
        in_specs=[
            pl.BlockSpec((m, d), lambda j: (0, 0)),
            pl.BlockSpec((d, tn), lambda j: (0, j)),
            pl.BlockSpec((1, tn), lambda j: (0, j)),
        ],
        out_specs=pl.BlockSpec((m, tn), lambda j: (0, j)),
        compiler_params=_cparams(("parallel",)),
        name="modulation",
    )(cc, w_mod, b_mod)


N_PRE_TILES = 5
N_GATE_COLS = 32


def _dot_nt(a, b):
    return lax.dot_general(a, b, (((1,), (1,)), ((), ())), preferred_element_type=F32)


def _inproj_kernel(x_ref, gain_ref, shift_ref, wt_ref, wg_ref, bg_ref, u_ref, g_ref, h_scr):
    @pl.when(pl.program_id(2) == 0)
    def _():
        x = x_ref[0]
        ms = jnp.mean(x * x, axis=-1, keepdims=True)
        h = x * lax.rsqrt(ms + EPS) * gain_ref[0] + shift_ref[0]
        hb = h.astype(BF16)
        h_scr[...] = hb
        gates = _dot_nt(hb, wg_ref[...].astype(BF16))
        pad = jnp.zeros((gates.shape[0], LANES - N_GATE_COLS), F32)
        g_ref[0] = jnp.concatenate([gates, pad], axis=1) + bg_ref[...]

    u_ref[0] = _dot_nt(h_scr[...], wt_ref[...].astype(BF16)).astype(BF16)


def _inproj(x, gain, shift, w_t, b_gates, tiles, tm):
    b_, s_, d = x.shape
    tn = 1024
    nj = len(tiles)
    if tiles == tuple(range(nj)):
        tile_of = lambda j: j
    else:
        assert tiles == (COL_Q, COL_K, COL_V, COL_UX), tiles
        tile_of = lambda j: jnp.where(j == 3, COL_UX, j)
    row_of = lambda j: tile_of(j) * tn + jnp.where(tile_of(j) >= N_PRE_TILES, N_GATE_COLS, 0)
    per_batch_mod = gain.shape[0] > 1
    mmap = (lambda b, i, j: (b, 0, 0)) if per_batch_mod else (lambda b, i, j: (0, 0, 0))
    gate_blk = N_PRE_TILES * tn // N_GATE_COLS
    return pl.pallas_call(
        _inproj_kernel,
        out_shape=(jax.ShapeDtypeStruct((b_, s_, nj * tn), BF16),
                   jax.ShapeDtypeStruct((b_, s_, LANES), F32)),
        grid=(b_, s_ // tm, nj),
        in_specs=[
            pl.BlockSpec((1, tm, d), lambda b, i, j: (b, i, 0)),
            pl.BlockSpec((1, 1, d), mmap),
            pl.BlockSpec((1, 1, d), mmap),
            pl.BlockSpec((pl.Element(tn), pl.Element(d)), lambda b, i, j: (pl.multiple_of(row_of(j), 8), 0)),
            pl.BlockSpec((N_GATE_COLS, d), lambda b, i, j: (gate_blk, 0)),
            pl.BlockSpec((1, LANES), lambda b, i, j: (0, 0)),
        ],
        out_specs=(pl.BlockSpec((1, tm, tn), lambda b, i, j: (b, i, j)),
                   pl.BlockSpec((1, tm, LANES), lambda b, i, j: (b, i, 0))),
        scratch_shapes=[pltpu.VMEM((tm, d), BF16)],
        compiler_params=_cparams(("parallel", "parallel", "arbitrary")),
        name="inproj",
    )(x, gain, shift, w_t, w_t, b_gates)


COL_Q, COL_K, COL_V, COL_OF, COL_OB, COL_UX, COL_UG, COL_GA, COL_GB = 0, 1, 2, 3, 4, 5, 6, 7, 9
CTX_UX = 3
N_COL_TILES = 11


def _mlstm_kernel(nc, nl, L, dh,
                  qc_f, kc_f, vc_f, ql_f, kl_f, vl_f, gc_f, gl_f, o_f,
                  qc_b, kc_b, vc_b, ql_b, kl_b, vl_b, gc_b, gl_b, o_b,
                  hg_ref, tri_ref,
                  out_f, out_b, c_scr, m_scr):
    s = pl.program_id(1)
    nh = MLSTM_HEADS
    scale = dh ** -0.5

    @pl.when(s == 0)
    def _():
        c_scr[...] = jnp.zeros_like(c_scr)
        m_scr[...] = jnp.full_like(m_scr, -1e30)

    is_ctx = s < nc
    ones_col = (lax.broadcasted_iota(jnp.int32, (L, dh), 1) == 0).astype(BF16)
    row = lax.broadcasted_iota(jnp.int32, (L, L), 0)
    col = lax.broadcasted_iota(jnp.int32, (L, L), 1)

    for d, (qc, kc, vc, ql, kl, vl, gc, gl, o_ref, out_ref) in enumerate((
            (qc_f, kc_f, vc_f, ql_f, kl_f, vl_f, gc_f, gl_f, o_f, out_f),
            (qc_b, kc_b, vc_b, ql_b, kl_b, vl_b, gc_b, gl_b, o_b, out_b))):
        q = jnp.where(is_ctx, qc[0], ql[0])
        k = jnp.where(is_ctx, kc[0], kl[0])
        v = jnp.where(is_ctx, vc[0], vl[0])
        g = jnp.where(is_ctx, gc[0], gl[0])
        gt = g.T
        ls = jnp.minimum(g, 0.0) - jnp.log(1.0 + jnp.exp(-jnp.abs(g)))
        lst = jnp.minimum(gt, 0.0) - jnp.log(1.0 + jnp.exp(-jnp.abs(gt)))
        tri = tri_ref[d]
        mask = (col <= row) if d == 0 else (col >= row)
        l1, l2, l3 = _split3(ls)
        bcol_all = _dot(tri, l1) + _dot(tri, l2) + _dot(tri, l3)
        t1, t2, t3 = _split3(lst)
        trit = tri_ref[1 - d]
        brow_all = _dot(t1, trit) + _dot(t2, trit) + _dot(t3, trit)
        last = L - 1 if d == 0 else 0

        bal = pltpu.roll(bcol_all, LANES - 8, axis=1)
        cmax = g - bal
        trow = lax.broadcasted_iota(jnp.int32, (L, LANES), 0)
        sft = 1
        while sft < L:
            if d == 0:
                shifted = jnp.where(trow >= sft, pltpu.roll(cmax, sft, axis=0), -jnp.inf)
            else:
                shifted = jnp.where(trow < L - sft, pltpu.roll(cmax, L - sft, axis=0), -jnp.inf)
            cmax = jnp.maximum(cmax, shifted)
            sft *= 2
        m_row = m_scr[d][0:1, :]
        gg = bal + m_row
        mt = jnp.maximum(gg, bal + cmax)
        e_col = bal - mt
        w_inter = jnp.exp(gg - mt)
        enm = jnp.exp(-mt)
        tot = bal[last:last + 1, :]
        wlog = tot - bal + g
        m_new = jnp.maximum(tot + m_row, jnp.max(wlog, axis=0, keepdims=True))
        ws = jnp.exp(wlog - m_new) * scale
        decay = jnp.exp(tot + m_row - m_new)
        m_scr[d] = jnp.broadcast_to(m_new, m_scr.shape[1:])

        for h in range(nh):
            u = d * nh + h
            ci = 16 * d + h
            cf = 16 * d + 8 + h
            hs = slice(h * dh, (h + 1) * dh)
            qh, kh, vh = q[:, hs], k[:, hs], v[:, hs]
            vext = jnp.concatenate([vh, ones_col], axis=1)
            cr = gt[ci:ci + 1, :] - brow_all[cf:cf + 1, :] + math.log(scale)
            c_prev = c_scr[u]

            sqk = lax.dot_general(qh, kh, (((1,), (1,)), ((), ())), preferred_element_type=F32)
            w_intra = jnp.exp(jnp.where(mask, e_col[:, ci:ci + 1] + cr, -jnp.inf))
            a = (sqk * w_intra).astype(BF16)
            qs = (qh.astype(F32) * w_inter[:, ci:ci + 1]).astype(BF16)
            r = _dot(jnp.concatenate([a, qs], axis=1),
                     jnp.concatenate([vext, c_prev.astype(BF16)], axis=0))

            num = r[:, :dh]
            den = r[:, dh:dh + 1]
            rden = 1.0 / jnp.maximum(jnp.abs(den), enm[:, ci:ci + 1])
            ssq = jnp.sum(num * num, axis=-1, keepdims=True)
            sc = rden * lax.rsqrt(rden * rden * ssq * (1.0 / dh) + EPS)
            gate = _sigmoid(o_ref[0, :, hs].astype(F32))
            out_ref[0, :, hs] = (num * sc * hg_ref[h:h + 1, :] * gate).astype(BF16)

            kw = (kh.astype(F32) * ws[:, ci:ci + 1]).astype(BF16)
            upd = lax.dot_general(kw, vext, (((0,), (0,)), ((), ())), preferred_element_type=F32)
            c_scr[u] = decay[:, ci:ci + 1] * c_prev + upd


def _mlstm(u_lat, u_ctx, g_lat, g_ctx, head_g, L):
    b_, s_, _ = u_lat.shape
    n_ctx = u_ctx.shape[1]
    nh = MLSTM_HEADS
    w = 1024
    dh = w // nh
    nc, nl = n_ctx // L, s_ // L
    steps = nc + nl
    t0 = jnp.tril(jnp.ones((L, L), F32))
    tri = jnp.stack([t0, t0.T]).astype(BF16)

    def cidx(d):
        if d == 0:
            return lambda st: jnp.minimum(st, nc - 1), lambda st: jnp.maximum(st - nc, 0)
        return lambda st: jnp.maximum(nc - 1 - st, 0), lambda st: jnp.minimum(nl - 1 + nc - st, nl - 1)

    in_specs, args = [], []
    for d in range(2):
        fc, fl = cidx(d)
        for t in range(3):
            in_specs.append(pl.BlockSpec((1, L, w), lambda b, st, fc=fc, t=t: (b, fc(st), t)))
            args.append(u_ctx)
        for t in (COL_Q, COL_K, COL_V):
            in_specs.append(pl.BlockSpec((1, L, w), lambda b, st, fl=fl, t=t: (b, fl(st), t)))
            args.append(u_lat)
        in_specs.append(pl.BlockSpec((1, L, LANES), lambda b, st, fc=fc: (b, fc(st), 0)))
        args.append(g_ctx)
        in_specs.append(pl.BlockSpec((1, L, LANES), lambda b, st, fl=fl: (b, fl(st), 0)))
        args.append(g_lat)
        in_specs.append(pl.BlockSpec((1, L, w), lambda b, st, fl=fl, t=COL_OF + d: (b, fl(st), t)))
        args.append(u_lat)
    in_specs += [pl.BlockSpec((nh, dh), lambda b, st: (0, 0)),
                 pl.BlockSpec((2, L, L), lambda b, st: (0, 0, 0))]
    args += [head_g, tri]
    out_specs = tuple(pl.BlockSpec((1, L, w), lambda b, st, fl=cidx(d)[1]: (b, fl(st), 0)) for d in range(2))
    return pl.pallas_call(
        functools.partial(_mlstm_kernel, nc, nl, L, dh),
        out_shape=(jax.ShapeDtypeStruct((b_, s_, w), BF16),) * 2,
        grid=(b_, steps),
        in_specs=in_specs,
        out_specs=out_specs,
        scratch_shapes=[pltpu.VMEM((2 * nh, dh, 2 * dh), F32), pltpu.VMEM((2, 8, LANES), F32)],
        compiler_params=_cparams(("parallel", "arbitrary")),
        name="mlstm",
    )(*args)


SUBLANES = 8
LRU_BLK = SUBLANES * SUBLANES


def _rglru_kernel(n_ctx, n_tot, ux_ref, cw_ref, cb_ref, w_ref, b_ref, lam_ref, out_ref,
                  us, z_nat, a_f, b_f, a_b, b_b, h_f, h_b, o_nat):
    c = LANES
    g = SUBLANES
    n_lat = n_tot - n_ctx
    nq = n_tot // g
    nblk = n_tot // LRU_BLK
    nblk_ctx = n_ctx // LRU_BLK

    zeros_g = jnp.zeros((g, c), F32)
    us[0:g, :] = zeros_g
    us[g + n_ctx:2 * g + n_ctx, :] = zeros_g
    us[2 * g + n_tot:3 * g + n_tot, :] = zeros_g
    us[g:g + n_ctx, :] = ux_ref[0, 0:n_ctx, :].astype(F32)
    us[2 * g + n_ctx:2 * g + n_tot, :] = ux_ref[0, n_ctx:n_tot, :].astype(F32)

    def conv(base, n):
        acc = cb_ref[...] + cw_ref[2:3, :] * us[base:base + n, :]
        for j, off in ((0, -2), (1, -1), (3, 1)):
            acc = acc + cw_ref[j:j + 1, :] * us[base + off:base + off + n, :]
        return acc

    z_nat[0:n_ctx, :] = conv(g, n_ctx)
    z_nat[n_ctx:n_tot, :] = conv(2 * g + n_ctx, n_lat)

    z = jnp.concatenate([z_nat[pl.ds(k, nq, stride=g), :] for k in range(g)], axis=0)
    p = _dot(z.astype(BF16), w_ref[0]) + b_ref[...]
    for d, (a_scr, b_scr) in enumerate(((a_f, b_f), (a_b, b_b))):
        r = _sigmoid(p[:, (2 * d) * c:(2 * d + 1) * c])
        i = _sigmoid(p[:, (2 * d + 1) * c:(2 * d + 2) * c])
        lam = lam_ref[d:d + 1, :]
        softplus = jnp.maximum(-lam, 0.0) + jnp.log(1.0 + jnp.exp(-jnp.abs(lam)))
        a = jnp.exp2((-LRU_C * math.log2(math.e) * softplus) * r)
        a_scr[...] = a
        b_scr[...] = jnp.sqrt(1.0 - a * a) * i * z

    sub = lax.broadcasted_iota(jnp.int32, (g, c), 0)

    def block_scan(a_scr, b_scr, h_scr, blk, carry, reverse):
        r0 = pl.multiple_of(blk * g, g)
        order = range(g - 1, -1, -1) if reverse else range(g)
        hs, ps = {}, {}
        h = p_ = None
        for k in order:
            a = a_scr[pl.ds(k * nq + r0, g), :]
            b = b_scr[pl.ds(k * nq + r0, g), :]
            h = b if h is None else a * h + b
            p_ = a if p_ is None else a * p_
            hs[k], ps[k] = h, p_
        def shift(x, sft, fill):
            if reverse:
                return jnp.where(sub < g - sft, pltpu.roll(x, g - sft, 0), fill)
            return jnp.where(sub >= sft, pltpu.roll(x, sft, 0), fill)
        pe, he = shift(p_, 1, 1.0), shift(h, 1, 0.0)
        for sft in (1, 2, 4):
            he = pe * shift(he, sft, 0.0) + he
            pe = pe * shift(pe, sft, 1.0)
        cin = pe * carry + he
        for k in order:
            h_scr[pl.ds(k * nq + r0, g), :] = hs[k] + ps[k] * cin
        last = 0 if reverse else g - 1
        tot = p_ * cin + h
        return tot[last:last + 1, :]

    def body(it, carry):
        cf, cb = carry
        cf = block_scan(a_f, b_f, h_f, it, cf, False)
        jb = jnp.where(it < nblk_ctx, nblk_ctx - 1 - it, nblk - 1 + nblk_ctx - it)
        cb = block_scan(a_b, b_b, h_b, jb, cb, True)
        return cf, cb

    zero = jnp.zeros((1, c), F32)
    lax.fori_loop(0, nblk, body, (zero, zero), unroll=2)

    q_ctx = n_ctx // g
    for k in range(g):
        o_nat[pl.ds(k, n_lat // g, stride=g), :] = (h_f[k * nq + q_ctx:(k + 1) * nq, :]
                                                    + h_b[k * nq + q_ctx:(k + 1) * nq, :])
    out_ref[0] = o_nat[...].astype(BF16)


def _rglru(ux_seq, conv_w, conv_b, w_cat, b_cat, lam, n_ctx):
    b_, n_tot, w = ux_seq.shape
    nb = LRU_BLOCKS
    c = w // nb
    return pl.pallas_call(
        functools.partial(_rglru_kernel, n_ctx, n_tot),
        out_shape=jax.ShapeDtypeStruct((b_, n_tot - n_ctx, w), BF16),
        grid=(b_, nb),
        in_specs=[
            pl.BlockSpec((1, n_tot, c), lambda b, k: (b, 0, k)),
            pl.BlockSpec((4, c), lambda b, k: (0, k)),
            pl.BlockSpec((1, c), lambda b, k: (0, k)),
            pl.BlockSpec((1, c, 4 * c), lambda b, k: (k, 0, 0)),
            pl.BlockSpec((1, 4 * c), lambda b, k: (0, k)),
            pl.BlockSpec((2, c), lambda b, k: (0, k)),
        ],
        out_specs=pl.BlockSpec((1, n_tot - n_ctx, c), lambda b, k: (b, 0, k)),
        scratch_shapes=([pltpu.VMEM((n_tot + 3 * SUBLANES, c), F32)] + [pltpu.VMEM((n_tot, c), F32)] * 7
                        + [pltpu.VMEM((n_tot - n_ctx, c), F32)]),
        compiler_params=_cparams(("parallel", "parallel")),
        name="rglru",
    )(ux_seq, conv_w, conv_b, w_cat, b_cat, lam)


def _outproj_kernel(yaf_ref, yab_ref, hr_ref, ug_ref, ga0_ref, ga1_ref, gb0_ref, gb1_ref, x_ref,
                    g1_ref, gain_ref, shift_ref, wpa_ref, wpb_ref, wout_ref, wr_ref,
                    x1_ref, h2_ref, lg_ref):
    ya = (yaf_ref[0].astype(F32) + yab_ref[0].astype(F32)).astype(BF16)
    ug = ug_ref[0].astype(F32)
    gelu = 0.5 * ug * (1.0 + jnp.tanh(0.7978845608028654 * (ug + 0.044715 * ug * ug * ug)))
    yb = (hr_ref[0].astype(F32) * gelu).astype(BF16)
    pa = _dot(ya, wpa_ref[...])
    pb = _dot(yb, wpb_ref[...])
    ga = jnp.concatenate([ga0_ref[0], ga1_ref[0]], axis=1).astype(F32)
    gb = jnp.concatenate([gb0_ref[0], gb1_ref[0]], axis=1).astype(F32)
    mix = (_sigmoid(ga) * pa + _sigmoid(gb) * pb).astype(BF16)
    x1 = x_ref[0] + g1_ref[0] * _dot(mix, wout_ref[...])
    x1_ref[0] = x1
    ms = jnp.mean(x1 * x1, axis=-1, keepdims=True)
    h2 = x1 * lax.rsqrt(ms + EPS) * gain_ref[0] + shift_ref[0]
    ns = h2.shape[1] // LANES
    for s in range(ns):
        h2_ref[pl.ds(s, h2.shape[0], stride=ns), :] = h2[:, s * LANES:(s + 1) * LANES]
    wr = wr_ref[...]
    h1, h2b, _ = _split3(h2)
    w1, w2b, _ = _split3(wr)
    lg_ref[0] = _dot(h1, w1) + (_dot(h1, w2b) + _dot(h2b, w1))


def _outproj(ya_f, ya_b, hr, u_lat, x, g1, gain2, shift2, w_pa, w_pb, w_out, w_r, tm):
    b_, s_, d = x.shape
    w = ya_f.shape[2]
    row = lambda b, i: (b, i, 0)
    const2 = lambda b, i: (0, 0)
    bmap = lambda b, i: (b, 0, 0)
    ucol = lambda t: (lambda b, i: (b, i, t))
    single = pl.Buffered(1)
    return pl.pallas_call(
        _outproj_kernel,
        out_shape=(jax.ShapeDtypeStruct((b_, s_, d), F32),
                   jax.ShapeDtypeStruct((b_ * s_ * (d // LANES), LANES), F32),
                   jax.ShapeDtypeStruct((b_, s_, LANES), F32)),
        grid=(b_, s_ // tm),
        in_specs=[
            pl.BlockSpec((1, tm, w), row), pl.BlockSpec((1, tm, w), row), pl.BlockSpec((1, tm, w), row),
            pl.BlockSpec((1, tm, w), ucol(COL_UG)),
            pl.BlockSpec((1, tm, w), ucol(COL_GA)), pl.BlockSpec((1, tm, w), ucol(COL_GA + 1)),
            pl.BlockSpec((1, tm, w), ucol(COL_GB)), pl.BlockSpec((1, tm, w), ucol(COL_GB + 1)),
            pl.BlockSpec((1, tm, d), row),
            pl.BlockSpec((1, 1, d), bmap), pl.BlockSpec((1, 1, d), bmap), pl.BlockSpec((1, 1, d), bmap),
            pl.BlockSpec((w, d), const2, pipeline_mode=single),
            pl.BlockSpec((w, d), const2, pipeline_mode=single),
            pl.BlockSpec((d, d), const2, pipeline_mode=single),
            pl.BlockSpec((d, LANES), const2, pipeline_mode=single),
        ],
        out_specs=(pl.BlockSpec((1, tm, d), row),
                   pl.BlockSpec((tm * (d // LANES), LANES), lambda b, i: (b * (s_ // tm) + i, 0)),
                   pl.BlockSpec((1, tm, LANES), row)),
        compiler_params=_cparams(("parallel", "parallel")),
        name="outproj",
    )(ya_f, ya_b, hr, u_lat, u_lat, u_lat, u_lat, u_lat, x, g1, gain2, shift2, w_pa, w_pb, w_out, w_r)


def _route_kernel(lg_ref, o_ref):
    lg = lg_ref[...]
    lane = lax.broadcasted_iota(jnp.int32, lg.shape, 1)
    neg = -jnp.inf
    big = jnp.int32(1 << 20)
    g_l = jnp.where(lane < N_GROUPS, lg, neg)
    g_max = jnp.max(g_l, axis=-1, keepdims=True)
    g_sel = jnp.min(jnp.where(g_l == g_max, lane, big), axis=-1, keepdims=True)
    p_g = 1.0 / jnp.sum(jnp.exp(g_l - g_max), axis=-1, keepdims=True)
    lo = N_GROUPS + g_sel * EXPERTS_PER_GROUP
    e_l = jnp.where((lane >= lo) & (lane < lo + EXPERTS_PER_GROUP), lg, neg)
    v1 = jnp.max(e_l, axis=-1, keepdims=True)
    i1 = jnp.min(jnp.where(e_l == v1, lane, big), axis=-1, keepdims=True)
    e_l2 = jnp.where(lane == i1, neg, e_l)
    v2 = jnp.max(e_l2, axis=-1, keepdims=True)
    i2 = jnp.min(jnp.where(e_l2 == v2, lane, big), axis=-1, keepdims=True)
    e2 = jnp.exp(v2 - v1)
    w1 = p_g / (1.0 + e2)
    w2 = p_g * e2 / (1.0 + e2)
    out = jnp.where(lane == 0, (i1 - N_GROUPS).astype(F32),
          jnp.where(lane == 1, (i2 - N_GROUPS).astype(F32),
          jnp.where(lane == 2, w1, jnp.where(lane == 3, w2, 0.0))))
    o_ref[...] = out


def _route(logits, tm):
    t_, _ = logits.shape
    return pl.pallas_call(
        _route_kernel,
        out_shape=jax.ShapeDtypeStruct((t_, LANES), F32),
        grid=(t_ // tm,),
        in_specs=[pl.BlockSpec((tm, LANES), lambda i: (i, 0))],
        out_specs=pl.BlockSpec((tm, LANES), lambda i: (i, 0)),
        compiler_params=_cparams(("parallel",)),
        name="route",
    )(logits)


MOE_DMA_UNROLL = 32


MOE_W_DEPTH = 3
MOE_W_SPLIT = 2


def _moe_kernel(R, SUB, nf, ni, item_e, item_rows, tok_ref, tokn_ref, dst_ref, dstp_ref, h2_hbm,
                w1_hbm, w3_hbm, w2_hbm,
                y_hbm, x_scr, xb_scr, acc_scr, y_scr, w1r, w3r, w2r, w1b, w3b, w2b, gsem, ssem, wsem):
    i = pl.program_id(0)
    j = pl.program_id(1)
    nsub_of = lambda r: (r + SUB - 1) // SUB
    rows = item_rows[i]
    nsub = nsub_of(rows)
    nsub_next = jnp.where(i + 1 < ni, nsub_of(item_rows[jnp.minimum(i + 1, ni - 1)]), 0)
    rows_prev = jnp.where(i > 0, item_rows[jnp.maximum(i - 1, 0)], 0)
    buf = i % 2
    groups = SUB // MOE_DMA_UNROLL

    d_model, tf = w1r.shape[1], w1r.shape[2]

    def weight_copies(c):
        it = c // nf
        jc = c % nf
        e = item_e[jnp.minimum(it, ni - 1)]
        slot = c % MOE_W_DEPTH
        f0 = pl.multiple_of(jc * tf, tf)
        out = []
        for h in range(MOE_W_SPLIT):
            ra, rb = d_model // MOE_W_SPLIT, tf // MOE_W_SPLIT
            out.append(pltpu.make_async_copy(w1_hbm.at[e, pl.ds(h * ra, ra), pl.ds(f0, tf)],
                                             w1r.at[slot, pl.ds(h * ra, ra), :], wsem.at[slot]))
            out.append(pltpu.make_async_copy(w3_hbm.at[e, pl.ds(h * ra, ra), pl.ds(f0, tf)],
                                             w3r.at[slot, pl.ds(h * ra, ra), :], wsem.at[slot]))
            out.append(pltpu.make_async_copy(w2_hbm.at[e, pl.ds(f0 + h * rb, rb), :],
                                             w2r.at[slot, pl.ds(h * rb, rb), :], wsem.at[slot]))
        return out

    def chunk_live(c):
        it = c // nf
        return (it < ni) & (item_rows[jnp.minimum(it, ni - 1)] > 0)

    cur = i * nf + j

    @pl.when((cur == 0) & chunk_live(0))
    def _():
        for cp in weight_copies(0):
            cp.start()

        @pl.when(chunk_live(1))
        def _():
            for cp in weight_copies(1):
                cp.start()

    @pl.when((rows > 0) & chunk_live(cur + 2))
    def _():
        for cp in weight_copies(cur + 2):
            cp.start()

    ns = acc_scr.shape[1] // LANES

    def tok_rows(ref, r):
        return ref.at[pl.ds(pl.multiple_of(r * ns, ns), ns), :]

    def gather_group(tref, b, gi):
        base = pl.multiple_of(gi * MOE_DMA_UNROLL, MOE_DMA_UNROLL)
        for k in range(MOE_DMA_UNROLL):
            tok = tref[0, 0, base + k]
            pltpu.make_async_copy(tok_rows(h2_hbm, tok), tok_rows(x_scr.at[b], base + k),
                                  gsem.at[b]).start(priority=1)

    def gather_wait_block(b):
        n = SUB * ns
        pltpu.make_async_copy(h2_hbm.at[pl.ds(0, n), :], x_scr.at[b, pl.ds(0, n), :], gsem.at[b]).wait()

    def scatter_copy(dref, r):
        dst = dref[0, 0, r]
        return pltpu.make_async_copy(tok_rows(y_scr, r), tok_rows(y_hbm, dst), ssem)

    def scatter_group(dref, gi):
        base = pl.multiple_of(gi * MOE_DMA_UNROLL, MOE_DMA_UNROLL)
        for k in range(MOE_DMA_UNROLL):
            scatter_copy(dref, base + k).start(priority=k % 2)

    def repeat(n, fn, lo=0):
        def body(q, c):
            fn(q)
            return c
        lax.fori_loop(lo, n, body, 0)

    def scatter_rows(dref, n):
        full = n // MOE_DMA_UNROLL
        repeat(full, lambda gi: scatter_group(dref, gi))
        repeat(n, lambda r: scatter_copy(dref, r).start(), lo=full * MOE_DMA_UNROLL)

    def scatter_wait_rows(n):
        u = MOE_DMA_UNROLL * ns
        full = n // MOE_DMA_UNROLL
        repeat(full, lambda q: pltpu.make_async_copy(y_scr.at[pl.ds(0, u), :], y_hbm.at[pl.ds(0, u), :],
                                                      ssem).wait())
        repeat(n, lambda r: scatter_copy(dst_ref, 0).wait(), lo=full * MOE_DMA_UNROLL)

    n_it = jnp.maximum(nsub * nf, 1)
    g_total = nsub_next * groups
    g_quota = (g_total + n_it - 1) // n_it
    s_full = rows_prev // MOE_DMA_UNROLL
    n_it_s = jnp.maximum(nsub * (nf - 1), 1)
    s_quota = (s_full + n_it_s - 1) // n_it_s

    @pl.when((i == 0) & (j == 0))
    def _():
        repeat(nsub * groups, lambda gi: gather_group(tok_ref, 0, gi))

    @pl.when(j == 0)
    def _():
        repeat(nsub, lambda q: gather_wait_block(buf))
        repeat(rows_prev, lambda r: scatter_copy(dstp_ref, r).start(), lo=s_full * MOE_DMA_UNROLL)
        repeat(jnp.where(rows > 0, 0, s_full), lambda gi: scatter_group(dstp_ref, gi))

    @pl.when(j == nf - 1)
    def _():
        scatter_wait_rows(rows_prev)

    @pl.when(rows > 0)
    def _():
        for cp in weight_copies(cur):
            cp.wait()
        slot = cur % MOE_W_DEPTH
        w1b[...] = w1r[slot].astype(BF16)
        w3b[...] = w3r[slot].astype(BF16)
        w2b[...] = w2r[slot].astype(BF16)

        @pl.when(j == 0)
        def _():
            def unsplit(sb, c):
                r0 = pl.multiple_of(sb * SUB, SUB)
                s0 = pl.multiple_of(sb * (SUB * ns), SUB * ns)
                xb_scr[pl.ds(r0, SUB), :] = jnp.concatenate(
                    [x_scr[buf, pl.ds(s0 + s, SUB, stride=ns), :] for s in range(ns)], axis=1).astype(BF16)
                return c
            lax.fori_loop(0, nsub, unsplit, 0)

        def sub(sb, c):
            r0 = pl.multiple_of(sb * SUB, SUB)
            s0 = pl.multiple_of(sb * (SUB * ns), SUB * ns)
            x = xb_scr[pl.ds(r0, SUB), :]
            h1 = _dot(x, w1b[...])
            h3 = _dot(x, w3b[...])
            hh = (h1 * _sigmoid(h1) * h3).astype(BF16)
            contrib = _dot(hh, w2b[...])

            @pl.when(j == 0)
            def _():
                acc_scr[pl.ds(r0, SUB), :] = contrib

            @pl.when((j > 0) & (j < nf - 1))
            def _():
                acc_scr[pl.ds(r0, SUB), :] += contrib

            @pl.when(j == nf - 1)
            def _():
                tot = acc_scr[pl.ds(r0, SUB), :] + contrib
                for s in range(ns):
                    y_scr[pl.ds(s0 + s, SUB, stride=ns), :] = tot[:, s * LANES:(s + 1) * LANES]

            t = j * nsub + sb
            repeat(jnp.minimum((t + 1) * g_quota, g_total), lambda gi: gather_group(tokn_ref, 1 - buf, gi),
                   lo=t * g_quota)

            @pl.when(j < nf - 1)
            def _():
                repeat(jnp.minimum((t + 1) * s_quota, s_full), lambda gi: scatter_group(dstp_ref, gi),
                       lo=t * s_quota)
            return c
        lax.fori_loop(0, nsub, sub, 0)

    @pl.when((j == nf - 1) & (i == ni - 1))
    def _():
        scatter_rows(dst_ref, rows)
        scatter_wait_rows(rows)


def _moe(h2s, item_e, item_rows, slot_tok, slot_dst, w1, w3, w2, n_out_rows):
    ne, d, f = w1.shape
    ns = d // LANES
    R, SUB, tf = MOE_ITEM_ROWS, MOE_SUB_ROWS, MOE_F_CHUNK
    ni = item_e.shape[0]
    nf = f // tf
    assert R % SUB == 0 and R <= nf * SUB and nf >= 2
    grid_spec = pltpu.PrefetchScalarGridSpec(
        num_scalar_prefetch=2,
        grid=(ni, nf),
        in_specs=[
            pl.BlockSpec((1, 1, R), lambda i, j, ie, ir: (i, 0, 0), memory_space=pltpu.SMEM),
            pl.BlockSpec((1, 1, R), lambda i, j, ie, ir: (jnp.minimum(i + 1, ni - 1), 0, 0),
                         memory_space=pltpu.SMEM),
            pl.BlockSpec((1, 1, R), lambda i, j, ie, ir: (i, 0, 0), memory_space=pltpu.SMEM),
            pl.BlockSpec((1, 1, R), lambda i, j, ie, ir: (jnp.maximum(i - 1, 0), 0, 0), memory_space=pltpu.SMEM),
            pl.BlockSpec(memory_space=pl.ANY),
            pl.BlockSpec(memory_space=pl.ANY),
            pl.BlockSpec(memory_space=pl.ANY),
            pl.BlockSpec(memory_space=pl.ANY),
        ],
        out_specs=pl.BlockSpec(memory_space=pl.ANY),
        scratch_shapes=[
            pltpu.VMEM((2, R * ns, LANES), F32), pltpu.VMEM((R, d), BF16),
            pltpu.VMEM((R, d), F32), pltpu.VMEM((R * ns, LANES), F32),
            pltpu.VMEM((MOE_W_DEPTH, d, tf), F32), pltpu.VMEM((MOE_W_DEPTH, d, tf), F32),
            pltpu.VMEM((MOE_W_DEPTH, tf, d), F32),
            pltpu.VMEM((d, tf), BF16), pltpu.VMEM((d, tf), BF16), pltpu.VMEM((tf, d), BF16),
            pltpu.SemaphoreType.DMA((2,)), pltpu.SemaphoreType.DMA(()), pltpu.SemaphoreType.DMA((MOE_W_DEPTH,)),
        ],
    )
    return pl.pallas_call(
        functools.partial(_moe_kernel, R, SUB, nf, ni),
        out_shape=jax.ShapeDtypeStruct((n_out_rows * ns, LANES), F32),
        grid_spec=grid_spec,
        compiler_params=_cparams(("arbitrary", "arbitrary")),
        name="moe_experts",
    )(item_e, item_rows, slot_tok, slot_tok, slot_dst, slot_dst, h2s, w1, w3, w2)


def _moe_plan(expert, t_):
    R = MOE_ITEM_ROWS
    na = expert.shape[0]
    ni = N_EXPERTS + na // R
    onehot = (expert[:, None] == jnp.arange(N_EXPERTS, dtype=jnp.int32)[None, :]).astype(jnp.int32)
    csum = jnp.cumsum(onehot, axis=0)
    rank = jnp.sum(csum * onehot, axis=1) - 1
    counts = csum[-1]
    n_items = (counts + R - 1) // R
    item_end = jnp.cumsum(n_items)
    item_start = item_end - n_items
    slot = jnp.sum(item_start[None, :] * onehot, axis=1) * R + rank
    slot_a = jnp.full((ni * R,), -1, jnp.int32).at[slot].set(jnp.arange(na, dtype=jnp.int32))
    valid = slot_a >= 0
    slot_tok = jnp.where(valid, slot_a // TOP_K, 0)
    slot_dst = jnp.where(valid, (slot_a % TOP_K) * t_ + slot_a // TOP_K, 0)
    ii = jnp.arange(ni, dtype=jnp.int32)
    e_of = jnp.minimum(jnp.sum((item_end[None, :] <= ii[:, None]).astype(jnp.int32), axis=1), N_EXPERTS - 1)
    oh_e = (e_of[:, None] == jnp.arange(N_EXPERTS, dtype=jnp.int32)[None, :]).astype(jnp.int32)
    cnt_e = jnp.sum(oh_e * counts[None, :], axis=1)
    start_e = jnp.sum(oh_e * item_start[None, :], axis=1)
    rows = jnp.clip(cnt_e - (ii - start_e) * R, 0, R).astype(jnp.int32)
    live = ii < item_end[-1]
    rows = jnp.where(live, rows, 0)
    last_e = jnp.max(jnp.where(live, e_of, 0))
    e_of = jnp.where(live, e_of, last_e).astype(jnp.int32)
    return e_of, rows, slot_tok.reshape(ni, 1, R), slot_dst.reshape(ni, 1, R)


def _final_kernel(x1_ref, y0_ref, y1_ref, rt_ref, g2_ref, fg_ref, o_ref):
    rt = rt_ref[0]
    tm = rt.shape[0]
    ns = y0_ref.shape[0] // tm
    unsplit = lambda ref: jnp.concatenate([ref[pl.ds(s, tm, stride=ns), :] for s in range(ns)], axis=1)
    y = rt[:, TOP_K:TOP_K + 1] * unsplit(y0_ref) + rt[:, TOP_K + 1:TOP_K + 2] * unsplit(y1_ref)
    x = x1_ref[0] + g2_ref[0] * y
    ms = jnp.mean(x * x, axis=-1, keepdims=True)
    o_ref[0] = x * lax.rsqrt(ms + EPS) * fg_ref[...]


def _final(x1, ybuf, routed, g2, final_g, tm):
    b_, s_, d = x1.shape
    nt = s_ // tm
    ns = d // LANES
    return pl.pallas_call(
        _final_kernel,
        out_shape=jax.ShapeDtypeStruct((b_, s_, d), F32),
        grid=(b_, nt),
        in_specs=[
            pl.BlockSpec((1, tm, d), lambda b, i: (b, i, 0)),
            pl.BlockSpec((tm * ns, LANES), lambda b, i: (b * nt + i, 0)),
            pl.BlockSpec((tm * ns, LANES), lambda b, i: (b_ * nt + b * nt + i, 0)),
            pl.BlockSpec((1, tm, LANES), lambda b, i: (b, i, 0)),
            pl.BlockSpec((1, 1, d), lambda b, i: (b, 0, 0)),
            pl.BlockSpec((1, d), lambda b, i: (0, 0)),
        ],
        out_specs=pl.BlockSpec((1, tm, d), lambda b, i: (b, i, 0)),
        compiler_params=_cparams(("parallel", "parallel")),
        name="final_norm",
    )(x1, ybuf, ybuf, routed, g2, final_g)


def kernel(x, c, ctx, c_ctx, w_mod, b_mod, norm1_g, w_in, b_gates, mlstm_head_g, conv_w, conv_b, lru_wa, lru_ba,
           lru_wx, lru_bx, lru_lam, w_pa, w_pb, w_out, norm2_g, w_rg, w_re, w1, w3, w2, final_g):
    b_, s_, d = x.shape
    n_ctx = ctx.shape[1]
    rows = s_ // GRID_W
    assert w_mod.shape[0] == 1, "single layer"
    wm = d // 2
    nh = MLSTM_HEADS

    cc = jnp.zeros((8, d), F32).at[:b_].set(c).at[b_].set(c_ctx)
    mod = _modulation(cc, w_mod[0], b_mod[0][None, :])
    sh1, sc1, g1, sh2, sc2, g2 = [mod[:b_, i * d:(i + 1) * d][:, None, :] for i in range(6)]
    csh1, csc1 = mod[b_:b_ + 1, 0:d][:, None, :], mod[b_:b_ + 1, d:2 * d][:, None, :]

    assert wm == 1024 and 4 * nh == N_GATE_COLS
    w_t = jnp.transpose(w_in[0])
    bg_pad = jnp.zeros((1, LANES), F32).at[0, :4 * nh].set(b_gates[0])

    ng = norm1_g[0][None, None, :]
    u_lat, g_lat = _inproj(x, ng * (1.0 + sc1), sh1, w_t, bg_pad, tuple(range(N_COL_TILES)), min(1024, s_))
    u_ctx, g_ctx = _inproj(ctx.reshape(1, b_ * n_ctx, d), ng * (1.0 + csc1), csh1, w_t, bg_pad,
                           (COL_Q, COL_K, COL_V, COL_UX), b_ * n_ctx)
    u_ctx = u_ctx.reshape(b_, n_ctx, -1)
    g_ctx = g_ctx.reshape(b_, n_ctx, LANES)

    ya_f, ya_b = _mlstm(u_lat, u_ctx, g_lat, g_ctx, mlstm_head_g[0], MLSTM_CHUNK)

    ux_lat = u_lat[:, :, COL_UX * wm:(COL_UX + 1) * wm]
    ux_col = ux_lat.reshape(b_, rows, GRID_W, wm).transpose(0, 2, 1, 3).reshape(b_, s_, wm)
    ux_seq = jnp.concatenate([u_ctx[:, :, CTX_UX * wm:(CTX_UX + 1) * wm], ux_col], axis=1)
    wa, wx = lru_wa[0], lru_wx[0]
    w_cat = jnp.concatenate([wa[0], wx[0], wa[1], wx[1]], axis=-1).astype(BF16)
    cblk = wm // LRU_BLOCKS
    blk = lambda v: v.reshape(LRU_BLOCKS, cblk)
    b_cat = jnp.concatenate([blk(lru_ba[0, 0]), blk(lru_bx[0, 0]), blk(lru_ba[0, 1]), blk(lru_bx[0, 1])],
                            axis=-1).reshape(1, 4 * wm)
    h_col = _rglru(ux_seq, conv_w[0], conv_b[0][None, :], w_cat, b_cat, lru_lam[0], n_ctx)
    hr = h_col.reshape(b_, GRID_W, rows, wm).transpose(0, 2, 1, 3).reshape(b_, s_, wm)

    w_r = jnp.zeros((d, LANES), F32).at[:, :N_GROUPS].set(w_rg[0]).at[:, N_GROUPS:N_GROUPS + N_EXPERTS].set(w_re[0])
    n2 = norm2_g[0][None, None, :]
    x1, h2, logits = _outproj(ya_f, ya_b, hr, u_lat, x, g1, n2 * (1.0 + sc2), sh2,
                              w_pa[0].astype(BF16), w_pb[0].astype(BF16), w_out[0].astype(BF16), w_r, 256)

    t_ = b_ * s_
    routed = _route(logits.reshape(t_, LANES), 512)
    expert = routed[:, :TOP_K].astype(jnp.int32).reshape(-1)
    item_e, item_rows, slot_tok, slot_dst = _moe_plan(expert, t_)
    ybuf = _moe(h2, item_e, item_rows, slot_tok, slot_dst, w1[0], w3[0], w2[0], TOP_K * t_)

    return _final(x1, ybuf, routed.reshape(b_, s_, LANES), g2, final_g[None, :], 256)
```

```python
import functools
import math

import jax
import jax.numpy as jnp
from jax import lax
from jax.experimental import pallas as pl
from jax.experimental.pallas import tpu as pltpu

F32 = jnp.float32
BF16 = jnp.bfloat16

EPS = 1e-6
GRID_W = 64
MLSTM_HEADS = 8
LRU_BLOCKS = 8
LRU_C = 8.0
N_GROUPS = 4
EXPERTS_PER_GROUP = 8
N_EXPERTS = N_GROUPS * EXPERTS_PER_GROUP
TOP_K = 2

VMEM_LIMIT_BYTES = 56 * 1024 * 1024
LANES = 128

MLSTM_CHUNK = 128
MOE_ITEM_ROWS = 768
MOE_SUB_ROWS = 256
MOE_F_CHUNK = 256


def _cparams(sem):
    return pltpu.CompilerParams(dimension_semantics=sem, vmem_limit_bytes=VMEM_LIMIT_BYTES)


def _sigmoid(x):
    return 0.5 * jnp.tanh(0.5 * x) + 0.5


def _dot(a, b):
    return jnp.dot(a, b, preferred_element_type=F32)


def _split3(x):
    x1 = x.astype(BF16)
    r1 = x - x1.astype(F32)
    x2 = r1.astype(BF16)
    x3 = (r1 - x2.astype(F32)).astype(BF16)
    return x1, x2, x3


def _mod_kernel(c_ref, w_ref, b_ref, o_ref):
    c = c_ref[...]
    s = (c * _sigmoid(c)).astype(BF16)
    o_ref[...] = _dot(s, w_ref[...].astype(BF16)) + b_ref[...]


def _modulation(cc, w_mod, b_mod):
    m, d = cc.shape
    n = w_mod.shape[1]
    tn = 1024
    return pl.pallas_call(
        _mod_kernel,
        out_shape=jax.ShapeDtypeStruct((m, n), F32),
        grid=(n // tn,),
        in_specs=[
            pl.BlockSpec((m, d), lambda j: (0, 0)),
            pl.BlockSpec((d, tn), lambda j: (0, j)),
            pl.BlockSpec((1, tn), lambda j: (0, j)),
        ],
        out_specs=pl.BlockSpec((m, tn), lambda j: (0, j)),
        compiler_params=_cparams(("parallel",)),
        name="modulation",
    )(cc, w_mod, b_mod)


N_PRE_TILES = 5
N_GATE_COLS = 32


def _dot_nt(a, b):
    return lax.dot_general(a, b, (((1,), (1,)), ((), ())), preferred_element_type=F32)


def _inproj_kernel(x_ref, gain_ref, shift_ref, wt_ref, wg_ref, bg_ref, u_ref, g_ref, h_scr):
    @pl.when(pl.program_id(2) == 0)
    def _():
        x = x_ref[0]
        ms = jnp.mean(x * x, axis=-1, keepdims=True)
        h = x * lax.rsqrt(ms + EPS) * gain_ref[0] + shift_ref[0]
        hb = h.astype(BF16)
        h_scr[...] = hb
        gates = _dot_nt(hb, wg_ref[...].astype(BF16))
        pad = jnp.zeros((gates.shape[0], LANES - N_GATE_COLS), F32)
        g_ref[0] = jnp.concatenate([gates, pad], axis=1) + bg_ref[...]

    u_ref[0] = _dot_nt(h_scr[...], wt_ref[...].astype(BF16)).astype(BF16)


def _inproj(x, gain, shift, w_t, b_gates, tiles, tm):
    b_, s_, d = x.shape
    tn = 1024
    nj = len(tiles)
    if tiles == tuple(range(nj)):
        tile_of = lambda j: j
    else:
        assert tiles == (COL_Q, COL_K, COL_V, COL_UX), tiles
        tile_of = lambda j: jnp.where(j == 3, COL_UX, j)
    row_of = lambda j: tile_of(j) * tn + jnp.where(tile_of(j) >= N_PRE_TILES, N_GATE_COLS, 0)
    per_batch_mod = gain.shape[0] > 1
    mmap = (lambda b, i, j: (b, 0, 0)) if per_batch_mod else (lambda b, i, j: (0, 0, 0))
    gate_blk = N_PRE_TILES * tn // N_GATE_COLS
    return pl.pallas_call(
        _inproj_kernel,
        out_shape=(jax.ShapeDtypeStruct((b_, s_, nj * tn), BF16),
                   jax.ShapeDtypeStruct((b_, s_, LANES), F32)),
        grid=(b_, s_ // tm, nj),
        in_specs=[
            pl.BlockSpec((1, tm, d), lambda b, i, j: (b, i, 0)),
            pl.BlockSpec((1, 1, d), mmap),
            pl.BlockSpec((1, 1, d), mmap),
            pl.BlockSpec((pl.Element(tn), pl.Element(d)), lambda b, i, j: (pl.multiple_of(row_of(j), 8), 0)),
            pl.BlockSpec((N_GATE_COLS, d), lambda b, i, j: (gate_blk, 0)),
            pl.BlockSpec((1, LANES), lambda b, i, j: (0, 0)),
        ],
        out_specs=(pl.BlockSpec((1, tm, tn), lambda b, i, j: (b, i, j)),
                   pl.BlockSpec((1, tm, LANES), lambda b, i, j: (b, i, 0))),
        scratch_shapes=[pltpu.VMEM((tm, d), BF16)],
        compiler_params=_cparams(("parallel", "parallel", "arbitrary")),
        name="inproj",
    )(x, gain, shift, w_t, w_t, b_gates)


COL_Q, COL_K, COL_V, COL_OF, COL_OB, COL_UX, COL_UG, COL_GA, COL_GB = 0, 1, 2, 3, 4, 5, 6, 7, 9
CTX_UX = 3
N_COL_TILES = 11


def _mlstm_kernel(nc, nl, L, dh,
                  qc_f, kc_f, vc_f, ql_f, kl_f, vl_f, gc_f, gl_f, o_f,
                  qc_b, kc_b, vc_b, ql_b, kl_b, vl_b, gc_b, gl_b, o_b,
                  hg_ref, tri_ref,
                  out_f, out_b, c_scr, m_scr):
    s = pl.program_id(1)
    nh = MLSTM_HEADS
    scale = dh ** -0.5

    @pl.when(s == 0)
    def _():
        c_scr[...] = jnp.zeros_like(c_scr)
        m_scr[...] = jnp.full_like(m_scr, -1e30)

    is_ctx = s < nc
    ones_col = (lax.broadcasted_iota(jnp.int32, (L, dh), 1) == 0).astype(BF16)
    row = lax.broadcasted_iota(jnp.int32, (L, L), 0)
    col = lax.broadcasted_iota(jnp.int32, (L, L), 1)

    for d, (qc, kc, vc, ql, kl, vl, gc, gl, o_ref, out_ref) in enumerate((
            (qc_f, kc_f, vc_f, ql_f, kl_f, vl_f, gc_f, gl_f, o_f, out_f),
            (qc_b, kc_b, vc_b, ql_b, kl_b, vl_b, gc_b, gl_b, o_b, out_b))):
        q = jnp.where(is_ctx, qc[0], ql[0])
        k = jnp.where(is_ctx, kc[0], kl[0])
        v = jnp.where(is_ctx, vc[0], vl[0])
        g = jnp.where(is_ctx, gc[0], gl[0])
        gt = g.T
        ls = jnp.minimum(g, 0.0) - jnp.log(1.0 + jnp.exp(-jnp.abs(g)))
        lst = jnp.minimum(gt, 0.0) - jnp.log(1.0 + jnp.exp(-jnp.abs(gt)))
        tri = tri_ref[d]
        mask = (col <= row) if d == 0 else (col >= row)
        l1, l2, l3 = _split3(ls)
        bcol_all = _dot(tri, l1) + _dot(tri, l2) + _dot(tri, l3)
        t1, t2, t3 = _split3(lst)
        trit = tri_ref[1 - d]
        brow_all = _dot(t1, trit) + _dot(t2, trit) + _dot(t3, trit)
        last = L - 1 if d == 0 else 0

        bal = pltpu.roll(bcol_all, LANES - 8, axis=1)
        cmax = g - bal
        trow = lax.broadcasted_iota(jnp.int32, (L, LANES), 0)
        sft = 1
        while sft < L:
            if d == 0:
                shifted = jnp.where(trow >= sft, pltpu.roll(cmax, sft, axis=0), -jnp.inf)
            else:
                shifted = jnp.where(trow < L - sft, pltpu.roll(cmax, L - sft, axis=0), -jnp.inf)
            cmax = jnp.maximum(cmax, shifted)
            sft *= 2
        m_row = m_scr[d][0:1, :]
        gg = bal + m_row
        mt = jnp.maximum(gg, bal + cmax)
        e_col = bal - mt
        w_inter = jnp.exp(gg - mt)
        enm = jnp.exp(-mt)
        tot = bal[last:last + 1, :]
        wlog = tot - bal + g
        m_new = jnp.maximum(tot + m_row, jnp.max(wlog, axis=0, keepdims=True))
        ws = jnp.exp(wlog - m_new) * scale
        decay = jnp.exp(tot + m_row - m_new)
        m_scr[d] = jnp.broadcast_to(m_new, m_scr.shape[1:])

        for h in range(nh):
            u = d * nh + h
            ci = 16 * d + h
            cf = 16 * d + 8 + h
            hs = slice(h * dh, (h + 1) * dh)
            qh, kh, vh = q[:, hs], k[:, hs], v[:, hs]
            vext = jnp.concatenate([vh, ones_col], axis=1)
            cr = gt[ci:ci + 1, :] - brow_all[cf:cf + 1, :] + math.log(scale)
            c_prev = c_scr[u]

            sqk = lax.dot_general(qh, kh, (((1,), (1,)), ((), ())), preferred_element_type=F32)
            w_intra = jnp.exp(jnp.where(mask, e_col[:, ci:ci + 1] + cr, -jnp.inf))
            a = (sqk * w_intra).astype(BF16)
            qs = (qh.astype(F32) * w_inter[:, ci:ci + 1]).astype(BF16)
            r = _dot(jnp.concatenate([a, qs], axis=1),
                     jnp.concatenate([vext, c_prev.astype(BF16)], axis=0))

            num = r[:, :dh]
            den = r[:, dh:dh + 1]
            rden = 1.0 / jnp.maximum(jnp.abs(den), enm[:, ci:ci + 1])
            ssq = jnp.sum(num * num, axis=-1, keepdims=True)
            sc = rden * lax.rsqrt(rden * rden * ssq * (1.0 / dh) + EPS)
            gate = _sigmoid(o_ref[0, :, hs].astype(F32))
            out_ref[0, :, hs] = (num * sc * hg_ref[h:h + 1, :] * gate).astype(BF16)

            kw = (kh.astype(F32) * ws[:, ci:ci + 1]).astype(BF16)
            upd = lax.dot_general(kw, vext, (((0,), (0,)), ((), ())), preferred_element_type=F32)
            c_scr[u] = decay[:, ci:ci + 1] * c_prev + upd


def _mlstm(u_lat, u_ctx, g_lat, g_ctx, head_g, L):
    b_, s_, _ = u_lat.shape
    n_ctx = u_ctx.shape[1]
    nh = MLSTM_HEADS
    w = 1024
    dh = w // nh
    nc, nl = n_ctx // L, s_ // L
    steps = nc + nl
    t0 = jnp.tril(jnp.ones((L, L), F32))
    tri = jnp.stack([t0, t0.T]).astype(BF16)

    def cidx(d):
        if d == 0:
            return lambda st: jnp.minimum(st, nc - 1), lambda st: jnp.maximum(st - nc, 0)
        return lambda st: jnp.maximum(nc - 1 - st, 0), lambda st: jnp.minimum(nl - 1 + nc - st, nl - 1)

    in_specs, args = [], []
    for d in range(2):
        fc, fl = cidx(d)
        for t in range(3):
            in_specs.append(pl.BlockSpec((1, L, w), lambda b, st, fc=fc, t=t: (b, fc(st), t)))
            args.append(u_ctx)
        for t in (COL_Q, COL_K, COL_V):
            in_specs.append(pl.BlockSpec((1, L, w), lambda b, st, fl=fl, t=t: (b, fl(st), t)))
            args.append(u_lat)
        in_specs.append(pl.BlockSpec((1, L, LANES), lambda b, st, fc=fc: (b, fc(st), 0)))
        args.append(g_ctx)
        in_specs.append(pl.BlockSpec((1, L, LANES), lambda b, st, fl=fl: (b, fl(st), 0)))
        args.append(g_lat)
        in_specs.append(pl.BlockSpec((1, L, w), lambda b, st, fl=fl, t=COL_OF + d: (b, fl(st), t)))
        args.append(u_lat)
    in_specs += [pl.BlockSpec((nh, dh), lambda b, st: (0, 0)),
                 pl.BlockSpec((2, L, L), lambda b, st: (0, 0, 0))]
    args += [head_g, tri]
    out_specs = tuple(pl.BlockSpec((1, L, w), lambda b, st, fl=cidx(d)[1]: (b, fl(st), 0)) for d in range(2))
    return pl.pallas_call(
        functools.partial(_mlstm_kernel, nc, nl, L, dh),
        out_shape=(jax.ShapeDtypeStruct((b_, s_, w), BF16),) * 2,
        grid=(b_, steps),
        in_specs=in_specs,
        out_specs=out_specs,
        scratch_shapes=[pltpu.VMEM((2 * nh, dh, 2 * dh), F32), pltpu.VMEM((2, 8, LANES), F32)],
        compiler_params=_cparams(("parallel", "arbitrary")),
        name="mlstm",
    )(*args)


SUBLANES = 8
LRU_BLK = SUBLANES * SUBLANES


def _rglru_kernel(n_ctx, n_tot, ux_ref, cw_ref, cb_ref, w_ref, b_ref, lam_ref, out_ref,
                  us, z_nat, a_f, b_f, a_b, b_b, h_f, h_b, o_nat):
    c = LANES
    g = SUBLANES
    n_lat = n_tot - n_ctx
    nq = n_tot // g
    nblk = n_tot // LRU_BLK
    nblk_ctx = n_ctx // LRU_BLK

    zeros_g = jnp.zeros((g, c), F32)
    us[0:g, :] = zeros_g
    us[g + n_ctx:2 * g + n_ctx, :] = zeros_g
    us[2 * g + n_tot:3 * g + n_tot, :] = zeros_g
    us[g:g + n_ctx, :] = ux_ref[0, 0:n_ctx, :].astype(F32)
    us[2 * g + n_ctx:2 * g + n_tot, :] = ux_ref[0, n_ctx:n_tot, :].astype(F32)

    def conv(base, n):
        acc = cb_ref[...] + cw_ref[2:3, :] * us[base:base + n, :]
        for j, off in ((0, -2), (1, -1), (3, 1)):
            acc = acc + cw_ref[j:j + 1, :] * us[base + off:base + off + n, :]
        return acc

    z_nat[0:n_ctx, :] = conv(g, n_ctx)
    z_nat[n_ctx:n_tot, :] = conv(2 * g + n_ctx, n_lat)

    z = jnp.concatenate([z_nat[pl.ds(k, nq, stride=g), :] for k in range(g)], axis=0)
    p = _dot(z.astype(BF16), w_ref[0]) + b_ref[...]
    for d, (a_scr, b_scr) in enumerate(((a_f, b_f), (a_b, b_b))):
        r = _sigmoid(p[:, (2 * d) * c:(2 * d + 1) * c])
        i = _sigmoid(p[:, (2 * d + 1) * c:(2 * d + 2) * c])
        lam = lam_ref[d:d + 1, :]
        softplus = jnp.maximum(-lam, 0.0) + jnp.log(1.0 + jnp.exp(-jnp.abs(lam)))
        a = jnp.exp2((-LRU_C * math.log2(math.e) * softplus) * r)
        a_scr[...] = a
        b_scr[...] = jnp.sqrt(1.0 - a * a) * i * z

    sub = lax.broadcasted_iota(jnp.int32, (g, c), 0)

    def block_scan(a_scr, b_scr, h_scr, blk, carry, reverse):
        r0 = pl.multiple_of(blk * g, g)
        order = range(g - 1, -1, -1) if reverse else range(g)
        hs, ps = {}, {}
        h = p_ = None
        for k in order:
            a = a_scr[pl.ds(k * nq + r0, g), :]
            b = b_scr[pl.ds(k * nq + r0, g), :]
            h = b if h is None else a * h + b
            p_ = a if p_ is None else a * p_
            hs[k], ps[k] = h, p_
        def shift(x, sft, fill):
            if reverse:
                return jnp.where(sub < g - sft, pltpu.roll(x, g - sft, 0), fill)
            return jnp.where(sub >= sft, pltpu.roll(x, sft, 0), fill)
        pe, he = shift(p_, 1, 1.0), shift(h, 1, 0.0)
        for sft in (1, 2, 4):
            he = pe * shift(he, sft, 0.0) + he
            pe = pe * shift(pe, sft, 1.0)
        cin = pe * carry + he
        for k in order:
            h_scr[pl.ds(k * nq + r0, g), :] = hs[k] + ps[k] * cin
        last = 0 if reverse else g - 1
        tot = p_ * cin + h
        return tot[last:last + 1, :]

    def body(it, carry):
        cf, cb = carry
        cf = block_scan(a_f, b_f, h_f, it, cf, False)
        jb = jnp.where(it < nblk_ctx, nblk_ctx - 1 - it, nblk - 1 + nblk_ctx - it)
        cb = block_scan(a_b, b_b, h_b, jb, cb, True)
        return cf, cb

    zero = jnp.zeros((1, c), F32)
    lax.fori_loop(0, nblk, body, (zero, zero), unroll=2)

    q_ctx = n_ctx // g
    for k in range(g):
        o_nat[pl.ds(k, n_lat // g, stride=g), :] = (h_f[k * nq + q_ctx:(k + 1) * nq, :]
                                                    + h_b[k * nq + q_ctx:(k + 1) * nq, :])
    out_ref[0] = o_nat[...].astype(BF16)


def _rglru(ux_seq, conv_w, conv_b, w_cat, b_cat, lam, n_ctx):
    b_, n_tot, w = ux_seq.shape
    nb = LRU_BLOCKS
    c = w // nb
    return pl.pallas_call(
        functools.partial(_rglru_kernel, n_ctx, n_tot),
        out_shape=jax.ShapeDtypeStruct((b_, n_tot - n_ctx, w), BF16),
        grid=(b_, nb),
        in_specs=[
            pl.BlockSpec((1, n_tot, c), lambda b, k: (b, 0, k)),
            pl.BlockSpec((4, c), lambda b, k: (0, k)),
            pl.BlockSpec((1, c), lambda b, k: (0, k)),
            pl.BlockSpec((1, c, 4 * c), lambda b, k: (k, 0, 0)),
            pl.BlockSpec((1, 4 * c), lambda b, k: (0, k)),
            pl.BlockSpec((2, c), lambda b, k: (0, k)),
        ],
        out_specs=pl.BlockSpec((1, n_tot - n_ctx, c), lambda b, k: (b, 0, k)),
        scratch_shapes=([pltpu.VMEM((n_tot + 3 * SUBLANES, c), F32)] + [pltpu.VMEM((n_tot, c), F32)] * 7
                        + [pltpu.VMEM((n_tot - n_ctx, c), F32)]),
        compiler_params=_cparams(("parallel", "parallel")),
        name="rglru",
    )(ux_seq, conv_w, conv_b, w_cat, b_cat, lam)


def _outproj_kernel(yaf_ref, yab_ref, hr_ref, ug_ref, ga0_ref, ga1_ref, gb0_ref, gb1_ref, x_ref,
                    g1_ref, gain_ref, shift_ref, wpa_ref, wpb_ref, wout_ref, wr_ref,
                    x1_ref, h2_ref, lg_ref):
    ya = (yaf_ref[0].astype(F32) + yab_ref[0].astype(F32)).astype(BF16)
    ug = ug_ref[0].astype(F32)
    gelu = 0.5 * ug * (1.0 + jnp.tanh(0.7978845608028654 * (ug + 0.044715 * ug * ug * ug)))
    yb = (hr_ref[0].astype(F32) * gelu).astype(BF16)
    pa = _dot(ya, wpa_ref[...])
    pb = _dot(yb, wpb_ref[...])
    ga = jnp.concatenate([ga0_ref[0], ga1_ref[0]], axis=1).astype(F32)
    gb = jnp.concatenate([gb0_ref[0], gb1_ref[0]], axis=1).astype(F32)
    mix = (_sigmoid(ga) * pa + _sigmoid(gb) * pb).astype(BF16)
    x1 = x_ref[0] + g1_ref[0] * _dot(mix, wout_ref[...])
    x1_ref[0] = x1
    ms = jnp.mean(x1 * x1, axis=-1, keepdims=True)
    h2 = x1 * lax.rsqrt(ms + EPS) * gain_ref[0] + shift_ref[0]
    ns = h2.shape[1] // LANES
    for s in range(ns):
        h2_ref[pl.ds(s, h2.shape[0], stride=ns), :] = h2[:, s * LANES:(s + 1) * LANES]
    wr = wr_ref[...]
    h1, h2b, _ = _split3(h2)
    w1, w2b, _ = _split3(wr)
    lg_ref[0] = _dot(h1, w1) + (_dot(h1, w2b) + _dot(h2b, w1))


def _outproj(ya_f, ya_b, hr, u_lat, x, g1, gain2, shift2, w_pa, w_pb, w_out, w_r, tm):
    b_, s_, d = x.shape
    w = ya_f.shape[2]
    row = lambda b, i: (b, i, 0)
    const2 = lambda b, i: (0, 0)
    bmap = lambda b, i: (b, 0, 0)
    ucol = lambda t: (lambda b, i: (b, i, t))
    single = pl.Buffered(1)
    return pl.pallas_call(
        _outproj_kernel,
        out_shape=(jax.ShapeDtypeStruct((b_, s_, d), F32),
                   jax.ShapeDtypeStruct((b_ * s_ * (d // LANES), LANES), F32),
                   jax.ShapeDtypeStruct((b_, s_, LANES), F32)),
        grid=(b_, s_ // tm),
        in_specs=[
            pl.BlockSpec((1, tm, w), row), pl.BlockSpec((1, tm, w), row), pl.BlockSpec((1, tm, w), row),
            pl.BlockSpec((1, tm, w), ucol(COL_UG)),
            pl.BlockSpec((1, tm, w), ucol(COL_GA)), pl.BlockSpec((1, tm, w), ucol(COL_GA + 1)),
            pl.BlockSpec((1, tm, w), ucol(COL_GB)), pl.BlockSpec((1, tm, w), ucol(COL_GB + 1)),
            pl.BlockSpec((1, tm, d), row),
            pl.BlockSpec((1, 1, d), bmap), pl.BlockSpec((1, 1, d), bmap), pl.BlockSpec((1, 1, d), bmap),
            pl.BlockSpec((w, d), const2, pipeline_mode=single),
            pl.BlockSpec((w, d), const2, pipeline_mode=single),
            pl.BlockSpec((d, d), const2, pipeline_mode=single),
            pl.BlockSpec((d, LANES), const2, pipeline_mode=single),
        ],
        out_specs=(pl.BlockSpec((1, tm, d), row),
                   pl.BlockSpec((tm * (d // LANES), LANES), lambda b, i: (b * (s_ // tm) + i, 0)),
                   pl.BlockSpec((1, tm, LANES), row)),
        compiler_params=_cparams(("parallel", "parallel")),
        name="outproj",
    )(ya_f, ya_b, hr, u_lat, u_lat, u_lat, u_lat, u_lat, x, g1, gain2, shift2, w_pa, w_pb, w_out, w_r)


def _route_kernel(lg_ref, o_ref):
    lg = lg_ref[...]
    lane = lax.broadcasted_iota(jnp.int32, lg.shape, 1)
    neg = -jnp.inf
    big = jnp.int32(1 << 20)
    g_l = jnp.where(lane < N_GROUPS, lg, neg)
    g_max = jnp.max(g_l, axis=-1, keepdims=True)
    g_sel = jnp.min(jnp.where(g_l == g_max, lane, big), axis=-1, keepdims=True)
    p_g = 1.0 / jnp.sum(jnp.exp(g_l - g_max), axis=-1, keepdims=True)
    lo = N_GROUPS + g_sel * EXPERTS_PER_GROUP
    e_l = jnp.where((lane >= lo) & (lane < lo + EXPERTS_PER_GROUP), lg, neg)
    v1 = jnp.max(e_l, axis=-1, keepdims=True)
    i1 = jnp.min(jnp.where(e_l == v1, lane, big), axis=-1, keepdims=True)
    e_l2 = jnp.where(lane == i1, neg, e_l)
    v2 = jnp.max(e_l2, axis=-1, keepdims=True)
    i2 = jnp.min(jnp.where(e_l2 == v2, lane, big), axis=-1, keepdims=True)
    e2 = jnp.exp(v2 - v1)
    w1 = p_g / (1.0 + e2)
    w2 = p_g * e2 / (1.0 + e2)
    out = jnp.where(lane == 0, (i1 - N_GROUPS).astype(F32),
          jnp.where(lane == 1, (i2 - N_GROUPS).astype(F32),
          jnp.where(lane == 2, w1, jnp.where(lane == 3, w2, 0.0))))
    o_ref[...] = out


def _route(logits, tm):
    t_, _ = logits.shape
    return pl.pallas_call(
        _route_kernel,
        out_shape=jax.ShapeDtypeStruct((t_, LANES), F32),
        grid=(t_ // tm,),
        in_specs=[pl.BlockSpec((tm, LANES), lambda i: (i, 0))],
        out_specs=pl.BlockSpec((tm, LANES), lambda i: (i, 0)),
        compiler_params=_cparams(("parallel",)),
        name="route",
    )(logits)


MOE_DMA_UNROLL = 32


MOE_W_DEPTH = 3
MOE_W_SPLIT = 2


def _moe_kernel(R, SUB, nf, ni, item_e, item_rows, tok_ref, tokn_ref, dst_ref, h2_hbm, w1_hbm, w3_hbm, w2_hbm,
                y_hbm, x_scr, xb_scr, acc_scr, y_scr, w1r, w3r, w2r, w1b, w3b, w2b, gsem, ssem, wsem):
    i = pl.program_id(0)
    j = pl.program_id(1)
    nsub_of = lambda r: (r + SUB - 1) // SUB
    rows = item_rows[i]
    nsub = nsub_of(rows)
    nsub_next = jnp.where(i + 1 < ni, nsub_of(item_rows[jnp.minimum(i + 1, ni - 1)]), 0)
    rows_prev = jnp.where(i > 0, item_rows[jnp.maximum(i - 1, 0)], 0)
    buf = i % 2
    groups = SUB // MOE_DMA_UNROLL

    d_model, tf = w1r.shape[1], w1r.shape[2]

    def weight_copies(c):
        it = c // nf
        jc = c % nf
        e = item_e[jnp.minimum(it, ni - 1)]
        slot = c % MOE_W_DEPTH
        f0 = pl.multiple_of(jc * tf, tf)
        out = []
        for h in range(MOE_W_SPLIT):
            ra, rb = d_model // MOE_W_SPLIT, tf // MOE_W_SPLIT
            out.append(pltpu.make_async_copy(w1_hbm.at[e, pl.ds(h * ra, ra), pl.ds(f0, tf)],
                                             w1r.at[slot, pl.ds(h * ra, ra), :], wsem.at[slot]))
            out.append(pltpu.make_async_copy(w3_hbm.at[e, pl.ds(h * ra, ra), pl.ds(f0, tf)],
                                             w3r.at[slot, pl.ds(h * ra, ra), :], wsem.at[slot]))
            out.append(pltpu.make_async_copy(w2_hbm.at[e, pl.ds(f0 + h * rb, rb), :],
                                             w2r.at[slot, pl.ds(h * rb, rb), :], wsem.at[slot]))
        return out

    def chunk_live(c):
        it = c // nf
        return (it < ni) & (item_rows[jnp.minimum(it, ni - 1)] > 0)

    cur = i * nf + j

    @pl.when((cur == 0) & chunk_live(0))
    def _():
        for cp in weight_copies(0):
            cp.start()

        @pl.when(chunk_live(1))
        def _():
            for cp in weight_copies(1):
                cp.start()

    @pl.when((rows > 0) & chunk_live(cur + 2))
    def _():
        for cp in weight_copies(cur + 2):
            cp.start()

    ns = acc_scr.shape[1] // LANES

    def tok_rows(ref, r):
        return ref.at[pl.ds(pl.multiple_of(r * ns, ns), ns), :]

    def gather_group(tref, b, gi):
        base = pl.multiple_of(gi * MOE_DMA_UNROLL, MOE_DMA_UNROLL)
        for k in range(MOE_DMA_UNROLL):
            tok = tref[0, 0, base + k]
            pltpu.make_async_copy(tok_rows(h2_hbm, tok), tok_rows(x_scr.at[b], base + k),
                                  gsem.at[b]).start(priority=k % 2)

    def gather_wait_block(b):
        n = SUB * ns
        pltpu.make_async_copy(h2_hbm.at[pl.ds(0, n), :], x_scr.at[b, pl.ds(0, n), :], gsem.at[b]).wait()

    def scatter_copy(dref, r):
        dst = dref[0, 0, r]
        return pltpu.make_async_copy(tok_rows(y_scr, r), tok_rows(y_hbm, dst), ssem)

    def scatter_group(dref, gi):
        base = pl.multiple_of(gi * MOE_DMA_UNROLL, MOE_DMA_UNROLL)
        for k in range(MOE_DMA_UNROLL):
            scatter_copy(dref, base + k).start(priority=k % 2)

    def repeat(n, fn, lo=0):
        def body(q, c):
            fn(q)
            return c
        lax.fori_loop(lo, n, body, 0)

    def scatter_rows(dref, n):
        full = n // MOE_DMA_UNROLL
        repeat(full, lambda gi: scatter_group(dref, gi))
        repeat(n, lambda r: scatter_copy(dref, r).start(), lo=full * MOE_DMA_UNROLL)

    def scatter_wait_rows(n):
        u = MOE_DMA_UNROLL * ns
        full = n // MOE_DMA_UNROLL
        repeat(full, lambda q: pltpu.make_async_copy(y_scr.at[pl.ds(0, u), :], y_hbm.at[pl.ds(0, u), :],
                                                      ssem).wait())
        repeat(n, lambda r: scatter_copy(dst_ref, 0).wait(), lo=full * MOE_DMA_UNROLL)

    @pl.when((i == 0) & (j == 0))
    def _():
        repeat(nsub * groups, lambda gi: gather_group(tok_ref, 0, gi))

    @pl.when(j < nsub_next)
    def _():
        repeat((j + 1) * groups, lambda gi: gather_group(tokn_ref, 1 - buf, gi), lo=j * groups)

    @pl.when(j == 0)
    def _():
        repeat(nsub, lambda q: gather_wait_block(buf))

    @pl.when(j == nf - 1)
    def _():
        scatter_wait_rows(rows_prev)

    @pl.when(rows > 0)
    def _():
        for cp in weight_copies(cur):
            cp.wait()
        slot = cur % MOE_W_DEPTH
        w1b[...] = w1r[slot].astype(BF16)
        w3b[...] = w3r[slot].astype(BF16)
        w2b[...] = w2r[slot].astype(BF16)

        @pl.when(j == 0)
        def _():
            def unsplit(sb, c):
                r0 = pl.multiple_of(sb * SUB, SUB)
                s0 = pl.multiple_of(sb * (SUB * ns), SUB * ns)
                xb_scr[pl.ds(r0, SUB), :] = jnp.concatenate(
                    [x_scr[buf, pl.ds(s0 + s, SUB, stride=ns), :] for s in range(ns)], axis=1).astype(BF16)
                return c
            lax.fori_loop(0, nsub, unsplit, 0)

        def experts(m):
            def run():
                x = xb_scr[0:m, :]
                h1 = _dot(x, w1b[...])
                h3 = _dot(x, w3b[...])
                hh = (h1 * _sigmoid(h1) * h3).astype(BF16)
                contrib = _dot(hh, w2b[...])

                @pl.when(j == 0)
                def _():
                    acc_scr[0:m, :] = contrib

                @pl.when((j > 0) & (j < nf - 1))
                def _():
                    acc_scr[0:m, :] += contrib

                @pl.when(j == nf - 1)
                def _():
                    tot = acc_scr[0:m, :] + contrib
                    for s in range(ns):
                        y_scr[pl.ds(s, m, stride=ns), :] = tot[:, s * LANES:(s + 1) * LANES]
            return run
        lax.switch(nsub - 1, [experts((q + 1) * SUB) for q in range(R // SUB)])

    @pl.when(j == nf - 1)
    def _():
        scatter_rows(dst_ref, rows)

        @pl.when(i == ni - 1)
        def _():
            scatter_wait_rows(rows)


def _moe(h2s, item_e, item_rows, slot_tok, slot_dst, w1, w3, w2, n_out_rows):
    ne, d, f = w1.shape
    ns = d // LANES
    R, SUB, tf = MOE_ITEM_ROWS, MOE_SUB_ROWS, MOE_F_CHUNK
    ni = item_e.shape[0]
    nf = f // tf
    assert R % SUB == 0 and R <= nf * SUB and nf >= 2
    grid_spec = pltpu.PrefetchScalarGridSpec(
        num_scalar_prefetch=2,
        grid=(ni, nf),
        in_specs=[
            pl.BlockSpec((1, 1, R), lambda i, j, ie, ir: (i, 0, 0), memory_space=pltpu.SMEM),
            pl.BlockSpec((1, 1, R), lambda i, j, ie, ir: (jnp.minimum(i + 1, ni - 1), 0, 0),
                         memory_space=pltpu.SMEM),
            pl.BlockSpec((1, 1, R), lambda i, j, ie, ir: (i, 0, 0), memory_space=pltpu.SMEM),
            pl.BlockSpec(memory_space=pl.ANY),
            pl.BlockSpec(memory_space=pl.ANY),
            pl.BlockSpec(memory_space=pl.ANY),
            pl.BlockSpec(memory_space=pl.ANY),
        ],
        out_specs=pl.BlockSpec(memory_space=pl.ANY),
        scratch_shapes=[
            pltpu.VMEM((2, R * ns, LANES), F32), pltpu.VMEM((R, d), BF16),
            pltpu.VMEM((R, d), F32), pltpu.VMEM((R * ns, LANES), F32),
            pltpu.VMEM((MOE_W_DEPTH, d, tf), F32), pltpu.VMEM((MOE_W_DEPTH, d, tf), F32),
            pltpu.VMEM((MOE_W_DEPTH, tf, d), F32),
            pltpu.VMEM((d, tf), BF16), pltpu.VMEM((d, tf), BF16), pltpu.VMEM((tf, d), BF16),
            pltpu.SemaphoreType.DMA((2,)), pltpu.SemaphoreType.DMA(()), pltpu.SemaphoreType.DMA((MOE_W_DEPTH,)),
        ],
    )
    return pl.pallas_call(
        functools.partial(_moe_kernel, R, SUB, nf, ni),
        out_shape=jax.ShapeDtypeStruct((n_out_rows * ns, LANES), F32),
        grid_spec=grid_spec,
        compiler_params=_cparams(("arbitrary", "arbitrary")),
        name="moe_experts",
    )(item_e, item_rows, slot_tok, slot_tok, slot_dst, h2s, w1, w3, w2)


def _moe_plan(expert, t_):
    R = MOE_ITEM_ROWS
    na = expert.shape[0]
    ni = N_EXPERTS + na // R
    onehot = (expert[:, None] == jnp.arange(N_EXPERTS, dtype=jnp.int32)[None, :]).astype(jnp.int32)
    csum = jnp.cumsum(onehot, axis=0)
    rank = jnp.sum(csum * onehot, axis=1) - 1
    counts = csum[-1]
    n_items = (counts + R - 1) // R
    item_end = jnp.cumsum(n_items)
    item_start = item_end - n_items
    slot = jnp.sum(item_start[None, :] * onehot, axis=1) * R + rank
    slot_a = jnp.full((ni * R,), -1, jnp.int32).at[slot].set(jnp.arange(na, dtype=jnp.int32))
    valid = slot_a >= 0
    slot_tok = jnp.where(valid, slot_a // TOP_K, 0)
    slot_dst = jnp.where(valid, (slot_a % TOP_K) * t_ + slot_a // TOP_K, 0)
    ii = jnp.arange(ni, dtype=jnp.int32)
    e_of = jnp.minimum(jnp.sum((item_end[None, :] <= ii[:, None]).astype(jnp.int32), axis=1), N_EXPERTS - 1)
    oh_e = (e_of[:, None] == jnp.arange(N_EXPERTS, dtype=jnp.int32)[None, :]).astype(jnp.int32)
    cnt_e = jnp.sum(oh_e * counts[None, :], axis=1)
    start_e = jnp.sum(oh_e * item_start[None, :], axis=1)
    rows = jnp.clip(cnt_e - (ii - start_e) * R, 0, R).astype(jnp.int32)
    live = ii < item_end[-1]
    rows = jnp.where(live, rows, 0)
    last_e = jnp.max(jnp.where(live, e_of, 0))
    e_of = jnp.where(live, e_of, last_e).astype(jnp.int32)
    return e_of, rows, slot_tok.reshape(ni, 1, R), slot_dst.reshape(ni, 1, R)


def _final_kernel(x1_ref, y0_ref, y1_ref, rt_ref, g2_ref, fg_ref, o_ref):
    rt = rt_ref[0]
    tm = rt.shape[0]
    ns = y0_ref.shape[0] // tm
    unsplit = lambda ref: jnp.concatenate([ref[pl.ds(s, tm, stride=ns), :] for s in range(ns)], axis=1)
    y = rt[:, TOP_K:TOP_K + 1] * unsplit(y0_ref) + rt[:, TOP_K + 1:TOP_K + 2] * unsplit(y1_ref)
    x = x1_ref[0] + g2_ref[0] * y
    ms = jnp.mean(x * x, axis=-1, keepdims=True)
    o_ref[0] = x * lax.rsqrt(ms + EPS) * fg_ref[...]


def _final(x1, ybuf, routed, g2, final_g, tm):
    b_, s_, d = x1.shape
    nt = s_ // tm
    ns = d // LANES
    return pl.pallas_call(
        _final_kernel,
        out_shape=jax.ShapeDtypeStruct((b_, s_, d), F32),
        grid=(b_, nt),
        in_specs=[
            pl.BlockSpec((1, tm, d), lambda b, i: (b, i, 0)),
            pl.BlockSpec((tm * ns, LANES), lambda b, i: (b * nt + i, 0)),
            pl.BlockSpec((tm * ns, LANES), lambda b, i: (b_ * nt + b * nt + i, 0)),
            pl.BlockSpec((1, tm, LANES), lambda b, i: (b, i, 0)),
            pl.BlockSpec((1, 1, d), lambda b, i: (b, 0, 0)),
            pl.BlockSpec((1, d), lambda b, i: (0, 0)),
        ],
        out_specs=pl.BlockSpec((1, tm, d), lambda b, i: (b, i, 0)),
        compiler_params=_cparams(("parallel", "parallel")),
        name="final_norm",
    )(x1, ybuf, ybuf, routed, g2, final_g)


def kernel(x, c, ctx, c_ctx, w_mod, b_mod, norm1_g, w_in, b_gates, mlstm_head_g, conv_w, conv_b, lru_wa, lru_ba,
           lru_wx, lru_bx, lru_lam, w_pa, w_pb, w_out, norm2_g, w_rg, w_re, w1, w3, w2, final_g):
    b_, s_, d = x.shape
    n_ctx = ctx.shape[1]
    rows = s_ // GRID_W
    assert w_mod.shape[0] == 1, "single layer"
    wm = d // 2
    nh = MLSTM_HEADS

    cc = jnp.zeros((8, d), F32).at[:b_].set(c).at[b_].set(c_ctx)
    mod = _modulation(cc, w_mod[0], b_mod[0][None, :])
    sh1, sc1, g1, sh2, sc2, g2 = [mod[:b_, i * d:(i + 1) * d][:, None, :] for i in range(6)]
    csh1, csc1 = mod[b_:b_ + 1, 0:d][:, None, :], mod[b_:b_ + 1, d:2 * d][:, None, :]

    assert wm == 1024 and 4 * nh == N_GATE_COLS
    w_t = jnp.transpose(w_in[0])
    bg_pad = jnp.zeros((1, LANES), F32).at[0, :4 * nh].set(b_gates[0])

    ng = norm1_g[0][None, None, :]
    u_lat, g_lat = _inproj(x, ng * (1.0 + sc1), sh1, w_t, bg_pad, tuple(range(N_COL_TILES)), min(1024, s_))
    u_ctx, g_ctx = _inproj(ctx.reshape(1, b_ * n_ctx, d), ng * (1.0 + csc1), csh1, w_t, bg_pad,
                           (COL_Q, COL_K, COL_V, COL_UX), b_ * n_ctx)
    u_ctx = u_ctx.reshape(b_, n_ctx, -1)
    g_ctx = g_ctx.reshape(b_, n_ctx, LANES)

    ya_f, ya_b = _mlstm(u_lat, u_ctx, g_lat, g_ctx, mlstm_head_g[0], MLSTM_CHUNK)

    ux_lat = u_lat[:, :, COL_UX * wm:(COL_UX + 1) * wm]
    ux_col = ux_lat.reshape(b_, rows, GRID_W, wm).transpose(0, 2, 1, 3).reshape(b_, s_, wm)
    ux_seq = jnp.concatenate([u_ctx[:, :, CTX_UX * wm:(CTX_UX + 1) * wm], ux_col], axis=1)
    wa, wx = lru_wa[0], lru_wx[0]
    w_cat = jnp.concatenate([wa[0], wx[0], wa[1], wx[1]], axis=-1).astype(BF16)
    cblk = wm // LRU_BLOCKS
    blk = lambda v: v.reshape(LRU_BLOCKS, cblk)
    b_cat = jnp.concatenate([blk(lru_ba[0, 0]), blk(lru_bx[0, 0]), blk(lru_ba[0, 1]), blk(lru_bx[0, 1])],
                            axis=-1).reshape(1, 4 * wm)
    h_col = _rglru(ux_seq, conv_w[0], conv_b[0][None, :], w_cat, b_cat, lru_lam[0], n_ctx)
    hr = h_col.reshape(b_, GRID_W, rows, wm).transpose(0, 2, 1, 3).reshape(b_, s_, wm)

    w_r = jnp.zeros((d, LANES), F32).at[:, :N_GROUPS].set(w_rg[0]).at[:, N_GROUPS:N_GROUPS + N_EXPERTS].set(w_re[0])
    n2 = norm2_g[0][None, None, :]
    x1, h2, logits = _outproj(ya_f, ya_b, hr, u_lat, x, g1, n2 * (1.0 + sc2), sh2,
                              w_pa[0].astype(BF16), w_pb[0].astype(BF16), w_out[0].astype(BF16), w_r, 256)

    t_ = b_ * s_
    routed = _route(logits.reshape(t_, LANES), 512)
    expert = routed[:, :TOP_K].astype(jnp.int32).reshape(-1)
    item_e, item_rows, slot_tok, slot_dst = _moe_plan(expert, t_)
    ybuf = _moe(h2, item_e, item_rows, slot_tok, slot_dst, w1[0], w3[0], w2[0], TOP_K * t_)

    return _final(x1, ybuf, routed.reshape(b_, s_, LANES), g2, final_g[None, :], 256)
```

```python
import functools
import math

import jax
import jax.numpy as jnp
from jax import lax
from jax.experimental import pallas as pl
from jax.experimental.pallas import tpu as pltpu

F32 = jnp.float32
BF16 = jnp.bfloat16

EPS = 1e-6
GRID_W = 64
MLSTM_HEADS = 8
LRU_BLOCKS = 8
LRU_C = 8.0
N_GROUPS = 4
EXPERTS_PER_GROUP = 8
N_EXPERTS = N_GROUPS * EXPERTS_PER_GROUP
TOP_K = 2

VMEM_LIMIT_BYTES = 56 * 1024 * 1024
LANES = 128

MLSTM_CHUNK = 128
MOE_ITEM_ROWS = 768
MOE_SUB_ROWS = 256
MOE_F_CHUNK = 256


def _cparams(sem):
    return pltpu.CompilerParams(dimension_semantics=sem, vmem_limit_bytes=VMEM_LIMIT_BYTES)


def _sigmoid(x):
    return 0.5 * jnp.tanh(0.5 * x) + 0.5


def _dot(a, b):
    return jnp.dot(a, b, preferred_element_type=F32)


def _split3(x):
    x1 = x.astype(BF16)
    r1 = x - x1.astype(F32)
    x2 = r1.astype(BF16)
    x3 = (r1 - x2.astype(F32)).astype(BF16)
    return x1, x2, x3


def _mod_kernel(c_ref, w_ref, b_ref, o_ref):
    c = c_ref[...]
    s = (c * _sigmoid(c)).astype(BF16)
    o_ref[...] = _dot(s, w_ref[...].astype(BF16)) + b_ref[...]


def _modulation(cc, w_mod, b_mod):
    m, d = cc.shape
    n = w_mod.shape[1]
    tn = 1024
    return pl.pallas_call(
        _mod_kernel,
        out_shape=jax.ShapeDtypeStruct((m, n), F32),
        grid=(n // tn,),
        in_specs=[
            pl.BlockSpec((m, d), lambda j: (0, 0)),
            pl.BlockSpec((d, tn), lambda j: (0, j)),
            pl.BlockSpec((1, tn), lambda j: (0, j)),
        ],
        out_specs=pl.BlockSpec((m, tn), lambda j: (0, j)),
        compiler_params=_cparams(("parallel",)),
        name="modulation",
    )(cc, w_mod, b_mod)


N_PRE_TILES = 5
N_GATE_COLS = 32


def _dot_nt(a, b):
    return lax.dot_general(a, b, (((1,), (1,)), ((), ())), preferred_element_type=F32)


def _inproj_kernel(x_ref, gain_ref, shift_ref, wt_ref, wg_ref, bg_ref, u_ref, g_ref, h_scr):
    @pl.when(pl.program_id(2) == 0)
    def _():
        x = x_ref[0]
        ms = jnp.mean(x * x, axis=-1, keepdims=True)
        h = x * lax.rsqrt(ms + EPS) * gain_ref[0] + shift_ref[0]
        hb = h.astype(BF16)
        h_scr[...] = hb
        gates = _dot_nt(hb, wg_ref[...].astype(BF16))
        pad = jnp.zeros((gates.shape[0], LANES - N_GATE_COLS), F32)
        g_ref[0] = jnp.concatenate([gates, pad], axis=1) + bg_ref[...]

    u_ref[0] = _dot_nt(h_scr[...], wt_ref[...].astype(BF16)).astype(BF16)


def _inproj(x, gain, shift, w_t, b_gates, tiles, tm):
    b_, s_, d = x.shape
    tn = 1024
    nj = len(tiles)
    if tiles == tuple(range(nj)):
        tile_of = lambda j: j
    else:
        assert tiles == (COL_Q, COL_K, COL_V, COL_UX), tiles
        tile_of = lambda j: jnp.where(j == 3, COL_UX, j)
    row_of = lambda j: tile_of(j) * tn + jnp.where(tile_of(j) >= N_PRE_TILES, N_GATE_COLS, 0)
    per_batch_mod = gain.shape[0] > 1
    mmap = (lambda b, i, j: (b, 0, 0)) if per_batch_mod else (lambda b, i, j: (0, 0, 0))
    gate_blk = N_PRE_TILES * tn // N_GATE_COLS
    return pl.pallas_call(
        _inproj_kernel,
        out_shape=(jax.ShapeDtypeStruct((b_, s_, nj * tn), BF16),
                   jax.ShapeDtypeStruct((b_, s_, LANES), F32)),
        grid=(b_, s_ // tm, nj),
        in_specs=[
            pl.BlockSpec((1, tm, d), lambda b, i, j: (b, i, 0)),
            pl.BlockSpec((1, 1, d), mmap),
            pl.BlockSpec((1, 1, d), mmap),
            pl.BlockSpec((pl.Element(tn), pl.Element(d)), lambda b, i, j: (pl.multiple_of(row_of(j), 8), 0)),
            pl.BlockSpec((N_GATE_COLS, d), lambda b, i, j: (gate_blk, 0)),
            pl.BlockSpec((1, LANES), lambda b, i, j: (0, 0)),
        ],
        out_specs=(pl.BlockSpec((1, tm, tn), lambda b, i, j: (b, i, j)),
                   pl.BlockSpec((1, tm, LANES), lambda b, i, j: (b, i, 0))),
        scratch_shapes=[pltpu.VMEM((tm, d), BF16)],
        compiler_params=_cparams(("parallel", "parallel", "arbitrary")),
        name="inproj",
    )(x, gain, shift, w_t, w_t, b_gates)


COL_Q, COL_K, COL_V, COL_OF, COL_OB, COL_UX, COL_UG, COL_GA, COL_GB = 0, 1, 2, 3, 4, 5, 6, 7, 9
CTX_UX = 3
N_COL_TILES = 11


def _mlstm_kernel(nc, nl, L, dh,
                  qc_f, kc_f, vc_f, ql_f, kl_f, vl_f, gc_f, gl_f, o_f,
                  qc_b, kc_b, vc_b, ql_b, kl_b, vl_b, gc_b, gl_b, o_b,
                  hg_ref, tri_ref,
                  out_f, out_b, c_scr, m_scr):
    s = pl.program_id(1)
    nh = MLSTM_HEADS
    scale = dh ** -0.5

    @pl.when(s == 0)
    def _():
        c_scr[...] = jnp.zeros_like(c_scr)
        m_scr[...] = jnp.full_like(m_scr, -1e30)

    is_ctx = s < nc
    ones_col = (lax.broadcasted_iota(jnp.int32, (L, dh), 1) == 0).astype(BF16)
    row = lax.broadcasted_iota(jnp.int32, (L, L), 0)
    col = lax.broadcasted_iota(jnp.int32, (L, L), 1)

    for d, (qc, kc, vc, ql, kl, vl, gc, gl, o_ref, out_ref) in enumerate((
            (qc_f, kc_f, vc_f, ql_f, kl_f, vl_f, gc_f, gl_f, o_f, out_f),
            (qc_b, kc_b, vc_b, ql_b, kl_b, vl_b, gc_b, gl_b, o_b, out_b))):
        q = jnp.where(is_ctx, qc[0], ql[0])
        k = jnp.where(is_ctx, kc[0], kl[0])
        v = jnp.where(is_ctx, vc[0], vl[0])
        g = jnp.where(is_ctx, gc[0], gl[0])
        gt = g.T
        ls = jnp.minimum(g, 0.0) - jnp.log(1.0 + jnp.exp(-jnp.abs(g)))
        lst = jnp.minimum(gt, 0.0) - jnp.log(1.0 + jnp.exp(-jnp.abs(gt)))
        tri = tri_ref[d]
        mask = (col <= row) if d == 0 else (col >= row)
        l1, l2, l3 = _split3(ls)
        bcol_all = _dot(tri, l1) + _dot(tri, l2) + _dot(tri, l3)
        t1, t2, t3 = _split3(lst)
        trit = tri_ref[1 - d]
        brow_all = _dot(t1, trit) + _dot(t2, trit) + _dot(t3, trit)
        last = L - 1 if d == 0 else 0

        bal = pltpu.roll(bcol_all, LANES - 8, axis=1)
        cmax = g - bal
        trow = lax.broadcasted_iota(jnp.int32, (L, LANES), 0)
        sft = 1
        while sft < L:
            if d == 0:
                shifted = jnp.where(trow >= sft, pltpu.roll(cmax, sft, axis=0), -jnp.inf)
            else:
                shifted = jnp.where(trow < L - sft, pltpu.roll(cmax, L - sft, axis=0), -jnp.inf)
            cmax = jnp.maximum(cmax, shifted)
            sft *= 2
        m_row = m_scr[d][0:1, :]
        gg = bal + m_row
        mt = jnp.maximum(gg, bal + cmax)
        e_col = bal - mt
        w_inter = jnp.exp(gg - mt)
        enm = jnp.exp(-mt)
        tot = bal[last:last + 1, :]
        wlog = tot - bal + g
        m_new = jnp.maximum(tot + m_row, jnp.max(wlog, axis=0, keepdims=True))
        ws = jnp.exp(wlog - m_new) * scale
        decay = jnp.exp(tot + m_row - m_new)
        m_scr[d] = jnp.broadcast_to(m_new, m_scr.shape[1:])

        for h in range(nh):
            u = d * nh + h
            ci = 16 * d + h
            cf = 16 * d + 8 + h
            hs = slice(h * dh, (h + 1) * dh)
            qh, kh, vh = q[:, hs], k[:, hs], v[:, hs]
            vext = jnp.concatenate([vh, ones_col], axis=1)
            cr = gt[ci:ci + 1, :] - brow_all[cf:cf + 1, :] + math.log(scale)
            c_prev = c_scr[u]

            sqk = lax.dot_general(qh, kh, (((1,), (1,)), ((), ())), preferred_element_type=F32)
            w_intra = jnp.exp(jnp.where(mask, e_col[:, ci:ci + 1] + cr, -jnp.inf))
            a = (sqk * w_intra).astype(BF16)
            qs = (qh.astype(F32) * w_inter[:, ci:ci + 1]).astype(BF16)
            r = _dot(jnp.concatenate([a, qs], axis=1),
                     jnp.concatenate([vext, c_prev.astype(BF16)], axis=0))

            num = r[:, :dh]
            den = r[:, dh:dh + 1]
            rden = 1.0 / jnp.maximum(jnp.abs(den), enm[:, ci:ci + 1])
            ssq = jnp.sum(num * num, axis=-1, keepdims=True)
            sc = rden * lax.rsqrt(rden * rden * ssq * (1.0 / dh) + EPS)
            gate = _sigmoid(o_ref[0, :, hs].astype(F32))
            out_ref[0, :, hs] = (num * sc * hg_ref[h:h + 1, :] * gate).astype(BF16)

            kw = (kh.astype(F32) * ws[:, ci:ci + 1]).astype(BF16)
            upd = lax.dot_general(kw, vext, (((0,), (0,)), ((), ())), preferred_element_type=F32)
            c_scr[u] = decay[:, ci:ci + 1] * c_prev + upd


def _mlstm(u_lat, u_ctx, g_lat, g_ctx, head_g, L):
    b_, s_, _ = u_lat.shape
    n_ctx = u_ctx.shape[1]
    nh = MLSTM_HEADS
    w = 1024
    dh = w // nh
    nc, nl = n_ctx // L, s_ // L
    steps = nc + nl
    t0 = jnp.tril(jnp.ones((L, L), F32))
    tri = jnp.stack([t0, t0.T]).astype(BF16)

    def cidx(d):
        if d == 0:
            return lambda st: jnp.minimum(st, nc - 1), lambda st: jnp.maximum(st - nc, 0)
        return lambda st: jnp.maximum(nc - 1 - st, 0), lambda st: jnp.minimum(nl - 1 + nc - st, nl - 1)

    in_specs, args = [], []
    for d in range(2):
        fc, fl = cidx(d)
        for t in range(3):
            in_specs.append(pl.BlockSpec((1, L, w), lambda b, st, fc=fc, t=t: (b, fc(st), t)))
            args.append(u_ctx)
        for t in (COL_Q, COL_K, COL_V):
            in_specs.append(pl.BlockSpec((1, L, w), lambda b, st, fl=fl, t=t: (b, fl(st), t)))
            args.append(u_lat)
        in_specs.append(pl.BlockSpec((1, L, LANES), lambda b, st, fc=fc: (b, fc(st), 0)))
        args.append(g_ctx)
        in_specs.append(pl.BlockSpec((1, L, LANES), lambda b, st, fl=fl: (b, fl(st), 0)))
        args.append(g_lat)
        in_specs.append(pl.BlockSpec((1, L, w), lambda b, st, fl=fl, t=COL_OF + d: (b, fl(st), t)))
        args.append(u_lat)
    in_specs += [pl.BlockSpec((nh, dh), lambda b, st: (0, 0)),
                 pl.BlockSpec((2, L, L), lambda b, st: (0, 0, 0))]
    args += [head_g, tri]
    out_specs = tuple(pl.BlockSpec((1, L, w), lambda b, st, fl=cidx(d)[1]: (b, fl(st), 0)) for d in range(2))
    return pl.pallas_call(
        functools.partial(_mlstm_kernel, nc, nl, L, dh),
        out_shape=(jax.ShapeDtypeStruct((b_, s_, w), BF16),) * 2,
        grid=(b_, steps),
        in_specs=in_specs,
        out_specs=out_specs,
        scratch_shapes=[pltpu.VMEM((2 * nh, dh, 2 * dh), F32), pltpu.VMEM((2, 8, LANES), F32)],
        compiler_params=_cparams(("parallel", "arbitrary")),
        name="mlstm",
    )(*args)


SUBLANES = 8
LRU_BLK = SUBLANES * SUBLANES


def _rglru_kernel(n_ctx, n_tot, ux_ref, cw_ref, cb_ref, w_ref, b_ref, lam_ref, out_ref,
                  us, z_nat, a_f, b_f, a_b, b_b, h_f, h_b, o_nat):
    c = LANES
    g = SUBLANES
    n_lat = n_tot - n_ctx
    nq = n_tot // g
    nblk = n_tot // LRU_BLK
    nblk_ctx = n_ctx // LRU_BLK

    zeros_g = jnp.zeros((g, c), F32)
    us[0:g, :] = zeros_g
    us[g + n_ctx:2 * g + n_ctx, :] = zeros_g
    us[2 * g + n_tot:3 * g + n_tot, :] = zeros_g
    us[g:g + n_ctx, :] = ux_ref[0, 0:n_ctx, :].astype(F32)
    us[2 * g + n_ctx:2 * g + n_tot, :] = ux_ref[0, n_ctx:n_tot, :].astype(F32)

    def conv(base, n):
        acc = cb_ref[...] + cw_ref[2:3, :] * us[base:base + n, :]
        for j, off in ((0, -2), (1, -1), (3, 1)):
            acc = acc + cw_ref[j:j + 1, :] * us[base + off:base + off + n, :]
        return acc

    z_nat[0:n_ctx, :] = conv(g, n_ctx)
    z_nat[n_ctx:n_tot, :] = conv(2 * g + n_ctx, n_lat)

    z = jnp.concatenate([z_nat[pl.ds(k, nq, stride=g), :] for k in range(g)], axis=0)
    p = _dot(z.astype(BF16), w_ref[0]) + b_ref[...]
    for d, (a_scr, b_scr) in enumerate(((a_f, b_f), (a_b, b_b))):
        r = _sigmoid(p[:, (2 * d) * c:(2 * d + 1) * c])
        i = _sigmoid(p[:, (2 * d + 1) * c:(2 * d + 2) * c])
        lam = lam_ref[d:d + 1, :]
        softplus = jnp.maximum(-lam, 0.0) + jnp.log(1.0 + jnp.exp(-jnp.abs(lam)))
        a = jnp.exp2((-LRU_C * math.log2(math.e) * softplus) * r)
        a_scr[...] = a
        b_scr[...] = jnp.sqrt(1.0 - a * a) * i * z

    sub = lax.broadcasted_iota(jnp.int32, (g, c), 0)

    def block_scan(a_scr, b_scr, h_scr, blk, carry, reverse):
        r0 = pl.multiple_of(blk * g, g)
        order = range(g - 1, -1, -1) if reverse else range(g)
        hs, ps = {}, {}
        h = p_ = None
        for k in order:
            a = a_scr[pl.ds(k * nq + r0, g), :]
            b = b_scr[pl.ds(k * nq + r0, g), :]
            h = b if h is None else a * h + b
            p_ = a if p_ is None else a * p_
            hs[k], ps[k] = h, p_
        def shift(x, sft, fill):
            if reverse:
                return jnp.where(sub < g - sft, pltpu.roll(x, g - sft, 0), fill)
            return jnp.where(sub >= sft, pltpu.roll(x, sft, 0), fill)
        pe, he = shift(p_, 1, 1.0), shift(h, 1, 0.0)
        for sft in (1, 2, 4):
            he = pe * shift(he, sft, 0.0) + he
            pe = pe * shift(pe, sft, 1.0)
        cin = pe * carry + he
        for k in order:
            h_scr[pl.ds(k * nq + r0, g), :] = hs[k] + ps[k] * cin
        last = 0 if reverse else g - 1
        tot = p_ * cin + h
        return tot[last:last + 1, :]

    def body(it, carry):
        cf, cb = carry
        cf = block_scan(a_f, b_f, h_f, it, cf, False)
        jb = jnp.where(it < nblk_ctx, nblk_ctx - 1 - it, nblk - 1 + nblk_ctx - it)
        cb = block_scan(a_b, b_b, h_b, jb, cb, True)
        return cf, cb

    zero = jnp.zeros((1, c), F32)
    lax.fori_loop(0, nblk, body, (zero, zero), unroll=2)

    q_ctx = n_ctx // g
    for k in range(g):
        o_nat[pl.ds(k, n_lat // g, stride=g), :] = (h_f[k * nq + q_ctx:(k + 1) * nq, :]
                                                    + h_b[k * nq + q_ctx:(k + 1) * nq, :])
    out_ref[0] = o_nat[...].astype(BF16)


def _rglru(ux_seq, conv_w, conv_b, w_cat, b_cat, lam, n_ctx):
    b_, n_tot, w = ux_seq.shape
    nb = LRU_BLOCKS
    c = w // nb
    return pl.pallas_call(
        functools.partial(_rglru_kernel, n_ctx, n_tot),
        out_shape=jax.ShapeDtypeStruct((b_, n_tot - n_ctx, w), BF16),
        grid=(b_, nb),
        in_specs=[
            pl.BlockSpec((1, n_tot, c), lambda b, k: (b, 0, k)),
            pl.BlockSpec((4, c), lambda b, k: (0, k)),
            pl.BlockSpec((1, c), lambda b, k: (0, k)),
            pl.BlockSpec((1, c, 4 * c), lambda b, k: (k, 0, 0)),
            pl.BlockSpec((1, 4 * c), lambda b, k: (0, k)),
            pl.BlockSpec((2, c), lambda b, k: (0, k)),
        ],
        out_specs=pl.BlockSpec((1, n_tot - n_ctx, c), lambda b, k: (b, 0, k)),
        scratch_shapes=([pltpu.VMEM((n_tot + 3 * SUBLANES, c), F32)] + [pltpu.VMEM((n_tot, c), F32)] * 7
                        + [pltpu.VMEM((n_tot - n_ctx, c), F32)]),
        compiler_params=_cparams(("parallel", "parallel")),
        name="rglru",
    )(ux_seq, conv_w, conv_b, w_cat, b_cat, lam)


def _outproj_kernel(yaf_ref, yab_ref, hr_ref, ug_ref, ga0_ref, ga1_ref, gb0_ref, gb1_ref, x_ref,
                    g1_ref, gain_ref, shift_ref, wpa_ref, wpb_ref, wout_ref, wr_ref,
                    x1_ref, h2_ref, lg_ref):
    ya = (yaf_ref[0].astype(F32) + yab_ref[0].astype(F32)).astype(BF16)
    ug = ug_ref[0].astype(F32)
    gelu = 0.5 * ug * (1.0 + jnp.tanh(0.7978845608028654 * (ug + 0.044715 * ug * ug * ug)))
    yb = (hr_ref[0].astype(F32) * gelu).astype(BF16)
    pa = _dot(ya, wpa_ref[...])
    pb = _dot(yb, wpb_ref[...])
    ga = jnp.concatenate([ga0_ref[0], ga1_ref[0]], axis=1).astype(F32)
    gb = jnp.concatenate([gb0_ref[0], gb1_ref[0]], axis=1).astype(F32)
    mix = (_sigmoid(ga) * pa + _sigmoid(gb) * pb).astype(BF16)
    x1 = x_ref[0] + g1_ref[0] * _dot(mix, wout_ref[...])
    x1_ref[0] = x1
    ms = jnp.mean(x1 * x1, axis=-1, keepdims=True)
    h2 = x1 * lax.rsqrt(ms + EPS) * gain_ref[0] + shift_ref[0]
    h2_ref[...] = h2
    wr = wr_ref[...]
    h1, h2b, _ = _split3(h2)
    w1, w2b, _ = _split3(wr)
    lg_ref[0] = _route_rows(_dot(h1, w1) + (_dot(h1, w2b) + _dot(h2b, w1)))


def _outproj(ya_f, ya_b, hr, u_lat, x, g1, gain2, shift2, w_pa, w_pb, w_out, w_r, tm):
    b_, s_, d = x.shape
    w = ya_f.shape[2]
    row = lambda b, i: (b, i, 0)
    const2 = lambda b, i: (0, 0)
    bmap = lambda b, i: (b, 0, 0)
    ucol = lambda t: (lambda b, i: (b, i, t))
    single = pl.Buffered(1)
    return pl.pallas_call(
        _outproj_kernel,
        out_shape=(jax.ShapeDtypeStruct((b_, s_, d), F32),
                   jax.ShapeDtypeStruct((b_ * s_, d), F32),
                   jax.ShapeDtypeStruct((b_, s_, LANES), F32)),
        grid=(b_, s_ // tm),
        in_specs=[
            pl.BlockSpec((1, tm, w), row), pl.BlockSpec((1, tm, w), row), pl.BlockSpec((1, tm, w), row),
            pl.BlockSpec((1, tm, w), ucol(COL_UG)),
            pl.BlockSpec((1, tm, w), ucol(COL_GA)), pl.BlockSpec((1, tm, w), ucol(COL_GA + 1)),
            pl.BlockSpec((1, tm, w), ucol(COL_GB)), pl.BlockSpec((1, tm, w), ucol(COL_GB + 1)),
            pl.BlockSpec((1, tm, d), row),
            pl.BlockSpec((1, 1, d), bmap), pl.BlockSpec((1, 1, d), bmap), pl.BlockSpec((1, 1, d), bmap),
            pl.BlockSpec((w, d), const2, pipeline_mode=single),
            pl.BlockSpec((w, d), const2, pipeline_mode=single),
            pl.BlockSpec((d, d), const2, pipeline_mode=single),
            pl.BlockSpec((d, LANES), const2, pipeline_mode=single),
        ],
        out_specs=(pl.BlockSpec((1, tm, d), row),
                   pl.BlockSpec((tm, d), lambda b, i: (b * (s_ // tm) + i, 0)),
                   pl.BlockSpec((1, tm, LANES), row)),
        compiler_params=_cparams(("parallel", "parallel")),
        name="outproj",
    )(ya_f, ya_b, hr, u_lat, u_lat, u_lat, u_lat, u_lat, x, g1, gain2, shift2, w_pa, w_pb, w_out, w_r)


def _route_rows(lg):
    lane = lax.broadcasted_iota(jnp.int32, lg.shape, 1)
    neg = -jnp.inf
    big = jnp.int32(1 << 20)
    g_l = jnp.where(lane < N_GROUPS, lg, neg)
    g_max = jnp.max(g_l, axis=-1, keepdims=True)
    g_sel = jnp.min(jnp.where(g_l == g_max, lane, big), axis=-1, keepdims=True)
    p_g = 1.0 / jnp.sum(jnp.exp(g_l - g_max), axis=-1, keepdims=True)
    lo = N_GROUPS + g_sel * EXPERTS_PER_GROUP
    e_l = jnp.where((lane >= lo) & (lane < lo + EXPERTS_PER_GROUP), lg, neg)
    v1 = jnp.max(e_l, axis=-1, keepdims=True)
    i1 = jnp.min(jnp.where(e_l == v1, lane, big), axis=-1, keepdims=True)
    e_l2 = jnp.where(lane == i1, neg, e_l)
    v2 = jnp.max(e_l2, axis=-1, keepdims=True)
    i2 = jnp.min(jnp.where(e_l2 == v2, lane, big), axis=-1, keepdims=True)
    e2 = jnp.exp(v2 - v1)
    w1 = p_g / (1.0 + e2)
    w2 = p_g * e2 / (1.0 + e2)
    return jnp.where(lane == 0, (i1 - N_GROUPS).astype(F32),
           jnp.where(lane == 1, (i2 - N_GROUPS).astype(F32),
           jnp.where(lane == 2, w1, jnp.where(lane == 3, w2, 0.0))))


MOE_DMA_UNROLL = 32


MOE_W_DEPTH = 3
MOE_W_SPLIT = 2


def _moe_kernel(R, SUB, nf, ni, item_e, item_rows, tok_ref, tokn_ref, dst_ref, h2_hbm, w1_hbm, w3_hbm, w2_hbm,
                y_hbm, x_scr, xb_scr, acc_scr, y_scr, w1r, w3r, w2r, w1b, w3b, w2b, gsem, ssem, wsem):
    i = pl.program_id(0)
    j = pl.program_id(1)
    nsub_of = lambda r: (r + SUB - 1) // SUB
    rows = item_rows[i]
    nsub = nsub_of(rows)
    nsub_next = jnp.where(i + 1 < ni, nsub_of(item_rows[jnp.minimum(i + 1, ni - 1)]), 0)
    rows_prev = jnp.where(i > 0, item_rows[jnp.maximum(i - 1, 0)], 0)
    buf = i % 2
    groups = SUB // MOE_DMA_UNROLL

    d_model, tf = w1r.shape[1], w1r.shape[2]

    def weight_copies(c):
        it = c // nf
        jc = c % nf
        e = item_e[jnp.minimum(it, ni - 1)]
        slot = c % MOE_W_DEPTH
        f0 = pl.multiple_of(jc * tf, tf)
        out = []
        for h in range(MOE_W_SPLIT):
            ra, rb = d_model // MOE_W_SPLIT, tf // MOE_W_SPLIT
            out.append(pltpu.make_async_copy(w1_hbm.at[e, pl.ds(h * ra, ra), pl.ds(f0, tf)],
                                             w1r.at[slot, pl.ds(h * ra, ra), :], wsem.at[slot]))
            out.append(pltpu.make_async_copy(w3_hbm.at[e, pl.ds(h * ra, ra), pl.ds(f0, tf)],
                                             w3r.at[slot, pl.ds(h * ra, ra), :], wsem.at[slot]))
            out.append(pltpu.make_async_copy(w2_hbm.at[e, pl.ds(f0 + h * rb, rb), :],
                                             w2r.at[slot, pl.ds(h * rb, rb), :], wsem.at[slot]))
        return out

    def chunk_live(c):
        it = c // nf
        return (it < ni) & (item_rows[jnp.minimum(it, ni - 1)] > 0)

    cur = i * nf + j

    @pl.when((cur == 0) & chunk_live(0))
    def _():
        for cp in weight_copies(0):
            cp.start()

        @pl.when(chunk_live(1))
        def _():
            for cp in weight_copies(1):
                cp.start()

    @pl.when((rows > 0) & chunk_live(cur + 2))
    def _():
        for cp in weight_copies(cur + 2):
            cp.start()

    def tok_rows(ref, r):
        return ref.at[pl.ds(r, 1), :]

    def gather_group(tref, b, gi):
        base = pl.multiple_of(gi * MOE_DMA_UNROLL, MOE_DMA_UNROLL)
        for k in range(MOE_DMA_UNROLL):
            tok = tref[0, 0, base + k]
            pltpu.make_async_copy(tok_rows(h2_hbm, tok), tok_rows(x_scr.at[b], base + k),
                                  gsem.at[b]).start(priority=k % 2)

    def gather_wait_block(b):
        pltpu.make_async_copy(h2_hbm.at[pl.ds(0, SUB), :], x_scr.at[b, pl.ds(0, SUB), :], gsem.at[b]).wait()

    def scatter_copy(dref, r):
        dst = dref[0, 0, r]
        return pltpu.make_async_copy(tok_rows(y_scr, r), tok_rows(y_hbm, dst), ssem)

    def scatter_group(dref, gi):
        base = pl.multiple_of(gi * MOE_DMA_UNROLL, MOE_DMA_UNROLL)
        for k in range(MOE_DMA_UNROLL):
            scatter_copy(dref, base + k).start(priority=k % 2)

    def repeat(n, fn, lo=0):
        def body(q, c):
            fn(q)
            return c
        lax.fori_loop(lo, n, body, 0)

    def scatter_rows(dref, n):
        full = n // MOE_DMA_UNROLL
        repeat(full, lambda gi: scatter_group(dref, gi))
        repeat(n, lambda r: scatter_copy(dref, r).start(), lo=full * MOE_DMA_UNROLL)

    def scatter_wait_rows(n):
        u = MOE_DMA_UNROLL
        full = n // MOE_DMA_UNROLL
        repeat(full, lambda q: pltpu.make_async_copy(y_scr.at[pl.ds(0, u), :], y_hbm.at[pl.ds(0, u), :],
                                                      ssem).wait())
        repeat(n, lambda r: scatter_copy(dst_ref, 0).wait(), lo=full * MOE_DMA_UNROLL)

    @pl.when((i == 0) & (j == 0))
    def _():
        repeat(nsub * groups, lambda gi: gather_group(tok_ref, 0, gi))

    @pl.when(j < nsub_next)
    def _():
        repeat((j + 1) * groups, lambda gi: gather_group(tokn_ref, 1 - buf, gi), lo=j * groups)

    @pl.when(j == 0)
    def _():
        repeat(nsub, lambda q: gather_wait_block(buf))

    @pl.when(j == nf - 1)
    def _():
        scatter_wait_rows(rows_prev)

    @pl.when(rows > 0)
    def _():
        for cp in weight_copies(cur):
            cp.wait()
        slot = cur % MOE_W_DEPTH
        w1b[...] = w1r[slot].astype(BF16)
        w3b[...] = w3r[slot].astype(BF16)
        w2b[...] = w2r[slot].astype(BF16)

        @pl.when(j == 0)
        def _():
            def to_bf16(sb, c):
                r0 = pl.multiple_of(sb * SUB, SUB)
                xb_scr[pl.ds(r0, SUB), :] = x_scr[buf, pl.ds(r0, SUB), :].astype(BF16)
                return c
            lax.fori_loop(0, nsub, to_bf16, 0)

        def experts(m):
            def run():
                x = xb_scr[0:m, :]
                h1 = _dot(x, w1b[...])
                h3 = _dot(x, w3b[...])
                hh = (h1 * _sigmoid(h1) * h3).astype(BF16)
                contrib = _dot(hh, w2b[...])

                @pl.when(j == 0)
                def _():
                    acc_scr[0:m, :] = contrib

                @pl.when((j > 0) & (j < nf - 1))
                def _():
                    acc_scr[0:m, :] += contrib

                @pl.when(j == nf - 1)
                def _():
                    y_scr[0:m, :] = acc_scr[0:m, :] + contrib
            return run
        lax.switch(nsub - 1, [experts((q + 1) * SUB) for q in range(R // SUB)])

    @pl.when(j == nf - 1)
    def _():
        scatter_rows(dst_ref, rows)

        @pl.when(i == ni - 1)
        def _():
            scatter_wait_rows(rows)


def _moe(h2, item_e, item_rows, slot_tok, slot_dst, w1, w3, w2, n_out_rows):
    ne, d, f = w1.shape
    R, SUB, tf = MOE_ITEM_ROWS, MOE_SUB_ROWS, MOE_F_CHUNK
    ni = item_e.shape[0]
    nf = f // tf
    assert R % SUB == 0 and R <= nf * SUB and nf >= 2
    grid_spec = pltpu.PrefetchScalarGridSpec(
        num_scalar_prefetch=2,
        grid=(ni, nf),
        in_specs=[
            pl.BlockSpec((1, 1, R), lambda i, j, ie, ir: (i, 0, 0), memory_space=pltpu.SMEM),
            pl.BlockSpec((1, 1, R), lambda i, j, ie, ir: (jnp.minimum(i + 1, ni - 1), 0, 0),
                         memory_space=pltpu.SMEM),
            pl.BlockSpec((1, 1, R), lambda i, j, ie, ir: (i, 0, 0), memory_space=pltpu.SMEM),
            pl.BlockSpec(memory_space=pl.ANY),
            pl.BlockSpec(memory_space=pl.ANY),
            pl.BlockSpec(memory_space=pl.ANY),
            pl.BlockSpec(memory_space=pl.ANY),
        ],
        out_specs=pl.BlockSpec(memory_space=pl.ANY),
        scratch_shapes=[
            pltpu.VMEM((2, R, d), F32), pltpu.VMEM((R, d), BF16),
            pltpu.VMEM((R, d), F32), pltpu.VMEM((R, d), F32),
            pltpu.VMEM((MOE_W_DEPTH, d, tf), F32), pltpu.VMEM((MOE_W_DEPTH, d, tf), F32),
            pltpu.VMEM((MOE_W_DEPTH, tf, d), F32),
            pltpu.VMEM((d, tf), BF16), pltpu.VMEM((d, tf), BF16), pltpu.VMEM((tf, d), BF16),
            pltpu.SemaphoreType.DMA((2,)), pltpu.SemaphoreType.DMA(()), pltpu.SemaphoreType.DMA((MOE_W_DEPTH,)),
        ],
    )
    return pl.pallas_call(
        functools.partial(_moe_kernel, R, SUB, nf, ni),
        out_shape=jax.ShapeDtypeStruct((n_out_rows, d), F32),
        grid_spec=grid_spec,
        compiler_params=_cparams(("arbitrary", "arbitrary")),
        name="moe_experts",
    )(item_e, item_rows, slot_tok, slot_tok, slot_dst, h2, w1, w3, w2)


def _moe_plan(expert, t_):
    R = MOE_ITEM_ROWS
    na = expert.shape[0]
    ni = N_EXPERTS + na // R
    onehot = (expert[:, None] == jnp.arange(N_EXPERTS, dtype=jnp.int32)[None, :]).astype(jnp.int32)
    csum = jnp.cumsum(onehot, axis=0)
    rank = jnp.sum(csum * onehot, axis=1) - 1
    counts = csum[-1]
    n_items = (counts + R - 1) // R
    item_end = jnp.cumsum(n_items)
    item_start = item_end - n_items
    slot = jnp.sum(item_start[None, :] * onehot, axis=1) * R + rank
    slot_a = jnp.full((ni * R,), -1, jnp.int32).at[slot].set(jnp.arange(na, dtype=jnp.int32))
    valid = slot_a >= 0
    slot_tok = jnp.where(valid, slot_a // TOP_K, 0)
    slot_dst = jnp.where(valid, (slot_a % TOP_K) * t_ + slot_a // TOP_K, 0)
    ii = jnp.arange(ni, dtype=jnp.int32)
    e_of = jnp.minimum(jnp.sum((item_end[None, :] <= ii[:, None]).astype(jnp.int32), axis=1), N_EXPERTS - 1)
    oh_e = (e_of[:, None] == jnp.arange(N_EXPERTS, dtype=jnp.int32)[None, :]).astype(jnp.int32)
    cnt_e = jnp.sum(oh_e * counts[None, :], axis=1)
    start_e = jnp.sum(oh_e * item_start[None, :], axis=1)
    rows = jnp.clip(cnt_e - (ii - start_e) * R, 0, R).astype(jnp.int32)
    live = ii < item_end[-1]
    rows = jnp.where(live, rows, 0)
    last_e = jnp.max(jnp.where(live, e_of, 0))
    e_of = jnp.where(live, e_of, last_e).astype(jnp.int32)
    return e_of, rows, slot_tok.reshape(ni, 1, R), slot_dst.reshape(ni, 1, R)


def _final_kernel(x1_ref, y0_ref, y1_ref, rt_ref, g2_ref, fg_ref, o_ref):
    rt = rt_ref[0]
    y = rt[:, TOP_K:TOP_K + 1] * y0_ref[...] + rt[:, TOP_K + 1:TOP_K + 2] * y1_ref[...]
    x = x1_ref[0] + g2_ref[0] * y
    ms = jnp.mean(x * x, axis=-1, keepdims=True)
    o_ref[0] = x * lax.rsqrt(ms + EPS) * fg_ref[...]


def _final(x1, ybuf, routed, g2, final_g, tm):
    b_, s_, d = x1.shape
    nt = s_ // tm
    return pl.pallas_call(
        _final_kernel,
        out_shape=jax.ShapeDtypeStruct((b_, s_, d), F32),
        grid=(b_, nt),
        in_specs=[
            pl.BlockSpec((1, tm, d), lambda b, i: (b, i, 0)),
            pl.BlockSpec((tm, d), lambda b, i: (b * nt + i, 0)),
            pl.BlockSpec((tm, d), lambda b, i: (b_ * nt + b * nt + i, 0)),
            pl.BlockSpec((1, tm, LANES), lambda b, i: (b, i, 0)),
            pl.BlockSpec((1, 1, d), lambda b, i: (b, 0, 0)),
            pl.BlockSpec((1, d), lambda b, i: (0, 0)),
        ],
        out_specs=pl.BlockSpec((1, tm, d), lambda b, i: (b, i, 0)),
        compiler_params=_cparams(("parallel", "parallel")),
        name="final_norm",
    )(x1, ybuf, ybuf, routed, g2, final_g)


def kernel(x, c, ctx, c_ctx, w_mod, b_mod, norm1_g, w_in, b_gates, mlstm_head_g, conv_w, conv_b, lru_wa, lru_ba,
           lru_wx, lru_bx, lru_lam, w_pa, w_pb, w_out, norm2_g, w_rg, w_re, w1, w3, w2, final_g):
    b_, s_, d = x.shape
    n_ctx = ctx.shape[1]
    rows = s_ // GRID_W
    assert w_mod.shape[0] == 1, "single layer"
    wm = d // 2
    nh = MLSTM_HEADS

    cc = jnp.zeros((8, d), F32).at[:b_].set(c).at[b_].set(c_ctx)
    mod = _modulation(cc, w_mod[0], b_mod[0][None, :])
    sh1, sc1, g1, sh2, sc2, g2 = [mod[:b_, i * d:(i + 1) * d][:, None, :] for i in range(6)]
    csh1, csc1 = mod[b_:b_ + 1, 0:d][:, None, :], mod[b_:b_ + 1, d:2 * d][:, None, :]

    assert wm == 1024 and 4 * nh == N_GATE_COLS
    w_t = jnp.transpose(w_in[0])
    bg_pad = jnp.zeros((1, LANES), F32).at[0, :4 * nh].set(b_gates[0])

    ng = norm1_g[0][None, None, :]
    u_lat, g_lat = _inproj(x, ng * (1.0 + sc1), sh1, w_t, bg_pad, tuple(range(N_COL_TILES)), min(1024, s_))
    u_ctx, g_ctx = _inproj(ctx.reshape(1, b_ * n_ctx, d), ng * (1.0 + csc1), csh1, w_t, bg_pad,
                           (COL_Q, COL_K, COL_V, COL_UX), b_ * n_ctx)
    u_ctx = u_ctx.reshape(b_, n_ctx, -1)
    g_ctx = g_ctx.reshape(b_, n_ctx, LANES)

    ya_f, ya_b = _mlstm(u_lat, u_ctx, g_lat, g_ctx, mlstm_head_g[0], MLSTM_CHUNK)

    ux_lat = u_lat[:, :, COL_UX * wm:(COL_UX + 1) * wm]
    ux_col = ux_lat.reshape(b_, rows, GRID_W, wm).transpose(0, 2, 1, 3).reshape(b_, s_, wm)
    ux_seq = jnp.concatenate([u_ctx[:, :, CTX_UX * wm:(CTX_UX + 1) * wm], ux_col], axis=1)
    wa, wx = lru_wa[0], lru_wx[0]
    w_cat = jnp.concatenate([wa[0], wx[0], wa[1], wx[1]], axis=-1).astype(BF16)
    cblk = wm // LRU_BLOCKS
    blk = lambda v: v.reshape(LRU_BLOCKS, cblk)
    b_cat = jnp.concatenate([blk(lru_ba[0, 0]), blk(lru_bx[0, 0]), blk(lru_ba[0, 1]), blk(lru_bx[0, 1])],
                            axis=-1).reshape(1, 4 * wm)
    h_col = _rglru(ux_seq, conv_w[0], conv_b[0][None, :], w_cat, b_cat, lru_lam[0], n_ctx)
    hr = h_col.reshape(b_, GRID_W, rows, wm).transpose(0, 2, 1, 3).reshape(b_, s_, wm)

    w_r = jnp.zeros((d, LANES), F32).at[:, :N_GROUPS].set(w_rg[0]).at[:, N_GROUPS:N_GROUPS + N_EXPERTS].set(w_re[0])
    n2 = norm2_g[0][None, None, :]
    x1, h2, routed = _outproj(ya_f, ya_b, hr, u_lat, x, g1, n2 * (1.0 + sc2), sh2,
                              w_pa[0].astype(BF16), w_pb[0].astype(BF16), w_out[0].astype(BF16), w_r, 256)

    t_ = b_ * s_
    expert = routed.reshape(t_, LANES)[:, :TOP_K].astype(jnp.int32).reshape(-1)
    item_e, item_rows, slot_tok, slot_dst = _moe_plan(expert, t_)
    ybuf = _moe(h2, item_e, item_rows, slot_tok, slot_dst, w1[0], w3[0], w2[0], TOP_K * t_)

    return _final(x1, ybuf, routed, g2, final_g[None, :], 256)
```

```python
import functools
import math

import jax
import jax.numpy as jnp
from jax import lax
from jax.experimental import pallas as pl
from jax.experimental.pallas import tpu as pltpu

F32 = jnp.float32
BF16 = jnp.bfloat16

EPS = 1e-6
GRID_W = 64
MLSTM_HEADS = 8
LRU_BLOCKS = 8
LRU_C = 8.0
N_GROUPS = 4
EXPERTS_PER_GROUP = 8
N_EXPERTS = N_GROUPS * EXPERTS_PER_GROUP
TOP_K = 2

VMEM_LIMIT_BYTES = 56 * 1024 * 1024
LANES = 128

MLSTM_CHUNK = 128
MOE_ITEM_ROWS = 768
MOE_SUB_ROWS = 128
MOE_F_CHUNK = 256


def _cparams(sem):
    return pltpu.CompilerParams(dimension_semantics=sem, vmem_limit_bytes=VMEM_LIMIT_BYTES)


def _sigmoid(x):
    return 0.5 * jnp.tanh(0.5 * x) + 0.5


def _dot(a, b):
    return jnp.dot(a, b, preferred_element_type=F32)


def _split3(x):
    x1 = x.astype(BF16)
    r1 = x - x1.astype(F32)
    x2 = r1.astype(BF16)
    x3 = (r1 - x2.astype(F32)).astype(BF16)
    return x1, x2, x3


def _mod_kernel(c_ref, w_ref, b_ref, o_ref):
    c = c_ref[...]
    s = (c * _sigmoid(c)).astype(BF16)
    o_ref[...] = _dot(s, w_ref[...].astype(BF16)) + b_ref[...]


def _modulation(cc, w_mod, b_mod):
    m, d = cc.shape
    n = w_mod.shape[1]
    tn = 1024
    return pl.pallas_call(
        _mod_kernel,
        out_shape=jax.ShapeDtypeStruct((m, n), F32),
        grid=(n // tn,),
        in_specs=[
            pl.BlockSpec((m, d), lambda j: (0, 0)),
            pl.BlockSpec((d, tn), lambda j: (0, j)),
            pl.BlockSpec((1, tn), lambda j: (0, j)),
        ],
        out_specs=pl.BlockSpec((m, tn), lambda j: (0, j)),
        compiler_params=_cparams(("parallel",)),
        name="modulation",
    )(cc, w_mod, b_mod)


N_PRE_TILES = 5
N_GATE_COLS = 32


def _dot_nt(a, b):
    return lax.dot_general(a, b, (((1,), (1,)), ((), ())), preferred_element_type=F32)


def _inproj_kernel(x_ref, gain_ref, shift_ref, wt_ref, wg_ref, bg_ref, u_ref, g_ref, h_scr):
    @pl.when(pl.program_id(2) == 0)
    def _():
        x = x_ref[0]
        ms = jnp.mean(x * x, axis=-1, keepdims=True)
        h = x * lax.rsqrt(ms + EPS) * gain_ref[0] + shift_ref[0]
        hb = h.astype(BF16)
        h_scr[...] = hb
        gates = _dot_nt(hb, wg_ref[...].astype(BF16))
        pad = jnp.zeros((gates.shape[0], LANES - N_GATE_COLS), F32)
        g_ref[0] = jnp.concatenate([gates, pad], axis=1) + bg_ref[...]

    u_ref[0] = _dot_nt(h_scr[...], wt_ref[...].astype(BF16)).astype(BF16)


def _inproj(x, gain, shift, w_t, b_gates, tiles, tm):
    b_, s_, d = x.shape
    tn = 1024
    nj = len(tiles)
    if tiles == tuple(range(nj)):
        tile_of = lambda j: j
    else:
        assert tiles == (COL_Q, COL_K, COL_V, COL_UX), tiles
        tile_of = lambda j: jnp.where(j == 3, COL_UX, j)
    row_of = lambda j: tile_of(j) * tn + jnp.where(tile_of(j) >= N_PRE_TILES, N_GATE_COLS, 0)
    per_batch_mod = gain.shape[0] > 1
    mmap = (lambda b, i, j: (b, 0, 0)) if per_batch_mod else (lambda b, i, j: (0, 0, 0))
    gate_blk = N_PRE_TILES * tn // N_GATE_COLS
    return pl.pallas_call(
        _inproj_kernel,
        out_shape=(jax.ShapeDtypeStruct((b_, s_, nj * tn), BF16),
                   jax.ShapeDtypeStruct((b_, s_, LANES), F32)),
        grid=(b_, s_ // tm, nj),
        in_specs=[
            pl.BlockSpec((1, tm, d), lambda b, i, j: (b, i, 0)),
            pl.BlockSpec((1, 1, d), mmap),
            pl.BlockSpec((1, 1, d), mmap),
            pl.BlockSpec((pl.Element(tn), pl.Element(d)), lambda b, i, j: (pl.multiple_of(row_of(j), 8), 0)),
            pl.BlockSpec((N_GATE_COLS, d), lambda b, i, j: (gate_blk, 0)),
            pl.BlockSpec((1, LANES), lambda b, i, j: (0, 0)),
        ],
        out_specs=(pl.BlockSpec((1, tm, tn), lambda b, i, j: (b, i, j)),
                   pl.BlockSpec((1, tm, LANES), lambda b, i, j: (b, i, 0))),
        scratch_shapes=[pltpu.VMEM((tm, d), BF16)],
        compiler_params=_cparams(("parallel", "parallel", "arbitrary")),
        name="inproj",
    )(x, gain, shift, w_t, w_t, b_gates)


COL_Q, COL_K, COL_V, COL_OF, COL_OB, COL_UX, COL_UG, COL_GA, COL_GB = 0, 1, 2, 3, 4, 5, 6, 7, 9
CTX_UX = 3
N_COL_TILES = 11


def _mlstm_kernel(nc, nl, L, dh,
                  qc_f, kc_f, vc_f, ql_f, kl_f, vl_f, gc_f, gl_f, o_f,
                  qc_b, kc_b, vc_b, ql_b, kl_b, vl_b, gc_b, gl_b, o_b,
                  hg_ref, tri_ref,
                  out_f, out_b, c_scr, m_scr):
    s = pl.program_id(1)
    nh = MLSTM_HEADS
    scale = dh ** -0.5

    @pl.when(s == 0)
    def _():
        c_scr[...] = jnp.zeros_like(c_scr)
        m_scr[...] = jnp.full_like(m_scr, -1e30)

    is_ctx = s < nc
    ones_col = (lax.broadcasted_iota(jnp.int32, (L, dh), 1) == 0).astype(BF16)
    row = lax.broadcasted_iota(jnp.int32, (L, L), 0)
    col = lax.broadcasted_iota(jnp.int32, (L, L), 1)

    for d, (qc, kc, vc, ql, kl, vl, gc, gl, o_ref, out_ref) in enumerate((
            (qc_f, kc_f, vc_f, ql_f, kl_f, vl_f, gc_f, gl_f, o_f, out_f),
            (qc_b, kc_b, vc_b, ql_b, kl_b, vl_b, gc_b, gl_b, o_b, out_b))):
        q = jnp.where(is_ctx, qc[0], ql[0])
        k = jnp.where(is_ctx, kc[0], kl[0])
        v = jnp.where(is_ctx, vc[0], vl[0])
        g = jnp.where(is_ctx, gc[0], gl[0])
        gt = g.T
        ls = jnp.minimum(g, 0.0) - jnp.log(1.0 + jnp.exp(-jnp.abs(g)))
        lst = jnp.minimum(gt, 0.0) - jnp.log(1.0 + jnp.exp(-jnp.abs(gt)))
        tri = tri_ref[d]
        mask = (col <= row) if d == 0 else (col >= row)
        l1, l2, l3 = _split3(ls)
        bcol_all = _dot(tri, l1) + _dot(tri, l2) + _dot(tri, l3)
        t1, t2, t3 = _split3(lst)
        trit = tri_ref[1 - d]
        brow_all = _dot(t1, trit) + _dot(t2, trit) + _dot(t3, trit)
        last = L - 1 if d == 0 else 0

        bal = pltpu.roll(bcol_all, LANES - 8, axis=1)
        cmax = g - bal
        trow = lax.broadcasted_iota(jnp.int32, (L, LANES), 0)
        sft = 1
        while sft < L:
            if d == 0:
                shifted = jnp.where(trow >= sft, pltpu.roll(cmax, sft, axis=0), -jnp.inf)
            else:
                shifted = jnp.where(trow < L - sft, pltpu.roll(cmax, L - sft, axis=0), -jnp.inf)
            cmax = jnp.maximum(cmax, shifted)
            sft *= 2
        m_row = m_scr[d][0:1, :]
        gg = bal + m_row
        mt = jnp.maximum(gg, bal + cmax)
        e_col = bal - mt
        w_inter = jnp.exp(gg - mt)
        enm = jnp.exp(-mt)
        tot = bal[last:last + 1, :]
        wlog = tot - bal + g
        m_new = jnp.maximum(tot + m_row, jnp.max(wlog, axis=0, keepdims=True))
        ws = jnp.exp(wlog - m_new) * scale
        decay = jnp.exp(tot + m_row - m_new)
        m_scr[d] = jnp.broadcast_to(m_new, m_scr.shape[1:])

        for h in range(nh):
            u = d * nh + h
            ci = 16 * d + h
            cf = 16 * d + 8 + h
            hs = slice(h * dh, (h + 1) * dh)
            qh, kh, vh = q[:, hs], k[:, hs], v[:, hs]
            vext = jnp.concatenate([vh, ones_col], axis=1)
            cr = gt[ci:ci + 1, :] - brow_all[cf:cf + 1, :] + math.log(scale)
            c_prev = c_scr[u]

            sqk = lax.dot_general(qh, kh, (((1,), (1,)), ((), ())), preferred_element_type=F32)
            w_intra = jnp.exp(jnp.where(mask, e_col[:, ci:ci + 1] + cr, -jnp.inf))
            a = (sqk * w_intra).astype(BF16)
            qs = (qh.astype(F32) * w_inter[:, ci:ci + 1]).astype(BF16)
            r = _dot(jnp.concatenate([a, qs], axis=1),
                     jnp.concatenate([vext, c_prev.astype(BF16)], axis=0))

            num = r[:, :dh]
            den = r[:, dh:dh + 1]
            rden = 1.0 / jnp.maximum(jnp.abs(den), enm[:, ci:ci + 1])
            ssq = jnp.sum(num * num, axis=-1, keepdims=True)
            sc = rden * lax.rsqrt(rden * rden * ssq * (1.0 / dh) + EPS)
            gate = _sigmoid(o_ref[0, :, hs].astype(F32))
            out_ref[0, :, hs] = (num * sc * hg_ref[h:h + 1, :] * gate).astype(BF16)

            kw = (kh.astype(F32) * ws[:, ci:ci + 1]).astype(BF16)
            upd = lax.dot_general(kw, vext, (((0,), (0,)), ((), ())), preferred_element_type=F32)
            c_scr[u] = decay[:, ci:ci + 1] * c_prev + upd


def _mlstm(u_lat, u_ctx, g_lat, g_ctx, head_g, L):
    b_, s_, _ = u_lat.shape
    n_ctx = u_ctx.shape[1]
    nh = MLSTM_HEADS
    w = 1024
    dh = w // nh
    nc, nl = n_ctx // L, s_ // L
    steps = nc + nl
    t0 = jnp.tril(jnp.ones((L, L), F32))
    tri = jnp.stack([t0, t0.T]).astype(BF16)

    def cidx(d):
        if d == 0:
            return lambda st: jnp.minimum(st, nc - 1), lambda st: jnp.maximum(st - nc, 0)
        return lambda st: jnp.maximum(nc - 1 - st, 0), lambda st: jnp.minimum(nl - 1 + nc - st, nl - 1)

    in_specs, args = [], []
    for d in range(2):
        fc, fl = cidx(d)
        for t in range(3):
            in_specs.append(pl.BlockSpec((1, L, w), lambda b, st, fc=fc, t=t: (b, fc(st), t)))
            args.append(u_ctx)
        for t in (COL_Q, COL_K, COL_V):
            in_specs.append(pl.BlockSpec((1, L, w), lambda b, st, fl=fl, t=t: (b, fl(st), t)))
            args.append(u_lat)
        in_specs.append(pl.BlockSpec((1, L, LANES), lambda b, st, fc=fc: (b, fc(st), 0)))
        args.append(g_ctx)
        in_specs.append(pl.BlockSpec((1, L, LANES), lambda b, st, fl=fl: (b, fl(st), 0)))
        args.append(g_lat)
        in_specs.append(pl.BlockSpec((1, L, w), lambda b, st, fl=fl, t=COL_OF + d: (b, fl(st), t)))
        args.append(u_lat)
    in_specs += [pl.BlockSpec((nh, dh), lambda b, st: (0, 0)),
                 pl.BlockSpec((2, L, L), lambda b, st: (0, 0, 0))]
    args += [head_g, tri]
    out_specs = tuple(pl.BlockSpec((1, L, w), lambda b, st, fl=cidx(d)[1]: (b, fl(st), 0)) for d in range(2))
    return pl.pallas_call(
        functools.partial(_mlstm_kernel, nc, nl, L, dh),
        out_shape=(jax.ShapeDtypeStruct((b_, s_, w), BF16),) * 2,
        grid=(b_, steps),
        in_specs=in_specs,
        out_specs=out_specs,
        scratch_shapes=[pltpu.VMEM((2 * nh, dh, 2 * dh), F32), pltpu.VMEM((2, 8, LANES), F32)],
        compiler_params=_cparams(("parallel", "arbitrary")),
        name="mlstm",
    )(*args)


SUBLANES = 8
LRU_BLK = SUBLANES * SUBLANES


def _rglru_kernel(n_ctx, n_tot, ux_ref, cw_ref, cb_ref, w_ref, b_ref, lam_ref, out_ref,
                  us, z_nat, a_f, b_f, a_b, b_b, h_f, h_b, o_nat):
    c = LANES
    g = SUBLANES
    n_lat = n_tot - n_ctx
    nq = n_tot // g
    nblk = n_tot // LRU_BLK
    nblk_ctx = n_ctx // LRU_BLK

    zeros_g = jnp.zeros((g, c), F32)
    us[0:g, :] = zeros_g
    us[g + n_ctx:2 * g + n_ctx, :] = zeros_g
    us[2 * g + n_tot:3 * g + n_tot, :] = zeros_g
    us[g:g + n_ctx, :] = ux_ref[0, 0:n_ctx, :].astype(F32)
    us[2 * g + n_ctx:2 * g + n_tot, :] = ux_ref[0, n_ctx:n_tot, :].astype(F32)

    def conv(base, n):
        acc = cb_ref[...] + cw_ref[2:3, :] * us[base:base + n, :]
        for j, off in ((0, -2), (1, -1), (3, 1)):
            acc = acc + cw_ref[j:j + 1, :] * us[base + off:base + off + n, :]
        return acc

    z_nat[0:n_ctx, :] = conv(g, n_ctx)
    z_nat[n_ctx:n_tot, :] = conv(2 * g + n_ctx, n_lat)

    z = jnp.concatenate([z_nat[pl.ds(k, nq, stride=g), :] for k in range(g)], axis=0)
    p = _dot(z.astype(BF16), w_ref[0]) + b_ref[...]
    for d, (a_scr, b_scr) in enumerate(((a_f, b_f), (a_b, b_b))):
        r = _sigmoid(p[:, (2 * d) * c:(2 * d + 1) * c])
        i = _sigmoid(p[:, (2 * d + 1) * c:(2 * d + 2) * c])
        lam = lam_ref[d:d + 1, :]
        softplus = jnp.maximum(-lam, 0.0) + jnp.log(1.0 + jnp.exp(-jnp.abs(lam)))
        a = jnp.exp2((-LRU_C * math.log2(math.e) * softplus) * r)
        a_scr[...] = a
        b_scr[...] = jnp.sqrt(1.0 - a * a) * i * z

    sub = lax.broadcasted_iota(jnp.int32, (g, c), 0)

    def block_scan(a_scr, b_scr, h_scr, blk, carry, reverse):
        r0 = pl.multiple_of(blk * g, g)
        order = range(g - 1, -1, -1) if reverse else range(g)
        hs, ps = {}, {}
        h = p_ = None
        for k in order:
            a = a_scr[pl.ds(k * nq + r0, g), :]
            b = b_scr[pl.ds(k * nq + r0, g), :]
            h = b if h is None else a * h + b
            p_ = a if p_ is None else a * p_
            hs[k], ps[k] = h, p_
        def shift(x, sft, fill):
            if reverse:
                return jnp.where(sub < g - sft, pltpu.roll(x, g - sft, 0), fill)
            return jnp.where(sub >= sft, pltpu.roll(x, sft, 0), fill)
        pe, he = shift(p_, 1, 1.0), shift(h, 1, 0.0)
        for sft in (1, 2, 4):
            he = pe * shift(he, sft, 0.0) + he
            pe = pe * shift(pe, sft, 1.0)
        cin = pe * carry + he
        for k in order:
            h_scr[pl.ds(k * nq + r0, g), :] = hs[k] + ps[k] * cin
        last = 0 if reverse else g - 1
        tot = p_ * cin + h
        return tot[last:last + 1, :]

    def body(it, carry):
        cf, cb = carry
        cf = block_scan(a_f, b_f, h_f, it, cf, False)
        jb = jnp.where(it < nblk_ctx, nblk_ctx - 1 - it, nblk - 1 + nblk_ctx - it)
        cb = block_scan(a_b, b_b, h_b, jb, cb, True)
        return cf, cb

    zero = jnp.zeros((1, c), F32)
    lax.fori_loop(0, nblk, body, (zero, zero), unroll=2)

    q_ctx = n_ctx // g
    for k in range(g):
        o_nat[pl.ds(k, n_lat // g, stride=g), :] = (h_f[k * nq + q_ctx:(k + 1) * nq, :]
                                                    + h_b[k * nq + q_ctx:(k + 1) * nq, :])
    out_ref[0] = o_nat[...].astype(BF16)


def _rglru(ux_seq, conv_w, conv_b, w_cat, b_cat, lam, n_ctx):
    b_, n_tot, w = ux_seq.shape
    nb = LRU_BLOCKS
    c = w // nb
    return pl.pallas_call(
        functools.partial(_rglru_kernel, n_ctx, n_tot),
        out_shape=jax.ShapeDtypeStruct((b_, n_tot - n_ctx, w), BF16),
        grid=(b_, nb),
        in_specs=[
            pl.BlockSpec((1, n_tot, c), lambda b, k: (b, 0, k)),
            pl.BlockSpec((4, c), lambda b, k: (0, k)),
            pl.BlockSpec((1, c), lambda b, k: (0, k)),
            pl.BlockSpec((1, c, 4 * c), lambda b, k: (k, 0, 0)),
            pl.BlockSpec((1, 4 * c), lambda b, k: (0, k)),
            pl.BlockSpec((2, c), lambda b, k: (0, k)),
        ],
        out_specs=pl.BlockSpec((1, n_tot - n_ctx, c), lambda b, k: (b, 0, k)),
        scratch_shapes=([pltpu.VMEM((n_tot + 3 * SUBLANES, c), F32)] + [pltpu.VMEM((n_tot, c), F32)] * 7
                        + [pltpu.VMEM((n_tot - n_ctx, c), F32)]),
        compiler_params=_cparams(("parallel", "parallel")),
        name="rglru",
    )(ux_seq, conv_w, conv_b, w_cat, b_cat, lam)


def _outproj_kernel(yaf_ref, yab_ref, hr_ref, ug_ref, ga0_ref, ga1_ref, gb0_ref, gb1_ref, x_ref,
                    g1_ref, gain_ref, shift_ref, wpa_ref, wpb_ref, wout_ref, wr_ref,
                    x1_ref, h2_ref, lg_ref):
    ya = (yaf_ref[0].astype(F32) + yab_ref[0].astype(F32)).astype(BF16)
    ug = ug_ref[0].astype(F32)
    gelu = 0.5 * ug * (1.0 + jnp.tanh(0.7978845608028654 * (ug + 0.044715 * ug * ug * ug)))
    yb = (hr_ref[0].astype(F32) * gelu).astype(BF16)
    pa = _dot(ya, wpa_ref[...])
    pb = _dot(yb, wpb_ref[...])
    ga = jnp.concatenate([ga0_ref[0], ga1_ref[0]], axis=1).astype(F32)
    gb = jnp.concatenate([gb0_ref[0], gb1_ref[0]], axis=1).astype(F32)
    mix = (_sigmoid(ga) * pa + _sigmoid(gb) * pb).astype(BF16)
    x1 = x_ref[0] + g1_ref[0] * _dot(mix, wout_ref[...])
    x1_ref[0] = x1
    ms = jnp.mean(x1 * x1, axis=-1, keepdims=True)
    h2 = x1 * lax.rsqrt(ms + EPS) * gain_ref[0] + shift_ref[0]
    h2_ref[...] = h2
    wr = wr_ref[...]
    h1, h2b, _ = _split3(h2)
    w1, w2b, _ = _split3(wr)
    lg_ref[0] = _route_rows(_dot(h1, w1) + (_dot(h1, w2b) + _dot(h2b, w1)))


def _outproj(ya_f, ya_b, hr, u_lat, x, g1, gain2, shift2, w_pa, w_pb, w_out, w_r, tm):
    b_, s_, d = x.shape
    w = ya_f.shape[2]
    row = lambda b, i: (b, i, 0)
    const2 = lambda b, i: (0, 0)
    bmap = lambda b, i: (b, 0, 0)
    ucol = lambda t: (lambda b, i: (b, i, t))
    single = pl.Buffered(1)
    return pl.pallas_call(
        _outproj_kernel,
        out_shape=(jax.ShapeDtypeStruct((b_, s_, d), F32),
                   jax.ShapeDtypeStruct((b_ * s_, d), F32),
                   jax.ShapeDtypeStruct((b_, s_, LANES), F32)),
        grid=(b_, s_ // tm),
        in_specs=[
            pl.BlockSpec((1, tm, w), row), pl.BlockSpec((1, tm, w), row), pl.BlockSpec((1, tm, w), row),
            pl.BlockSpec((1, tm, w), ucol(COL_UG)),
            pl.BlockSpec((1, tm, w), ucol(COL_GA)), pl.BlockSpec((1, tm, w), ucol(COL_GA + 1)),
            pl.BlockSpec((1, tm, w), ucol(COL_GB)), pl.BlockSpec((1, tm, w), ucol(COL_GB + 1)),
            pl.BlockSpec((1, tm, d), row),
            pl.BlockSpec((1, 1, d), bmap), pl.BlockSpec((1, 1, d), bmap), pl.BlockSpec((1, 1, d), bmap),
            pl.BlockSpec((w, d), const2, pipeline_mode=single),
            pl.BlockSpec((w, d), const2, pipeline_mode=single),
            pl.BlockSpec((d, d), const2, pipeline_mode=single),
            pl.BlockSpec((d, LANES), const2, pipeline_mode=single),
        ],
        out_specs=(pl.BlockSpec((1, tm, d), row),
                   pl.BlockSpec((tm, d), lambda b, i: (b * (s_ // tm) + i, 0)),
                   pl.BlockSpec((1, tm, LANES), row)),
        compiler_params=_cparams(("parallel", "parallel")),
        name="outproj",
    )(ya_f, ya_b, hr, u_lat, u_lat, u_lat, u_lat, u_lat, x, g1, gain2, shift2, w_pa, w_pb, w_out, w_r)


def _route_rows(lg):
    lane = lax.broadcasted_iota(jnp.int32, lg.shape, 1)
    neg = -jnp.inf
    big = jnp.int32(1 << 20)
    g_l = jnp.where(lane < N_GROUPS, lg, neg)
    g_max = jnp.max(g_l, axis=-1, keepdims=True)
    g_sel = jnp.min(jnp.where(g_l == g_max, lane, big), axis=-1, keepdims=True)
    p_g = 1.0 / jnp.sum(jnp.exp(g_l - g_max), axis=-1, keepdims=True)
    lo = N_GROUPS + g_sel * EXPERTS_PER_GROUP
    e_l = jnp.where((lane >= lo) & (lane < lo + EXPERTS_PER_GROUP), lg, neg)
    v1 = jnp.max(e_l, axis=-1, keepdims=True)
    i1 = jnp.min(jnp.where(e_l == v1, lane, big), axis=-1, keepdims=True)
    e_l2 = jnp.where(lane == i1, neg, e_l)
    v2 = jnp.max(e_l2, axis=-1, keepdims=True)
    i2 = jnp.min(jnp.where(e_l2 == v2, lane, big), axis=-1, keepdims=True)
    e2 = jnp.exp(v2 - v1)
    w1 = p_g / (1.0 + e2)
    w2 = p_g * e2 / (1.0 + e2)
    return jnp.where(lane == 0, (i1 - N_GROUPS).astype(F32),
           jnp.where(lane == 1, (i2 - N_GROUPS).astype(F32),
           jnp.where(lane == 2, w1, jnp.where(lane == 3, w2, 0.0))))


MOE_DMA_UNROLL = 32


MOE_W_DEPTH = 3
MOE_W_SPLIT = 2


def _moe_kernel(R, SUB, nf, ni, item_e, item_rows, tok_ref, tokn_ref, dst_ref, h2_hbm, w1_hbm, w3_hbm, w2_hbm,
                y_hbm, x_scr, xb_scr, acc_scr, y_scr, w1r, w3r, w2r, gsem, ssem, wsem):
    i = pl.program_id(0)
    j = pl.program_id(1)
    nsub_of = lambda r: (r + SUB - 1) // SUB
    rows = item_rows[i]
    nsub = nsub_of(rows)
    nsub_next = jnp.where(i + 1 < ni, nsub_of(item_rows[jnp.minimum(i + 1, ni - 1)]), 0)
    rows_prev = jnp.where(i > 0, item_rows[jnp.maximum(i - 1, 0)], 0)
    buf = i % 2
    groups = SUB // MOE_DMA_UNROLL

    d_model, tf = w1r.shape[1], w1r.shape[2]

    def weight_copies(c):
        it = c // nf
        jc = c % nf
        e = item_e[jnp.minimum(it, ni - 1)]
        slot = c % MOE_W_DEPTH
        f0 = pl.multiple_of(jc * tf, tf)
        out = []
        for h in range(MOE_W_SPLIT):
            ra, rb = d_model // MOE_W_SPLIT, tf // MOE_W_SPLIT
            out.append(pltpu.make_async_copy(w1_hbm.at[e, pl.ds(h * ra, ra), pl.ds(f0, tf)],
                                             w1r.at[slot, pl.ds(h * ra, ra), :], wsem.at[slot]))
            out.append(pltpu.make_async_copy(w3_hbm.at[e, pl.ds(h * ra, ra), pl.ds(f0, tf)],
                                             w3r.at[slot, pl.ds(h * ra, ra), :], wsem.at[slot]))
            out.append(pltpu.make_async_copy(w2_hbm.at[e, pl.ds(f0 + h * rb, rb), :],
                                             w2r.at[slot, pl.ds(h * rb, rb), :], wsem.at[slot]))
        return out

    def chunk_live(c):
        it = c // nf
        return (it < ni) & (item_rows[jnp.minimum(it, ni - 1)] > 0)

    cur = i * nf + j

    @pl.when((cur == 0) & chunk_live(0))
    def _():
        for cp in weight_copies(0):
            cp.start()

        @pl.when(chunk_live(1))
        def _():
            for cp in weight_copies(1):
                cp.start()

    @pl.when((rows > 0) & chunk_live(cur + 2))
    def _():
        for cp in weight_copies(cur + 2):
            cp.start()

    def tok_rows(ref, r):
        return ref.at[pl.ds(r, 1), :]

    def gather_group(tref, b, gi):
        base = pl.multiple_of(gi * MOE_DMA_UNROLL, MOE_DMA_UNROLL)
        for k in range(MOE_DMA_UNROLL):
            tok = tref[0, 0, base + k]
            pltpu.make_async_copy(tok_rows(h2_hbm, tok), tok_rows(x_scr.at[b], base + k),
                                  gsem.at[b]).start(priority=k % 2)

    def gather_wait_block(b):
        pltpu.make_async_copy(h2_hbm.at[pl.ds(0, SUB), :], x_scr.at[b, pl.ds(0, SUB), :], gsem.at[b]).wait()

    def scatter_copy(dref, r):
        dst = dref[0, 0, r]
        return pltpu.make_async_copy(tok_rows(y_scr, r), tok_rows(y_hbm, dst), ssem)

    def scatter_group(dref, gi):
        base = pl.multiple_of(gi * MOE_DMA_UNROLL, MOE_DMA_UNROLL)
        for k in range(MOE_DMA_UNROLL):
            scatter_copy(dref, base + k).start(priority=k % 2)

    def repeat(n, fn, lo=0):
        def body(q, c):
            fn(q)
            return c
        lax.fori_loop(lo, n, body, 0)

    def scatter_rows(dref, n):
        full = n // MOE_DMA_UNROLL
        repeat(full, lambda gi: scatter_group(dref, gi))
        repeat(n, lambda r: scatter_copy(dref, r).start(), lo=full * MOE_DMA_UNROLL)

    def scatter_wait_rows(n):
        u = MOE_DMA_UNROLL
        full = n // MOE_DMA_UNROLL
        repeat(full, lambda q: pltpu.make_async_copy(y_scr.at[pl.ds(0, u), :], y_hbm.at[pl.ds(0, u), :],
                                                      ssem).wait())
        repeat(n, lambda r: scatter_copy(dst_ref, 0).wait(), lo=full * MOE_DMA_UNROLL)

    @pl.when((i == 0) & (j == 0))
    def _():
        repeat(nsub * groups, lambda gi: gather_group(tok_ref, 0, gi))

    bps = -(-(R // SUB) // nf)
    @pl.when(j * bps < nsub_next)
    def _():
        repeat(jnp.minimum((j + 1) * bps, nsub_next) * groups, lambda gi: gather_group(tokn_ref, 1 - buf, gi),
               lo=j * bps * groups)

    @pl.when(j == 0)
    def _():
        repeat(nsub, lambda q: gather_wait_block(buf))

    @pl.when(j == nf - 1)
    def _():
        scatter_wait_rows(rows_prev)

    @pl.when(rows > 0)
    def _():
        for cp in weight_copies(cur):
            cp.wait()
        slot = cur % MOE_W_DEPTH

        @pl.when(j == 0)
        def _():
            def to_bf16(sb, c):
                r0 = pl.multiple_of(sb * SUB, SUB)
                xb_scr[pl.ds(r0, SUB), :] = x_scr[buf, pl.ds(r0, SUB), :].astype(BF16)
                return c
            lax.fori_loop(0, nsub, to_bf16, 0)

        def experts(m):
            def run():
                x = xb_scr[0:m, :]
                h1 = _dot(x, w1r[slot].astype(BF16))
                h3 = _dot(x, w3r[slot].astype(BF16))
                hh = (h1 * _sigmoid(h1) * h3).astype(BF16)
                contrib = _dot(hh, w2r[slot].astype(BF16))

                @pl.when(j == 0)
                def _():
                    acc_scr[0:m, :] = contrib

                @pl.when((j > 0) & (j < nf - 1))
                def _():
                    acc_scr[0:m, :] += contrib

                @pl.when(j == nf - 1)
                def _():
                    y_scr[0:m, :] = acc_scr[0:m, :] + contrib
            return run
        lax.switch(nsub - 1, [experts((q + 1) * SUB) for q in range(R // SUB)])

    @pl.when(j == nf - 1)
    def _():
        scatter_rows(dst_ref, rows)

        @pl.when(i == ni - 1)
        def _():
            scatter_wait_rows(rows)


def _moe(h2, item_e, item_rows, slot_tok, slot_dst, w1, w3, w2, n_out_rows):
    ne, d, f = w1.shape
    R, SUB, tf = MOE_ITEM_ROWS, MOE_SUB_ROWS, MOE_F_CHUNK
    ni = item_e.shape[0]
    nf = f // tf
    assert R % SUB == 0 and SUB % MOE_DMA_UNROLL == 0 and nf >= 2
    grid_spec = pltpu.PrefetchScalarGridSpec(
        num_scalar_prefetch=2,
        grid=(ni, nf),
        in_specs=[
            pl.BlockSpec((1, 1, R), lambda i, j, ie, ir: (i, 0, 0), memory_space=pltpu.SMEM),
            pl.BlockSpec((1, 1, R), lambda i, j, ie, ir: (jnp.minimum(i + 1, ni - 1), 0, 0),
                         memory_space=pltpu.SMEM),
            pl.BlockSpec((1, 1, R), lambda i, j, ie, ir: (i, 0, 0), memory_space=pltpu.SMEM),
            pl.BlockSpec(memory_space=pl.ANY),
            pl.BlockSpec(memory_space=pl.ANY),
            pl.BlockSpec(memory_space=pl.ANY),
            pl.BlockSpec(memory_space=pl.ANY),
        ],
        out_specs=pl.BlockSpec(memory_space=pl.ANY),
        scratch_shapes=[
            pltpu.VMEM((2, R, d), F32), pltpu.VMEM((R, d), BF16),
            pltpu.VMEM((R, d), F32), pltpu.VMEM((R, d), F32),
            pltpu.VMEM((MOE_W_DEPTH, d, tf), F32), pltpu.VMEM((MOE_W_DEPTH, d, tf), F32),
            pltpu.VMEM((MOE_W_DEPTH, tf, d), F32),
            pltpu.SemaphoreType.DMA((2,)), pltpu.SemaphoreType.DMA(()), pltpu.SemaphoreType.DMA((MOE_W_DEPTH,)),
        ],
    )
    return pl.pallas_call(
        functools.partial(_moe_kernel, R, SUB, nf, ni),
        out_shape=jax.ShapeDtypeStruct((n_out_rows, d), F32),
        grid_spec=grid_spec,
        compiler_params=_cparams(("arbitrary", "arbitrary")),
        name="moe_experts",
    )(item_e, item_rows, slot_tok, slot_tok, slot_dst, h2, w1, w3, w2)


def _moe_plan(expert, t_):
    R = MOE_ITEM_ROWS
    na = expert.shape[0]
    ni = N_EXPERTS + na // R
    onehot = (expert[:, None] == jnp.arange(N_EXPERTS, dtype=jnp.int32)[None, :]).astype(jnp.int32)
    csum = jnp.cumsum(onehot, axis=0)
    rank = jnp.sum(csum * onehot, axis=1) - 1
    counts = csum[-1]
    n_items = (counts + R - 1) // R
    item_end = jnp.cumsum(n_items)
    item_start = item_end - n_items
    slot = jnp.sum(item_start[None, :] * onehot, axis=1) * R + rank
    slot_a = jnp.full((ni * R,), -1, jnp.int32).at[slot].set(jnp.arange(na, dtype=jnp.int32))
    valid = slot_a >= 0
    slot_tok = jnp.where(valid, slot_a // TOP_K, 0)
    slot_dst = jnp.where(valid, (slot_a % TOP_K) * t_ + slot_a // TOP_K, 0)
    ii = jnp.arange(ni, dtype=jnp.int32)
    e_of = jnp.minimum(jnp.sum((item_end[None, :] <= ii[:, None]).astype(jnp.int32), axis=1), N_EXPERTS - 1)
    oh_e = (e_of[:, None] == jnp.arange(N_EXPERTS, dtype=jnp.int32)[None, :]).astype(jnp.int32)
    cnt_e = jnp.sum(oh_e * counts[None, :], axis=1)
    start_e = jnp.sum(oh_e * item_start[None, :], axis=1)
    rows = jnp.clip(cnt_e - (ii - start_e) * R, 0, R).astype(jnp.int32)
    live = ii < item_end[-1]
    rows = jnp.where(live, rows, 0)
    last_e = jnp.max(jnp.where(live, e_of, 0))
    e_of = jnp.where(live, e_of, last_e).astype(jnp.int32)
    return e_of, rows, slot_tok.reshape(ni, 1, R), slot_dst.reshape(ni, 1, R)


def _final_kernel(x1_ref, y0_ref, y1_ref, rt_ref, g2_ref, fg_ref, o_ref):
    rt = rt_ref[0]
    y = rt[:, TOP_K:TOP_K + 1] * y0_ref[...] + rt[:, TOP_K + 1:TOP_K + 2] * y1_ref[...]
    x = x1_ref[0] + g2_ref[0] * y
    ms = jnp.mean(x * x, axis=-1, keepdims=True)
    o_ref[0] = x * lax.rsqrt(ms + EPS) * fg_ref[...]


def _final(x1, ybuf, routed, g2, final_g, tm):
    b_, s_, d = x1.shape
    nt = s_ // tm
    return pl.pallas_call(
        _final_kernel,
        out_shape=jax.ShapeDtypeStruct((b_, s_, d), F32),
        grid=(b_, nt),
        in_specs=[
            pl.BlockSpec((1, tm, d), lambda b, i: (b, i, 0)),
            pl.BlockSpec((tm, d), lambda b, i: (b * nt + i, 0)),
            pl.BlockSpec((tm, d), lambda b, i: (b_ * nt + b * nt + i, 0)),
            pl.BlockSpec((1, tm, LANES), lambda b, i: (b, i, 0)),
            pl.BlockSpec((1, 1, d), lambda b, i: (b, 0, 0)),
            pl.BlockSpec((1, d), lambda b, i: (0, 0)),
        ],
        out_specs=pl.BlockSpec((1, tm, d), lambda b, i: (b, i, 0)),
        compiler_params=_cparams(("parallel", "parallel")),
        name="final_norm",
    )(x1, ybuf, ybuf, routed, g2, final_g)


def kernel(x, c, ctx, c_ctx, w_mod, b_mod, norm1_g, w_in, b_gates, mlstm_head_g, conv_w, conv_b, lru_wa, lru_ba,
           lru_wx, lru_bx, lru_lam, w_pa, w_pb, w_out, norm2_g, w_rg, w_re, w1, w3, w2, final_g):
    b_, s_, d = x.shape
    n_ctx = ctx.shape[1]
    rows = s_ // GRID_W
    assert w_mod.shape[0] == 1, "single layer"
    wm = d // 2
    nh = MLSTM_HEADS

    cc = jnp.zeros((8, d), F32).at[:b_].set(c).at[b_].set(c_ctx)
    mod = _modulation(cc, w_mod[0], b_mod[0][None, :])
    sh1, sc1, g1, sh2, sc2, g2 = [mod[:b_, i * d:(i + 1) * d][:, None, :] for i in range(6)]
    csh1, csc1 = mod[b_:b_ + 1, 0:d][:, None, :], mod[b_:b_ + 1, d:2 * d][:, None, :]

    assert wm == 1024 and 4 * nh == N_GATE_COLS
    w_t = jnp.transpose(w_in[0])
    bg_pad = jnp.zeros((1, LANES), F32).at[0, :4 * nh].set(b_gates[0])

    ng = norm1_g[0][None, None, :]
    u_lat, g_lat = _inproj(x, ng * (1.0 + sc1), sh1, w_t, bg_pad, tuple(range(N_COL_TILES)), min(1024, s_))
    u_ctx, g_ctx = _inproj(ctx.reshape(1, b_ * n_ctx, d), ng * (1.0 + csc1), csh1, w_t, bg_pad,
                           (COL_Q, COL_K, COL_V, COL_UX), b_ * n_ctx)
    u_ctx = u_ctx.reshape(b_, n_ctx, -1)
    g_ctx = g_ctx.reshape(b_, n_ctx, LANES)

    ya_f, ya_b = _mlstm(u_lat, u_ctx, g_lat, g_ctx, mlstm_head_g[0], MLSTM_CHUNK)

    ux_lat = u_lat[:, :, COL_UX * wm:(COL_UX + 1) * wm]
    ux_col = ux_lat.reshape(b_, rows, GRID_W, wm).transpose(0, 2, 1, 3).reshape(b_, s_, wm)
    ux_seq = jnp.concatenate([u_ctx[:, :, CTX_UX * wm:(CTX_UX + 1) * wm], ux_col], axis=1)
    wa, wx = lru_wa[0], lru_wx[0]
    w_cat = jnp.concatenate([wa[0], wx[0], wa[1], wx[1]], axis=-1).astype(BF16)
    cblk = wm // LRU_BLOCKS
    blk = lambda v: v.reshape(LRU_BLOCKS, cblk)
    b_cat = jnp.concatenate([blk(lru_ba[0, 0]), blk(lru_bx[0, 0]), blk(lru_ba[0, 1]), blk(lru_bx[0, 1])],
                            axis=-1).reshape(1, 4 * wm)
    h_col = _rglru(ux_seq, conv_w[0], conv_b[0][None, :], w_cat, b_cat, lru_lam[0], n_ctx)
    hr = h_col.reshape(b_, GRID_W, rows, wm).transpose(0, 2, 1, 3).reshape(b_, s_, wm)

    w_r = jnp.zeros((d, LANES), F32).at[:, :N_GROUPS].set(w_rg[0]).at[:, N_GROUPS:N_GROUPS + N_EXPERTS].set(w_re[0])
    n2 = norm2_g[0][None, None, :]
    x1, h2, routed = _outproj(ya_f, ya_b, hr, u_lat, x, g1, n2 * (1.0 + sc2), sh2,
                              w_pa[0].astype(BF16), w_pb[0].astype(BF16), w_out[0].astype(BF16), w_r, 256)

    t_ = b_ * s_
    expert = routed.reshape(t_, LANES)[:, :TOP_K].astype(jnp.int32).reshape(-1)
    item_e, item_rows, slot_tok, slot_dst = _moe_plan(expert, t_)
    ybuf = _moe(h2, item_e, item_rows, slot_tok, slot_dst, w1[0], w3[0], w2[0], TOP_K * t_)

    return _final(x1, ybuf, routed, g2, final_g[None, :], 256)
```

```python
import functools
import math

import jax
import jax.numpy as jnp
from jax import lax
from jax.experimental import pallas as pl
from jax.experimental.pallas import tpu as pltpu

F32 = jnp.float32
BF16 = jnp.bfloat16

EPS = 1e-6
GRID_W = 64
MLSTM_HEADS = 8
LRU_BLOCKS = 8
LRU_C = 8.0
N_GROUPS = 4
EXPERTS_PER_GROUP = 8
N_EXPERTS = N_GROUPS * EXPERTS_PER_GROUP
TOP_K = 2

VMEM_LIMIT_BYTES = 56 * 1024 * 1024
LANES = 128

MLSTM_CHUNK = 128
MOE_ITEM_ROWS = 768
MOE_SUB_ROWS = 128
MOE_F_CHUNK = 256


def _cparams(sem):
    return pltpu.CompilerParams(dimension_semantics=sem, vmem_limit_bytes=VMEM_LIMIT_BYTES)


def _sigmoid(x):
    return 0.5 * jnp.tanh(0.5 * x) + 0.5


def _dot(a, b):
    return jnp.dot(a, b, preferred_element_type=F32)


def _split3(x):
    x1 = x.astype(BF16)
    r1 = x - x1.astype(F32)
    x2 = r1.astype(BF16)
    x3 = (r1 - x2.astype(F32)).astype(BF16)
    return x1, x2, x3


def _mod_kernel(c_ref, w_ref, b_ref, o_ref):
    c = c_ref[...]
    s = (c * _sigmoid(c)).astype(BF16)
    o_ref[...] = _dot(s, w_ref[...].astype(BF16)) + b_ref[...]


def _modulation(cc, w_mod, b_mod):
    m, d = cc.shape
    n = w_mod.shape[1]
    tn = 1024
    return pl.pallas_call(
        _mod_kernel,
        out_shape=jax.ShapeDtypeStruct((m, n), F32),
        grid=(n // tn,),
        in_specs=[
            pl.BlockSpec((m, d), lambda j: (0, 0)),
            pl.BlockSpec((d, tn), lambda j: (0, j)),
            pl.BlockSpec((1, tn), lambda j: (0, j)),
        ],
        out_specs=pl.BlockSpec((m, tn), lambda j: (0, j)),
        compiler_params=_cparams(("parallel",)),
        name="modulation",
    )(cc, w_mod, b_mod)


N_PRE_TILES = 5
N_GATE_COLS = 32


def _dot_nt(a, b):
    return lax.dot_general(a, b, (((1,), (1,)), ((), ())), preferred_element_type=F32)


def _inproj_kernel(x_ref, gain_ref, shift_ref, wt_ref, wg_ref, bg_ref, u_ref, g_ref, h_scr):
    @pl.when(pl.program_id(2) == 0)
    def _():
        x = x_ref[0]
        ms = jnp.mean(x * x, axis=-1, keepdims=True)
        h = x * lax.rsqrt(ms + EPS) * gain_ref[0] + shift_ref[0]
        hb = h.astype(BF16)
        h_scr[...] = hb
        gates = _dot_nt(hb, wg_ref[...].astype(BF16))
        pad = jnp.zeros((gates.shape[0], LANES - N_GATE_COLS), F32)
        g_ref[0] = jnp.concatenate([gates, pad], axis=1) + bg_ref[...]

    u_ref[0] = _dot_nt(h_scr[...], wt_ref[...].astype(BF16)).astype(BF16)


def _inproj(x, gain, shift, w_t, b_gates, tiles, tm):
    b_, s_, d = x.shape
    tn = 1024
    nj = len(tiles)
    if tiles == tuple(range(nj)):
        tile_of = lambda j: j
    else:
        assert tiles == (COL_Q, COL_K, COL_V, COL_UX), tiles
        tile_of = lambda j: jnp.where(j == 3, COL_UX, j)
    row_of = lambda j: tile_of(j) * tn + jnp.where(tile_of(j) >= N_PRE_TILES, N_GATE_COLS, 0)
    per_batch_mod = gain.shape[0] > 1
    mmap = (lambda b, i, j: (b, 0, 0)) if per_batch_mod else (lambda b, i, j: (0, 0, 0))
    gate_blk = N_PRE_TILES * tn // N_GATE_COLS
    return pl.pallas_call(
        _inproj_kernel,
        out_shape=(jax.ShapeDtypeStruct((b_, s_, nj * tn), BF16),
                   jax.ShapeDtypeStruct((b_, s_, LANES), F32)),
        grid=(b_, s_ // tm, nj),
        in_specs=[
            pl.BlockSpec((1, tm, d), lambda b, i, j: (b, i, 0)),
            pl.BlockSpec((1, 1, d), mmap),
            pl.BlockSpec((1, 1, d), mmap),
            pl.BlockSpec((pl.Element(tn), pl.Element(d)), lambda b, i, j: (pl.multiple_of(row_of(j), 8), 0)),
            pl.BlockSpec((N_GATE_COLS, d), lambda b, i, j: (gate_blk, 0)),
            pl.BlockSpec((1, LANES), lambda b, i, j: (0, 0)),
        ],
        out_specs=(pl.BlockSpec((1, tm, tn), lambda b, i, j: (b, i, j)),
                   pl.BlockSpec((1, tm, LANES), lambda b, i, j: (b, i, 0))),
        scratch_shapes=[pltpu.VMEM((tm, d), BF16)],
        compiler_params=_cparams(("parallel", "parallel", "arbitrary")),
        name="inproj",
    )(x, gain, shift, w_t, w_t, b_gates)


COL_Q, COL_K, COL_V, COL_OF, COL_OB, COL_UX, COL_UG, COL_GA, COL_GB = 0, 1, 2, 3, 4, 5, 6, 7, 9
CTX_UX = 3
N_COL_TILES = 11


def _mlstm_kernel(nc, nl, L, dh,
                  qc_f, kc_f, vc_f, ql_f, kl_f, vl_f, gc_f, gl_f, o_f,
                  qc_b, kc_b, vc_b, ql_b, kl_b, vl_b, gc_b, gl_b, o_b,
                  hg_ref, tri_ref,
                  out_f, out_b, c_scr, m_scr):
    s = pl.program_id(1)
    nh = MLSTM_HEADS
    scale = dh ** -0.5

    @pl.when(s == 0)
    def _():
        c_scr[...] = jnp.zeros_like(c_scr)
        m_scr[...] = jnp.full_like(m_scr, -1e30)

    is_ctx = s < nc
    ones_col = (lax.broadcasted_iota(jnp.int32, (L, dh), 1) == 0).astype(BF16)
    row = lax.broadcasted_iota(jnp.int32, (L, L), 0)
    col = lax.broadcasted_iota(jnp.int32, (L, L), 1)

    for d, (qc, kc, vc, ql, kl, vl, gc, gl, o_ref, out_ref) in enumerate((
            (qc_f, kc_f, vc_f, ql_f, kl_f, vl_f, gc_f, gl_f, o_f, out_f),
            (qc_b, kc_b, vc_b, ql_b, kl_b, vl_b, gc_b, gl_b, o_b, out_b))):
        q = jnp.where(is_ctx, qc[0], ql[0])
        k = jnp.where(is_ctx, kc[0], kl[0])
        v = jnp.where(is_ctx, vc[0], vl[0])
        g = jnp.where(is_ctx, gc[0], gl[0])
        gt = g.T
        ls = jnp.minimum(g, 0.0) - jnp.log(1.0 + jnp.exp(-jnp.abs(g)))
        lst = jnp.minimum(gt, 0.0) - jnp.log(1.0 + jnp.exp(-jnp.abs(gt)))
        tri = tri_ref[d]
        mask = (col <= row) if d == 0 else (col >= row)
        l1, l2, l3 = _split3(ls)
        bcol_all = _dot(tri, l1) + _dot(tri, l2) + _dot(tri, l3)
        t1, t2, t3 = _split3(lst)
        trit = tri_ref[1 - d]
        brow_all = _dot(t1, trit) + _dot(t2, trit) + _dot(t3, trit)
        last = L - 1 if d == 0 else 0

        bal = pltpu.roll(bcol_all, LANES - 8, axis=1)
        cmax = g - bal
        trow = lax.broadcasted_iota(jnp.int32, (L, LANES), 0)
        sft = 1
        while sft < L:
            if d == 0:
                shifted = jnp.where(trow >= sft, pltpu.roll(cmax, sft, axis=0), -jnp.inf)
            else:
                shifted = jnp.where(trow < L - sft, pltpu.roll(cmax, L - sft, axis=0), -jnp.inf)
            cmax = jnp.maximum(cmax, shifted)
            sft *= 2
        m_row = m_scr[d][0:1, :]
        gg = bal + m_row
        mt = jnp.maximum(gg, bal + cmax)
        e_col = bal - mt
        w_inter = jnp.exp(gg - mt)
        enm = jnp.exp(-mt)
        tot = bal[last:last + 1, :]
        wlog = tot - bal + g
        m_new = jnp.maximum(tot + m_row, jnp.max(wlog, axis=0, keepdims=True))
        ws = jnp.exp(wlog - m_new) * scale
        decay = jnp.exp(tot + m_row - m_new)
        m_scr[d] = jnp.broadcast_to(m_new, m_scr.shape[1:])

        for h in range(nh):
            u = d * nh + h
            ci = 16 * d + h
            cf = 16 * d + 8 + h
            hs = slice(h * dh, (h + 1) * dh)
            qh, kh, vh = q[:, hs], k[:, hs], v[:, hs]
            vext = jnp.concatenate([vh, ones_col], axis=1)
            cr = gt[ci:ci + 1, :] - brow_all[cf:cf + 1, :] + math.log(scale)
            c_prev = c_scr[u]

            sqk = lax.dot_general(qh, kh, (((1,), (1,)), ((), ())), preferred_element_type=F32)
            w_intra = jnp.exp(jnp.where(mask, e_col[:, ci:ci + 1] + cr, -jnp.inf))
            a = (sqk * w_intra).astype(BF16)
            qs = (qh.astype(F32) * w_inter[:, ci:ci + 1]).astype(BF16)
            r = _dot(jnp.concatenate([a, qs], axis=1),
                     jnp.concatenate([vext, c_prev.astype(BF16)], axis=0))

            num = r[:, :dh]
            den = r[:, dh:dh + 1]
            rden = 1.0 / jnp.maximum(jnp.abs(den), enm[:, ci:ci + 1])
            ssq = jnp.sum(num * num, axis=-1, keepdims=True)
            sc = rden * lax.rsqrt(rden * rden * ssq * (1.0 / dh) + EPS)
            gate = _sigmoid(o_ref[0, :, hs].astype(F32))
            out_ref[0, :, hs] = (num * sc * hg_ref[h:h + 1, :] * gate).astype(BF16)

            kw = (kh.astype(F32) * ws[:, ci:ci + 1]).astype(BF16)
            upd = lax.dot_general(kw, vext, (((0,), (0,)), ((), ())), preferred_element_type=F32)
            c_scr[u] = decay[:, ci:ci + 1] * c_prev + upd


def _mlstm(u_lat, u_ctx, g_lat, g_ctx, head_g, L):
    b_, s_, _ = u_lat.shape
    n_ctx = u_ctx.shape[1]
    nh = MLSTM_HEADS
    w = 1024
    dh = w // nh
    nc, nl = n_ctx // L, s_ // L
    steps = nc + nl
    t0 = jnp.tril(jnp.ones((L, L), F32))
    tri = jnp.stack([t0, t0.T]).astype(BF16)

    def cidx(d):
        if d == 0:
            return lambda st: jnp.minimum(st, nc - 1), lambda st: jnp.maximum(st - nc, 0)
        return lambda st: jnp.maximum(nc - 1 - st, 0), lambda st: jnp.minimum(nl - 1 + nc - st, nl - 1)

    in_specs, args = [], []
    for d in range(2):
        fc, fl = cidx(d)
        for t in range(3):
            in_specs.append(pl.BlockSpec((1, L, w), lambda b, st, fc=fc, t=t: (b, fc(st), t)))
            args.append(u_ctx)
        for t in (COL_Q, COL_K, COL_V):
            in_specs.append(pl.BlockSpec((1, L, w), lambda b, st, fl=fl, t=t: (b, fl(st), t)))
            args.append(u_lat)
        in_specs.append(pl.BlockSpec((1, L, LANES), lambda b, st, fc=fc: (b, fc(st), 0)))
        args.append(g_ctx)
        in_specs.append(pl.BlockSpec((1, L, LANES), lambda b, st, fl=fl: (b, fl(st), 0)))
        args.append(g_lat)
        in_specs.append(pl.BlockSpec((1, L, w), lambda b, st, fl=fl, t=COL_OF + d: (b, fl(st), t)))
        args.append(u_lat)
    in_specs += [pl.BlockSpec((nh, dh), lambda b, st: (0, 0)),
                 pl.BlockSpec((2, L, L), lambda b, st: (0, 0, 0))]
    args += [head_g, tri]
    out_specs = tuple(pl.BlockSpec((1, L, w), lambda b, st, fl=cidx(d)[1]: (b, fl(st), 0)) for d in range(2))
    return pl.pallas_call(
        functools.partial(_mlstm_kernel, nc, nl, L, dh),
        out_shape=(jax.ShapeDtypeStruct((b_, s_, w), BF16),) * 2,
        grid=(b_, steps),
        in_specs=in_specs,
        out_specs=out_specs,
        scratch_shapes=[pltpu.VMEM((2 * nh, dh, 2 * dh), F32), pltpu.VMEM((2, 8, LANES), F32)],
        compiler_params=_cparams(("parallel", "arbitrary")),
        name="mlstm",
    )(*args)


SUBLANES = 8
LRU_BLK = SUBLANES * SUBLANES


def _rglru_kernel(n_ctx, n_tot, ux_ref, cw_ref, cb_ref, w_ref, b_ref, lam_ref, out_ref,
                  us, z_nat, a_f, b_f, a_b, b_b, h_f, h_b, o_nat):
    c = LANES
    g = SUBLANES
    n_lat = n_tot - n_ctx
    nq = n_tot // g
    nblk = n_tot // LRU_BLK
    nblk_ctx = n_ctx // LRU_BLK

    zeros_g = jnp.zeros((g, c), F32)
    us[0:g, :] = zeros_g
    us[g + n_ctx:2 * g + n_ctx, :] = zeros_g
    us[2 * g + n_tot:3 * g + n_tot, :] = zeros_g
    us[g:g + n_ctx, :] = ux_ref[0, 0:n_ctx, :].astype(F32)
    us[2 * g + n_ctx:2 * g + n_tot, :] = ux_ref[0, n_ctx:n_tot, :].astype(F32)

    def conv(base, n):
        acc = cb_ref[...] + cw_ref[2:3, :] * us[base:base + n, :]
        for j, off in ((0, -2), (1, -1), (3, 1)):
            acc = acc + cw_ref[j:j + 1, :] * us[base + off:base + off + n, :]
        return acc

    z_nat[0:n_ctx, :] = conv(g, n_ctx)
    z_nat[n_ctx:n_tot, :] = conv(2 * g + n_ctx, n_lat)

    z = jnp.concatenate([z_nat[pl.ds(k, nq, stride=g), :] for k in range(g)], axis=0)
    p = _dot(z.astype(BF16), w_ref[0]) + b_ref[...]
    for d, (a_scr, b_scr) in enumerate(((a_f, b_f), (a_b, b_b))):
        r = _sigmoid(p[:, (2 * d) * c:(2 * d + 1) * c])
        i = _sigmoid(p[:, (2 * d + 1) * c:(2 * d + 2) * c])
        lam = lam_ref[d:d + 1, :]
        softplus = jnp.maximum(-lam, 0.0) + jnp.log(1.0 + jnp.exp(-jnp.abs(lam)))
        a = jnp.exp2((-LRU_C * math.log2(math.e) * softplus) * r)
        a_scr[...] = a
        b_scr[...] = jnp.sqrt(1.0 - a * a) * i * z

    sub = lax.broadcasted_iota(jnp.int32, (g, c), 0)

    def block_scan(a_scr, b_scr, h_scr, blk, carry, reverse):
        r0 = pl.multiple_of(blk * g, g)
        order = range(g - 1, -1, -1) if reverse else range(g)
        hs, ps = {}, {}
        h = p_ = None
        for k in order:
            a = a_scr[pl.ds(k * nq + r0, g), :]
            b = b_scr[pl.ds(k * nq + r0, g), :]
            h = b if h is None else a * h + b
            p_ = a if p_ is None else a * p_
            hs[k], ps[k] = h, p_
        def shift(x, sft, fill):
            if reverse:
                return jnp.where(sub < g - sft, pltpu.roll(x, g - sft, 0), fill)
            return jnp.where(sub >= sft, pltpu.roll(x, sft, 0), fill)
        pe, he = shift(p_, 1, 1.0), shift(h, 1, 0.0)
        for sft in (1, 2, 4):
            he = pe * shift(he, sft, 0.0) + he
            pe = pe * shift(pe, sft, 1.0)
        cin = pe * carry + he
        for k in order:
            h_scr[pl.ds(k * nq + r0, g), :] = hs[k] + ps[k] * cin
        last = 0 if reverse else g - 1
        tot = p_ * cin + h
        return tot[last:last + 1, :]

    def body(it, carry):
        cf, cb = carry
        cf = block_scan(a_f, b_f, h_f, it, cf, False)
        jb = jnp.where(it < nblk_ctx, nblk_ctx - 1 - it, nblk - 1 + nblk_ctx - it)
        cb = block_scan(a_b, b_b, h_b, jb, cb, True)
        return cf, cb

    zero = jnp.zeros((1, c), F32)
    lax.fori_loop(0, nblk, body, (zero, zero), unroll=2)

    q_ctx = n_ctx // g
    for k in range(g):
        o_nat[pl.ds(k, n_lat // g, stride=g), :] = (h_f[k * nq + q_ctx:(k + 1) * nq, :]
                                                    + h_b[k * nq + q_ctx:(k + 1) * nq, :])
    out_ref[0] = o_nat[...].astype(BF16)


def _rglru(ux_seq, conv_w, conv_b, w_cat, b_cat, lam, n_ctx):
    b_, n_tot, w = ux_seq.shape
    nb = LRU_BLOCKS
    c = w // nb
    return pl.pallas_call(
        functools.partial(_rglru_kernel, n_ctx, n_tot),
        out_shape=jax.ShapeDtypeStruct((b_, n_tot - n_ctx, w), BF16),
        grid=(b_, nb),
        in_specs=[
            pl.BlockSpec((1, n_tot, c), lambda b, k: (b, 0, k)),
            pl.BlockSpec((4, c), lambda b, k: (0, k)),
            pl.BlockSpec((1, c), lambda b, k: (0, k)),
            pl.BlockSpec((1, c, 4 * c), lambda b, k: (k, 0, 0)),
            pl.BlockSpec((1, 4 * c), lambda b, k: (0, k)),
            pl.BlockSpec((2, c), lambda b, k: (0, k)),
        ],
        out_specs=pl.BlockSpec((1, n_tot - n_ctx, c), lambda b, k: (b, 0, k)),
        scratch_shapes=([pltpu.VMEM((n_tot + 3 * SUBLANES, c), F32)] + [pltpu.VMEM((n_tot, c), F32)] * 7
                        + [pltpu.VMEM((n_tot - n_ctx, c), F32)]),
        compiler_params=_cparams(("parallel", "parallel")),
        name="rglru",
    )(ux_seq, conv_w, conv_b, w_cat, b_cat, lam)


def _outproj_kernel(yaf_ref, yab_ref, hr_ref, ug_ref, ga0_ref, ga1_ref, gb0_ref, gb1_ref, x_ref,
                    g1_ref, gain_ref, shift_ref, wpa_ref, wpb_ref, wout_ref, wr_ref,
                    x1_ref, h2_ref, lg_ref):
    ya = (yaf_ref[0].astype(F32) + yab_ref[0].astype(F32)).astype(BF16)
    ug = ug_ref[0].astype(F32)
    gelu = 0.5 * ug * (1.0 + jnp.tanh(0.7978845608028654 * (ug + 0.044715 * ug * ug * ug)))
    yb = (hr_ref[0].astype(F32) * gelu).astype(BF16)
    pa = _dot(ya, wpa_ref[...])
    pb = _dot(yb, wpb_ref[...])
    ga = jnp.concatenate([ga0_ref[0], ga1_ref[0]], axis=1).astype(F32)
    gb = jnp.concatenate([gb0_ref[0], gb1_ref[0]], axis=1).astype(F32)
    mix = (_sigmoid(ga) * pa + _sigmoid(gb) * pb).astype(BF16)
    x1 = x_ref[0] + g1_ref[0] * _dot(mix, wout_ref[...])
    x1_ref[0] = x1
    ms = jnp.mean(x1 * x1, axis=-1, keepdims=True)
    h2 = x1 * lax.rsqrt(ms + EPS) * gain_ref[0] + shift_ref[0]
    h2_ref[...] = h2
    wr = wr_ref[...]
    h1, h2b, _ = _split3(h2)
    w1, w2b, _ = _split3(wr)
    lg_ref[0] = _route_rows(_dot(h1, w1) + (_dot(h1, w2b) + _dot(h2b, w1)))


def _outproj(ya_f, ya_b, hr, u_lat, x, g1, gain2, shift2, w_pa, w_pb, w_out, w_r, tm):
    b_, s_, d = x.shape
    w = ya_f.shape[2]
    row = lambda b, i: (b, i, 0)
    const2 = lambda b, i: (0, 0)
    bmap = lambda b, i: (b, 0, 0)
    ucol = lambda t: (lambda b, i: (b, i, t))
    single = pl.Buffered(1)
    return pl.pallas_call(
        _outproj_kernel,
        out_shape=(jax.ShapeDtypeStruct((b_, s_, d), F32),
                   jax.ShapeDtypeStruct((b_ * s_, d), F32),
                   jax.ShapeDtypeStruct((b_, s_, LANES), F32)),
        grid=(b_, s_ // tm),
        in_specs=[
            pl.BlockSpec((1, tm, w), row), pl.BlockSpec((1, tm, w), row), pl.BlockSpec((1, tm, w), row),
            pl.BlockSpec((1, tm, w), ucol(COL_UG)),
            pl.BlockSpec((1, tm, w), ucol(COL_GA)), pl.BlockSpec((1, tm, w), ucol(COL_GA + 1)),
            pl.BlockSpec((1, tm, w), ucol(COL_GB)), pl.BlockSpec((1, tm, w), ucol(COL_GB + 1)),
            pl.BlockSpec((1, tm, d), row),
            pl.BlockSpec((1, 1, d), bmap), pl.BlockSpec((1, 1, d), bmap), pl.BlockSpec((1, 1, d), bmap),
            pl.BlockSpec((w, d), const2, pipeline_mode=single),
            pl.BlockSpec((w, d), const2, pipeline_mode=single),
            pl.BlockSpec((d, d), const2, pipeline_mode=single),
            pl.BlockSpec((d, LANES), const2, pipeline_mode=single),
        ],
        out_specs=(pl.BlockSpec((1, tm, d), row),
                   pl.BlockSpec((tm, d), lambda b, i: (b * (s_ // tm) + i, 0)),
                   pl.BlockSpec((1, tm, LANES), row)),
        compiler_params=_cparams(("parallel", "parallel")),
        name="outproj",
    )(ya_f, ya_b, hr, u_lat, u_lat, u_lat, u_lat, u_lat, x, g1, gain2, shift2, w_pa, w_pb, w_out, w_r)


def _route_rows(lg):
    lane = lax.broadcasted_iota(jnp.int32, lg.shape, 1)
    neg = -jnp.inf
    big = jnp.int32(1 << 20)
    g_l = jnp.where(lane < N_GROUPS, lg, neg)
    g_max = jnp.max(g_l, axis=-1, keepdims=True)
    g_sel = jnp.min(jnp.where(g_l == g_max, lane, big), axis=-1, keepdims=True)
    p_g = 1.0 / jnp.sum(jnp.exp(g_l - g_max), axis=-1, keepdims=True)
    lo = N_GROUPS + g_sel * EXPERTS_PER_GROUP
    e_l = jnp.where((lane >= lo) & (lane < lo + EXPERTS_PER_GROUP), lg, neg)
    v1 = jnp.max(e_l, axis=-1, keepdims=True)
    i1 = jnp.min(jnp.where(e_l == v1, lane, big), axis=-1, keepdims=True)
    e_l2 = jnp.where(lane == i1, neg, e_l)
    v2 = jnp.max(e_l2, axis=-1, keepdims=True)
    i2 = jnp.min(jnp.where(e_l2 == v2, lane, big), axis=-1, keepdims=True)
    e2 = jnp.exp(v2 - v1)
    w1 = p_g / (1.0 + e2)
    w2 = p_g * e2 / (1.0 + e2)
    return jnp.where(lane == 0, (i1 - N_GROUPS).astype(F32),
           jnp.where(lane == 1, (i2 - N_GROUPS).astype(F32),
           jnp.where(lane == 2, w1, jnp.where(lane == 3, w2, 0.0))))


MOE_DMA_UNROLL = 32


MOE_W_DEPTH = 3
MOE_W_SPLIT = 2


def _moe_kernel(R, SUB, nf, ni, item_e, item_rows, tok_ref, tokn_ref, dst_ref, dstp_ref, h2_hbm,
                w1_hbm, w3_hbm, w2_hbm,
                y_hbm, x_scr, xb_scr, acc_scr, y_scr, w1r, w3r, w2r, gsem, ssem, wsem):
    i = pl.program_id(0)
    j = pl.program_id(1)
    nsub_of = lambda r: (r + SUB - 1) // SUB
    rows = item_rows[i]
    nsub = nsub_of(rows)
    nsub_next = jnp.where(i + 1 < ni, nsub_of(item_rows[jnp.minimum(i + 1, ni - 1)]), 0)
    rows_prev = jnp.where(i > 0, item_rows[jnp.maximum(i - 1, 0)], 0)
    buf = i % 2
    groups = SUB // MOE_DMA_UNROLL

    d_model, tf = w1r.shape[1], w1r.shape[2]

    def weight_copies(c):
        it = c // nf
        jc = c % nf
        e = item_e[jnp.minimum(it, ni - 1)]
        slot = c % MOE_W_DEPTH
        f0 = pl.multiple_of(jc * tf, tf)
        out = []
        for h in range(MOE_W_SPLIT):
            ra, rb = d_model // MOE_W_SPLIT, tf // MOE_W_SPLIT
            out.append(pltpu.make_async_copy(w1_hbm.at[e, pl.ds(h * ra, ra), pl.ds(f0, tf)],
                                             w1r.at[slot, pl.ds(h * ra, ra), :], wsem.at[slot]))
            out.append(pltpu.make_async_copy(w3_hbm.at[e, pl.ds(h * ra, ra), pl.ds(f0, tf)],
                                             w3r.at[slot, pl.ds(h * ra, ra), :], wsem.at[slot]))
            out.append(pltpu.make_async_copy(w2_hbm.at[e, pl.ds(f0 + h * rb, rb), :],
                                             w2r.at[slot, pl.ds(h * rb, rb), :], wsem.at[slot]))
        return out

    def chunk_live(c):
        it = c // nf
        return (it < ni) & (item_rows[jnp.minimum(it, ni - 1)] > 0)

    cur = i * nf + j

    @pl.when((cur == 0) & chunk_live(0))
    def _():
        for cp in weight_copies(0):
            cp.start()

        @pl.when(chunk_live(1))
        def _():
            for cp in weight_copies(1):
                cp.start()

    @pl.when((rows > 0) & chunk_live(cur + 2))
    def _():
        for cp in weight_copies(cur + 2):
            cp.start()

    def tok_rows(ref, r):
        return ref.at[pl.ds(r, 1), :]

    def gather_group(tref, b, gi):
        base = pl.multiple_of(gi * MOE_DMA_UNROLL, MOE_DMA_UNROLL)
        for k in range(MOE_DMA_UNROLL):
            tok = tref[0, 0, base + k]
            pltpu.make_async_copy(tok_rows(h2_hbm, tok), tok_rows(x_scr.at[b], base + k),
                                  gsem.at[b]).start(priority=k % 2)

    def gather_wait_block(b):
        pltpu.make_async_copy(h2_hbm.at[pl.ds(0, SUB), :], x_scr.at[b, pl.ds(0, SUB), :], gsem.at[b]).wait()

    def scatter_copy(dref, r):
        dst = dref[0, 0, r]
        return pltpu.make_async_copy(tok_rows(y_scr, r), tok_rows(y_hbm, dst), ssem)

    def scatter_group(dref, gi):
        base = pl.multiple_of(gi * MOE_DMA_UNROLL, MOE_DMA_UNROLL)
        for k in range(MOE_DMA_UNROLL):
            scatter_copy(dref, base + k).start(priority=k % 2)

    def repeat(n, fn, lo=0):
        def body(q, c):
            fn(q)
            return c
        lax.fori_loop(lo, n, body, 0)

    def scatter_rows(dref, n):
        full = n // MOE_DMA_UNROLL
        repeat(full, lambda gi: scatter_group(dref, gi))
        repeat(n, lambda r: scatter_copy(dref, r).start(), lo=full * MOE_DMA_UNROLL)

    def scatter_wait_rows(n):
        u = MOE_DMA_UNROLL
        full = n // MOE_DMA_UNROLL
        repeat(full, lambda q: pltpu.make_async_copy(y_scr.at[pl.ds(0, u), :], y_hbm.at[pl.ds(0, u), :],
                                                      ssem).wait())
        repeat(n, lambda r: scatter_copy(dst_ref, 0).wait(), lo=full * MOE_DMA_UNROLL)

    @pl.when((i == 0) & (j == 0))
    def _():
        repeat(nsub * groups, lambda gi: gather_group(tok_ref, 0, gi))

    bps = -(-(R // SUB) // nf)
    @pl.when(j * bps < nsub_next)
    def _():
        repeat(jnp.minimum((j + 1) * bps, nsub_next) * groups, lambda gi: gather_group(tokn_ref, 1 - buf, gi),
               lo=j * bps * groups)

    @pl.when(j == 0)
    def _():
        repeat(nsub, lambda q: gather_wait_block(buf))

    s_full = rows_prev // MOE_DMA_UNROLL
    s_share = (s_full + (nf - 2)) // (nf - 1)

    @pl.when(j < nf - 1)
    def _():
        repeat(jnp.minimum((j + 1) * s_share, s_full), lambda gi: scatter_group(dstp_ref, gi), lo=j * s_share)

    @pl.when(j == 0)
    def _():
        repeat(rows_prev, lambda r: scatter_copy(dstp_ref, r).start(), lo=s_full * MOE_DMA_UNROLL)

    @pl.when(j == nf - 1)
    def _():
        scatter_wait_rows(rows_prev)

    @pl.when(rows > 0)
    def _():
        for cp in weight_copies(cur):
            cp.wait()
        slot = cur % MOE_W_DEPTH

        @pl.when(j == 0)
        def _():
            def to_bf16(sb, c):
                r0 = pl.multiple_of(sb * SUB, SUB)
                xb_scr[pl.ds(r0, SUB), :] = x_scr[buf, pl.ds(r0, SUB), :].astype(BF16)
                return c
            lax.fori_loop(0, nsub, to_bf16, 0)

        def experts(m):
            def run():
                x = xb_scr[0:m, :]
                h1 = _dot(x, w1r[slot].astype(BF16))
                h3 = _dot(x, w3r[slot].astype(BF16))
                hh = (h1 * _sigmoid(h1) * h3).astype(BF16)
                contrib = _dot(hh, w2r[slot].astype(BF16))

                @pl.when(j == 0)
                def _():
                    acc_scr[0:m, :] = contrib

                @pl.when((j > 0) & (j < nf - 1))
                def _():
                    acc_scr[0:m, :] += contrib

                @pl.when(j == nf - 1)
                def _():
                    y_scr[0:m, :] = acc_scr[0:m, :] + contrib
            return run
        lax.switch(nsub - 1, [experts((q + 1) * SUB) for q in range(R // SUB)])

    @pl.when((j == nf - 1) & (i == ni - 1))
    def _():
        scatter_rows(dst_ref, rows)
        scatter_wait_rows(rows)


def _moe(h2, item_e, item_rows, slot_tok, slot_dst, w1, w3, w2, n_out_rows):
    ne, d, f = w1.shape
    R, SUB, tf = MOE_ITEM_ROWS, MOE_SUB_ROWS, MOE_F_CHUNK
    ni = item_e.shape[0]
    nf = f // tf
    assert R % SUB == 0 and SUB % MOE_DMA_UNROLL == 0 and nf >= 2
    grid_spec = pltpu.PrefetchScalarGridSpec(
        num_scalar_prefetch=2,
        grid=(ni, nf),
        in_specs=[
            pl.BlockSpec((1, 1, R), lambda i, j, ie, ir: (i, 0, 0), memory_space=pltpu.SMEM),
            pl.BlockSpec((1, 1, R), lambda i, j, ie, ir: (jnp.minimum(i + 1, ni - 1), 0, 0),
                         memory_space=pltpu.SMEM),
            pl.BlockSpec((1, 1, R), lambda i, j, ie, ir: (i, 0, 0), memory_space=pltpu.SMEM),
            pl.BlockSpec((1, 1, R), lambda i, j, ie, ir: (jnp.maximum(i - 1, 0), 0, 0), memory_space=pltpu.SMEM),
            pl.BlockSpec(memory_space=pl.ANY),
            pl.BlockSpec(memory_space=pl.ANY),
            pl.BlockSpec(memory_space=pl.ANY),
            pl.BlockSpec(memory_space=pl.ANY),
        ],
        out_specs=pl.BlockSpec(memory_space=pl.ANY),
        scratch_shapes=[
            pltpu.VMEM((2, R, d), F32), pltpu.VMEM((R, d), BF16),
            pltpu.VMEM((R, d), F32), pltpu.VMEM((R, d), F32),
            pltpu.VMEM((MOE_W_DEPTH, d, tf), F32), pltpu.VMEM((MOE_W_DEPTH, d, tf), F32),
            pltpu.VMEM((MOE_W_DEPTH, tf, d), F32),
            pltpu.SemaphoreType.DMA((2,)), pltpu.SemaphoreType.DMA(()), pltpu.SemaphoreType.DMA((MOE_W_DEPTH,)),
        ],
    )
    return pl.pallas_call(
        functools.partial(_moe_kernel, R, SUB, nf, ni),
        out_shape=jax.ShapeDtypeStruct((n_out_rows, d), F32),
        grid_spec=grid_spec,
        compiler_params=_cparams(("arbitrary", "arbitrary")),
        name="moe_experts",
    )(item_e, item_rows, slot_tok, slot_tok, slot_dst, slot_dst, h2, w1, w3, w2)


def _moe_plan(expert, t_):
    R = MOE_ITEM_ROWS
    na = expert.shape[0]
    ni = N_EXPERTS + na // R
    onehot = (expert[:, None] == jnp.arange(N_EXPERTS, dtype=jnp.int32)[None, :]).astype(jnp.int32)
    csum = jnp.cumsum(onehot, axis=0)
    rank = jnp.sum(csum * onehot, axis=1) - 1
    counts = csum[-1]
    n_items = (counts + R - 1) // R
    item_end = jnp.cumsum(n_items)
    item_start = item_end - n_items
    slot = jnp.sum(item_start[None, :] * onehot, axis=1) * R + rank
    slot_a = jnp.full((ni * R,), -1, jnp.int32).at[slot].set(jnp.arange(na, dtype=jnp.int32))
    valid = slot_a >= 0
    slot_tok = jnp.where(valid, slot_a // TOP_K, 0)
    slot_dst = jnp.where(valid, (slot_a % TOP_K) * t_ + slot_a // TOP_K, 0)
    ii = jnp.arange(ni, dtype=jnp.int32)
    e_of = jnp.minimum(jnp.sum((item_end[None, :] <= ii[:, None]).astype(jnp.int32), axis=1), N_EXPERTS - 1)
    oh_e = (e_of[:, None] == jnp.arange(N_EXPERTS, dtype=jnp.int32)[None, :]).astype(jnp.int32)
    cnt_e = jnp.sum(oh_e * counts[None, :], axis=1)
    start_e = jnp.sum(oh_e * item_start[None, :], axis=1)
    rows = jnp.clip(cnt_e - (ii - start_e) * R, 0, R).astype(jnp.int32)
    live = ii < item_end[-1]
    rows = jnp.where(live, rows, 0)
    last_e = jnp.max(jnp.where(live, e_of, 0))
    e_of = jnp.where(live, e_of, last_e).astype(jnp.int32)
    return e_of, rows, slot_tok.reshape(ni, 1, R), slot_dst.reshape(ni, 1, R)


def _final_kernel(x1_ref, y0_ref, y1_ref, rt_ref, g2_ref, fg_ref, o_ref):
    rt = rt_ref[0]
    y = rt[:, TOP_K:TOP_K + 1] * y0_ref[...] + rt[:, TOP_K + 1:TOP_K + 2] * y1_ref[...]
    x = x1_ref[0] + g2_ref[0] * y
    ms = jnp.mean(x * x, axis=-1, keepdims=True)
    o_ref[0] = x * lax.rsqrt(ms + EPS) * fg_ref[...]


def _final(x1, ybuf, routed, g2, final_g, tm):
    b_, s_, d = x1.shape
    nt = s_ // tm
    return pl.pallas_call(
        _final_kernel,
        out_shape=jax.ShapeDtypeStruct((b_, s_, d), F32),
        grid=(b_, nt),
        in_specs=[
            pl.BlockSpec((1, tm, d), lambda b, i: (b, i, 0)),
            pl.BlockSpec((tm, d), lambda b, i: (b * nt + i, 0)),
            pl.BlockSpec((tm, d), lambda b, i: (b_ * nt + b * nt + i, 0)),
            pl.BlockSpec((1, tm, LANES), lambda b, i: (b, i, 0)),
            pl.BlockSpec((1, 1, d), lambda b, i: (b, 0, 0)),
            pl.BlockSpec((1, d), lambda b, i: (0, 0)),
        ],
        out_specs=pl.BlockSpec((1, tm, d), lambda b, i: (b, i, 0)),
        compiler_params=_cparams(("parallel", "parallel")),
        name="final_norm",
    )(x1, ybuf, ybuf, routed, g2, final_g)


def kernel(x, c, ctx, c_ctx, w_mod, b_mod, norm1_g, w_in, b_gates, mlstm_head_g, conv_w, conv_b, lru_wa, lru_ba,
           lru_wx, lru_bx, lru_lam, w_pa, w_pb, w_out, norm2_g, w_rg, w_re, w1, w3, w2, final_g):
    b_, s_, d = x.shape
    n_ctx = ctx.shape[1]
    rows = s_ // GRID_W
    assert w_mod.shape[0] == 1, "single layer"
    wm = d // 2
    nh = MLSTM_HEADS

    cc = jnp.zeros((8, d), F32).at[:b_].set(c).at[b_].set(c_ctx)
    mod = _modulation(cc, w_mod[0], b_mod[0][None, :])
    sh1, sc1, g1, sh2, sc2, g2 = [mod[:b_, i * d:(i + 1) * d][:, None, :] for i in range(6)]
    csh1, csc1 = mod[b_:b_ + 1, 0:d][:, None, :], mod[b_:b_ + 1, d:2 * d][:, None, :]

    assert wm == 1024 and 4 * nh == N_GATE_COLS
    w_t = jnp.transpose(w_in[0])
    bg_pad = jnp.zeros((1, LANES), F32).at[0, :4 * nh].set(b_gates[0])

    ng = norm1_g[0][None, None, :]
    u_lat, g_lat = _inproj(x, ng * (1.0 + sc1), sh1, w_t, bg_pad, tuple(range(N_COL_TILES)), min(1024, s_))
    u_ctx, g_ctx = _inproj(ctx.reshape(1, b_ * n_ctx, d), ng * (1.0 + csc1), csh1, w_t, bg_pad,
                           (COL_Q, COL_K, COL_V, COL_UX), b_ * n_ctx)
    u_ctx = u_ctx.reshape(b_, n_ctx, -1)
    g_ctx = g_ctx.reshape(b_, n_ctx, LANES)

    ya_f, ya_b = _mlstm(u_lat, u_ctx, g_lat, g_ctx, mlstm_head_g[0], MLSTM_CHUNK)

    ux_lat = u_lat[:, :, COL_UX * wm:(COL_UX + 1) * wm]
    ux_col = ux_lat.reshape(b_, rows, GRID_W, wm).transpose(0, 2, 1, 3).reshape(b_, s_, wm)
    ux_seq = jnp.concatenate([u_ctx[:, :, CTX_UX * wm:(CTX_UX + 1) * wm], ux_col], axis=1)
    wa, wx = lru_wa[0], lru_wx[0]
    w_cat = jnp.concatenate([wa[0], wx[0], wa[1], wx[1]], axis=-1).astype(BF16)
    cblk = wm // LRU_BLOCKS
    blk = lambda v: v.reshape(LRU_BLOCKS, cblk)
    b_cat = jnp.concatenate([blk(lru_ba[0, 0]), blk(lru_bx[0, 0]), blk(lru_ba[0, 1]), blk(lru_bx[0, 1])],
                            axis=-1).reshape(1, 4 * wm)
    h_col = _rglru(ux_seq, conv_w[0], conv_b[0][None, :], w_cat, b_cat, lru_lam[0], n_ctx)
    hr = h_col.reshape(b_, GRID_W, rows, wm).transpose(0, 2, 1, 3).reshape(b_, s_, wm)

    w_r = jnp.zeros((d, LANES), F32).at[:, :N_GROUPS].set(w_rg[0]).at[:, N_GROUPS:N_GROUPS + N_EXPERTS].set(w_re[0])
    n2 = norm2_g[0][None, None, :]
    x1, h2, routed = _outproj(ya_f, ya_b, hr, u_lat, x, g1, n2 * (1.0 + sc2), sh2,
                              w_pa[0].astype(BF16), w_pb[0].astype(BF16), w_out[0].astype(BF16), w_r, 256)

    t_ = b_ * s_
    expert = routed.reshape(t_, LANES)[:, :TOP_K].astype(jnp.int32).reshape(-1)
    item_e, item_rows, slot_tok, slot_dst = _moe_plan(expert, t_)
    ybuf = _moe(h2, item_e, item_rows, slot_tok, slot_dst, w1[0], w3[0], w2[0], TOP_K * t_)

    return _final(x1, ybuf, routed, g2, final_g[None, :], 256)
```

```python
import functools
import math

import jax
import jax.numpy as jnp
from jax import lax
from jax.experimental import pallas as pl
from jax.experimental.pallas import tpu as pltpu

F32 = jnp.float32
BF16 = jnp.bfloat16

EPS = 1e-6
GRID_W = 64
MLSTM_HEADS = 8
LRU_BLOCKS = 8
LRU_C = 8.0
N_GROUPS = 4
EXPERTS_PER_GROUP = 8
N_EXPERTS = N_GROUPS * EXPERTS_PER_GROUP
TOP_K = 2
TOP_K_SHIFT = 1
assert 1 << TOP_K_SHIFT == TOP_K

VMEM_LIMIT_BYTES = 56 * 1024 * 1024
LANES = 128

MLSTM_CHUNK = 128
MOE_ITEM_ROWS = 768
MOE_SUB_ROWS = 128
MOE_F_CHUNK = 256


def _cparams(sem):
    return pltpu.CompilerParams(dimension_semantics=sem, vmem_limit_bytes=VMEM_LIMIT_BYTES)


def _sigmoid(x):
    return 0.5 * jnp.tanh(0.5 * x) + 0.5


def _dot(a, b):
    return jnp.dot(a, b, preferred_element_type=F32)


def _split3(x):
    x1 = x.astype(BF16)
    r1 = x - x1.astype(F32)
    x2 = r1.astype(BF16)
    x3 = (r1 - x2.astype(F32)).astype(BF16)
    return x1, x2, x3


def _mod_kernel(c_ref, w_ref, b_ref, o_ref):
    c = c_ref[...]
    s = (c * _sigmoid(c)).astype(BF16)
    o_ref[...] = _dot(s, w_ref[...].astype(BF16)) + b_ref[...]


def _modulation(cc, w_mod, b_mod):
    m, d = cc.shape
    n = w_mod.shape[1]
    tn = 1024
    return pl.pallas_call(
        _mod_kernel,
        out_shape=jax.ShapeDtypeStruct((m, n), F32),
        grid=(n // tn,),
        in_specs=[
            pl.BlockSpec((m, d), lambda j: (0, 0)),
            pl.BlockSpec((d, tn), lambda j: (0, j)),
            pl.BlockSpec((1, tn), lambda j: (0, j)),
        ],
        out_specs=pl.BlockSpec((m, tn), lambda j: (0, j)),
        compiler_params=_cparams(("parallel",)),
        name="modulation",
    )(cc, w_mod, b_mod)


N_PRE_TILES = 5
N_GATE_COLS = 32


def _dot_nt(a, b):
    return lax.dot_general(a, b, (((1,), (1,)), ((), ())), preferred_element_type=F32)


def _inproj_kernel(x_ref, gain_ref, shift_ref, wt_ref, wg_ref, bg_ref, u_ref, g_ref, h_scr):
    @pl.when(pl.program_id(2) == 0)
    def _():
        x = x_ref[0]
        ms = jnp.mean(x * x, axis=-1, keepdims=True)
        h = x * lax.rsqrt(ms + EPS) * gain_ref[0] + shift_ref[0]
        hb = h.astype(BF16)
        h_scr[...] = hb
        gates = _dot_nt(hb, wg_ref[...].astype(BF16))
        pad = jnp.zeros((gates.shape[0], LANES - N_GATE_COLS), F32)
        g_ref[0] = jnp.concatenate([gates, pad], axis=1) + bg_ref[...]

    u_ref[0] = _dot_nt(h_scr[...], wt_ref[...].astype(BF16)).astype(BF16)


def _inproj(x, gain, shift, w_t, b_gates, tiles, tm):
    b_, s_, d = x.shape
    tn = 1024
    nj = len(tiles)
    if tiles == tuple(range(nj)):
        tile_of = lambda j: j
    else:
        assert tiles == (COL_Q, COL_K, COL_V, COL_UX), tiles
        tile_of = lambda j: jnp.where(j == 3, COL_UX, j)
    row_of = lambda j: tile_of(j) * tn + jnp.where(tile_of(j) >= N_PRE_TILES, N_GATE_COLS, 0)
    per_batch_mod = gain.shape[0] > 1
    mmap = (lambda b, i, j: (b, 0, 0)) if per_batch_mod else (lambda b, i, j: (0, 0, 0))
    gate_blk = N_PRE_TILES * tn // N_GATE_COLS
    return pl.pallas_call(
        _inproj_kernel,
        out_shape=(jax.ShapeDtypeStruct((b_, s_, nj * tn), BF16),
                   jax.ShapeDtypeStruct((b_, s_, LANES), F32)),
        grid=(b_, s_ // tm, nj),
        in_specs=[
            pl.BlockSpec((1, tm, d), lambda b, i, j: (b, i, 0)),
            pl.BlockSpec((1, 1, d), mmap),
            pl.BlockSpec((1, 1, d), mmap),
            pl.BlockSpec((pl.Element(tn), pl.Element(d)), lambda b, i, j: (pl.multiple_of(row_of(j), 8), 0)),
            pl.BlockSpec((N_GATE_COLS, d), lambda b, i, j: (gate_blk, 0)),
            pl.BlockSpec((1, LANES), lambda b, i, j: (0, 0)),
        ],
        out_specs=(pl.BlockSpec((1, tm, tn), lambda b, i, j: (b, i, j)),
                   pl.BlockSpec((1, tm, LANES), lambda b, i, j: (b, i, 0))),
        scratch_shapes=[pltpu.VMEM((tm, d), BF16)],
        compiler_params=_cparams(("parallel", "parallel", "arbitrary")),
        name="inproj",
    )(x, gain, shift, w_t, w_t, b_gates)


COL_Q, COL_K, COL_V, COL_OF, COL_OB, COL_UX, COL_UG, COL_GA, COL_GB = 0, 1, 2, 3, 4, 5, 6, 7, 9
CTX_UX = 3
N_COL_TILES = 11


def _mlstm_kernel(nc, nl, L, dh,
                  qc_f, kc_f, vc_f, ql_f, kl_f, vl_f, gc_f, gl_f, o_f,
                  qc_b, kc_b, vc_b, ql_b, kl_b, vl_b, gc_b, gl_b, o_b,
                  hg_ref, tri_ref,
                  out_f, out_b, c_scr, m_scr):
    s = pl.program_id(1)
    nh = MLSTM_HEADS
    scale = dh ** -0.5

    @pl.when(s == 0)
    def _():
        c_scr[...] = jnp.zeros_like(c_scr)
        m_scr[...] = jnp.full_like(m_scr, -1e30)

    is_ctx = s < nc
    ones_col = (lax.broadcasted_iota(jnp.int32, (L, dh), 1) == 0).astype(BF16)
    row = lax.broadcasted_iota(jnp.int32, (L, L), 0)
    col = lax.broadcasted_iota(jnp.int32, (L, L), 1)

    for d, (qc, kc, vc, ql, kl, vl, gc, gl, o_ref, out_ref) in enumerate((
            (qc_f, kc_f, vc_f, ql_f, kl_f, vl_f, gc_f, gl_f, o_f, out_f),
            (qc_b, kc_b, vc_b, ql_b, kl_b, vl_b, gc_b, gl_b, o_b, out_b))):
        q = jnp.where(is_ctx, qc[0], ql[0])
        k = jnp.where(is_ctx, kc[0], kl[0])
        v = jnp.where(is_ctx, vc[0], vl[0])
        g = jnp.where(is_ctx, gc[0], gl[0])
        gt = g.T
        ls = jnp.minimum(g, 0.0) - jnp.log(1.0 + jnp.exp(-jnp.abs(g)))
        lst = jnp.minimum(gt, 0.0) - jnp.log(1.0 + jnp.exp(-jnp.abs(gt)))
        tri = tri_ref[d]
        mask = (col <= row) if d == 0 else (col >= row)
        l1, l2, l3 = _split3(ls)
        bcol_all = _dot(tri, l1) + _dot(tri, l2) + _dot(tri, l3)
        t1, t2, t3 = _split3(lst)
        trit = tri_ref[1 - d]
        brow_all = _dot(t1, trit) + _dot(t2, trit) + _dot(t3, trit)
        last = L - 1 if d == 0 else 0

        bal = pltpu.roll(bcol_all, LANES - 8, axis=1)
        cmax = g - bal
        trow = lax.broadcasted_iota(jnp.int32, (L, LANES), 0)
        sft = 1
        while sft < L:
            if d == 0:
                shifted = jnp.where(trow >= sft, pltpu.roll(cmax, sft, axis=0), -jnp.inf)
            else:
                shifted = jnp.where(trow < L - sft, pltpu.roll(cmax, L - sft, axis=0), -jnp.inf)
            cmax = jnp.maximum(cmax, shifted)
            sft *= 2
        m_row = m_scr[d][0:1, :]
        gg = bal + m_row
        mt = jnp.maximum(gg, bal + cmax)
        e_col = bal - mt
        w_inter = jnp.exp(gg - mt)
        enm = jnp.exp(-mt)
        tot = bal[last:last + 1, :]
        wlog = tot - bal + g
        m_new = jnp.maximum(tot + m_row, jnp.max(wlog, axis=0, keepdims=True))
        ws = jnp.exp(wlog - m_new) * scale
        decay = jnp.exp(tot + m_row - m_new)
        m_scr[d] = jnp.broadcast_to(m_new, m_scr.shape[1:])

        for h in range(nh):
            u = d * nh + h
            ci = 16 * d + h
            cf = 16 * d + 8 + h
            hs = slice(h * dh, (h + 1) * dh)
            qh, kh, vh = q[:, hs], k[:, hs], v[:, hs]
            vext = jnp.concatenate([vh, ones_col], axis=1)
            cr = gt[ci:ci + 1, :] - brow_all[cf:cf + 1, :] + math.log(scale)
            c_prev = c_scr[u]

            sqk = lax.dot_general(qh, kh, (((1,), (1,)), ((), ())), preferred_element_type=F32)
            w_intra = jnp.exp(jnp.where(mask, e_col[:, ci:ci + 1] + cr, -jnp.inf))
            a = (sqk * w_intra).astype(BF16)
            qs = (qh.astype(F32) * w_inter[:, ci:ci + 1]).astype(BF16)
            r = _dot(jnp.concatenate([a, qs], axis=1),
                     jnp.concatenate([vext, c_prev.astype(BF16)], axis=0))

            num = r[:, :dh]
            den = r[:, dh:dh + 1]
            rden = 1.0 / jnp.maximum(jnp.abs(den), enm[:, ci:ci + 1])
            ssq = jnp.sum(num * num, axis=-1, keepdims=True)
            sc = rden * lax.rsqrt(rden * rden * ssq * (1.0 / dh) + EPS)
            gate = _sigmoid(o_ref[0, :, hs].astype(F32))
            out_ref[0, :, hs] = (num * sc * hg_ref[h:h + 1, :] * gate).astype(BF16)

            kw = (kh.astype(F32) * ws[:, ci:ci + 1]).astype(BF16)
            upd = lax.dot_general(kw, vext, (((0,), (0,)), ((), ())), preferred_element_type=F32)
            c_scr[u] = decay[:, ci:ci + 1] * c_prev + upd


def _mlstm(u_lat, u_ctx, g_lat, g_ctx, head_g, L):
    b_, s_, _ = u_lat.shape
    n_ctx = u_ctx.shape[1]
    nh = MLSTM_HEADS
    w = 1024
    dh = w // nh
    nc, nl = n_ctx // L, s_ // L
    steps = nc + nl
    t0 = jnp.tril(jnp.ones((L, L), F32))
    tri = jnp.stack([t0, t0.T]).astype(BF16)

    def cidx(d):
        if d == 0:
            return lambda st: jnp.minimum(st, nc - 1), lambda st: jnp.maximum(st - nc, 0)
        return lambda st: jnp.maximum(nc - 1 - st, 0), lambda st: jnp.minimum(nl - 1 + nc - st, nl - 1)

    in_specs, args = [], []
    for d in range(2):
        fc, fl = cidx(d)
        for t in range(3):
            in_specs.append(pl.BlockSpec((1, L, w), lambda b, st, fc=fc, t=t: (b, fc(st), t)))
            args.append(u_ctx)
        for t in (COL_Q, COL_K, COL_V):
            in_specs.append(pl.BlockSpec((1, L, w), lambda b, st, fl=fl, t=t: (b, fl(st), t)))
            args.append(u_lat)
        in_specs.append(pl.BlockSpec((1, L, LANES), lambda b, st, fc=fc: (b, fc(st), 0)))
        args.append(g_ctx)
        in_specs.append(pl.BlockSpec((1, L, LANES), lambda b, st, fl=fl: (b, fl(st), 0)))
        args.append(g_lat)
        in_specs.append(pl.BlockSpec((1, L, w), lambda b, st, fl=fl, t=COL_OF + d: (b, fl(st), t)))
        args.append(u_lat)
    in_specs += [pl.BlockSpec((nh, dh), lambda b, st: (0, 0)),
                 pl.BlockSpec((2, L, L), lambda b, st: (0, 0, 0))]
    args += [head_g, tri]
    out_specs = tuple(pl.BlockSpec((1, L, w), lambda b, st, fl=cidx(d)[1]: (b, fl(st), 0)) for d in range(2))
    return pl.pallas_call(
        functools.partial(_mlstm_kernel, nc, nl, L, dh),
        out_shape=(jax.ShapeDtypeStruct((b_, s_, w), BF16),) * 2,
        grid=(b_, steps),
        in_specs=in_specs,
        out_specs=out_specs,
        scratch_shapes=[pltpu.VMEM((2 * nh, dh, 2 * dh), F32), pltpu.VMEM((2, 8, LANES), F32)],
        compiler_params=_cparams(("parallel", "arbitrary")),
        name="mlstm",
    )(*args)


SUBLANES = 8
LRU_BLK = SUBLANES * SUBLANES


def _rglru_kernel(n_ctx, n_tot, ux_ref, cw_ref, cb_ref, w_ref, b_ref, lam_ref, out_ref,
                  us, z_nat, a_f, b_f, a_b, b_b, h_f, h_b, o_nat):
    c = LANES
    g = SUBLANES
    n_lat = n_tot - n_ctx
    nq = n_tot // g
    nblk = n_tot // LRU_BLK
    nblk_ctx = n_ctx // LRU_BLK

    zeros_g = jnp.zeros((g, c), F32)
    us[0:g, :] = zeros_g
    us[g + n_ctx:2 * g + n_ctx, :] = zeros_g
    us[2 * g + n_tot:3 * g + n_tot, :] = zeros_g
    us[g:g + n_ctx, :] = ux_ref[0, 0:n_ctx, :].astype(F32)
    us[2 * g + n_ctx:2 * g + n_tot, :] = ux_ref[0, n_ctx:n_tot, :].astype(F32)

    def conv(base, n):
        acc = cb_ref[...] + cw_ref[2:3, :] * us[base:base + n, :]
        for j, off in ((0, -2), (1, -1), (3, 1)):
            acc = acc + cw_ref[j:j + 1, :] * us[base + off:base + off + n, :]
        return acc

    z_nat[0:n_ctx, :] = conv(g, n_ctx)
    z_nat[n_ctx:n_tot, :] = conv(2 * g + n_ctx, n_lat)

    z = jnp.concatenate([z_nat[pl.ds(k, nq, stride=g), :] for k in range(g)], axis=0)
    p = _dot(z.astype(BF16), w_ref[0]) + b_ref[...]
    for d, (a_scr, b_scr) in enumerate(((a_f, b_f), (a_b, b_b))):
        r = _sigmoid(p[:, (2 * d) * c:(2 * d + 1) * c])
        i = _sigmoid(p[:, (2 * d + 1) * c:(2 * d + 2) * c])
        lam = lam_ref[d:d + 1, :]
        softplus = jnp.maximum(-lam, 0.0) + jnp.log(1.0 + jnp.exp(-jnp.abs(lam)))
        a = jnp.exp2((-LRU_C * math.log2(math.e) * softplus) * r)
        a_scr[...] = a
        b_scr[...] = jnp.sqrt(1.0 - a * a) * i * z

    sub = lax.broadcasted_iota(jnp.int32, (g, c), 0)

    def block_scan(a_scr, b_scr, h_scr, blk, carry, reverse):
        r0 = pl.multiple_of(blk * g, g)
        order = range(g - 1, -1, -1) if reverse else range(g)
        hs, ps = {}, {}
        h = p_ = None
        for k in order:
            a = a_scr[pl.ds(k * nq + r0, g), :]
            b = b_scr[pl.ds(k * nq + r0, g), :]
            h = b if h is None else a * h + b
            p_ = a if p_ is None else a * p_
            hs[k], ps[k] = h, p_
        def shift(x, sft, fill):
            if reverse:
                return jnp.where(sub < g - sft, pltpu.roll(x, g - sft, 0), fill)
            return jnp.where(sub >= sft, pltpu.roll(x, sft, 0), fill)
        pe, he = shift(p_, 1, 1.0), shift(h, 1, 0.0)
        for sft in (1, 2, 4):
            he = pe * shift(he, sft, 0.0) + he
            pe = pe * shift(pe, sft, 1.0)
        cin = pe * carry + he
        for k in order:
            h_scr[pl.ds(k * nq + r0, g), :] = hs[k] + ps[k] * cin
        last = 0 if reverse else g - 1
        tot = p_ * cin + h
        return tot[last:last + 1, :]

    def body(it, carry):
        cf, cb = carry
        cf = block_scan(a_f, b_f, h_f, it, cf, False)
        jb = jnp.where(it < nblk_ctx, nblk_ctx - 1 - it, nblk - 1 + nblk_ctx - it)
        cb = block_scan(a_b, b_b, h_b, jb, cb, True)
        return cf, cb

    zero = jnp.zeros((1, c), F32)
    lax.fori_loop(0, nblk, body, (zero, zero), unroll=2)

    q_ctx = n_ctx // g
    for k in range(g):
        o_nat[pl.ds(k, n_lat // g, stride=g), :] = (h_f[k * nq + q_ctx:(k + 1) * nq, :]
                                                    + h_b[k * nq + q_ctx:(k + 1) * nq, :])
    out_ref[0] = o_nat[...].astype(BF16)


def _rglru(ux_seq, conv_w, conv_b, w_cat, b_cat, lam, n_ctx):
    b_, n_tot, w = ux_seq.shape
    nb = LRU_BLOCKS
    c = w // nb
    return pl.pallas_call(
        functools.partial(_rglru_kernel, n_ctx, n_tot),
        out_shape=jax.ShapeDtypeStruct((b_, n_tot - n_ctx, w), BF16),
        grid=(b_, nb),
        in_specs=[
            pl.BlockSpec((1, n_tot, c), lambda b, k: (b, 0, k)),
            pl.BlockSpec((4, c), lambda b, k: (0, k)),
            pl.BlockSpec((1, c), lambda b, k: (0, k)),
            pl.BlockSpec((1, c, 4 * c), lambda b, k: (k, 0, 0)),
            pl.BlockSpec((1, 4 * c), lambda b, k: (0, k)),
            pl.BlockSpec((2, c), lambda b, k: (0, k)),
        ],
        out_specs=pl.BlockSpec((1, n_tot - n_ctx, c), lambda b, k: (b, 0, k)),
        scratch_shapes=([pltpu.VMEM((n_tot + 3 * SUBLANES, c), F32)] + [pltpu.VMEM((n_tot, c), F32)] * 7
                        + [pltpu.VMEM((n_tot - n_ctx, c), F32)]),
        compiler_params=_cparams(("parallel", "parallel")),
        name="rglru",
    )(ux_seq, conv_w, conv_b, w_cat, b_cat, lam)


def _outproj_kernel(yaf_ref, yab_ref, hr_ref, ug_ref, ga0_ref, ga1_ref, gb0_ref, gb1_ref, x_ref,
                    g1_ref, gain_ref, shift_ref, wpa_ref, wpb_ref, wout_ref, wr_ref,
                    x1_ref, h2_ref, lg_ref):
    ya = (yaf_ref[0].astype(F32) + yab_ref[0].astype(F32)).astype(BF16)
    ug = ug_ref[0].astype(F32)
    gelu = 0.5 * ug * (1.0 + jnp.tanh(0.7978845608028654 * (ug + 0.044715 * ug * ug * ug)))
    yb = (hr_ref[0].astype(F32) * gelu).astype(BF16)
    pa = _dot(ya, wpa_ref[...])
    pb = _dot(yb, wpb_ref[...])
    ga = jnp.concatenate([ga0_ref[0], ga1_ref[0]], axis=1).astype(F32)
    gb = jnp.concatenate([gb0_ref[0], gb1_ref[0]], axis=1).astype(F32)
    mix = (_sigmoid(ga) * pa + _sigmoid(gb) * pb).astype(BF16)
    x1 = x_ref[0] + g1_ref[0] * _dot(mix, wout_ref[...])
    x1_ref[0] = x1
    ms = jnp.mean(x1 * x1, axis=-1, keepdims=True)
    h2 = x1 * lax.rsqrt(ms + EPS) * gain_ref[0] + shift_ref[0]
    h2_ref[...] = h2
    wr = wr_ref[...]
    h1, h2b, _ = _split3(h2)
    w1, w2b, _ = _split3(wr)
    lg_ref[0] = _route_rows(_dot(h1, w1) + (_dot(h1, w2b) + _dot(h2b, w1)))


def _outproj(ya_f, ya_b, hr, u_lat, x, g1, gain2, shift2, w_pa, w_pb, w_out, w_r, tm):
    b_, s_, d = x.shape
    w = ya_f.shape[2]
    row = lambda b, i: (b, i, 0)
    const2 = lambda b, i: (0, 0)
    bmap = lambda b, i: (b, 0, 0)
    ucol = lambda t: (lambda b, i: (b, i, t))
    single = pl.Buffered(1)
    return pl.pallas_call(
        _outproj_kernel,
        out_shape=(jax.ShapeDtypeStruct((b_, s_, d), F32),
                   jax.ShapeDtypeStruct((b_ * s_, d), F32),
                   jax.ShapeDtypeStruct((b_, s_, LANES), F32)),
        grid=(b_, s_ // tm),
        in_specs=[
            pl.BlockSpec((1, tm, w), row), pl.BlockSpec((1, tm, w), row), pl.BlockSpec((1, tm, w), row),
            pl.BlockSpec((1, tm, w), ucol(COL_UG)),
            pl.BlockSpec((1, tm, w), ucol(COL_GA)), pl.BlockSpec((1, tm, w), ucol(COL_GA + 1)),
            pl.BlockSpec((1, tm, w), ucol(COL_GB)), pl.BlockSpec((1, tm, w), ucol(COL_GB + 1)),
            pl.BlockSpec((1, tm, d), row),
            pl.BlockSpec((1, 1, d), bmap), pl.BlockSpec((1, 1, d), bmap), pl.BlockSpec((1, 1, d), bmap),
            pl.BlockSpec((w, d), const2, pipeline_mode=single),
            pl.BlockSpec((w, d), const2, pipeline_mode=single),
            pl.BlockSpec((d, d), const2, pipeline_mode=single),
            pl.BlockSpec((d, LANES), const2, pipeline_mode=single),
        ],
        out_specs=(pl.BlockSpec((1, tm, d), row),
                   pl.BlockSpec((tm, d), lambda b, i: (b * (s_ // tm) + i, 0)),
                   pl.BlockSpec((1, tm, LANES), row)),
        compiler_params=_cparams(("parallel", "parallel")),
        name="outproj",
    )(ya_f, ya_b, hr, u_lat, u_lat, u_lat, u_lat, u_lat, x, g1, gain2, shift2, w_pa, w_pb, w_out, w_r)


def _route_rows(lg):
    lane = lax.broadcasted_iota(jnp.int32, lg.shape, 1)
    neg = -jnp.inf
    big = jnp.int32(1 << 20)
    g_l = jnp.where(lane < N_GROUPS, lg, neg)
    g_max = jnp.max(g_l, axis=-1, keepdims=True)
    g_sel = jnp.min(jnp.where(g_l == g_max, lane, big), axis=-1, keepdims=True)
    p_g = 1.0 / jnp.sum(jnp.exp(g_l - g_max), axis=-1, keepdims=True)
    lo = N_GROUPS + g_sel * EXPERTS_PER_GROUP
    e_l = jnp.where((lane >= lo) & (lane < lo + EXPERTS_PER_GROUP), lg, neg)
    v1 = jnp.max(e_l, axis=-1, keepdims=True)
    i1 = jnp.min(jnp.where(e_l == v1, lane, big), axis=-1, keepdims=True)
    e_l2 = jnp.where(lane == i1, neg, e_l)
    v2 = jnp.max(e_l2, axis=-1, keepdims=True)
    i2 = jnp.min(jnp.where(e_l2 == v2, lane, big), axis=-1, keepdims=True)
    e2 = jnp.exp(v2 - v1)
    w1 = p_g / (1.0 + e2)
    w2 = p_g * e2 / (1.0 + e2)
    return jnp.where(lane == 0, (i1 - N_GROUPS).astype(F32),
           jnp.where(lane == 1, (i2 - N_GROUPS).astype(F32),
           jnp.where(lane == 2, w1, jnp.where(lane == 3, w2, 0.0))))


MOE_DMA_UNROLL = 32


MOE_W_DEPTH = 3
MOE_W_SPLIT = 2


def _moe_kernel(R, SUB, nf, ni, n_tok, item_e, item_rows, item_base, order, h2_hbm, w1_hbm, w3_hbm, w2_hbm,
                y_hbm, x_scr, xb_scr, acc_scr, y_scr, w1r, w3r, w2r, gsem, ssem, wsem):
    i = pl.program_id(0)
    j = pl.program_id(1)
    nsub_of = lambda r: (r + SUB - 1) // SUB
    rows = item_rows[i]
    nsub = nsub_of(rows)
    i_next = jnp.minimum(i + 1, ni - 1)
    nsub_next = jnp.where(i + 1 < ni, nsub_of(item_rows[i_next]), 0)
    rows_prev = jnp.where(i > 0, item_rows[jnp.maximum(i - 1, 0)], 0)
    base = item_base[i]
    base_next = item_base[i_next]
    n_assign = order.shape[0]
    buf = i % 2
    groups = SUB // MOE_DMA_UNROLL

    def assignment(b0, r):
        return order[jnp.minimum(b0 + r, n_assign - 1)]

    d_model, tf = w1r.shape[1], w1r.shape[2]

    def weight_copies(c):
        it = c // nf
        jc = c % nf
        e = item_e[jnp.minimum(it, ni - 1)]
        slot = c % MOE_W_DEPTH
        f0 = pl.multiple_of(jc * tf, tf)
        out = []
        for h in range(MOE_W_SPLIT):
            ra, rb = d_model // MOE_W_SPLIT, tf // MOE_W_SPLIT
            out.append(pltpu.make_async_copy(w1_hbm.at[e, pl.ds(h * ra, ra), pl.ds(f0, tf)],
                                             w1r.at[slot, pl.ds(h * ra, ra), :], wsem.at[slot]))
            out.append(pltpu.make_async_copy(w3_hbm.at[e, pl.ds(h * ra, ra), pl.ds(f0, tf)],
                                             w3r.at[slot, pl.ds(h * ra, ra), :], wsem.at[slot]))
            out.append(pltpu.make_async_copy(w2_hbm.at[e, pl.ds(f0 + h * rb, rb), :],
                                             w2r.at[slot, pl.ds(h * rb, rb), :], wsem.at[slot]))
        return out

    def chunk_live(c):
        it = c // nf
        return (it < ni) & (item_rows[jnp.minimum(it, ni - 1)] > 0)

    cur = i * nf + j

    @pl.when((cur == 0) & chunk_live(0))
    def _():
        for cp in weight_copies(0):
            cp.start()

        @pl.when(chunk_live(1))
        def _():
            for cp in weight_copies(1):
                cp.start()

    @pl.when((rows > 0) & chunk_live(cur + 2))
    def _():
        for cp in weight_copies(cur + 2):
            cp.start()

    def tok_rows(ref, r):
        return ref.at[pl.ds(r, 1), :]

    def gather_group(b0, b, gi):
        r0 = pl.multiple_of(gi * MOE_DMA_UNROLL, MOE_DMA_UNROLL)
        for k in range(MOE_DMA_UNROLL):
            tok = lax.shift_right_logical(assignment(b0, r0 + k), TOP_K_SHIFT)
            pltpu.make_async_copy(tok_rows(h2_hbm, tok), tok_rows(x_scr.at[b], r0 + k),
                                  gsem.at[b]).start(priority=k % 2)

    def gather_wait_block(b):
        pltpu.make_async_copy(h2_hbm.at[pl.ds(0, SUB), :], x_scr.at[b, pl.ds(0, SUB), :], gsem.at[b]).wait()

    def scatter_copy(b0, r):
        a = assignment(b0, r)
        dst = (a & (TOP_K - 1)) * n_tok + lax.shift_right_logical(a, TOP_K_SHIFT)
        return pltpu.make_async_copy(tok_rows(y_scr, r), tok_rows(y_hbm, dst), ssem)

    def scatter_group(b0, gi):
        r0 = pl.multiple_of(gi * MOE_DMA_UNROLL, MOE_DMA_UNROLL)
        for k in range(MOE_DMA_UNROLL):
            scatter_copy(b0, r0 + k).start(priority=k % 2)

    def repeat(n, fn, lo=0):
        def body(q, c):
            fn(q)
            return c
        lax.fori_loop(lo, n, body, 0)

    def scatter_rows(b0, n):
        full = n // MOE_DMA_UNROLL
        repeat(full, lambda gi: scatter_group(b0, gi))
        repeat(n, lambda r: scatter_copy(b0, r).start(), lo=full * MOE_DMA_UNROLL)

    def scatter_wait_rows(n):
        u = MOE_DMA_UNROLL
        full = n // MOE_DMA_UNROLL
        repeat(full, lambda q: pltpu.make_async_copy(y_scr.at[pl.ds(0, u), :], y_hbm.at[pl.ds(0, u), :],
                                                      ssem).wait())
        repeat(n, lambda r: scatter_copy(0, 0).wait(), lo=full * MOE_DMA_UNROLL)

    @pl.when((i == 0) & (j == 0))
    def _():
        repeat(nsub * groups, lambda gi: gather_group(base, 0, gi))

    bps = -(-(R // SUB) // nf)
    @pl.when(j * bps < nsub_next)
    def _():
        repeat(jnp.minimum((j + 1) * bps, nsub_next) * groups, lambda gi: gather_group(base_next, 1 - buf, gi),
               lo=j * bps * groups)

    @pl.when(j == 0)
    def _():
        repeat(nsub, lambda q: gather_wait_block(buf))

    @pl.when(j == nf - 1)
    def _():
        scatter_wait_rows(rows_prev)

    @pl.when(rows > 0)
    def _():
        for cp in weight_copies(cur):
            cp.wait()
        slot = cur % MOE_W_DEPTH

        @pl.when(j == 0)
        def _():
            def to_bf16(sb, c):
                r0 = pl.multiple_of(sb * SUB, SUB)
                xb_scr[pl.ds(r0, SUB), :] = x_scr[buf, pl.ds(r0, SUB), :].astype(BF16)
                return c
            lax.fori_loop(0, nsub, to_bf16, 0)

        def experts(m):
            def run():
                x = xb_scr[0:m, :]
                h1 = _dot(x, w1r[slot].astype(BF16))
                h3 = _dot(x, w3r[slot].astype(BF16))
                hh = (h1 * _sigmoid(h1) * h3).astype(BF16)
                contrib = _dot(hh, w2r[slot].astype(BF16))

                @pl.when(j == 0)
                def _():
                    acc_scr[0:m, :] = contrib

                @pl.when((j > 0) & (j < nf - 1))
                def _():
                    acc_scr[0:m, :] += contrib

                @pl.when(j == nf - 1)
                def _():
                    y_scr[0:m, :] = acc_scr[0:m, :] + contrib
            return run
        lax.switch(nsub - 1, [experts((q + 1) * SUB) for q in range(R // SUB)])

    @pl.when(j == nf - 1)
    def _():
        scatter_rows(base, rows)

        @pl.when(i == ni - 1)
        def _():
            scatter_wait_rows(rows)


def _moe(h2, item_e, item_rows, item_base, order, w1, w3, w2):
    t_ = h2.shape[0]
    ne, d, f = w1.shape
    R, SUB, tf = MOE_ITEM_ROWS, MOE_SUB_ROWS, MOE_F_CHUNK
    ni = item_e.shape[0]
    nf = f // tf
    assert R % SUB == 0 and SUB % MOE_DMA_UNROLL == 0 and nf >= 2
    grid_spec = pltpu.PrefetchScalarGridSpec(
        num_scalar_prefetch=4,
        grid=(ni, nf),
        in_specs=[
            pl.BlockSpec(memory_space=pl.ANY),
            pl.BlockSpec(memory_space=pl.ANY),
            pl.BlockSpec(memory_space=pl.ANY),
            pl.BlockSpec(memory_space=pl.ANY),
        ],
        out_specs=pl.BlockSpec(memory_space=pl.ANY),
        scratch_shapes=[
            pltpu.VMEM((2, R, d), F32), pltpu.VMEM((R, d), BF16),
            pltpu.VMEM((R, d), F32), pltpu.VMEM((R, d), F32),
            pltpu.VMEM((MOE_W_DEPTH, d, tf), F32), pltpu.VMEM((MOE_W_DEPTH, d, tf), F32),
            pltpu.VMEM((MOE_W_DEPTH, tf, d), F32),
            pltpu.SemaphoreType.DMA((2,)), pltpu.SemaphoreType.DMA(()), pltpu.SemaphoreType.DMA((MOE_W_DEPTH,)),
        ],
    )
    return pl.pallas_call(
        functools.partial(_moe_kernel, R, SUB, nf, ni, t_),
        out_shape=jax.ShapeDtypeStruct((TOP_K * t_, d), F32),
        grid_spec=grid_spec,
        compiler_params=_cparams(("arbitrary", "arbitrary")),
        name="moe_experts",
    )(item_e, item_rows, item_base, order, h2, w1, w3, w2)


def _moe_plan(expert):
    R = MOE_ITEM_ROWS
    na = expert.shape[0]
    ni = N_EXPERTS + na // R
    order = jnp.argsort(expert, stable=True).astype(jnp.int32)
    eids = jnp.arange(N_EXPERTS, dtype=jnp.int32)
    counts = jnp.sum((expert[:, None] == eids[None, :]).astype(jnp.int32), axis=0)
    first = jnp.cumsum(counts) - counts
    n_items = (counts + R - 1) // R
    item_end = jnp.cumsum(n_items)
    item_start = item_end - n_items
    ii = jnp.arange(ni, dtype=jnp.int32)
    e_of = jnp.minimum(jnp.sum((item_end[None, :] <= ii[:, None]).astype(jnp.int32), axis=1), N_EXPERTS - 1)
    oh_e = (e_of[:, None] == eids[None, :]).astype(jnp.int32)
    pick = lambda v: jnp.sum(oh_e * v[None, :], axis=1)
    part = ii - pick(item_start)
    rows = jnp.clip(pick(counts) - part * R, 0, R).astype(jnp.int32)
    base = (pick(first) + part * R).astype(jnp.int32)
    live = ii < item_end[-1]
    rows = jnp.where(live, rows, 0)
    base = jnp.where(live, base, 0)
    last_e = jnp.max(jnp.where(live, e_of, 0))
    e_of = jnp.where(live, e_of, last_e).astype(jnp.int32)
    return e_of, rows, base, order


def _final_kernel(x1_ref, y0_ref, y1_ref, rt_ref, g2_ref, fg_ref, o_ref):
    rt = rt_ref[0]
    y = rt[:, TOP_K:TOP_K + 1] * y0_ref[...] + rt[:, TOP_K + 1:TOP_K + 2] * y1_ref[...]
    x = x1_ref[0] + g2_ref[0] * y
    ms = jnp.mean(x * x, axis=-1, keepdims=True)
    o_ref[0] = x * lax.rsqrt(ms + EPS) * fg_ref[...]


def _final(x1, ybuf, routed, g2, final_g, tm):
    b_, s_, d = x1.shape
    nt = s_ // tm
    return pl.pallas_call(
        _final_kernel,
        out_shape=jax.ShapeDtypeStruct((b_, s_, d), F32),
        grid=(b_, nt),
        in_specs=[
            pl.BlockSpec((1, tm, d), lambda b, i: (b, i, 0)),
            pl.BlockSpec((tm, d), lambda b, i: (b * nt + i, 0)),
            pl.BlockSpec((tm, d), lambda b, i: (b_ * nt + b * nt + i, 0)),
            pl.BlockSpec((1, tm, LANES), lambda b, i: (b, i, 0)),
            pl.BlockSpec((1, 1, d), lambda b, i: (b, 0, 0)),
            pl.BlockSpec((1, d), lambda b, i: (0, 0)),
        ],
        out_specs=pl.BlockSpec((1, tm, d), lambda b, i: (b, i, 0)),
        compiler_params=_cparams(("parallel", "parallel")),
        name="final_norm",
    )(x1, ybuf, ybuf, routed, g2, final_g)


def kernel(x, c, ctx, c_ctx, w_mod, b_mod, norm1_g, w_in, b_gates, mlstm_head_g, conv_w, conv_b, lru_wa, lru_ba,
           lru_wx, lru_bx, lru_lam, w_pa, w_pb, w_out, norm2_g, w_rg, w_re, w1, w3, w2, final_g):
    b_, s_, d = x.shape
    n_ctx = ctx.shape[1]
    rows = s_ // GRID_W
    assert w_mod.shape[0] == 1, "single layer"
    wm = d // 2
    nh = MLSTM_HEADS

    cc = jnp.zeros((8, d), F32).at[:b_].set(c).at[b_].set(c_ctx)
    mod = _modulation(cc, w_mod[0], b_mod[0][None, :])
    sh1, sc1, g1, sh2, sc2, g2 = [mod[:b_, i * d:(i + 1) * d][:, None, :] for i in range(6)]
    csh1, csc1 = mod[b_:b_ + 1, 0:d][:, None, :], mod[b_:b_ + 1, d:2 * d][:, None, :]

    assert wm == 1024 and 4 * nh == N_GATE_COLS
    w_t = jnp.transpose(w_in[0])
    bg_pad = jnp.zeros((1, LANES), F32).at[0, :4 * nh].set(b_gates[0])

    ng = norm1_g[0][None, None, :]
    u_lat, g_lat = _inproj(x, ng * (1.0 + sc1), sh1, w_t, bg_pad, tuple(range(N_COL_TILES)), min(1024, s_))
    u_ctx, g_ctx = _inproj(ctx.reshape(1, b_ * n_ctx, d), ng * (1.0 + csc1), csh1, w_t, bg_pad,
                           (COL_Q, COL_K, COL_V, COL_UX), b_ * n_ctx)
    u_ctx = u_ctx.reshape(b_, n_ctx, -1)
    g_ctx = g_ctx.reshape(b_, n_ctx, LANES)

    ya_f, ya_b = _mlstm(u_lat, u_ctx, g_lat, g_ctx, mlstm_head_g[0], MLSTM_CHUNK)

    ux_lat = u_lat[:, :, COL_UX * wm:(COL_UX + 1) * wm]
    ux_col = ux_lat.reshape(b_, rows, GRID_W, wm).transpose(0, 2, 1, 3).reshape(b_, s_, wm)
    ux_seq = jnp.concatenate([u_ctx[:, :, CTX_UX * wm:(CTX_UX + 1) * wm], ux_col], axis=1)
    wa, wx = lru_wa[0], lru_wx[0]
    w_cat = jnp.concatenate([wa[0], wx[0], wa[1], wx[1]], axis=-1).astype(BF16)
    cblk = wm // LRU_BLOCKS
    blk = lambda v: v.reshape(LRU_BLOCKS, cblk)
    b_cat = jnp.concatenate([blk(lru_ba[0, 0]), blk(lru_bx[0, 0]), blk(lru_ba[0, 1]), blk(lru_bx[0, 1])],
                            axis=-1).reshape(1, 4 * wm)
    h_col = _rglru(ux_seq, conv_w[0], conv_b[0][None, :], w_cat, b_cat, lru_lam[0], n_ctx)
    hr = h_col.reshape(b_, GRID_W, rows, wm).transpose(0, 2, 1, 3).reshape(b_, s_, wm)

    w_r = jnp.zeros((d, LANES), F32).at[:, :N_GROUPS].set(w_rg[0]).at[:, N_GROUPS:N_GROUPS + N_EXPERTS].set(w_re[0])
    n2 = norm2_g[0][None, None, :]
    x1, h2, routed = _outproj(ya_f, ya_b, hr, u_lat, x, g1, n2 * (1.0 + sc2), sh2,
                              w_pa[0].astype(BF16), w_pb[0].astype(BF16), w_out[0].astype(BF16), w_r, 256)

    t_ = b_ * s_
    expert = routed.reshape(t_, LANES)[:, :TOP_K].astype(jnp.int32).reshape(-1)
    item_e, item_rows, item_base, order = _moe_plan(expert)
    ybuf = _moe(h2, item_e, item_rows, item_base, order, w1[0], w3[0], w2[0])

    return _final(x1, ybuf, routed, g2, final_g[None, :], 256)
```

```python
import functools
import math

import jax
import jax.numpy as jnp
from jax import lax
from jax.experimental import pallas as pl
from jax.experimental.pallas import tpu as pltpu

F32 = jnp.float32
BF16 = jnp.bfloat16

EPS = 1e-6
GRID_W = 64
MLSTM_HEADS = 8
LRU_BLOCKS = 8
LRU_C = 8.0
N_GROUPS = 4
EXPERTS_PER_GROUP = 8
N_EXPERTS = N_GROUPS * EXPERTS_PER_GROUP
TOP_K = 2
TOP_K_SHIFT = 1
assert 1 << TOP_K_SHIFT == TOP_K

VMEM_LIMIT_BYTES = 56 * 1024 * 1024
LANES = 128

MLSTM_CHUNK = 128
MOE_ITEM_ROWS = 768
MOE_SUB_ROWS = 128
MOE_F_CHUNK = 256


def _cparams(sem):
    return pltpu.CompilerParams(dimension_semantics=sem, vmem_limit_bytes=VMEM_LIMIT_BYTES)


def _sigmoid(x):
    return 0.5 * jnp.tanh(0.5 * x) + 0.5


def _dot(a, b):
    return jnp.dot(a, b, preferred_element_type=F32)


def _split3(x):
    x1 = x.astype(BF16)
    r1 = x - x1.astype(F32)
    x2 = r1.astype(BF16)
    x3 = (r1 - x2.astype(F32)).astype(BF16)
    return x1, x2, x3


def _mod_kernel(c_ref, w_ref, b_ref, o_ref):
    c = c_ref[...]
    s = (c * _sigmoid(c)).astype(BF16)
    o_ref[...] = _dot(s, w_ref[...].astype(BF16)) + b_ref[...]


def _modulation(cc, w_mod, b_mod):
    m, d = cc.shape
    n = w_mod.shape[1]
    tn = 1024
    return pl.pallas_call(
        _mod_kernel,
        out_shape=jax.ShapeDtypeStruct((m, n), F32),
        grid=(n // tn,),
        in_specs=[
            pl.BlockSpec((m, d), lambda j: (0, 0)),
            pl.BlockSpec((d, tn), lambda j: (0, j)),
            pl.BlockSpec((1, tn), lambda j: (0, j)),
        ],
        out_specs=pl.BlockSpec((m, tn), lambda j: (0, j)),
        compiler_params=_cparams(("parallel",)),
        name="modulation",
    )(cc, w_mod, b_mod)


N_PRE_TILES = 5
N_GATE_COLS = 32


def _dot_nt(a, b):
    return lax.dot_general(a, b, (((1,), (1,)), ((), ())), preferred_element_type=F32)


def _inproj_kernel(x_ref, gain_ref, shift_ref, wt_ref, wg_ref, bg_ref, u_ref, g_ref, h_scr):
    @pl.when(pl.program_id(2) == 0)
    def _():
        x = x_ref[0]
        ms = jnp.mean(x * x, axis=-1, keepdims=True)
        h = x * lax.rsqrt(ms + EPS) * gain_ref[0] + shift_ref[0]
        hb = h.astype(BF16)
        h_scr[...] = hb
        gates = _dot_nt(hb, wg_ref[...].astype(BF16))
        pad = jnp.zeros((gates.shape[0], LANES - N_GATE_COLS), F32)
        g_ref[0] = jnp.concatenate([gates, pad], axis=1) + bg_ref[...]

    u_ref[0] = _dot_nt(h_scr[...], wt_ref[...].astype(BF16)).astype(BF16)


def _inproj(x, gain, shift, w_t, b_gates, tiles, tm):
    b_, s_, d = x.shape
    tn = 1024
    nj = len(tiles)
    if tiles == tuple(range(nj)):
        tile_of = lambda j: j
    else:
        assert tiles == (COL_Q, COL_K, COL_V, COL_UX), tiles
        tile_of = lambda j: jnp.where(j == 3, COL_UX, j)
    row_of = lambda j: tile_of(j) * tn + jnp.where(tile_of(j) >= N_PRE_TILES, N_GATE_COLS, 0)
    per_batch_mod = gain.shape[0] > 1
    mmap = (lambda b, i, j: (b, 0, 0)) if per_batch_mod else (lambda b, i, j: (0, 0, 0))
    gate_blk = N_PRE_TILES * tn // N_GATE_COLS
    return pl.pallas_call(
        _inproj_kernel,
        out_shape=(jax.ShapeDtypeStruct((b_, s_, nj * tn), BF16),
                   jax.ShapeDtypeStruct((b_, s_, LANES), F32)),
        grid=(b_, s_ // tm, nj),
        in_specs=[
            pl.BlockSpec((1, tm, d), lambda b, i, j: (b, i, 0)),
            pl.BlockSpec((1, 1, d), mmap),
            pl.BlockSpec((1, 1, d), mmap),
            pl.BlockSpec((pl.Element(tn), pl.Element(d)), lambda b, i, j: (pl.multiple_of(row_of(j), 8), 0)),
            pl.BlockSpec((N_GATE_COLS, d), lambda b, i, j: (gate_blk, 0)),
            pl.BlockSpec((1, LANES), lambda b, i, j: (0, 0)),
        ],
        out_specs=(pl.BlockSpec((1, tm, tn), lambda b, i, j: (b, i, j)),
                   pl.BlockSpec((1, tm, LANES), lambda b, i, j: (b, i, 0))),
        scratch_shapes=[pltpu.VMEM((tm, d), BF16)],
        compiler_params=_cparams(("parallel", "parallel", "arbitrary")),
        name="inproj",
    )(x, gain, shift, w_t, w_t, b_gates)


COL_Q, COL_K, COL_V, COL_OF, COL_OB, COL_UX, COL_UG, COL_GA, COL_GB = 0, 1, 2, 3, 4, 5, 6, 7, 9
CTX_UX = 3
N_COL_TILES = 11


def _mlstm_kernel(nc, nl, L, dh,
                  qc_f, kc_f, vc_f, ql_f, kl_f, vl_f, gc_f, gl_f, o_f,
                  qc_b, kc_b, vc_b, ql_b, kl_b, vl_b, gc_b, gl_b, o_b,
                  hg_ref, tri_ref,
                  out_f, out_b, c_scr, m_scr):
    s = pl.program_id(1)
    nh = MLSTM_HEADS
    scale = dh ** -0.5

    @pl.when(s == 0)
    def _():
        c_scr[...] = jnp.zeros_like(c_scr)
        m_scr[...] = jnp.full_like(m_scr, -1e30)

    is_ctx = s < nc
    ones_col = (lax.broadcasted_iota(jnp.int32, (L, dh), 1) == 0).astype(BF16)
    row = lax.broadcasted_iota(jnp.int32, (L, L), 0)
    col = lax.broadcasted_iota(jnp.int32, (L, L), 1)

    for d, (qc, kc, vc, ql, kl, vl, gc, gl, o_ref, out_ref) in enumerate((
            (qc_f, kc_f, vc_f, ql_f, kl_f, vl_f, gc_f, gl_f, o_f, out_f),
            (qc_b, kc_b, vc_b, ql_b, kl_b, vl_b, gc_b, gl_b, o_b, out_b))):
        q = jnp.where(is_ctx, qc[0], ql[0])
        k = jnp.where(is_ctx, kc[0], kl[0])
        v = jnp.where(is_ctx, vc[0], vl[0])
        g = jnp.where(is_ctx, gc[0], gl[0])
        gt = g.T
        ls = jnp.minimum(g, 0.0) - jnp.log(1.0 + jnp.exp(-jnp.abs(g)))
        lst = jnp.minimum(gt, 0.0) - jnp.log(1.0 + jnp.exp(-jnp.abs(gt)))
        tri = tri_ref[d]
        mask = (col <= row) if d == 0 else (col >= row)
        l1, l2, l3 = _split3(ls)
        bcol_all = _dot(tri, l1) + _dot(tri, l2) + _dot(tri, l3)
        t1, t2, t3 = _split3(lst)
        trit = tri_ref[1 - d]
        brow_all = _dot(t1, trit) + _dot(t2, trit) + _dot(t3, trit)
        last = L - 1 if d == 0 else 0

        bal = pltpu.roll(bcol_all, LANES - 8, axis=1)
        cmax = g - bal
        trow = lax.broadcasted_iota(jnp.int32, (L, LANES), 0)
        sft = 1
        while sft < L:
            if d == 0:
                shifted = jnp.where(trow >= sft, pltpu.roll(cmax, sft, axis=0), -jnp.inf)
            else:
                shifted = jnp.where(trow < L - sft, pltpu.roll(cmax, L - sft, axis=0), -jnp.inf)
            cmax = jnp.maximum(cmax, shifted)
            sft *= 2
        m_row = m_scr[d][0:1, :]
        gg = bal + m_row
        mt = jnp.maximum(gg, bal + cmax)
        e_col = bal - mt
        w_inter = jnp.exp(gg - mt)
        enm = jnp.exp(-mt)
        tot = bal[last:last + 1, :]
        wlog = tot - bal + g
        m_new = jnp.maximum(tot + m_row, jnp.max(wlog, axis=0, keepdims=True))
        ws = jnp.exp(wlog - m_new) * scale
        decay = jnp.exp(tot + m_row - m_new)
        m_scr[d] = jnp.broadcast_to(m_new, m_scr.shape[1:])

        for h in range(nh):
            u = d * nh + h
            ci = 16 * d + h
            cf = 16 * d + 8 + h
            hs = slice(h * dh, (h + 1) * dh)
            qh, kh, vh = q[:, hs], k[:, hs], v[:, hs]
            vext = jnp.concatenate([vh, ones_col], axis=1)
            cr = gt[ci:ci + 1, :] - brow_all[cf:cf + 1, :] + math.log(scale)
            c_prev = c_scr[u]

            sqk = lax.dot_general(qh, kh, (((1,), (1,)), ((), ())), preferred_element_type=F32)
            w_intra = jnp.exp(jnp.where(mask, e_col[:, ci:ci + 1] + cr, -jnp.inf))
            a = (sqk * w_intra).astype(BF16)
            qs = (qh.astype(F32) * w_inter[:, ci:ci + 1]).astype(BF16)
            r = _dot(jnp.concatenate([a, qs], axis=1),
                     jnp.concatenate([vext, c_prev.astype(BF16)], axis=0))

            num = r[:, :dh]
            den = r[:, dh:dh + 1]
            rden = 1.0 / jnp.maximum(jnp.abs(den), enm[:, ci:ci + 1])
            ssq = jnp.sum(num * num, axis=-1, keepdims=True)
            sc = rden * lax.rsqrt(rden * rden * ssq * (1.0 / dh) + EPS)
            gate = _sigmoid(o_ref[0, :, hs].astype(F32))
            out_ref[0, :, hs] = (num * sc * hg_ref[h:h + 1, :] * gate).astype(BF16)

            kw = (kh.astype(F32) * ws[:, ci:ci + 1]).astype(BF16)
            upd = lax.dot_general(kw, vext, (((0,), (0,)), ((), ())), preferred_element_type=F32)
            c_scr[u] = decay[:, ci:ci + 1] * c_prev + upd


def _mlstm(u_lat, u_ctx, g_lat, g_ctx, head_g, L):
    b_, s_, _ = u_lat.shape
    n_ctx = u_ctx.shape[1]
    nh = MLSTM_HEADS
    w = 1024
    dh = w // nh
    nc, nl = n_ctx // L, s_ // L
    steps = nc + nl
    t0 = jnp.tril(jnp.ones((L, L), F32))
    tri = jnp.stack([t0, t0.T]).astype(BF16)

    def cidx(d):
        if d == 0:
            return lambda st: jnp.minimum(st, nc - 1), lambda st: jnp.maximum(st - nc, 0)
        return lambda st: jnp.maximum(nc - 1 - st, 0), lambda st: jnp.minimum(nl - 1 + nc - st, nl - 1)

    in_specs, args = [], []
    for d in range(2):
        fc, fl = cidx(d)
        for t in range(3):
            in_specs.append(pl.BlockSpec((1, L, w), lambda b, st, fc=fc, t=t: (b, fc(st), t)))
            args.append(u_ctx)
        for t in (COL_Q, COL_K, COL_V):
            in_specs.append(pl.BlockSpec((1, L, w), lambda b, st, fl=fl, t=t: (b, fl(st), t)))
            args.append(u_lat)
        in_specs.append(pl.BlockSpec((1, L, LANES), lambda b, st, fc=fc: (b, fc(st), 0)))
        args.append(g_ctx)
        in_specs.append(pl.BlockSpec((1, L, LANES), lambda b, st, fl=fl: (b, fl(st), 0)))
        args.append(g_lat)
        in_specs.append(pl.BlockSpec((1, L, w), lambda b, st, fl=fl, t=COL_OF + d: (b, fl(st), t)))
        args.append(u_lat)
    in_specs += [pl.BlockSpec((nh, dh), lambda b, st: (0, 0)),
                 pl.BlockSpec((2, L, L), lambda b, st: (0, 0, 0))]
    args += [head_g, tri]
    out_specs = tuple(pl.BlockSpec((1, L, w), lambda b, st, fl=cidx(d)[1]: (b, fl(st), 0)) for d in range(2))
    return pl.pallas_call(
        functools.partial(_mlstm_kernel, nc, nl, L, dh),
        out_shape=(jax.ShapeDtypeStruct((b_, s_, w), BF16),) * 2,
        grid=(b_, steps),
        in_specs=in_specs,
        out_specs=out_specs,
        scratch_shapes=[pltpu.VMEM((2 * nh, dh, 2 * dh), F32), pltpu.VMEM((2, 8, LANES), F32)],
        compiler_params=_cparams(("parallel", "arbitrary")),
        name="mlstm",
    )(*args)


SUBLANES = 8
LRU_BLK = SUBLANES * SUBLANES


def _rglru_kernel(n_ctx, n_tot, ux_ref, cw_ref, cb_ref, w_ref, b_ref, lam_ref, out_ref,
                  us, z_nat, a_f, b_f, a_b, b_b, h_f, h_b, o_nat):
    c = LANES
    g = SUBLANES
    n_lat = n_tot - n_ctx
    nq = n_tot // g
    nblk = n_tot // LRU_BLK
    nblk_ctx = n_ctx // LRU_BLK

    zeros_g = jnp.zeros((g, c), F32)
    us[0:g, :] = zeros_g
    us[g + n_ctx:2 * g + n_ctx, :] = zeros_g
    us[2 * g + n_tot:3 * g + n_tot, :] = zeros_g
    us[g:g + n_ctx, :] = ux_ref[0, 0:n_ctx, :].astype(F32)
    us[2 * g + n_ctx:2 * g + n_tot, :] = ux_ref[0, n_ctx:n_tot, :].astype(F32)

    def conv(base, n):
        acc = cb_ref[...] + cw_ref[2:3, :] * us[base:base + n, :]
        for j, off in ((0, -2), (1, -1), (3, 1)):
            acc = acc + cw_ref[j:j + 1, :] * us[base + off:base + off + n, :]
        return acc

    z_nat[0:n_ctx, :] = conv(g, n_ctx)
    z_nat[n_ctx:n_tot, :] = conv(2 * g + n_ctx, n_lat)

    z = jnp.concatenate([z_nat[pl.ds(k, nq, stride=g), :] for k in range(g)], axis=0)
    t = jnp.tanh(_dot(z.astype(BF16), w_ref[0]) + b_ref[...])
    for d, (a_scr, b_scr) in enumerate(((a_f, b_f), (a_b, b_b))):
        t_r = t[:, (2 * d) * c:(2 * d + 1) * c]
        i = 0.5 * t[:, (2 * d + 1) * c:(2 * d + 2) * c] + 0.5
        lam = lam_ref[d:d + 1, :]
        softplus = jnp.maximum(-lam, 0.0) + jnp.log(1.0 + jnp.exp(-jnp.abs(lam)))
        half_k = (-0.5 * LRU_C * math.log2(math.e)) * softplus
        a = jnp.exp2(half_k * t_r + half_k)
        om = 1.0 - a * a
        mult = jnp.where(om > 0.0, om * lax.rsqrt(om), 0.0)
        a_scr[...] = a
        b_scr[...] = mult * i * z

    sub = lax.broadcasted_iota(jnp.int32, (g, c), 0)

    def block_scan(a_scr, b_scr, h_scr, blk, carry, reverse):
        r0 = pl.multiple_of(blk * g, g)
        order = range(g - 1, -1, -1) if reverse else range(g)
        hs, ps = {}, {}
        h = p_ = None
        for k in order:
            a = a_scr[pl.ds(k * nq + r0, g), :]
            b = b_scr[pl.ds(k * nq + r0, g), :]
            h = b if h is None else a * h + b
            p_ = a if p_ is None else a * p_
            hs[k], ps[k] = h, p_
        def shift(x, sft, fill):
            if reverse:
                return jnp.where(sub < g - sft, pltpu.roll(x, g - sft, 0), fill)
            return jnp.where(sub >= sft, pltpu.roll(x, sft, 0), fill)
        pe, he = shift(p_, 1, 1.0), shift(h, 1, 0.0)
        for sft in (1, 2, 4):
            he = pe * shift(he, sft, 0.0) + he
            pe = pe * shift(pe, sft, 1.0)
        cin = pe * carry + he
        for k in order:
            h_scr[pl.ds(k * nq + r0, g), :] = hs[k] + ps[k] * cin
        last = 0 if reverse else g - 1
        tot = p_ * cin + h
        return tot[last:last + 1, :]

    def body(it, carry):
        cf, cb = carry
        cf = block_scan(a_f, b_f, h_f, it, cf, False)
        jb = jnp.where(it < nblk_ctx, nblk_ctx - 1 - it, nblk - 1 + nblk_ctx - it)
        cb = block_scan(a_b, b_b, h_b, jb, cb, True)
        return cf, cb

    zero = jnp.zeros((1, c), F32)
    lax.fori_loop(0, nblk, body, (zero, zero), unroll=2)

    q_ctx = n_ctx // g
    for k in range(g):
        o_nat[pl.ds(k, n_lat // g, stride=g), :] = (h_f[k * nq + q_ctx:(k + 1) * nq, :]
                                                    + h_b[k * nq + q_ctx:(k + 1) * nq, :])
    out_ref[0] = o_nat[...].astype(BF16)


def _rglru(ux_seq, conv_w, conv_b, w_cat, b_cat, lam, n_ctx):
    b_, n_tot, w = ux_seq.shape
    nb = LRU_BLOCKS
    c = w // nb
    return pl.pallas_call(
        functools.partial(_rglru_kernel, n_ctx, n_tot),
        out_shape=jax.ShapeDtypeStruct((b_, n_tot - n_ctx, w), BF16),
        grid=(b_, nb),
        in_specs=[
            pl.BlockSpec((1, n_tot, c), lambda b, k: (b, 0, k)),
            pl.BlockSpec((4, c), lambda b, k: (0, k)),
            pl.BlockSpec((1, c), lambda b, k: (0, k)),
            pl.BlockSpec((1, c, 4 * c), lambda b, k: (k, 0, 0)),
            pl.BlockSpec((1, 4 * c), lambda b, k: (0, k)),
            pl.BlockSpec((2, c), lambda b, k: (0, k)),
        ],
        out_specs=pl.BlockSpec((1, n_tot - n_ctx, c), lambda b, k: (b, 0, k)),
        scratch_shapes=([pltpu.VMEM((n_tot + 3 * SUBLANES, c), F32)] + [pltpu.VMEM((n_tot, c), F32)] * 7
                        + [pltpu.VMEM((n_tot - n_ctx, c), F32)]),
        compiler_params=_cparams(("parallel", "parallel")),
        name="rglru",
    )(ux_seq, conv_w, conv_b, w_cat, b_cat, lam)


def _outproj_kernel(yaf_ref, yab_ref, hr_ref, ug_ref, ga0_ref, ga1_ref, gb0_ref, gb1_ref, x_ref,
                    g1_ref, gain_ref, shift_ref, wpa_ref, wpb_ref, wout_ref, wr_ref,
                    x1_ref, h2_ref, lg_ref):
    ya = (yaf_ref[0].astype(F32) + yab_ref[0].astype(F32)).astype(BF16)
    ug = ug_ref[0].astype(F32)
    gelu = 0.5 * ug * (1.0 + jnp.tanh(0.7978845608028654 * (ug + 0.044715 * ug * ug * ug)))
    yb = (hr_ref[0].astype(F32) * gelu).astype(BF16)
    pa = _dot(ya, wpa_ref[...])
    pb = _dot(yb, wpb_ref[...])
    ga = jnp.concatenate([ga0_ref[0], ga1_ref[0]], axis=1).astype(F32)
    gb = jnp.concatenate([gb0_ref[0], gb1_ref[0]], axis=1).astype(F32)
    mix = (_sigmoid(ga) * pa + _sigmoid(gb) * pb).astype(BF16)
    x1 = x_ref[0] + g1_ref[0] * _dot(mix, wout_ref[...])
    x1_ref[0] = x1
    ms = jnp.mean(x1 * x1, axis=-1, keepdims=True)
    h2 = x1 * lax.rsqrt(ms + EPS) * gain_ref[0] + shift_ref[0]
    h2_ref[...] = h2
    wr = wr_ref[...]
    h1, h2b, _ = _split3(h2)
    w1, w2b, _ = _split3(wr)
    lg_ref[0] = _route_rows(_dot(h1, w1) + (_dot(h1, w2b) + _dot(h2b, w1)))


def _outproj(ya_f, ya_b, hr, u_lat, x, g1, gain2, shift2, w_pa, w_pb, w_out, w_r, tm):
    b_, s_, d = x.shape
    w = ya_f.shape[2]
    row = lambda b, i: (b, i, 0)
    const2 = lambda b, i: (0, 0)
    bmap = lambda b, i: (b, 0, 0)
    ucol = lambda t: (lambda b, i: (b, i, t))
    single = pl.Buffered(1)
    return pl.pallas_call(
        _outproj_kernel,
        out_shape=(jax.ShapeDtypeStruct((b_, s_, d), F32),
                   jax.ShapeDtypeStruct((b_ * s_, d), F32),
                   jax.ShapeDtypeStruct((b_, s_, LANES), F32)),
        grid=(b_, s_ // tm),
        in_specs=[
            pl.BlockSpec((1, tm, w), row), pl.BlockSpec((1, tm, w), row), pl.BlockSpec((1, tm, w), row),
            pl.BlockSpec((1, tm, w), ucol(COL_UG)),
            pl.BlockSpec((1, tm, w), ucol(COL_GA)), pl.BlockSpec((1, tm, w), ucol(COL_GA + 1)),
            pl.BlockSpec((1, tm, w), ucol(COL_GB)), pl.BlockSpec((1, tm, w), ucol(COL_GB + 1)),
            pl.BlockSpec((1, tm, d), row),
            pl.BlockSpec((1, 1, d), bmap), pl.BlockSpec((1, 1, d), bmap), pl.BlockSpec((1, 1, d), bmap),
            pl.BlockSpec((w, d), const2, pipeline_mode=single),
            pl.BlockSpec((w, d), const2, pipeline_mode=single),
            pl.BlockSpec((d, d), const2, pipeline_mode=single),
            pl.BlockSpec((d, LANES), const2, pipeline_mode=single),
        ],
        out_specs=(pl.BlockSpec((1, tm, d), row),
                   pl.BlockSpec((tm, d), lambda b, i: (b * (s_ // tm) + i, 0)),
                   pl.BlockSpec((1, tm, LANES), row)),
        compiler_params=_cparams(("parallel", "parallel")),
        name="outproj",
    )(ya_f, ya_b, hr, u_lat, u_lat, u_lat, u_lat, u_lat, x, g1, gain2, shift2, w_pa, w_pb, w_out, w_r)


def _route_rows(lg):
    lane = lax.broadcasted_iota(jnp.int32, lg.shape, 1)
    neg = -jnp.inf
    big = jnp.int32(1 << 20)
    g_l = jnp.where(lane < N_GROUPS, lg, neg)
    g_max = jnp.max(g_l, axis=-1, keepdims=True)
    g_sel = jnp.min(jnp.where(g_l == g_max, lane, big), axis=-1, keepdims=True)
    p_g = 1.0 / jnp.sum(jnp.exp(g_l - g_max), axis=-1, keepdims=True)
    lo = N_GROUPS + g_sel * EXPERTS_PER_GROUP
    e_l = jnp.where((lane >= lo) & (lane < lo + EXPERTS_PER_GROUP), lg, neg)
    v1 = jnp.max(e_l, axis=-1, keepdims=True)
    i1 = jnp.min(jnp.where(e_l == v1, lane, big), axis=-1, keepdims=True)
    e_l2 = jnp.where(lane == i1, neg, e_l)
    v2 = jnp.max(e_l2, axis=-1, keepdims=True)
    i2 = jnp.min(jnp.where(e_l2 == v2, lane, big), axis=-1, keepdims=True)
    e2 = jnp.exp(v2 - v1)
    w1 = p_g / (1.0 + e2)
    w2 = p_g * e2 / (1.0 + e2)
    return jnp.where(lane == 0, (i1 - N_GROUPS).astype(F32),
           jnp.where(lane == 1, (i2 - N_GROUPS).astype(F32),
           jnp.where(lane == 2, w1, jnp.where(lane == 3, w2, 0.0))))


MOE_DMA_UNROLL = 32


MOE_W_DEPTH = 3
MOE_W_SPLIT = 2


def _moe_kernel(R, SUB, nf, ni, n_tok, item_e, item_rows, item_base, order, h2_hbm, w1_hbm, w3_hbm, w2_hbm,
                y_hbm, x_scr, acc_scr, y_scr, w1r, w3r, w2r, gsem, ssem, wsem):
    i = pl.program_id(0)
    j = pl.program_id(1)
    nsub_of = lambda r: (r + SUB - 1) // SUB
    rows = item_rows[i]
    nsub = nsub_of(rows)
    i_next = jnp.minimum(i + 1, ni - 1)
    nsub_next = jnp.where(i + 1 < ni, nsub_of(item_rows[i_next]), 0)
    rows_prev = jnp.where(i > 0, item_rows[jnp.maximum(i - 1, 0)], 0)
    base = item_base[i]
    base_next = item_base[i_next]
    n_assign = order.shape[0]
    buf = i % 2
    groups = SUB // MOE_DMA_UNROLL

    def assignment(b0, r):
        return order[jnp.minimum(b0 + r, n_assign - 1)]

    d_model, tf = w1r.shape[1], w1r.shape[2]

    def weight_copies(c):
        it = c // nf
        jc = c % nf
        e = item_e[jnp.minimum(it, ni - 1)]
        slot = c % MOE_W_DEPTH
        f0 = pl.multiple_of(jc * tf, tf)
        out = []
        for h in range(MOE_W_SPLIT):
            ra, rb = d_model // MOE_W_SPLIT, tf // MOE_W_SPLIT
            out.append(pltpu.make_async_copy(w1_hbm.at[e, pl.ds(h * ra, ra), pl.ds(f0, tf)],
                                             w1r.at[slot, pl.ds(h * ra, ra), :], wsem.at[slot]))
            out.append(pltpu.make_async_copy(w3_hbm.at[e, pl.ds(h * ra, ra), pl.ds(f0, tf)],
                                             w3r.at[slot, pl.ds(h * ra, ra), :], wsem.at[slot]))
            out.append(pltpu.make_async_copy(w2_hbm.at[e, pl.ds(f0 + h * rb, rb), :],
                                             w2r.at[slot, pl.ds(h * rb, rb), :], wsem.at[slot]))
        return out

    def chunk_live(c):
        it = c // nf
        return (it < ni) & (item_rows[jnp.minimum(it, ni - 1)] > 0)

    cur = i * nf + j

    @pl.when((cur == 0) & chunk_live(0))
    def _():
        for cp in weight_copies(0):
            cp.start()

        @pl.when(chunk_live(1))
        def _():
            for cp in weight_copies(1):
                cp.start()

    @pl.when((rows > 0) & chunk_live(cur + 2))
    def _():
        for cp in weight_copies(cur + 2):
            cp.start()

    def tok_rows(ref, r):
        return ref.at[pl.ds(r, 1), :]

    def gather_group(b0, b, gi):
        r0 = pl.multiple_of(gi * MOE_DMA_UNROLL, MOE_DMA_UNROLL)
        for k in range(MOE_DMA_UNROLL):
            tok = lax.shift_right_logical(assignment(b0, r0 + k), TOP_K_SHIFT)
            pltpu.make_async_copy(tok_rows(h2_hbm, tok), tok_rows(x_scr.at[b], r0 + k),
                                  gsem.at[b]).start(priority=k % 2)

    def gather_wait_block(b):
        pltpu.make_async_copy(h2_hbm.at[pl.ds(0, SUB), :], x_scr.at[b, pl.ds(0, SUB), :], gsem.at[b]).wait()

    def scatter_copy(b0, r):
        a = assignment(b0, r)
        dst = (a & (TOP_K - 1)) * n_tok + lax.shift_right_logical(a, TOP_K_SHIFT)
        return pltpu.make_async_copy(tok_rows(y_scr, r), tok_rows(y_hbm, dst), ssem)

    def scatter_group(b0, gi):
        r0 = pl.multiple_of(gi * MOE_DMA_UNROLL, MOE_DMA_UNROLL)
        for k in range(MOE_DMA_UNROLL):
            scatter_copy(b0, r0 + k).start(priority=k % 2)

    def repeat(n, fn, lo=0):
        def body(q, c):
            fn(q)
            return c
        lax.fori_loop(lo, n, body, 0)

    def scatter_rows(b0, n):
        full = n // MOE_DMA_UNROLL
        repeat(full, lambda gi: scatter_group(b0, gi))
        repeat(n, lambda r: scatter_copy(b0, r).start(), lo=full * MOE_DMA_UNROLL)

    def scatter_wait_rows(n):
        u = MOE_DMA_UNROLL
        full = n // MOE_DMA_UNROLL
        repeat(full, lambda q: pltpu.make_async_copy(y_scr.at[pl.ds(0, u), :], y_hbm.at[pl.ds(0, u), :],
                                                      ssem).wait())
        repeat(n, lambda r: scatter_copy(0, 0).wait(), lo=full * MOE_DMA_UNROLL)

    @pl.when((i == 0) & (j == 0))
    def _():
        repeat(nsub * groups, lambda gi: gather_group(base, 0, gi))

    bps = -(-(R // SUB) // nf)
    @pl.when(j * bps < nsub_next)
    def _():
        repeat(jnp.minimum((j + 1) * bps, nsub_next) * groups, lambda gi: gather_group(base_next, 1 - buf, gi),
               lo=j * bps * groups)

    @pl.when(j == 0)
    def _():
        repeat(nsub, lambda q: gather_wait_block(buf))

    @pl.when(j == nf - 1)
    def _():
        scatter_wait_rows(rows_prev)

    @pl.when(rows > 0)
    def _():
        for cp in weight_copies(cur):
            cp.wait()
        slot = cur % MOE_W_DEPTH

        def experts(m):
            def run():
                x = x_scr[buf, 0:m, :].astype(BF16)
                h1 = _dot(x, w1r[slot].astype(BF16))
                h3 = _dot(x, w3r[slot].astype(BF16))
                hh = (h1 * _sigmoid(h1) * h3).astype(BF16)
                contrib = _dot(hh, w2r[slot].astype(BF16))

                @pl.when(j == 0)
                def _():
                    acc_scr[0:m, :] = contrib

                @pl.when((j > 0) & (j < nf - 1))
                def _():
                    acc_scr[0:m, :] += contrib

                @pl.when(j == nf - 1)
                def _():
                    y_scr[0:m, :] = acc_scr[0:m, :] + contrib
            return run
        lax.switch(nsub - 1, [experts((q + 1) * SUB) for q in range(R // SUB)])

    @pl.when(j == nf - 1)
    def _():
        scatter_rows(base, rows)

        @pl.when(i == ni - 1)
        def _():
            scatter_wait_rows(rows)


def _moe(h2, item_e, item_rows, item_base, order, w1, w3, w2):
    t_ = h2.shape[0]
    ne, d, f = w1.shape
    R, SUB, tf = MOE_ITEM_ROWS, MOE_SUB_ROWS, MOE_F_CHUNK
    ni = item_e.shape[0]
    nf = f // tf
    assert R % SUB == 0 and SUB % MOE_DMA_UNROLL == 0 and nf >= 2
    grid_spec = pltpu.PrefetchScalarGridSpec(
        num_scalar_prefetch=4,
        grid=(ni, nf),
        in_specs=[
            pl.BlockSpec(memory_space=pl.ANY),
            pl.BlockSpec(memory_space=pl.ANY),
            pl.BlockSpec(memory_space=pl.ANY),
            pl.BlockSpec(memory_space=pl.ANY),
        ],
        out_specs=pl.BlockSpec(memory_space=pl.ANY),
        scratch_shapes=[
            pltpu.VMEM((2, R, d), F32), pltpu.VMEM((R, d), F32), pltpu.VMEM((R, d), F32),
            pltpu.VMEM((MOE_W_DEPTH, d, tf), F32), pltpu.VMEM((MOE_W_DEPTH, d, tf), F32),
            pltpu.VMEM((MOE_W_DEPTH, tf, d), F32),
            pltpu.SemaphoreType.DMA((2,)), pltpu.SemaphoreType.DMA(()), pltpu.SemaphoreType.DMA((MOE_W_DEPTH,)),
        ],
    )
    return pl.pallas_call(
        functools.partial(_moe_kernel, R, SUB, nf, ni, t_),
        out_shape=jax.ShapeDtypeStruct((TOP_K * t_, d), F32),
        grid_spec=grid_spec,
        compiler_params=_cparams(("arbitrary", "arbitrary")),
        name="moe_experts",
    )(item_e, item_rows, item_base, order, h2, w1, w3, w2)


def _moe_plan(expert):
    R = MOE_ITEM_ROWS
    na = expert.shape[0]
    ni = N_EXPERTS + na // R
    order = jnp.argsort(expert, stable=True).astype(jnp.int32)
    eids = jnp.arange(N_EXPERTS, dtype=jnp.int32)
    counts = jnp.sum((expert[:, None] == eids[None, :]).astype(jnp.int32), axis=0)
    first = jnp.cumsum(counts) - counts
    n_items = (counts + R - 1) // R
    item_end = jnp.cumsum(n_items)
    item_start = item_end - n_items
    ii = jnp.arange(ni, dtype=jnp.int32)
    e_of = jnp.minimum(jnp.sum((item_end[None, :] <= ii[:, None]).astype(jnp.int32), axis=1), N_EXPERTS - 1)
    oh_e = (e_of[:, None] == eids[None, :]).astype(jnp.int32)
    pick = lambda v: jnp.sum(oh_e * v[None, :], axis=1)
    part = ii - pick(item_start)
    rows = jnp.clip(pick(counts) - part * R, 0, R).astype(jnp.int32)
    base = (pick(first) + part * R).astype(jnp.int32)
    live = ii < item_end[-1]
    rows = jnp.where(live, rows, 0)
    base = jnp.where(live, base, 0)
    last_e = jnp.max(jnp.where(live, e_of, 0))
    e_of = jnp.where(live, e_of, last_e).astype(jnp.int32)
    return e_of, rows, base, order


def _final_kernel(x1_ref, y0_ref, y1_ref, rt_ref, g2_ref, fg_ref, o_ref):
    rt = rt_ref[0]
    y = rt[:, TOP_K:TOP_K + 1] * y0_ref[...] + rt[:, TOP_K + 1:TOP_K + 2] * y1_ref[...]
    x = x1_ref[0] + g2_ref[0] * y
    ms = jnp.mean(x * x, axis=-1, keepdims=True)
    o_ref[0] = x * lax.rsqrt(ms + EPS) * fg_ref[...]


def _final(x1, ybuf, routed, g2, final_g, tm):
    b_, s_, d = x1.shape
    nt = s_ // tm
    return pl.pallas_call(
        _final_kernel,
        out_shape=jax.ShapeDtypeStruct((b_, s_, d), F32),
        grid=(b_, nt),
        in_specs=[
            pl.BlockSpec((1, tm, d), lambda b, i: (b, i, 0)),
            pl.BlockSpec((tm, d), lambda b, i: (b * nt + i, 0)),
            pl.BlockSpec((tm, d), lambda b, i: (b_ * nt + b * nt + i, 0)),
            pl.BlockSpec((1, tm, LANES), lambda b, i: (b, i, 0)),
            pl.BlockSpec((1, 1, d), lambda b, i: (b, 0, 0)),
            pl.BlockSpec((1, d), lambda b, i: (0, 0)),
        ],
        out_specs=pl.BlockSpec((1, tm, d), lambda b, i: (b, i, 0)),
        compiler_params=_cparams(("parallel", "parallel")),
        name="final_norm",
    )(x1, ybuf, ybuf, routed, g2, final_g)


def kernel(x, c, ctx, c_ctx, w_mod, b_mod, norm1_g, w_in, b_gates, mlstm_head_g, conv_w, conv_b, lru_wa, lru_ba,
           lru_wx, lru_bx, lru_lam, w_pa, w_pb, w_out, norm2_g, w_rg, w_re, w1, w3, w2, final_g):
    b_, s_, d = x.shape
    n_ctx = ctx.shape[1]
    rows = s_ // GRID_W
    assert w_mod.shape[0] == 1, "single layer"
    wm = d // 2
    nh = MLSTM_HEADS

    cc = jnp.zeros((8, d), F32).at[:b_].set(c).at[b_].set(c_ctx)
    mod = _modulation(cc, w_mod[0], b_mod[0][None, :])
    sh1, sc1, g1, sh2, sc2, g2 = [mod[:b_, i * d:(i + 1) * d][:, None, :] for i in range(6)]
    csh1, csc1 = mod[b_:b_ + 1, 0:d][:, None, :], mod[b_:b_ + 1, d:2 * d][:, None, :]

    assert wm == 1024 and 4 * nh == N_GATE_COLS
    w_t = jnp.transpose(w_in[0])
    bg_pad = jnp.zeros((1, LANES), F32).at[0, :4 * nh].set(b_gates[0])

    ng = norm1_g[0][None, None, :]
    u_lat, g_lat = _inproj(x, ng * (1.0 + sc1), sh1, w_t, bg_pad, tuple(range(N_COL_TILES)), min(1024, s_))
    u_ctx, g_ctx = _inproj(ctx.reshape(1, b_ * n_ctx, d), ng * (1.0 + csc1), csh1, w_t, bg_pad,
                           (COL_Q, COL_K, COL_V, COL_UX), b_ * n_ctx)
    u_ctx = u_ctx.reshape(b_, n_ctx, -1)
    g_ctx = g_ctx.reshape(b_, n_ctx, LANES)

    ya_f, ya_b = _mlstm(u_lat, u_ctx, g_lat, g_ctx, mlstm_head_g[0], MLSTM_CHUNK)

    ux_lat = u_lat[:, :, COL_UX * wm:(COL_UX + 1) * wm]
    ux_col = ux_lat.reshape(b_, rows, GRID_W, wm).transpose(0, 2, 1, 3).reshape(b_, s_, wm)
    ux_seq = jnp.concatenate([u_ctx[:, :, CTX_UX * wm:(CTX_UX + 1) * wm], ux_col], axis=1)
    wa, wx = lru_wa[0], lru_wx[0]
    w_cat = (0.5 * jnp.concatenate([wa[0], wx[0], wa[1], wx[1]], axis=-1)).astype(BF16)
    cblk = wm // LRU_BLOCKS
    blk = lambda v: v.reshape(LRU_BLOCKS, cblk)
    b_cat = 0.5 * jnp.concatenate([blk(lru_ba[0, 0]), blk(lru_bx[0, 0]), blk(lru_ba[0, 1]), blk(lru_bx[0, 1])],
                                  axis=-1).reshape(1, 4 * wm)
    h_col = _rglru(ux_seq, conv_w[0], conv_b[0][None, :], w_cat, b_cat, lru_lam[0], n_ctx)
    hr = h_col.reshape(b_, GRID_W, rows, wm).transpose(0, 2, 1, 3).reshape(b_, s_, wm)

    w_r = jnp.zeros((d, LANES), F32).at[:, :N_GROUPS].set(w_rg[0]).at[:, N_GROUPS:N_GROUPS + N_EXPERTS].set(w_re[0])
    n2 = norm2_g[0][None, None, :]
    x1, h2, routed = _outproj(ya_f, ya_b, hr, u_lat, x, g1, n2 * (1.0 + sc2), sh2,
                              w_pa[0].astype(BF16), w_pb[0].astype(BF16), w_out[0].astype(BF16), w_r, 256)

    t_ = b_ * s_
    expert = routed.reshape(t_, LANES)[:, :TOP_K].astype(jnp.int32).reshape(-1)
    item_e, item_rows, item_base, order = _moe_plan(expert)
    ybuf = _moe(h2, item_e, item_rows, item_base, order, w1[0], w3[0], w2[0])

    return _final(x1, ybuf, routed, g2, final_g[None, :], 256)
```

```python
import functools
import math

import jax
import jax.numpy as jnp
from jax import lax
from jax.experimental import pallas as pl
from jax.experimental.pallas import tpu as pltpu

F32 = jnp.float32
BF16 = jnp.bfloat16

EPS = 1e-6
GRID_W = 64
MLSTM_HEADS = 8
LRU_BLOCKS = 8
LRU_C = 8.0
N_GROUPS = 4
EXPERTS_PER_GROUP = 8
N_EXPERTS = N_GROUPS * EXPERTS_PER_GROUP
TOP_K = 2

VMEM_LIMIT_BYTES = 56 * 1024 * 1024
LANES = 128

MLSTM_CHUNK = 128
MOE_ITEM_ROWS = 768
MOE_SUB_ROWS = 128
MOE_F_CHUNK = 256


def _cparams(sem):
    return pltpu.CompilerParams(dimension_semantics=sem, vmem_limit_bytes=VMEM_LIMIT_BYTES)


def _sigmoid(x):
    return 0.5 * jnp.tanh(0.5 * x) + 0.5


def _dot(a, b):
    return jnp.dot(a, b, preferred_element_type=F32)


def _split3(x):
    x1 = x.astype(BF16)
    r1 = x - x1.astype(F32)
    x2 = r1.astype(BF16)
    x3 = (r1 - x2.astype(F32)).astype(BF16)
    return x1, x2, x3


def _mod_kernel(c_ref, w_ref, b_ref, o_ref):
    c = c_ref[...]
    s = (c * _sigmoid(c)).astype(BF16)
    o_ref[...] = _dot(s, w_ref[...].astype(BF16)) + b_ref[...]


def _modulation(cc, w_mod, b_mod):
    m, d = cc.shape
    n = w_mod.shape[1]
    tn = 1024
    return pl.pallas_call(
        _mod_kernel,
        out_shape=jax.ShapeDtypeStruct((m, n), F32),
        grid=(n // tn,),
        in_specs=[
            pl.BlockSpec((m, d), lambda j: (0, 0)),
            pl.BlockSpec((d, tn), lambda j: (0, j)),
            pl.BlockSpec((1, tn), lambda j: (0, j)),
        ],
        out_specs=pl.BlockSpec((m, tn), lambda j: (0, j)),
        compiler_params=_cparams(("parallel",)),
        name="modulation",
    )(cc, w_mod, b_mod)


N_PRE_TILES = 5
N_GATE_COLS = 32


def _dot_nt(a, b):
    return lax.dot_general(a, b, (((1,), (1,)), ((), ())), preferred_element_type=F32)


def _inproj_kernel(x_ref, gain_ref, shift_ref, wt_ref, wg_ref, bg_ref, u_ref, g_ref, h_scr):
    @pl.when(pl.program_id(2) == 0)
    def _():
        x = x_ref[0]
        ms = jnp.mean(x * x, axis=-1, keepdims=True)
        h = x * lax.rsqrt(ms + EPS) * gain_ref[0] + shift_ref[0]
        hb = h.astype(BF16)
        h_scr[...] = hb
        gates = _dot_nt(hb, wg_ref[...].astype(BF16))
        pad = jnp.zeros((gates.shape[0], LANES - N_GATE_COLS), F32)
        g_ref[0] = jnp.concatenate([gates, pad], axis=1) + bg_ref[...]

    u_ref[0] = _dot_nt(h_scr[...], wt_ref[...].astype(BF16)).astype(BF16)


def _inproj(x, gain, shift, w_t, b_gates, tiles, tm):
    b_, s_, d = x.shape
    tn = 1024
    nj = len(tiles)
    if tiles == tuple(range(nj)):
        tile_of = lambda j: j
    else:
        assert tiles == (COL_Q, COL_K, COL_V, COL_UX), tiles
        tile_of = lambda j: jnp.where(j == 3, COL_UX, j)
    row_of = lambda j: tile_of(j) * tn + jnp.where(tile_of(j) >= N_PRE_TILES, N_GATE_COLS, 0)
    per_batch_mod = gain.shape[0] > 1
    mmap = (lambda b, i, j: (b, 0, 0)) if per_batch_mod else (lambda b, i, j: (0, 0, 0))
    gate_blk = N_PRE_TILES * tn // N_GATE_COLS
    return pl.pallas_call(
        _inproj_kernel,
        out_shape=(jax.ShapeDtypeStruct((b_, s_, nj * tn), BF16),
                   jax.ShapeDtypeStruct((b_, s_, LANES), F32)),
        grid=(b_, s_ // tm, nj),
        in_specs=[
            pl.BlockSpec((1, tm, d), lambda b, i, j: (b, i, 0)),
            pl.BlockSpec((1, 1, d), mmap),
            pl.BlockSpec((1, 1, d), mmap),
            pl.BlockSpec((pl.Element(tn), pl.Element(d)), lambda b, i, j: (pl.multiple_of(row_of(j), 8), 0)),
            pl.BlockSpec((N_GATE_COLS, d), lambda b, i, j: (gate_blk, 0)),
            pl.BlockSpec((1, LANES), lambda b, i, j: (0, 0)),
        ],
        out_specs=(pl.BlockSpec((1, tm, tn), lambda b, i, j: (b, i, j)),
                   pl.BlockSpec((1, tm, LANES), lambda b, i, j: (b, i, 0))),
        scratch_shapes=[pltpu.VMEM((tm, d), BF16)],
        compiler_params=_cparams(("parallel", "parallel", "arbitrary")),
        name="inproj",
    )(x, gain, shift, w_t, w_t, b_gates)


COL_Q, COL_K, COL_V, COL_OF, COL_OB, COL_UX, COL_UG, COL_GA, COL_GB = 0, 1, 2, 3, 4, 5, 6, 7, 9
CTX_UX = 3
N_COL_TILES = 11


def _mlstm_kernel(nc, nl, L, dh,
                  qc_f, kc_f, vc_f, ql_f, kl_f, vl_f, gc_f, gl_f, o_f,
                  qc_b, kc_b, vc_b, ql_b, kl_b, vl_b, gc_b, gl_b, o_b,
                  hg_ref, tri_ref,
                  out_f, out_b, c_scr, m_scr):
    s = pl.program_id(1)
    nh = MLSTM_HEADS
    scale = dh ** -0.5

    @pl.when(s == 0)
    def _():
        c_scr[...] = jnp.zeros_like(c_scr)
        m_scr[...] = jnp.full_like(m_scr, -1e30)

    is_ctx = s < nc
    ones_col = (lax.broadcasted_iota(jnp.int32, (L, dh), 1) == 0).astype(BF16)
    row = lax.broadcasted_iota(jnp.int32, (L, L), 0)
    col = lax.broadcasted_iota(jnp.int32, (L, L), 1)

    for d, (qc, kc, vc, ql, kl, vl, gc, gl, o_ref, out_ref) in enumerate((
            (qc_f, kc_f, vc_f, ql_f, kl_f, vl_f, gc_f, gl_f, o_f, out_f),
            (qc_b, kc_b, vc_b, ql_b, kl_b, vl_b, gc_b, gl_b, o_b, out_b))):
        q = jnp.where(is_ctx, qc[0], ql[0])
        k = jnp.where(is_ctx, kc[0], kl[0])
        v = jnp.where(is_ctx, vc[0], vl[0])
        g = jnp.where(is_ctx, gc[0], gl[0])
        gt = g.T
        ls = jnp.minimum(g, 0.0) - jnp.log(1.0 + jnp.exp(-jnp.abs(g)))
        lst = jnp.minimum(gt, 0.0) - jnp.log(1.0 + jnp.exp(-jnp.abs(gt)))
        tri = tri_ref[d]
        mask = (col <= row) if d == 0 else (col >= row)
        l1, l2, l3 = _split3(ls)
        bcol_all = _dot(tri, l1) + _dot(tri, l2) + _dot(tri, l3)
        t1, t2, t3 = _split3(lst)
        trit = tri_ref[1 - d]
        brow_all = _dot(t1, trit) + _dot(t2, trit) + _dot(t3, trit)
        last = L - 1 if d == 0 else 0

        bal = pltpu.roll(bcol_all, LANES - 8, axis=1)
        cmax = g - bal
        trow = lax.broadcasted_iota(jnp.int32, (L, LANES), 0)
        sft = 1
        while sft < L:
            if d == 0:
                shifted = jnp.where(trow >= sft, pltpu.roll(cmax, sft, axis=0), -jnp.inf)
            else:
                shifted = jnp.where(trow < L - sft, pltpu.roll(cmax, L - sft, axis=0), -jnp.inf)
            cmax = jnp.maximum(cmax, shifted)
            sft *= 2
        m_row = m_scr[d][0:1, :]
        gg = bal + m_row
        mt = jnp.maximum(gg, bal + cmax)
        e_col = bal - mt
        w_inter = jnp.exp(gg - mt)
        enm = jnp.exp(-mt)
        tot = bal[last:last + 1, :]
        wlog = tot - bal + g
        m_new = jnp.maximum(tot + m_row, jnp.max(wlog, axis=0, keepdims=True))
        ws = jnp.exp(wlog - m_new) * scale
        decay = jnp.exp(tot + m_row - m_new)
        m_scr[d] = jnp.broadcast_to(m_new, m_scr.shape[1:])

        for h in range(nh):
            u = d * nh + h
            ci = 16 * d + h
            cf = 16 * d + 8 + h
            hs = slice(h * dh, (h + 1) * dh)
            qh, kh, vh = q[:, hs], k[:, hs], v[:, hs]
            vext = jnp.concatenate([vh, ones_col], axis=1)
            cr = gt[ci:ci + 1, :] - brow_all[cf:cf + 1, :] + math.log(scale)
            c_prev = c_scr[u]

            sqk = lax.dot_general(qh, kh, (((1,), (1,)), ((), ())), preferred_element_type=F32)
            w_intra = jnp.exp(jnp.where(mask, e_col[:, ci:ci + 1] + cr, -jnp.inf))
            a = (sqk * w_intra).astype(BF16)
            qs = (qh.astype(F32) * w_inter[:, ci:ci + 1]).astype(BF16)
            r = _dot(jnp.concatenate([a, qs], axis=1),
                     jnp.concatenate([vext, c_prev.astype(BF16)], axis=0))

            num = r[:, :dh]
            den = r[:, dh:dh + 1]
            rden = 1.0 / jnp.maximum(jnp.abs(den), enm[:, ci:ci + 1])
            ssq = jnp.sum(num * num, axis=-1, keepdims=True)
            sc = rden * lax.rsqrt(rden * rden * ssq * (1.0 / dh) + EPS)
            gate = _sigmoid(o_ref[0, :, hs].astype(F32))
            out_ref[0, :, hs] = (num * sc * hg_ref[h:h + 1, :] * gate).astype(BF16)

            kw = (kh.astype(F32) * ws[:, ci:ci + 1]).astype(BF16)
            upd = lax.dot_general(kw, vext, (((0,), (0,)), ((), ())), preferred_element_type=F32)
            c_scr[u] = decay[:, ci:ci + 1] * c_prev + upd


def _mlstm(u_lat, u_ctx, g_lat, g_ctx, head_g, L):
    b_, s_, _ = u_lat.shape
    n_ctx = u_ctx.shape[1]
    nh = MLSTM_HEADS
    w = 1024
    dh = w // nh
    nc, nl = n_ctx // L, s_ // L
    steps = nc + nl
    t0 = jnp.tril(jnp.ones((L, L), F32))
    tri = jnp.stack([t0, t0.T]).astype(BF16)

    def cidx(d):
        if d == 0:
            return lambda st: jnp.minimum(st, nc - 1), lambda st: jnp.maximum(st - nc, 0)
        return lambda st: jnp.maximum(nc - 1 - st, 0), lambda st: jnp.minimum(nl - 1 + nc - st, nl - 1)

    in_specs, args = [], []
    for d in range(2):
        fc, fl = cidx(d)
        for t in range(3):
            in_specs.append(pl.BlockSpec((1, L, w), lambda b, st, fc=fc, t=t: (b, fc(st), t)))
            args.append(u_ctx)
        for t in (COL_Q, COL_K, COL_V):
            in_specs.append(pl.BlockSpec((1, L, w), lambda b, st, fl=fl, t=t: (b, fl(st), t)))
            args.append(u_lat)
        in_specs.append(pl.BlockSpec((1, L, LANES), lambda b, st, fc=fc: (b, fc(st), 0)))
        args.append(g_ctx)
        in_specs.append(pl.BlockSpec((1, L, LANES), lambda b, st, fl=fl: (b, fl(st), 0)))
        args.append(g_lat)
        in_specs.append(pl.BlockSpec((1, L, w), lambda b, st, fl=fl, t=COL_OF + d: (b, fl(st), t)))
        args.append(u_lat)
    in_specs += [pl.BlockSpec((nh, dh), lambda b, st: (0, 0)),
                 pl.BlockSpec((2, L, L), lambda b, st: (0, 0, 0))]
    args += [head_g, tri]
    out_specs = tuple(pl.BlockSpec((1, L, w), lambda b, st, fl=cidx(d)[1]: (b, fl(st), 0)) for d in range(2))
    return pl.pallas_call(
        functools.partial(_mlstm_kernel, nc, nl, L, dh),
        out_shape=(jax.ShapeDtypeStruct((b_, s_, w), BF16),) * 2,
        grid=(b_, steps),
        in_specs=in_specs,
        out_specs=out_specs,
        scratch_shapes=[pltpu.VMEM((2 * nh, dh, 2 * dh), F32), pltpu.VMEM((2, 8, LANES), F32)],
        compiler_params=_cparams(("parallel", "arbitrary")),
        name="mlstm",
    )(*args)


SUBLANES = 8
LRU_BLK = SUBLANES * SUBLANES


def _rglru_kernel(n_ctx, n_tot, ux_ref, cw_ref, cb_ref, w_ref, b_ref, lam_ref, out_ref,
                  us, z_nat, a_f, b_f, a_b, b_b, h_f, h_b, o_nat):
    c = LANES
    g = SUBLANES
    n_lat = n_tot - n_ctx
    nq = n_tot // g
    nblk = n_tot // LRU_BLK
    nblk_ctx = n_ctx // LRU_BLK

    zeros_g = jnp.zeros((g, c), F32)
    us[0:g, :] = zeros_g
    us[g + n_ctx:2 * g + n_ctx, :] = zeros_g
    us[2 * g + n_tot:3 * g + n_tot, :] = zeros_g
    us[g:g + n_ctx, :] = ux_ref[0, 0:n_ctx, :].astype(F32)
    us[2 * g + n_ctx:2 * g + n_tot, :] = ux_ref[0, n_ctx:n_tot, :].astype(F32)

    def conv(base, n):
        acc = cb_ref[...] + cw_ref[2:3, :] * us[base:base + n, :]
        for j, off in ((0, -2), (1, -1), (3, 1)):
            acc = acc + cw_ref[j:j + 1, :] * us[base + off:base + off + n, :]
        return acc

    z_nat[0:n_ctx, :] = conv(g, n_ctx)
    z_nat[n_ctx:n_tot, :] = conv(2 * g + n_ctx, n_lat)

    z = jnp.concatenate([z_nat[pl.ds(k, nq, stride=g), :] for k in range(g)], axis=0)
    t = jnp.tanh(_dot(z.astype(BF16), w_ref[0]) + b_ref[...])
    for d, (a_scr, b_scr) in enumerate(((a_f, b_f), (a_b, b_b))):
        t_r = t[:, (2 * d) * c:(2 * d + 1) * c]
        i = 0.5 * t[:, (2 * d + 1) * c:(2 * d + 2) * c] + 0.5
        lam = lam_ref[d:d + 1, :]
        softplus = jnp.maximum(-lam, 0.0) + jnp.log(1.0 + jnp.exp(-jnp.abs(lam)))
        half_k = (-0.5 * LRU_C * math.log2(math.e)) * softplus
        a = jnp.exp2(half_k * t_r + half_k)
        om = 1.0 - a * a
        mult = jnp.where(om > 0.0, om * lax.rsqrt(om), 0.0)
        a_scr[...] = a
        b_scr[...] = mult * i * z

    sub = lax.broadcasted_iota(jnp.int32, (g, c), 0)

    def block_scan(a_scr, b_scr, h_scr, blk, carry, reverse):
        r0 = pl.multiple_of(blk * g, g)
        order = range(g - 1, -1, -1) if reverse else range(g)
        hs, ps = {}, {}
        h = p_ = None
        for k in order:
            a = a_scr[pl.ds(k * nq + r0, g), :]
            b = b_scr[pl.ds(k * nq + r0, g), :]
            h = b if h is None else a * h + b
            p_ = a if p_ is None else a * p_
            hs[k], ps[k] = h, p_
        def shift(x, sft, fill):
            if reverse:
                return jnp.where(sub < g - sft, pltpu.roll(x, g - sft, 0), fill)
            return jnp.where(sub >= sft, pltpu.roll(x, sft, 0), fill)
        pe, he = shift(p_, 1, 1.0), shift(h, 1, 0.0)
        for sft in (1, 2, 4):
            he = pe * shift(he, sft, 0.0) + he
            pe = pe * shift(pe, sft, 1.0)
        cin = pe * carry + he
        for k in order:
            h_scr[pl.ds(k * nq + r0, g), :] = hs[k] + ps[k] * cin
        last = 0 if reverse else g - 1
        tot = p_ * cin + h
        return tot[last:last + 1, :]

    def body(it, carry):
        cf, cb = carry
        cf = block_scan(a_f, b_f, h_f, it, cf, False)
        jb = jnp.where(it < nblk_ctx, nblk_ctx - 1 - it, nblk - 1 + nblk_ctx - it)
        cb = block_scan(a_b, b_b, h_b, jb, cb, True)
        return cf, cb

    zero = jnp.zeros((1, c), F32)
    lax.fori_loop(0, nblk, body, (zero, zero), unroll=2)

    q_ctx = n_ctx // g
    for k in range(g):
        o_nat[pl.ds(k, n_lat // g, stride=g), :] = (h_f[k * nq + q_ctx:(k + 1) * nq, :]
                                                    + h_b[k * nq + q_ctx:(k + 1) * nq, :])
    out_ref[0] = o_nat[...].astype(BF16)


def _rglru(ux_seq, conv_w, conv_b, w_cat, b_cat, lam, n_ctx):
    b_, n_tot, w = ux_seq.shape
    nb = LRU_BLOCKS
    c = w // nb
    return pl.pallas_call(
        functools.partial(_rglru_kernel, n_ctx, n_tot),
        out_shape=jax.ShapeDtypeStruct((b_, n_tot - n_ctx, w), BF16),
        grid=(b_, nb),
        in_specs=[
            pl.BlockSpec((1, n_tot, c), lambda b, k: (b, 0, k)),
            pl.BlockSpec((4, c), lambda b, k: (0, k)),
            pl.BlockSpec((1, c), lambda b, k: (0, k)),
            pl.BlockSpec((1, c, 4 * c), lambda b, k: (k, 0, 0)),
            pl.BlockSpec((1, 4 * c), lambda b, k: (0, k)),
            pl.BlockSpec((2, c), lambda b, k: (0, k)),
        ],
        out_specs=pl.BlockSpec((1, n_tot - n_ctx, c), lambda b, k: (b, 0, k)),
        scratch_shapes=([pltpu.VMEM((n_tot + 3 * SUBLANES, c), F32)] + [pltpu.VMEM((n_tot, c), F32)] * 7
                        + [pltpu.VMEM((n_tot - n_ctx, c), F32)]),
        compiler_params=_cparams(("parallel", "parallel")),
        name="rglru",
    )(ux_seq, conv_w, conv_b, w_cat, b_cat, lam)


def _outproj_kernel(yaf_ref, yab_ref, hr_ref, ug_ref, ga0_ref, ga1_ref, gb0_ref, gb1_ref, x_ref,
                    g1_ref, gain_ref, shift_ref, wpa_ref, wpb_ref, wout_ref, wr_ref,
                    x1_ref, h2_ref, lg_ref):
    ya = (yaf_ref[0].astype(F32) + yab_ref[0].astype(F32)).astype(BF16)
    ug = ug_ref[0].astype(F32)
    gelu = 0.5 * ug * (1.0 + jnp.tanh(0.7978845608028654 * (ug + 0.044715 * ug * ug * ug)))
    yb = (hr_ref[0].astype(F32) * gelu).astype(BF16)
    pa = _dot(ya, wpa_ref[...])
    pb = _dot(yb, wpb_ref[...])
    ga = jnp.concatenate([ga0_ref[0], ga1_ref[0]], axis=1).astype(F32)
    gb = jnp.concatenate([gb0_ref[0], gb1_ref[0]], axis=1).astype(F32)
    mix = (_sigmoid(ga) * pa + _sigmoid(gb) * pb).astype(BF16)
    x1 = x_ref[0] + g1_ref[0] * _dot(mix, wout_ref[...])
    x1_ref[0] = x1
    ms = jnp.mean(x1 * x1, axis=-1, keepdims=True)
    h2 = x1 * lax.rsqrt(ms + EPS) * gain_ref[0] + shift_ref[0]
    h2_ref[...] = h2
    wr = wr_ref[...]
    h1, h2b, _ = _split3(h2)
    w1, w2b, _ = _split3(wr)
    lg_ref[0] = _route_rows(_dot(h1, w1) + (_dot(h1, w2b) + _dot(h2b, w1)))


def _outproj(ya_f, ya_b, hr, u_lat, x, g1, gain2, shift2, w_pa, w_pb, w_out, w_r, tm):
    b_, s_, d = x.shape
    w = ya_f.shape[2]
    row = lambda b, i: (b, i, 0)
    const2 = lambda b, i: (0, 0)
    bmap = lambda b, i: (b, 0, 0)
    ucol = lambda t: (lambda b, i: (b, i, t))
    single = pl.Buffered(1)
    return pl.pallas_call(
        _outproj_kernel,
        out_shape=(jax.ShapeDtypeStruct((b_, s_, d), F32),
                   jax.ShapeDtypeStruct((b_ * s_, d), F32),
                   jax.ShapeDtypeStruct((b_, s_, LANES), F32)),
        grid=(b_, s_ // tm),
        in_specs=[
            pl.BlockSpec((1, tm, w), row), pl.BlockSpec((1, tm, w), row), pl.BlockSpec((1, tm, w), row),
            pl.BlockSpec((1, tm, w), ucol(COL_UG)),
            pl.BlockSpec((1, tm, w), ucol(COL_GA)), pl.BlockSpec((1, tm, w), ucol(COL_GA + 1)),
            pl.BlockSpec((1, tm, w), ucol(COL_GB)), pl.BlockSpec((1, tm, w), ucol(COL_GB + 1)),
            pl.BlockSpec((1, tm, d), row),
            pl.BlockSpec((1, 1, d), bmap), pl.BlockSpec((1, 1, d), bmap), pl.BlockSpec((1, 1, d), bmap),
            pl.BlockSpec((w, d), const2, pipeline_mode=single),
            pl.BlockSpec((w, d), const2, pipeline_mode=single),
            pl.BlockSpec((d, d), const2, pipeline_mode=single),
            pl.BlockSpec((d, LANES), const2, pipeline_mode=single),
        ],
        out_specs=(pl.BlockSpec((1, tm, d), row),
                   pl.BlockSpec((tm, d), lambda b, i: (b * (s_ // tm) + i, 0)),
                   pl.BlockSpec((1, tm, LANES), row)),
        compiler_params=_cparams(("parallel", "parallel")),
        name="outproj",
    )(ya_f, ya_b, hr, u_lat, u_lat, u_lat, u_lat, u_lat, x, g1, gain2, shift2, w_pa, w_pb, w_out, w_r)


def _route_rows(lg):
    lane = lax.broadcasted_iota(jnp.int32, lg.shape, 1)
    neg = -jnp.inf
    big = jnp.int32(1 << 20)
    g_l = jnp.where(lane < N_GROUPS, lg, neg)
    g_max = jnp.max(g_l, axis=-1, keepdims=True)
    g_sel = jnp.min(jnp.where(g_l == g_max, lane, big), axis=-1, keepdims=True)
    p_g = 1.0 / jnp.sum(jnp.exp(g_l - g_max), axis=-1, keepdims=True)
    lo = N_GROUPS + g_sel * EXPERTS_PER_GROUP
    e_l = jnp.where((lane >= lo) & (lane < lo + EXPERTS_PER_GROUP), lg, neg)
    v1 = jnp.max(e_l, axis=-1, keepdims=True)
    i1 = jnp.min(jnp.where(e_l == v1, lane, big), axis=-1, keepdims=True)
    e_l2 = jnp.where(lane == i1, neg, e_l)
    v2 = jnp.max(e_l2, axis=-1, keepdims=True)
    i2 = jnp.min(jnp.where(e_l2 == v2, lane, big), axis=-1, keepdims=True)
    e2 = jnp.exp(v2 - v1)
    w1 = p_g / (1.0 + e2)
    w2 = p_g * e2 / (1.0 + e2)
    return jnp.where(lane == 0, (i1 - N_GROUPS).astype(F32),
           jnp.where(lane == 1, (i2 - N_GROUPS).astype(F32),
           jnp.where(lane == 2, w1, jnp.where(lane == 3, w2, 0.0))))


MOE_DMA_UNROLL = 32


MOE_W_DEPTH = 3
MOE_W_SPLIT = 2


def _moe_kernel(R, SUB, nf, ni, item_e, item_rows, item_base, src_tok, dst_row, h2_hbm, w1_hbm, w3_hbm, w2_hbm,
                y_hbm, x_scr, acc_scr, y_scr, w1r, w3r, w2r, gsem, ssem, wsem):
    i = pl.program_id(0)
    j = pl.program_id(1)
    nsub_of = lambda r: (r + SUB - 1) // SUB
    rows = item_rows[i]
    nsub = nsub_of(rows)
    i_next = jnp.minimum(i + 1, ni - 1)
    nsub_next = jnp.where(i + 1 < ni, nsub_of(item_rows[i_next]), 0)
    rows_prev = jnp.where(i > 0, item_rows[jnp.maximum(i - 1, 0)], 0)
    base = item_base[i]
    base_next = item_base[i_next]
    buf = i % 2
    groups = SUB // MOE_DMA_UNROLL

    d_model, tf = w1r.shape[1], w1r.shape[2]

    def weight_copies(c):
        it = c // nf
        jc = c % nf
        e = item_e[jnp.minimum(it, ni - 1)]
        slot = c % MOE_W_DEPTH
        f0 = pl.multiple_of(jc * tf, tf)
        out = []
        for h in range(MOE_W_SPLIT):
            ra, rb = d_model // MOE_W_SPLIT, tf // MOE_W_SPLIT
            out.append(pltpu.make_async_copy(w1_hbm.at[e, pl.ds(h * ra, ra), pl.ds(f0, tf)],
                                             w1r.at[slot, pl.ds(h * ra, ra), :], wsem.at[slot]))
            out.append(pltpu.make_async_copy(w3_hbm.at[e, pl.ds(h * ra, ra), pl.ds(f0, tf)],
                                             w3r.at[slot, pl.ds(h * ra, ra), :], wsem.at[slot]))
            out.append(pltpu.make_async_copy(w2_hbm.at[e, pl.ds(f0 + h * rb, rb), :],
                                             w2r.at[slot, pl.ds(h * rb, rb), :], wsem.at[slot]))
        return out

    def chunk_live(c):
        it = c // nf
        return (it < ni) & (item_rows[jnp.minimum(it, ni - 1)] > 0)

    cur = i * nf + j

    @pl.when((cur == 0) & chunk_live(0))
    def _():
        for cp in weight_copies(0):
            cp.start()

        @pl.when(chunk_live(1))
        def _():
            for cp in weight_copies(1):
                cp.start()

    @pl.when((rows > 0) & chunk_live(cur + 2))
    def _():
        for cp in weight_copies(cur + 2):
            cp.start()

    def tok_rows(ref, r):
        return ref.at[pl.ds(r, 1), :]

    def gather_group(b0, b, gi):
        r0 = pl.multiple_of(gi * MOE_DMA_UNROLL, MOE_DMA_UNROLL)
        for k in range(MOE_DMA_UNROLL):
            pltpu.make_async_copy(tok_rows(h2_hbm, src_tok[b0 + r0 + k]), tok_rows(x_scr.at[b], r0 + k),
                                  gsem.at[b]).start(priority=k % 2)

    def gather_wait_block(b):
        pltpu.make_async_copy(h2_hbm.at[pl.ds(0, SUB), :], x_scr.at[b, pl.ds(0, SUB), :], gsem.at[b]).wait()

    def scatter_copy(b0, r):
        return pltpu.make_async_copy(tok_rows(y_scr, r), tok_rows(y_hbm, dst_row[b0 + r]), ssem)

    def scatter_group(b0, gi):
        r0 = pl.multiple_of(gi * MOE_DMA_UNROLL, MOE_DMA_UNROLL)
        for k in range(MOE_DMA_UNROLL):
            scatter_copy(b0, r0 + k).start(priority=k % 2)

    def repeat(n, fn, lo=0):
        def body(q, c):
            fn(q)
            return c
        lax.fori_loop(lo, n, body, 0)

    def scatter_rows(b0, n):
        full = n // MOE_DMA_UNROLL
        repeat(full, lambda gi: scatter_group(b0, gi))
        repeat(n, lambda r: scatter_copy(b0, r).start(), lo=full * MOE_DMA_UNROLL)

    def scatter_wait_rows(n):
        u = MOE_DMA_UNROLL
        full = n // MOE_DMA_UNROLL
        repeat(full, lambda q: pltpu.make_async_copy(y_scr.at[pl.ds(0, u), :], y_hbm.at[pl.ds(0, u), :],
                                                      ssem).wait())
        repeat(n, lambda r: scatter_copy(0, 0).wait(), lo=full * MOE_DMA_UNROLL)

    @pl.when((i == 0) & (j == 0))
    def _():
        repeat(nsub * groups, lambda gi: gather_group(base, 0, gi))

    bps = -(-(R // SUB) // nf)
    @pl.when(j * bps < nsub_next)
    def _():
        repeat(jnp.minimum((j + 1) * bps, nsub_next) * groups, lambda gi: gather_group(base_next, 1 - buf, gi),
               lo=j * bps * groups)

    @pl.when(j == 0)
    def _():
        repeat(nsub, lambda q: gather_wait_block(buf))

    @pl.when(j == nf - 1)
    def _():
        scatter_wait_rows(rows_prev)

    @pl.when(rows > 0)
    def _():
        for cp in weight_copies(cur):
            cp.wait()
        slot = cur % MOE_W_DEPTH

        def experts(m):
            def run():
                x = x_scr[buf, 0:m, :].astype(BF16)
                h1 = _dot(x, w1r[slot].astype(BF16))
                h3 = _dot(x, w3r[slot].astype(BF16))
                hh = (h1 * _sigmoid(h1) * h3).astype(BF16)
                contrib = _dot(hh, w2r[slot].astype(BF16))

                @pl.when(j == 0)
                def _():
                    acc_scr[0:m, :] = contrib

                @pl.when((j > 0) & (j < nf - 1))
                def _():
                    acc_scr[0:m, :] += contrib

                @pl.when(j == nf - 1)
                def _():
                    y_scr[0:m, :] = acc_scr[0:m, :] + contrib
            return run
        lax.switch(nsub - 1, [experts((q + 1) * SUB) for q in range(R // SUB)])

    @pl.when(j == nf - 1)
    def _():
        scatter_rows(base, rows)

        @pl.when(i == ni - 1)
        def _():
            scatter_wait_rows(rows)


def _moe(h2, item_e, item_rows, item_base, src_tok, dst_row, w1, w3, w2):
    t_ = h2.shape[0]
    ne, d, f = w1.shape
    R, SUB, tf = MOE_ITEM_ROWS, MOE_SUB_ROWS, MOE_F_CHUNK
    ni = item_e.shape[0]
    nf = f // tf
    assert R % SUB == 0 and SUB % MOE_DMA_UNROLL == 0 and nf >= 2
    grid_spec = pltpu.PrefetchScalarGridSpec(
        num_scalar_prefetch=5,
        grid=(ni, nf),
        in_specs=[
            pl.BlockSpec(memory_space=pl.ANY),
            pl.BlockSpec(memory_space=pl.ANY),
            pl.BlockSpec(memory_space=pl.ANY),
            pl.BlockSpec(memory_space=pl.ANY),
        ],
        out_specs=pl.BlockSpec(memory_space=pl.ANY),
        scratch_shapes=[
            pltpu.VMEM((2, R, d), F32), pltpu.VMEM((R, d), F32), pltpu.VMEM((R, d), F32),
            pltpu.VMEM((MOE_W_DEPTH, d, tf), F32), pltpu.VMEM((MOE_W_DEPTH, d, tf), F32),
            pltpu.VMEM((MOE_W_DEPTH, tf, d), F32),
            pltpu.SemaphoreType.DMA((2,)), pltpu.SemaphoreType.DMA(()), pltpu.SemaphoreType.DMA((MOE_W_DEPTH,)),
        ],
    )
    return pl.pallas_call(
        functools.partial(_moe_kernel, R, SUB, nf, ni),
        out_shape=jax.ShapeDtypeStruct((TOP_K * t_, d), F32),
        grid_spec=grid_spec,
        compiler_params=_cparams(("arbitrary", "arbitrary")),
        name="moe_experts",
    )(item_e, item_rows, item_base, src_tok, dst_row, h2, w1, w3, w2)


def _moe_plan(expert, t_):
    R = MOE_ITEM_ROWS
    na = expert.shape[0]
    ni = N_EXPERTS + na // R
    order = jnp.argsort(expert, stable=True).astype(jnp.int32)
    pad = jnp.zeros((R,), jnp.int32)
    src_tok = jnp.concatenate([order // TOP_K, pad])
    dst_row = jnp.concatenate([(order % TOP_K) * t_ + order // TOP_K, pad])
    eids = jnp.arange(N_EXPERTS, dtype=jnp.int32)
    counts = jnp.sum((expert[:, None] == eids[None, :]).astype(jnp.int32), axis=0)
    first = jnp.cumsum(counts) - counts
    n_items = (counts + R - 1) // R
    item_end = jnp.cumsum(n_items)
    item_start = item_end - n_items
    ii = jnp.arange(ni, dtype=jnp.int32)
    e_of = jnp.minimum(jnp.sum((item_end[None, :] <= ii[:, None]).astype(jnp.int32), axis=1), N_EXPERTS - 1)
    oh_e = (e_of[:, None] == eids[None, :]).astype(jnp.int32)
    pick = lambda v: jnp.sum(oh_e * v[None, :], axis=1)
    part = ii - pick(item_start)
    rows = jnp.clip(pick(counts) - part * R, 0, R).astype(jnp.int32)
    base = (pick(first) + part * R).astype(jnp.int32)
    live = ii < item_end[-1]
    rows = jnp.where(live, rows, 0)
    base = jnp.where(live, base, 0)
    last_e = jnp.max(jnp.where(live, e_of, 0))
    e_of = jnp.where(live, e_of, last_e).astype(jnp.int32)
    return e_of, rows, base, src_tok, dst_row


def _final_kernel(x1_ref, y0_ref, y1_ref, rt_ref, g2_ref, fg_ref, o_ref):
    rt = rt_ref[0]
    y = rt[:, TOP_K:TOP_K + 1] * y0_ref[...] + rt[:, TOP_K + 1:TOP_K + 2] * y1_ref[...]
    x = x1_ref[0] + g2_ref[0] * y
    ms = jnp.mean(x * x, axis=-1, keepdims=True)
    o_ref[0] = x * lax.rsqrt(ms + EPS) * fg_ref[...]


def _final(x1, ybuf, routed, g2, final_g, tm):
    b_, s_, d = x1.shape
    nt = s_ // tm
    return pl.pallas_call(
        _final_kernel,
        out_shape=jax.ShapeDtypeStruct((b_, s_, d), F32),
        grid=(b_, nt),
        in_specs=[
            pl.BlockSpec((1, tm, d), lambda b, i: (b, i, 0)),
            pl.BlockSpec((tm, d), lambda b, i: (b * nt + i, 0)),
            pl.BlockSpec((tm, d), lambda b, i: (b_ * nt + b * nt + i, 0)),
            pl.BlockSpec((1, tm, LANES), lambda b, i: (b, i, 0)),
            pl.BlockSpec((1, 1, d), lambda b, i: (b, 0, 0)),
            pl.BlockSpec((1, d), lambda b, i: (0, 0)),
        ],
        out_specs=pl.BlockSpec((1, tm, d), lambda b, i: (b, i, 0)),
        compiler_params=_cparams(("parallel", "parallel")),
        name="final_norm",
    )(x1, ybuf, ybuf, routed, g2, final_g)


def kernel(x, c, ctx, c_ctx, w_mod, b_mod, norm1_g, w_in, b_gates, mlstm_head_g, conv_w, conv_b, lru_wa, lru_ba,
           lru_wx, lru_bx, lru_lam, w_pa, w_pb, w_out, norm2_g, w_rg, w_re, w1, w3, w2, final_g):
    b_, s_, d = x.shape
    n_ctx = ctx.shape[1]
    rows = s_ // GRID_W
    assert w_mod.shape[0] == 1, "single layer"
    wm = d // 2
    nh = MLSTM_HEADS

    cc = jnp.zeros((8, d), F32).at[:b_].set(c).at[b_].set(c_ctx)
    mod = _modulation(cc, w_mod[0], b_mod[0][None, :])
    sh1, sc1, g1, sh2, sc2, g2 = [mod[:b_, i * d:(i + 1) * d][:, None, :] for i in range(6)]
    csh1, csc1 = mod[b_:b_ + 1, 0:d][:, None, :], mod[b_:b_ + 1, d:2 * d][:, None, :]

    assert wm == 1024 and 4 * nh == N_GATE_COLS
    w_t = jnp.transpose(w_in[0])
    bg_pad = jnp.zeros((1, LANES), F32).at[0, :4 * nh].set(b_gates[0])

    ng = norm1_g[0][None, None, :]
    u_lat, g_lat = _inproj(x, ng * (1.0 + sc1), sh1, w_t, bg_pad, tuple(range(N_COL_TILES)), min(1024, s_))
    u_ctx, g_ctx = _inproj(ctx.reshape(1, b_ * n_ctx, d), ng * (1.0 + csc1), csh1, w_t, bg_pad,
                           (COL_Q, COL_K, COL_V, COL_UX), b_ * n_ctx)
    u_ctx = u_ctx.reshape(b_, n_ctx, -1)
    g_ctx = g_ctx.reshape(b_, n_ctx, LANES)

    ya_f, ya_b = _mlstm(u_lat, u_ctx, g_lat, g_ctx, mlstm_head_g[0], MLSTM_CHUNK)

    ux_lat = u_lat[:, :, COL_UX * wm:(COL_UX + 1) * wm]
    ux_col = ux_lat.reshape(b_, rows, GRID_W, wm).transpose(0, 2, 1, 3).reshape(b_, s_, wm)
    ux_seq = jnp.concatenate([u_ctx[:, :, CTX_UX * wm:(CTX_UX + 1) * wm], ux_col], axis=1)
    wa, wx = lru_wa[0], lru_wx[0]
    w_cat = (0.5 * jnp.concatenate([wa[0], wx[0], wa[1], wx[1]], axis=-1)).astype(BF16)
    cblk = wm // LRU_BLOCKS
    blk = lambda v: v.reshape(LRU_BLOCKS, cblk)
    b_cat = 0.5 * jnp.concatenate([blk(lru_ba[0, 0]), blk(lru_bx[0, 0]), blk(lru_ba[0, 1]), blk(lru_bx[0, 1])],
                                  axis=-1).reshape(1, 4 * wm)
    h_col = _rglru(ux_seq, conv_w[0], conv_b[0][None, :], w_cat, b_cat, lru_lam[0], n_ctx)
    hr = h_col.reshape(b_, GRID_W, rows, wm).transpose(0, 2, 1, 3).reshape(b_, s_, wm)

    w_r = jnp.zeros((d, LANES), F32).at[:, :N_GROUPS].set(w_rg[0]).at[:, N_GROUPS:N_GROUPS + N_EXPERTS].set(w_re[0])
    n2 = norm2_g[0][None, None, :]
    x1, h2, routed = _outproj(ya_f, ya_b, hr, u_lat, x, g1, n2 * (1.0 + sc2), sh2,
                              w_pa[0].astype(BF16), w_pb[0].astype(BF16), w_out[0].astype(BF16), w_r, 256)

    t_ = b_ * s_
    expert = routed.reshape(t_, LANES)[:, :TOP_K].astype(jnp.int32).reshape(-1)
    item_e, item_rows, item_base, src_tok, dst_row = _moe_plan(expert, t_)
    ybuf = _moe(h2, item_e, item_rows, item_base, src_tok, dst_row, w1[0], w3[0], w2[0])

    return _final(x1, ybuf, routed, g2, final_g[None, :], 256)
```

```python
import functools
import math

import jax
import jax.numpy as jnp
from jax import lax
from jax.experimental import pallas as pl
from jax.experimental.pallas import tpu as pltpu

F32 = jnp.float32
BF16 = jnp.bfloat16

EPS = 1e-6
GRID_W = 64
MLSTM_HEADS = 8
LRU_BLOCKS = 8
LRU_C = 8.0
N_GROUPS = 4
EXPERTS_PER_GROUP = 8
N_EXPERTS = N_GROUPS * EXPERTS_PER_GROUP
TOP_K = 2

VMEM_LIMIT_BYTES = 56 * 1024 * 1024
LANES = 128

MLSTM_CHUNK = 128
MOE_ITEM_ROWS = 768
MOE_SUB_ROWS = 128
MOE_F_CHUNK = 256


def _cparams(sem):
    return pltpu.CompilerParams(dimension_semantics=sem, vmem_limit_bytes=VMEM_LIMIT_BYTES)


def _sigmoid(x):
    return 0.5 * jnp.tanh(0.5 * x) + 0.5


def _dot(a, b):
    return jnp.dot(a, b, preferred_element_type=F32)


def _split3(x):
    x1 = x.astype(BF16)
    r1 = x - x1.astype(F32)
    x2 = r1.astype(BF16)
    x3 = (r1 - x2.astype(F32)).astype(BF16)
    return x1, x2, x3


def _mod_kernel(c_ref, w_ref, b_ref, o_ref):
    c = c_ref[...]
    s = (c * _sigmoid(c)).astype(BF16)
    o_ref[...] = _dot(s, w_ref[...].astype(BF16)) + b_ref[...]


def _modulation(cc, w_mod, b_mod):
    m, d = cc.shape
    n = w_mod.shape[1]
    tn = 1024
    return pl.pallas_call(
        _mod_kernel,
        out_shape=jax.ShapeDtypeStruct((m, n), F32),
        grid=(n // tn,),
        in_specs=[
            pl.BlockSpec((m, d), lambda j: (0, 0)),
            pl.BlockSpec((d, tn), lambda j: (0, j)),
            pl.BlockSpec((1, tn), lambda j: (0, j)),
        ],
        out_specs=pl.BlockSpec((m, tn), lambda j: (0, j)),
        compiler_params=_cparams(("parallel",)),
        name="modulation",
    )(cc, w_mod, b_mod)


N_PRE_TILES = 5
N_GATE_COLS = 32


def _dot_nt(a, b):
    return lax.dot_general(a, b, (((1,), (1,)), ((), ())), preferred_element_type=F32)


def _inproj_kernel(x_ref, gain_ref, shift_ref, wt_ref, wg_ref, bg_ref, u_ref, g_ref, h_scr):
    @pl.when(pl.program_id(2) == 0)
    def _():
        x = x_ref[0]
        ms = jnp.mean(x * x, axis=-1, keepdims=True)
        h = x * lax.rsqrt(ms + EPS) * gain_ref[0] + shift_ref[0]
        hb = h.astype(BF16)
        h_scr[...] = hb
        gates = _dot_nt(hb, wg_ref[...].astype(BF16))
        pad = jnp.zeros((gates.shape[0], LANES - N_GATE_COLS), F32)
        g_ref[0] = jnp.concatenate([gates, pad], axis=1) + bg_ref[...]

    u_ref[0] = _dot_nt(h_scr[...], wt_ref[...].astype(BF16)).astype(BF16)


def _inproj(x, gain, shift, w_t, b_gates, tiles, tm):
    b_, s_, d = x.shape
    tn = 1024
    nj = len(tiles)
    if tiles == tuple(range(nj)):
        tile_of = lambda j: j
    else:
        assert tiles == (COL_Q, COL_K, COL_V, COL_UX), tiles
        tile_of = lambda j: jnp.where(j == 3, COL_UX, j)
    row_of = lambda j: tile_of(j) * tn + jnp.where(tile_of(j) >= N_PRE_TILES, N_GATE_COLS, 0)
    per_batch_mod = gain.shape[0] > 1
    mmap = (lambda b, i, j: (b, 0, 0)) if per_batch_mod else (lambda b, i, j: (0, 0, 0))
    gate_blk = N_PRE_TILES * tn // N_GATE_COLS
    return pl.pallas_call(
        _inproj_kernel,
        out_shape=(jax.ShapeDtypeStruct((b_, s_, nj * tn), BF16),
                   jax.ShapeDtypeStruct((b_, s_, LANES), F32)),
        grid=(b_, s_ // tm, nj),
        in_specs=[
            pl.BlockSpec((1, tm, d), lambda b, i, j: (b, i, 0)),
            pl.BlockSpec((1, 1, d), mmap),
            pl.BlockSpec((1, 1, d), mmap),
            pl.BlockSpec((pl.Element(tn), pl.Element(d)), lambda b, i, j: (pl.multiple_of(row_of(j), 8), 0)),
            pl.BlockSpec((N_GATE_COLS, d), lambda b, i, j: (gate_blk, 0)),
            pl.BlockSpec((1, LANES), lambda b, i, j: (0, 0)),
        ],
        out_specs=(pl.BlockSpec((1, tm, tn), lambda b, i, j: (b, i, j)),
                   pl.BlockSpec((1, tm, LANES), lambda b, i, j: (b, i, 0))),
        scratch_shapes=[pltpu.VMEM((tm, d), BF16)],
        compiler_params=_cparams(("parallel", "parallel", "arbitrary")),
        name="inproj",
    )(x, gain, shift, w_t, w_t, b_gates)


COL_Q, COL_K, COL_V, COL_OF, COL_OB, COL_UX, COL_UG, COL_GA, COL_GB = 0, 1, 2, 3, 4, 5, 6, 7, 9
CTX_UX = 3
N_COL_TILES = 11


def _mlstm_kernel(nc, nl, L, dh, dirs, *refs):
    nd = len(dirs)
    per_dir = [refs[9 * li:9 * li + 9] for li in range(nd)]
    hg_ref, tri_ref = refs[9 * nd], refs[9 * nd + 1]
    outs = refs[9 * nd + 2:9 * nd + 2 + nd]
    c_scr, m_scr = refs[-2:]
    s = pl.program_id(1)
    nh = MLSTM_HEADS
    scale = dh ** -0.5

    @pl.when(s == 0)
    def _():
        c_scr[...] = jnp.zeros_like(c_scr)
        m_scr[...] = jnp.full_like(m_scr, -1e30)

    is_ctx = s < nc
    ones_col = (lax.broadcasted_iota(jnp.int32, (L, dh), 1) == 0).astype(BF16)
    row = lax.broadcasted_iota(jnp.int32, (L, L), 0)
    col = lax.broadcasted_iota(jnp.int32, (L, L), 1)

    for li, d in enumerate(dirs):
        qc, kc, vc, ql, kl, vl, gc, gl, o_ref = per_dir[li]
        out_ref = outs[li]
        q = jnp.where(is_ctx, qc[0], ql[0])
        k = jnp.where(is_ctx, kc[0], kl[0])
        v = jnp.where(is_ctx, vc[0], vl[0])
        g = jnp.where(is_ctx, gc[0], gl[0])
        gt = g.T
        ls = jnp.minimum(g, 0.0) - jnp.log(1.0 + jnp.exp(-jnp.abs(g)))
        lst = jnp.minimum(gt, 0.0) - jnp.log(1.0 + jnp.exp(-jnp.abs(gt)))
        tri = tri_ref[d]
        mask = (col <= row) if d == 0 else (col >= row)
        l1, l2, l3 = _split3(ls)
        bcol_all = _dot(tri, l1) + _dot(tri, l2) + _dot(tri, l3)
        t1, t2, t3 = _split3(lst)
        trit = tri_ref[1 - d]
        brow_all = _dot(t1, trit) + _dot(t2, trit) + _dot(t3, trit)
        last = L - 1 if d == 0 else 0

        bal = pltpu.roll(bcol_all, LANES - 8, axis=1)
        cmax = g - bal
        trow = lax.broadcasted_iota(jnp.int32, (L, LANES), 0)
        sft = 1
        while sft < L:
            if d == 0:
                shifted = jnp.where(trow >= sft, pltpu.roll(cmax, sft, axis=0), -jnp.inf)
            else:
                shifted = jnp.where(trow < L - sft, pltpu.roll(cmax, L - sft, axis=0), -jnp.inf)
            cmax = jnp.maximum(cmax, shifted)
            sft *= 2
        m_row = m_scr[li][0:1, :]
        gg = bal + m_row
        mt = jnp.maximum(gg, bal + cmax)
        e_col = bal - mt
        w_inter = jnp.exp(gg - mt)
        enm = jnp.exp(-mt)
        tot = bal[last:last + 1, :]
        wlog = tot - bal + g
        m_new = jnp.maximum(tot + m_row, jnp.max(wlog, axis=0, keepdims=True))
        ws = jnp.exp(wlog - m_new) * scale
        decay = jnp.exp(tot + m_row - m_new)
        m_scr[li] = jnp.broadcast_to(m_new, m_scr.shape[1:])

        for h in range(nh):
            u = li * nh + h
            ci = 16 * d + h
            cf = 16 * d + 8 + h
            hs = slice(h * dh, (h + 1) * dh)
            qh, kh, vh = q[:, hs], k[:, hs], v[:, hs]
            vext = jnp.concatenate([vh, ones_col], axis=1)
            cr = gt[ci:ci + 1, :] - brow_all[cf:cf + 1, :] + math.log(scale)
            c_prev = c_scr[u]

            sqk = lax.dot_general(qh, kh, (((1,), (1,)), ((), ())), preferred_element_type=F32)
            w_intra = jnp.exp(jnp.where(mask, e_col[:, ci:ci + 1] + cr, -jnp.inf))
            a = (sqk * w_intra).astype(BF16)
            qs = (qh.astype(F32) * w_inter[:, ci:ci + 1]).astype(BF16)
            r = _dot(jnp.concatenate([a, qs], axis=1),
                     jnp.concatenate([vext, c_prev.astype(BF16)], axis=0))

            num = r[:, :dh]
            den = r[:, dh:dh + 1]
            rden = 1.0 / jnp.maximum(jnp.abs(den), enm[:, ci:ci + 1])
            ssq = jnp.sum(num * num, axis=-1, keepdims=True)
            sc = rden * lax.rsqrt(rden * rden * ssq * (1.0 / dh) + EPS)
            gate = _sigmoid(o_ref[0, :, hs].astype(F32))
            out_ref[0, :, hs] = (num * sc * hg_ref[h:h + 1, :] * gate).astype(BF16)

            kw = (kh.astype(F32) * ws[:, ci:ci + 1]).astype(BF16)
            upd = lax.dot_general(kw, vext, (((0,), (0,)), ((), ())), preferred_element_type=F32)
            c_scr[u] = decay[:, ci:ci + 1] * c_prev + upd


def _mlstm(u_lat, u_ctx, g_lat, g_ctx, head_g, L, dirs):
    b_, s_, _ = u_lat.shape
    n_ctx = u_ctx.shape[1]
    nh = MLSTM_HEADS
    w = 1024
    dh = w // nh
    nc, nl = n_ctx // L, s_ // L
    steps = nc + nl
    t0 = jnp.tril(jnp.ones((L, L), F32))
    tri = jnp.stack([t0, t0.T]).astype(BF16)

    def cidx(d):
        if d == 0:
            return lambda st: jnp.minimum(st, nc - 1), lambda st: jnp.maximum(st - nc, 0)
        return lambda st: jnp.maximum(nc - 1 - st, 0), lambda st: jnp.minimum(nl - 1 + nc - st, nl - 1)

    in_specs, args = [], []
    for d in dirs:
        fc, fl = cidx(d)
        for t in range(3):
            in_specs.append(pl.BlockSpec((1, L, w), lambda b, st, fc=fc, t=t: (b, fc(st), t)))
            args.append(u_ctx)
        for t in (COL_Q, COL_K, COL_V):
            in_specs.append(pl.BlockSpec((1, L, w), lambda b, st, fl=fl, t=t: (b, fl(st), t)))
            args.append(u_lat)
        in_specs.append(pl.BlockSpec((1, L, LANES), lambda b, st, fc=fc: (b, fc(st), 0)))
        args.append(g_ctx)
        in_specs.append(pl.BlockSpec((1, L, LANES), lambda b, st, fl=fl: (b, fl(st), 0)))
        args.append(g_lat)
        in_specs.append(pl.BlockSpec((1, L, w), lambda b, st, fl=fl, t=COL_OF + d: (b, fl(st), t)))
        args.append(u_lat)
    in_specs += [pl.BlockSpec((nh, dh), lambda b, st: (0, 0)),
                 pl.BlockSpec((2, L, L), lambda b, st: (0, 0, 0))]
    args += [head_g, tri]
    nd = len(dirs)
    out_specs = tuple(pl.BlockSpec((1, L, w), lambda b, st, fl=cidx(d)[1]: (b, fl(st), 0)) for d in dirs)
    return pl.pallas_call(
        functools.partial(_mlstm_kernel, nc, nl, L, dh, tuple(dirs)),
        out_shape=(jax.ShapeDtypeStruct((b_, s_, w), BF16),) * nd,
        grid=(b_, steps),
        in_specs=in_specs,
        out_specs=out_specs,
        scratch_shapes=[pltpu.VMEM((nd * nh, dh, 2 * dh), F32), pltpu.VMEM((nd, 8, LANES), F32)],
        compiler_params=_cparams(("parallel", "arbitrary")),
        name="mlstm",
    )(*args)


SUBLANES = 8
LRU_BLK = SUBLANES * SUBLANES


def _rglru_kernel(n_ctx, n_tot, ux_ref, cw_ref, cb_ref, w_ref, b_ref, lam_ref, out_ref,
                  us, z_nat, a_f, b_f, a_b, b_b, h_f, h_b, o_nat):
    c = LANES
    g = SUBLANES
    n_lat = n_tot - n_ctx
    nq = n_tot // g
    nblk = n_tot // LRU_BLK
    nblk_ctx = n_ctx // LRU_BLK

    zeros_g = jnp.zeros((g, c), F32)
    us[0:g, :] = zeros_g
    us[g + n_ctx:2 * g + n_ctx, :] = zeros_g
    us[2 * g + n_tot:3 * g + n_tot, :] = zeros_g
    us[g:g + n_ctx, :] = ux_ref[0, 0:n_ctx, :].astype(F32)
    us[2 * g + n_ctx:2 * g + n_tot, :] = ux_ref[0, n_ctx:n_tot, :].astype(F32)

    def conv(base, n):
        acc = cb_ref[...] + cw_ref[2:3, :] * us[base:base + n, :]
        for j, off in ((0, -2), (1, -1), (3, 1)):
            acc = acc + cw_ref[j:j + 1, :] * us[base + off:base + off + n, :]
        return acc

    z_nat[0:n_ctx, :] = conv(g, n_ctx)
    z_nat[n_ctx:n_tot, :] = conv(2 * g + n_ctx, n_lat)

    z = jnp.concatenate([z_nat[pl.ds(k, nq, stride=g), :] for k in range(g)], axis=0)
    t = jnp.tanh(_dot(z.astype(BF16), w_ref[0]) + b_ref[...])
    for d, (a_scr, b_scr) in enumerate(((a_f, b_f), (a_b, b_b))):
        t_r = t[:, (2 * d) * c:(2 * d + 1) * c]
        i = 0.5 * t[:, (2 * d + 1) * c:(2 * d + 2) * c] + 0.5
        lam = lam_ref[d:d + 1, :]
        softplus = jnp.maximum(-lam, 0.0) + jnp.log(1.0 + jnp.exp(-jnp.abs(lam)))
        half_k = (-0.5 * LRU_C * math.log2(math.e)) * softplus
        a = jnp.exp2(half_k * t_r + half_k)
        om = 1.0 - a * a
        mult = jnp.where(om > 0.0, om * lax.rsqrt(om), 0.0)
        a_scr[...] = a
        b_scr[...] = mult * i * z

    sub = lax.broadcasted_iota(jnp.int32, (g, c), 0)

    def block_scan(a_scr, b_scr, h_scr, blk, carry, reverse):
        r0 = pl.multiple_of(blk * g, g)
        order = range(g - 1, -1, -1) if reverse else range(g)
        hs, ps = {}, {}
        h = p_ = None
        for k in order:
            a = a_scr[pl.ds(k * nq + r0, g), :]
            b = b_scr[pl.ds(k * nq + r0, g), :]
            h = b if h is None else a * h + b
            p_ = a if p_ is None else a * p_
            hs[k], ps[k] = h, p_
        def shift(x, sft, fill):
            if reverse:
                return jnp.where(sub < g - sft, pltpu.roll(x, g - sft, 0), fill)
            return jnp.where(sub >= sft, pltpu.roll(x, sft, 0), fill)
        pe, he = shift(p_, 1, 1.0), shift(h, 1, 0.0)
        for sft in (1, 2, 4):
            he = pe * shift(he, sft, 0.0) + he
            pe = pe * shift(pe, sft, 1.0)
        cin = pe * carry + he
        for k in order:
            h_scr[pl.ds(k * nq + r0, g), :] = hs[k] + ps[k] * cin
        last = 0 if reverse else g - 1
        tot = p_ * cin + h
        return tot[last:last + 1, :]

    def body(it, carry):
        cf, cb = carry
        cf = block_scan(a_f, b_f, h_f, it, cf, False)
        jb = jnp.where(it < nblk_ctx, nblk_ctx - 1 - it, nblk - 1 + nblk_ctx - it)
        cb = block_scan(a_b, b_b, h_b, jb, cb, True)
        return cf, cb

    zero = jnp.zeros((1, c), F32)
    lax.fori_loop(0, nblk, body, (zero, zero), unroll=2)

    q_ctx = n_ctx // g
    for k in range(g):
        o_nat[pl.ds(k, n_lat // g, stride=g), :] = (h_f[k * nq + q_ctx:(k + 1) * nq, :]
                                                    + h_b[k * nq + q_ctx:(k + 1) * nq, :])
    out_ref[0] = o_nat[...].astype(BF16)


def _rglru(ux_seq, conv_w, conv_b, w_cat, b_cat, lam, n_ctx):
    b_, n_tot, w = ux_seq.shape
    nb = LRU_BLOCKS
    c = w // nb
    return pl.pallas_call(
        functools.partial(_rglru_kernel, n_ctx, n_tot),
        out_shape=jax.ShapeDtypeStruct((b_, n_tot - n_ctx, w), BF16),
        grid=(b_, nb),
        in_specs=[
            pl.BlockSpec((1, n_tot, c), lambda b, k: (b, 0, k)),
            pl.BlockSpec((4, c), lambda b, k: (0, k)),
            pl.BlockSpec((1, c), lambda b, k: (0, k)),
            pl.BlockSpec((1, c, 4 * c), lambda b, k: (k, 0, 0)),
            pl.BlockSpec((1, 4 * c), lambda b, k: (0, k)),
            pl.BlockSpec((2, c), lambda b, k: (0, k)),
        ],
        out_specs=pl.BlockSpec((1, n_tot - n_ctx, c), lambda b, k: (b, 0, k)),
        scratch_shapes=([pltpu.VMEM((n_tot + 3 * SUBLANES, c), F32)] + [pltpu.VMEM((n_tot, c), F32)] * 7
                        + [pltpu.VMEM((n_tot - n_ctx, c), F32)]),
        compiler_params=_cparams(("parallel", "parallel")),
        name="rglru",
    )(ux_seq, conv_w, conv_b, w_cat, b_cat, lam)


def _outproj_kernel(yaf_ref, yab_ref, hr_ref, ug_ref, ga0_ref, ga1_ref, gb0_ref, gb1_ref, x_ref,
                    g1_ref, gain_ref, shift_ref, wpa_ref, wpb_ref, wout_ref, wr_ref,
                    x1_ref, h2_ref, lg_ref):
    ya = (yaf_ref[0].astype(F32) + yab_ref[0].astype(F32)).astype(BF16)
    ug = ug_ref[0].astype(F32)
    gelu = 0.5 * ug * (1.0 + jnp.tanh(0.7978845608028654 * (ug + 0.044715 * ug * ug * ug)))
    yb = (hr_ref[0].astype(F32) * gelu).astype(BF16)
    pa = _dot(ya, wpa_ref[...])
    pb = _dot(yb, wpb_ref[...])
    ga = jnp.concatenate([ga0_ref[0], ga1_ref[0]], axis=1).astype(F32)
    gb = jnp.concatenate([gb0_ref[0], gb1_ref[0]], axis=1).astype(F32)
    mix = (_sigmoid(ga) * pa + _sigmoid(gb) * pb).astype(BF16)
    x1 = x_ref[0] + g1_ref[0] * _dot(mix, wout_ref[...])
    x1_ref[0] = x1
    ms = jnp.mean(x1 * x1, axis=-1, keepdims=True)
    h2 = x1 * lax.rsqrt(ms + EPS) * gain_ref[0] + shift_ref[0]
    h2_ref[...] = h2
    wr = wr_ref[...]
    h1, h2b, _ = _split3(h2)
    w1, w2b, _ = _split3(wr)
    lg_ref[0] = _route_rows(_dot(h1, w1) + (_dot(h1, w2b) + _dot(h2b, w1)))


def _outproj(ya_f, ya_b, hr, u_lat, x, g1, gain2, shift2, w_pa, w_pb, w_out, w_r, tm):
    b_, s_, d = x.shape
    w = ya_f.shape[2]
    row = lambda b, i: (b, i, 0)
    const2 = lambda b, i: (0, 0)
    bmap = lambda b, i: (b, 0, 0)
    ucol = lambda t: (lambda b, i: (b, i, t))
    single = pl.Buffered(1)
    return pl.pallas_call(
        _outproj_kernel,
        out_shape=(jax.ShapeDtypeStruct((b_, s_, d), F32),
                   jax.ShapeDtypeStruct((b_ * s_, d), F32),
                   jax.ShapeDtypeStruct((b_, s_, LANES), F32)),
        grid=(b_, s_ // tm),
        in_specs=[
            pl.BlockSpec((1, tm, w), row), pl.BlockSpec((1, tm, w), row), pl.BlockSpec((1, tm, w), row),
            pl.BlockSpec((1, tm, w), ucol(COL_UG)),
            pl.BlockSpec((1, tm, w), ucol(COL_GA)), pl.BlockSpec((1, tm, w), ucol(COL_GA + 1)),
            pl.BlockSpec((1, tm, w), ucol(COL_GB)), pl.BlockSpec((1, tm, w), ucol(COL_GB + 1)),
            pl.BlockSpec((1, tm, d), row),
            pl.BlockSpec((1, 1, d), bmap), pl.BlockSpec((1, 1, d), bmap), pl.BlockSpec((1, 1, d), bmap),
            pl.BlockSpec((w, d), const2, pipeline_mode=single),
            pl.BlockSpec((w, d), const2, pipeline_mode=single),
            pl.BlockSpec((d, d), const2, pipeline_mode=single),
            pl.BlockSpec((d, LANES), const2, pipeline_mode=single),
        ],
        out_specs=(pl.BlockSpec((1, tm, d), row),
                   pl.BlockSpec((tm, d), lambda b, i: (b * (s_ // tm) + i, 0)),
                   pl.BlockSpec((1, tm, LANES), row)),
        compiler_params=_cparams(("parallel", "parallel")),
        name="outproj",
    )(ya_f, ya_b, hr, u_lat, u_lat, u_lat, u_lat, u_lat, x, g1, gain2, shift2, w_pa, w_pb, w_out, w_r)


def _route_rows(lg):
    lane = lax.broadcasted_iota(jnp.int32, lg.shape, 1)
    neg = -jnp.inf
    big = jnp.int32(1 << 20)
    g_l = jnp.where(lane < N_GROUPS, lg, neg)
    g_max = jnp.max(g_l, axis=-1, keepdims=True)
    g_sel = jnp.min(jnp.where(g_l == g_max, lane, big), axis=-1, keepdims=True)
    p_g = 1.0 / jnp.sum(jnp.exp(g_l - g_max), axis=-1, keepdims=True)
    lo = N_GROUPS + g_sel * EXPERTS_PER_GROUP
    e_l = jnp.where((lane >= lo) & (lane < lo + EXPERTS_PER_GROUP), lg, neg)
    v1 = jnp.max(e_l, axis=-1, keepdims=True)
    i1 = jnp.min(jnp.where(e_l == v1, lane, big), axis=-1, keepdims=True)
    e_l2 = jnp.where(lane == i1, neg, e_l)
    v2 = jnp.max(e_l2, axis=-1, keepdims=True)
    i2 = jnp.min(jnp.where(e_l2 == v2, lane, big), axis=-1, keepdims=True)
    e2 = jnp.exp(v2 - v1)
    w1 = p_g / (1.0 + e2)
    w2 = p_g * e2 / (1.0 + e2)
    return jnp.where(lane == 0, (i1 - N_GROUPS).astype(F32),
           jnp.where(lane == 1, (i2 - N_GROUPS).astype(F32),
           jnp.where(lane == 2, w1, jnp.where(lane == 3, w2, 0.0))))


MOE_DMA_UNROLL = 32


MOE_W_DEPTH = 3
MOE_W_SPLIT = 2


def _moe_kernel(R, SUB, nf, ni, item_e, item_rows, item_base, src_tok, dst_row, h2_hbm, w1_hbm, w3_hbm, w2_hbm,
                y_hbm, x_scr, acc_scr, y_scr, w1r, w3r, w2r, gsem, ssem, wsem):
    i = pl.program_id(0)
    j = pl.program_id(1)
    nsub_of = lambda r: (r + SUB - 1) // SUB
    rows = item_rows[i]
    nsub = nsub_of(rows)
    i_next = jnp.minimum(i + 1, ni - 1)
    rows_prev = jnp.where(i > 0, item_rows[jnp.maximum(i - 1, 0)], 0)
    base = item_base[i]
    base_next = item_base[i_next]
    buf = i % 2

    d_model, tf = w1r.shape[1], w1r.shape[2]

    def weight_copies(c):
        it = c // nf
        jc = c % nf
        e = item_e[jnp.minimum(it, ni - 1)]
        slot = c % MOE_W_DEPTH
        f0 = pl.multiple_of(jc * tf, tf)
        out = []
        for h in range(MOE_W_SPLIT):
            ra, rb = d_model // MOE_W_SPLIT, tf // MOE_W_SPLIT
            out.append(pltpu.make_async_copy(w1_hbm.at[e, pl.ds(h * ra, ra), pl.ds(f0, tf)],
                                             w1r.at[slot, pl.ds(h * ra, ra), :], wsem.at[slot]))
            out.append(pltpu.make_async_copy(w3_hbm.at[e, pl.ds(h * ra, ra), pl.ds(f0, tf)],
                                             w3r.at[slot, pl.ds(h * ra, ra), :], wsem.at[slot]))
            out.append(pltpu.make_async_copy(w2_hbm.at[e, pl.ds(f0 + h * rb, rb), :],
                                             w2r.at[slot, pl.ds(h * rb, rb), :], wsem.at[slot]))
        return out

    def chunk_live(c):
        it = c // nf
        return (it < ni) & (item_rows[jnp.minimum(it, ni - 1)] > 0)

    cur = i * nf + j

    @pl.when((cur == 0) & chunk_live(0))
    def _():
        for cp in weight_copies(0):
            cp.start()

        @pl.when(chunk_live(1))
        def _():
            for cp in weight_copies(1):
                cp.start()

    @pl.when((rows > 0) & chunk_live(cur + 2))
    def _():
        for cp in weight_copies(cur + 2):
            cp.start()

    sl = SUBLANES
    tpg = MOE_DMA_UNROLL // sl

    def row_at(ref, r):
        return ref.at[pl.ds(lax.shift_right_logical(r, sl.bit_length() - 1), 1), pl.ds(r & (sl - 1), 1), :]

    def row_in_group(ref, g0, k):
        return ref.at[pl.ds(g0 + k // sl, 1), pl.ds(k % sl, 1), :]

    def gather_group(b0, b, gi):
        g0 = pl.multiple_of(gi * tpg, tpg)
        for k in range(MOE_DMA_UNROLL):
            pltpu.make_async_copy(row_at(h2_hbm, src_tok[b0 + gi * MOE_DMA_UNROLL + k]),
                                  row_in_group(x_scr.at[b], g0, k), gsem.at[b]).start(priority=k % 2)

    def gather_wait_group(b):
        pltpu.make_async_copy(h2_hbm.at[pl.ds(0, tpg)], x_scr.at[b, pl.ds(0, tpg)], gsem.at[b]).wait()

    def scatter_copy(b0, r):
        return pltpu.make_async_copy(row_at(y_scr, r), row_at(y_hbm, dst_row[b0 + r]), ssem)

    def scatter_group(b0, gi):
        g0 = pl.multiple_of(gi * tpg, tpg)
        for k in range(MOE_DMA_UNROLL):
            pltpu.make_async_copy(row_in_group(y_scr, g0, k),
                                  row_at(y_hbm, dst_row[b0 + gi * MOE_DMA_UNROLL + k]), ssem).start(priority=k % 2)

    def repeat(n, fn, lo=0):
        def body(q, c):
            fn(q)
            return c
        lax.fori_loop(lo, n, body, 0)

    def scatter_rows(b0, n):
        full = n // MOE_DMA_UNROLL
        repeat(full, lambda gi: scatter_group(b0, gi))
        repeat(n, lambda r: scatter_copy(b0, r).start(), lo=full * MOE_DMA_UNROLL)

    def scatter_wait_rows(n):
        full = n // MOE_DMA_UNROLL
        repeat(full, lambda q: pltpu.make_async_copy(y_scr.at[pl.ds(0, tpg)], y_hbm.at[pl.ds(0, tpg)], ssem).wait())
        repeat(n, lambda r: scatter_copy(0, 0).wait(), lo=full * MOE_DMA_UNROLL)

    n_groups = lambda r: (r + MOE_DMA_UNROLL - 1) // MOE_DMA_UNROLL
    g_cur = n_groups(rows)
    g_next = jnp.where(i + 1 < ni, n_groups(item_rows[i_next]), 0)

    @pl.when((i == 0) & (j == 0))
    def _():
        x_scr[...] = jnp.zeros_like(x_scr)
        repeat(g_cur, lambda gi: gather_group(base, 0, gi))

    gps = -(-(R // MOE_DMA_UNROLL) // nf)
    repeat(jnp.minimum((j + 1) * gps, g_next), lambda gi: gather_group(base_next, 1 - buf, gi), lo=j * gps)

    @pl.when(j == 0)
    def _():
        repeat(g_cur, lambda q: gather_wait_group(buf))

    @pl.when(j == nf - 1)
    def _():
        scatter_wait_rows(rows_prev)

    @pl.when(rows > 0)
    def _():
        for cp in weight_copies(cur):
            cp.wait()
        slot = cur % MOE_W_DEPTH

        def experts(m):
            def run():
                x = x_scr[buf, 0:m // sl].reshape(m, d_model).astype(BF16)
                h1 = _dot(x, w1r[slot].astype(BF16))
                h3 = _dot(x, w3r[slot].astype(BF16))
                hh = (h1 * _sigmoid(h1) * h3).astype(BF16)
                contrib = _dot(hh, w2r[slot].astype(BF16))

                @pl.when(j == 0)
                def _():
                    acc_scr[0:m, :] = contrib

                @pl.when((j > 0) & (j < nf - 1))
                def _():
                    acc_scr[0:m, :] += contrib

                @pl.when(j == nf - 1)
                def _():
                    y_scr[0:m // sl] = (acc_scr[0:m, :] + contrib).reshape(m // sl, sl, d_model)
            return run
        lax.switch(nsub - 1, [experts((q + 1) * SUB) for q in range(R // SUB)])

    @pl.when(j == nf - 1)
    def _():
        scatter_rows(base, rows)

        @pl.when(i == ni - 1)
        def _():
            scatter_wait_rows(rows)


def _moe(h2, item_e, item_rows, item_base, src_tok, dst_row, w1, w3, w2):
    t_ = h2.shape[0]
    ne, d, f = w1.shape
    R, SUB, tf = MOE_ITEM_ROWS, MOE_SUB_ROWS, MOE_F_CHUNK
    sl = SUBLANES
    ni = item_e.shape[0]
    nf = f // tf
    assert R % SUB == 0 and SUB % MOE_DMA_UNROLL == 0 and MOE_DMA_UNROLL % sl == 0 and t_ % sl == 0 and nf >= 2
    grid_spec = pltpu.PrefetchScalarGridSpec(
        num_scalar_prefetch=5,
        grid=(ni, nf),
        in_specs=[
            pl.BlockSpec(memory_space=pl.ANY),
            pl.BlockSpec(memory_space=pl.ANY),
            pl.BlockSpec(memory_space=pl.ANY),
            pl.BlockSpec(memory_space=pl.ANY),
        ],
        out_specs=pl.BlockSpec(memory_space=pl.ANY),
        scratch_shapes=[
            pltpu.VMEM((2, R // sl, sl, d), F32), pltpu.VMEM((R, d), F32), pltpu.VMEM((R // sl, sl, d), F32),
            pltpu.VMEM((MOE_W_DEPTH, d, tf), F32), pltpu.VMEM((MOE_W_DEPTH, d, tf), F32),
            pltpu.VMEM((MOE_W_DEPTH, tf, d), F32),
            pltpu.SemaphoreType.DMA((2,)), pltpu.SemaphoreType.DMA(()), pltpu.SemaphoreType.DMA((MOE_W_DEPTH,)),
        ],
    )
    y = pl.pallas_call(
        functools.partial(_moe_kernel, R, SUB, nf, ni),
        out_shape=jax.ShapeDtypeStruct((TOP_K * t_ // sl, sl, d), F32),
        grid_spec=grid_spec,
        compiler_params=_cparams(("arbitrary", "arbitrary")),
        name="moe_experts",
    )(item_e, item_rows, item_base, src_tok, dst_row, h2.reshape(t_ // sl, sl, d), w1, w3, w2)
    return y.reshape(TOP_K * t_, d)


def _moe_plan(expert, t_):
    R = MOE_ITEM_ROWS
    na = expert.shape[0]
    ni = N_EXPERTS + na // R
    order = jnp.argsort(expert, stable=True).astype(jnp.int32)
    pad = jnp.zeros((R,), jnp.int32)
    src_tok = jnp.concatenate([order // TOP_K, pad])
    dst_row = jnp.concatenate([(order % TOP_K) * t_ + order // TOP_K, pad])
    eids = jnp.arange(N_EXPERTS, dtype=jnp.int32)
    counts = jnp.sum((expert[:, None] == eids[None, :]).astype(jnp.int32), axis=0)
    first = jnp.cumsum(counts) - counts
    n_items = (counts + R - 1) // R
    item_end = jnp.cumsum(n_items)
    item_start = item_end - n_items
    ii = jnp.arange(ni, dtype=jnp.int32)
    e_of = jnp.minimum(jnp.sum((item_end[None, :] <= ii[:, None]).astype(jnp.int32), axis=1), N_EXPERTS - 1)
    oh_e = (e_of[:, None] == eids[None, :]).astype(jnp.int32)
    pick = lambda v: jnp.sum(oh_e * v[None, :], axis=1)
    part = ii - pick(item_start)
    rows = jnp.clip(pick(counts) - part * R, 0, R).astype(jnp.int32)
    base = (pick(first) + part * R).astype(jnp.int32)
    live = ii < item_end[-1]
    rows = jnp.where(live, rows, 0)
    base = jnp.where(live, base, 0)
    last_e = jnp.max(jnp.where(live, e_of, 0))
    e_of = jnp.where(live, e_of, last_e).astype(jnp.int32)
    return e_of, rows, base, src_tok, dst_row


def _final_kernel(x1_ref, y0_ref, y1_ref, rt_ref, g2_ref, fg_ref, o_ref):
    rt = rt_ref[0]
    y = rt[:, TOP_K:TOP_K + 1] * y0_ref[...] + rt[:, TOP_K + 1:TOP_K + 2] * y1_ref[...]
    x = x1_ref[0] + g2_ref[0] * y
    ms = jnp.mean(x * x, axis=-1, keepdims=True)
    o_ref[0] = x * lax.rsqrt(ms + EPS) * fg_ref[...]


def _final(x1, ybuf, routed, g2, final_g, tm):
    b_, s_, d = x1.shape
    nt = s_ // tm
    return pl.pallas_call(
        _final_kernel,
        out_shape=jax.ShapeDtypeStruct((b_, s_, d), F32),
        grid=(b_, nt),
        in_specs=[
            pl.BlockSpec((1, tm, d), lambda b, i: (b, i, 0)),
            pl.BlockSpec((tm, d), lambda b, i: (b * nt + i, 0)),
            pl.BlockSpec((tm, d), lambda b, i: (b_ * nt + b * nt + i, 0)),
            pl.BlockSpec((1, tm, LANES), lambda b, i: (b, i, 0)),
            pl.BlockSpec((1, 1, d), lambda b, i: (b, 0, 0)),
            pl.BlockSpec((1, d), lambda b, i: (0, 0)),
        ],
        out_specs=pl.BlockSpec((1, tm, d), lambda b, i: (b, i, 0)),
        compiler_params=_cparams(("parallel", "parallel")),
        name="final_norm",
    )(x1, ybuf, ybuf, routed, g2, final_g)


def kernel(x, c, ctx, c_ctx, w_mod, b_mod, norm1_g, w_in, b_gates, mlstm_head_g, conv_w, conv_b, lru_wa, lru_ba,
           lru_wx, lru_bx, lru_lam, w_pa, w_pb, w_out, norm2_g, w_rg, w_re, w1, w3, w2, final_g):
    b_, s_, d = x.shape
    n_ctx = ctx.shape[1]
    rows = s_ // GRID_W
    assert w_mod.shape[0] == 1, "single layer"
    wm = d // 2
    nh = MLSTM_HEADS

    cc = jnp.zeros((8, d), F32).at[:b_].set(c).at[b_].set(c_ctx)
    mod = _modulation(cc, w_mod[0], b_mod[0][None, :])
    sh1, sc1, g1, sh2, sc2, g2 = [mod[:b_, i * d:(i + 1) * d][:, None, :] for i in range(6)]
    csh1, csc1 = mod[b_:b_ + 1, 0:d][:, None, :], mod[b_:b_ + 1, d:2 * d][:, None, :]

    assert wm == 1024 and 4 * nh == N_GATE_COLS
    w_t = jnp.transpose(w_in[0])
    bg_pad = jnp.zeros((1, LANES), F32).at[0, :4 * nh].set(b_gates[0])

    ng = norm1_g[0][None, None, :]
    u_lat, g_lat = _inproj(x, ng * (1.0 + sc1), sh1, w_t, bg_pad, tuple(range(N_COL_TILES)), min(1024, s_))
    u_ctx, g_ctx = _inproj(ctx.reshape(1, b_ * n_ctx, d), ng * (1.0 + csc1), csh1, w_t, bg_pad,
                           (COL_Q, COL_K, COL_V, COL_UX), b_ * n_ctx)
    u_ctx = u_ctx.reshape(b_, n_ctx, -1)
    g_ctx = g_ctx.reshape(b_, n_ctx, LANES)

    (ya_f,) = _mlstm(u_lat, u_ctx, g_lat, g_ctx, mlstm_head_g[0], MLSTM_CHUNK, (0,))
    (ya_b,) = _mlstm(u_lat, u_ctx, g_lat, g_ctx, mlstm_head_g[0], MLSTM_CHUNK, (1,))

    ux_lat = u_lat[:, :, COL_UX * wm:(COL_UX + 1) * wm]
    ux_col = ux_lat.reshape(b_, rows, GRID_W, wm).transpose(0, 2, 1, 3).reshape(b_, s_, wm)
    ux_seq = jnp.concatenate([u_ctx[:, :, CTX_UX * wm:(CTX_UX + 1) * wm], ux_col], axis=1)
    wa, wx = lru_wa[0], lru_wx[0]
    w_cat = (0.5 * jnp.concatenate([wa[0], wx[0], wa[1], wx[1]], axis=-1)).astype(BF16)
    cblk = wm // LRU_BLOCKS
    blk = lambda v: v.reshape(LRU_BLOCKS, cblk)
    b_cat = 0.5 * jnp.concatenate([blk(lru_ba[0, 0]), blk(lru_bx[0, 0]), blk(lru_ba[0, 1]), blk(lru_bx[0, 1])],
                                  axis=-1).reshape(1, 4 * wm)
    h_col = _rglru(ux_seq, conv_w[0], conv_b[0][None, :], w_cat, b_cat, lru_lam[0], n_ctx)
    hr = h_col.reshape(b_, GRID_W, rows, wm).transpose(0, 2, 1, 3).reshape(b_, s_, wm)

    w_r = jnp.zeros((d, LANES), F32).at[:, :N_GROUPS].set(w_rg[0]).at[:, N_GROUPS:N_GROUPS + N_EXPERTS].set(w_re[0])
    n2 = norm2_g[0][None, None, :]
    x1, h2, routed = _outproj(ya_f, ya_b, hr, u_lat, x, g1, n2 * (1.0 + sc2), sh2,
                              w_pa[0].astype(BF16), w_pb[0].astype(BF16), w_out[0].astype(BF16), w_r, 256)

    t_ = b_ * s_
    expert = routed.reshape(t_, LANES)[:, :TOP_K].astype(jnp.int32).reshape(-1)
    item_e, item_rows, item_base, src_tok, dst_row = _moe_plan(expert, t_)
    ybuf = _moe(h2, item_e, item_rows, item_base, src_tok, dst_row, w1[0], w3[0], w2[0])

    return _final(x1, ybuf, routed, g2, final_g[None, :], 256)
```
